```python
import math
import jax, jax.numpy as jnp
from jax import lax
import numpy as np

D_MODEL = 1024
BATCH = 16
SEQ = 2048
DEPTH = 1

HEAD_DIM = 64
D_RWKV = D_MODEL // 2
D_ATTN = D_MODEL - D_RWKV
H_RWKV = D_RWKV // HEAD_DIM
H_ATTN = D_ATTN // HEAD_DIM
DECAY_LORA = 64
ICLR_LORA = 64
GATE_LORA = 128
LNX_EPS = 64e-5
DIL_PATTERNS = ((128, 1), (512, 4), (2048, 16))
NEG_INF = -1e30
N_BUCKETS = 32
MAX_DISTANCE = 2048
N_GROUPS = 4
EXPERTS_PER_GROUP = 8
N_EXPERTS = N_GROUPS * EXPERTS_PER_GROUP
TOP_K = 2
D_EXPERT = 512
MOE_BLOCK = 128
D_PLE = 256
LN_EPS = 1e-5
ALPHA = (2 * DEPTH) ** 0.25
BETA = (8 * DEPTH) ** -0.25

kernel_name = "hymba_rwkv7_dilated_hmoe_deepnorm"


def layer_norm(x, g, b):
    xf = x.astype(jnp.float32)
    mu = jnp.mean(xf, -1, keepdims=True)
    var = jnp.mean(jnp.square(xf - mu), -1, keepdims=True)
    return ((xf - mu) * lax.rsqrt(var + LN_EPS) * g + b).astype(x.dtype)


def time_shift(t):
    return jnp.pad(t, ((0, 0), (1, 0), (0, 0)))[:, :-1]


def wkv7_scan(r, decay, k, v, aa, bb):
    B, S, H, N = r.shape

    def step(state, inp):
        r_t, w_t, k_t, v_t, a_t, b_t = inp
        sa = jnp.einsum('bhvk,bhk->bhv', state, a_t)
        state = (state * w_t[:, :, None, :] + sa[..., None] * b_t[:, :, None, :]
                 + v_t[..., None] * k_t[:, :, None, :])
        y_t = jnp.einsum('bhvk,bhk->bhv', state, r_t)
        return state, y_t

    xs = tuple(jnp.moveaxis(t, 1, 0) for t in (r, decay, k, v, aa, bb))
    state0 = jnp.zeros((B, H, N, N), jnp.float32)
    _, y = lax.scan(step, state0, xs)
    return jnp.moveaxis(y, 0, 1)


def rwkv7_mix(h, r_in, k_in, v_in, mu_rkv, mu_lora, w0, w1, w2, a0, a1, a2, g1, g2,
              k_k, k_a, r_k, lnx_g, lnx_b):
    B, S, _ = h.shape
    f32 = jnp.float32
    shift_mix = lambda t, mu: (t + (time_shift(t) - t) * mu).astype(f32)
    r = shift_mix(r_in, mu_rkv[0])
    k = shift_mix(k_in, mu_rkv[1])
    v = shift_mix(v_in, mu_rkv[2])
    dh = time_shift(h) - h
    xw = h + dh * mu_lora[0]
    xa = h + dh * mu_lora[1]
    xg = h + dh * mu_lora[2]
    w_log = -jax.nn.softplus(-(w0 + jnp.tanh(xw @ w1) @ w2).astype(f32)) - 0.5
    decay = jnp.exp(-jnp.exp(w_log))
    a = jax.nn.sigmoid((a0 + (xa @ a1) @ a2).astype(f32))
    g = (jax.nn.sigmoid(xg @ g1) @ g2).astype(f32)
    heads = lambda t: t.reshape(B, S, H_RWKV, HEAD_DIM)
    kk = heads(k * k_k)
    kk = kk * lax.rsqrt(jnp.maximum(jnp.sum(kk * kk, -1, keepdims=True), 1e-24))
    k = k * (1.0 + (a - 1.0) * k_a)
    r, k, v, a, decay = heads(r), heads(k), heads(v), heads(a), heads(decay)
    y = wkv7_scan(r, decay, k, v, -kk, kk * a)
    mu = jnp.mean(y, -1, keepdims=True)
    var = jnp.mean(jnp.square(y - mu), -1, keepdims=True)
    y = ((y - mu) * lax.rsqrt(var + LNX_EPS)).reshape(B, S, D_RWKV) * lnx_g + lnx_b
    bonus = jnp.sum(r * k * r_k, -1, keepdims=True) * v
    return ((y + bonus.reshape(B, S, D_RWKV)) * g).astype(h.dtype)


def t5_bucket(n):
    exact = N_BUCKETS // 2
    nf = np.maximum(n, 1).astype(np.float32)
    large = exact + (np.log(nf / exact) / math.log(MAX_DISTANCE / exact) * (N_BUCKETS - exact)).astype(np.int32)
    large = np.minimum(large, N_BUCKETS - 1)
    return np.where(n < exact, n, large).astype(np.int32)


def dilated_branch(q, k, v, rel_bias, window, dil):
    B, S, H, Dh = q.shape
    L = S // dil
    blk = window // dil
    nb = -(-L // blk)
    Lp = nb * blk

    def subsample(t):
        t = t.reshape(B, L, dil, H, Dh).transpose(0, 2, 3, 1, 4)
        return jnp.pad(t, ((0, 0), (0, 0), (0, 0), (0, Lp - L), (0, 0)))

    def band(t):
        t = jnp.pad(subsample(t), ((0, 0), (0, 0), (0, 0), (blk, 0), (0, 0))).reshape(B, dil, H, nb + 1, blk, Dh)
        return jnp.concatenate([t[:, :, :, :-1], t[:, :, :, 1:]], axis=4)

    qs = subsample(q).reshape(B, dil, H, nb, blk, Dh)
    kb, vb = band(k), band(v)
    rel = np.arange(blk)[:, None] + blk - np.arange(2 * blk)[None, :]
    kpos = np.arange(nb)[:, None] * blk + np.arange(2 * blk)[None, :] - blk
    valid = ((rel >= 0) & (rel <= blk))[None] & (kpos >= 0)[:, None, :]
    bias = jnp.transpose(rel_bias[t5_bucket(np.clip(rel, 0, None) * dil)], (2, 0, 1)).astype(jnp.float32)
    logits = jnp.einsum('bdhnqc,bdhnkc->bdhnqk', qs, kb) * (HEAD_DIM ** -0.5) + bias[None, None, :, None]
    logits = jnp.where(valid, logits, NEG_INF)
    m = jnp.max(logits, -1, keepdims=True)
    e = jnp.exp(logits - m)
    s = jnp.sum(e, -1, keepdims=True)
    o = jnp.einsum('bdhnqk,bdhnkc->bdhnqc', e, vb) / s
    lse = (m + jnp.log(s))[..., 0]

    def unsample(t):
        t = t.reshape(B, dil, H, Lp, *t.shape[5:])[:, :, :, :L]
        t = jnp.moveaxis(t, 3, 1)
        return t.reshape(B, S, H, *t.shape[4:])

    return unsample(o), unsample(lse)


def dilated_attention(q, k, v, rel_bias):
    outs, lses = [], []
    for window, dil in DIL_PATTERNS:
        o, l = dilated_branch(q, k, v, rel_bias, window, dil)
        outs.append(o)
        lses.append(l)
    wts = jax.nn.softmax(jnp.stack(lses, 0), axis=0)
    return jnp.einsum('pbsh,pbshc->bshc', wts, jnp.stack(outs, 0))


def hier_moe(x2, wr_g, br_g, wr_e, br_e, w_gate, w_up, w_down):
    N, D = x2.shape
    lg = (x2 @ wr_g).astype(jnp.float32) + br_g
    pg = jax.nn.softmax(lg, -1)
    grp = jnp.argmax(lg, -1)
    wg = jnp.take_along_axis(pg, grp[:, None], axis=-1)
    le = ((x2 @ wr_e).astype(jnp.float32) + br_e).reshape(N, N_GROUPS, EXPERTS_PER_GROUP)
    le = le[jnp.arange(N), grp]
    top_v, top_i = lax.top_k(le, TOP_K)
    we = jax.nn.softmax(top_v, -1) * wg
    eid = grp[:, None] * EXPERTS_PER_GROUP + top_i
    A = N * TOP_K
    e_flat = eid.reshape(A).astype(jnp.int32)
    w_flat = we.reshape(A)
    tok_flat = jnp.arange(A, dtype=jnp.int32) // TOP_K
    order = jnp.argsort(e_flat)
    e_sorted = e_flat[order]
    counts = jnp.bincount(e_flat, length=N_EXPERTS)
    start = jnp.cumsum(counts) - counts
    padded = (counts + MOE_BLOCK - 1) // MOE_BLOCK * MOE_BLOCK
    pend = jnp.cumsum(padded)
    pstart = pend - padded
    dest = pstart[e_sorted] + (jnp.arange(A) - start[e_sorted])
    nblk = -(-A // MOE_BLOCK) + N_EXPERTS
    P = nblk * MOE_BLOCK
    row_tok = jnp.zeros((P,), jnp.int32).at[dest].set(tok_flat[order])
    row_w = jnp.zeros((P,), jnp.float32).at[dest].set(w_flat[order])
    blk_e = jnp.minimum(jnp.searchsorted(pend, jnp.arange(nblk) * MOE_BLOCK, side='right'), N_EXPERTS - 1)
    xb = x2[row_tok].reshape(nblk, MOE_BLOCK, D)

    def run_block(args):
        xblk, e = args
        hdn = jax.nn.silu(xblk @ w_gate[e]) * (xblk @ w_up[e])
        return hdn @ w_down[e]

    yb = lax.map(run_block, (xb, blk_e)).reshape(P, D)
    return jnp.zeros((N, D), x2.dtype).at[row_tok].add((yb * row_w[:, None]).astype(x2.dtype))


def setup_inputs(seed: int = 0) -> dict:
    key = jax.random.key(seed)
    ks = iter(jax.random.split(key, 48))
    nrm = lambda shape, scale: jax.random.normal(next(ks), shape, jnp.float32) * scale
    L = DEPTH
    ramp = jnp.arange(D_RWKV, dtype=jnp.float32) / (D_RWKV - 1)
    d_in = 3 * D_RWKV + 3 * D_ATTN
    return {
        "x": nrm((BATCH, SEQ, D_MODEL), 1.0),
        "p": nrm((DEPTH, BATCH, SEQ, D_PLE), 1.0),
        "w_in": nrm((L, D_MODEL, d_in), D_MODEL ** -0.5),
        "mu_rkv": jax.random.uniform(next(ks), (L, 3, D_RWKV), jnp.float32),
        "mu_lora": jax.random.uniform(next(ks), (L, 3, D_MODEL), jnp.float32),
        "w0": (-6.5 + 5.0 * ramp ** 0.85)[None] + nrm((L, D_RWKV), 0.1),
        "w_lora1": nrm((L, D_MODEL, DECAY_LORA), D_MODEL ** -0.5),
        "w_lora2": nrm((L, DECAY_LORA, D_RWKV), 0.5 * DECAY_LORA ** -0.5),
        "a0": nrm((L, D_RWKV), 0.1),
        "a_lora1": nrm((L, D_MODEL, ICLR_LORA), D_MODEL ** -0.5),
        "a_lora2": nrm((L, ICLR_LORA, D_RWKV), ICLR_LORA ** -0.5),
        "g_lora1": nrm((L, D_MODEL, GATE_LORA), D_MODEL ** -0.5),
        "g_lora2": nrm((L, GATE_LORA, D_RWKV), GATE_LORA ** -0.5),
        "k_k": 0.85 + nrm((L, D_RWKV), 0.02),
        "k_a": 1.0 + nrm((L, D_RWKV), 0.02),
        "r_k": nrm((L, H_RWKV, HEAD_DIM), 0.1),
        "lnx_g": 1.0 + nrm((L, D_RWKV), 0.02),
        "lnx_b": nrm((L, D_RWKV), 0.02),
        "rel_bias": nrm((N_BUCKETS, H_ATTN), 0.5),
        "w_o": nrm((L, D_MODEL, D_MODEL), BETA * D_MODEL ** -0.5),
        "ln1_g": 1.0 + nrm((L, D_MODEL), 0.02),
        "ln1_b": nrm((L, D_MODEL), 0.02),
        "router_g": nrm((L, D_MODEL, N_GROUPS), D_MODEL ** -0.5),
        "router_g_b": nrm((L, N_GROUPS), 0.01),
        "router_e": nrm((L, D_MODEL, N_EXPERTS), D_MODEL ** -0.5),
        "router_e_b": nrm((L, N_EXPERTS), 0.01),
        "w_gate": nrm((L, N_EXPERTS, D_MODEL, D_EXPERT), D_MODEL ** -0.5),
        "w_up": nrm((L, N_EXPERTS, D_MODEL, D_EXPERT), D_MODEL ** -0.5),
        "w_down": nrm((L, N_EXPERTS, D_EXPERT, D_MODEL), BETA * D_EXPERT ** -0.5),
        "ple_gate": nrm((L, D_MODEL, D_MODEL), D_MODEL ** -0.5),
        "ple_proj": nrm((L, D_PLE, D_MODEL), BETA * D_PLE ** -0.5),
        "ln2_g": 1.0 + nrm((L, D_MODEL), 0.02),
        "ln2_b": nrm((L, D_MODEL), 0.02),
    }


def reference(x, p, w_in, mu_rkv, mu_lora, w0, w_lora1, w_lora2, a0, a_lora1, a_lora2,
              g_lora1, g_lora2, k_k, k_a, r_k, lnx_g, lnx_b, rel_bias, w_o, ln1_g, ln1_b,
              router_g, router_g_b, router_e, router_e_b, w_gate, w_up, w_down,
              ple_gate, ple_proj, ln2_g, ln2_b):
    B, S, D = x.shape
    splits = [D_RWKV, 2 * D_RWKV, 3 * D_RWKV, 3 * D_RWKV + D_ATTN, 3 * D_RWKV + 2 * D_ATTN]
    for i in range(DEPTH):
        h = x
        proj = h @ w_in[i]
        r_in, k_in, v_in, q_at, k_at, v_at = jnp.split(proj, splits, axis=-1)
        y_rwkv = rwkv7_mix(h, r_in, k_in, v_in, mu_rkv[i], mu_lora[i], w0[i], w_lora1[i], w_lora2[i],
                           a0[i], a_lora1[i], a_lora2[i], g_lora1[i], g_lora2[i],
                           k_k[i], k_a[i], r_k[i], lnx_g[i], lnx_b[i])
        to_heads = lambda t: t.astype(jnp.float32).reshape(B, S, H_ATTN, HEAD_DIM)
        y_att = dilated_attention(to_heads(q_at), to_heads(k_at), to_heads(v_at), rel_bias)
        y_att = y_att.reshape(B, S, D_ATTN).astype(x.dtype)
        mix = jnp.concatenate([y_rwkv, y_att], axis=-1) @ w_o[i]
        x = layer_norm(ALPHA * x + mix, ln1_g[i], ln1_b[i])
        moe = hier_moe(x.reshape(B * S, D), router_g[i], router_g_b[i], router_e[i], router_e_b[i],
                       w_gate[i], w_up[i], w_down[i]).reshape(B, S, D)
        ple = jax.nn.sigmoid(x @ ple_gate[i]) * (p[i] @ ple_proj[i])
        x = layer_norm(ALPHA * x + moe + ple, ln2_g[i], ln2_b[i])
    return x
```

```python
import functools
import math

import jax
import jax.numpy as jnp
import numpy as np
from jax import lax
from jax.experimental import pallas as pl
from jax.experimental.pallas import tpu as pltpu

F32 = jnp.float32
BF16 = jnp.bfloat16

HEAD_DIM = 64
LANES = 128
DECAY_LORA = 64
ICLR_LORA = 64
GATE_LORA = 128
LNX_EPS = 64e-5
DIL_PATTERNS = ((128, 1), (512, 4), (2048, 16))
NEG_INF = -1e30
N_BUCKETS = 32
MAX_DISTANCE = 2048
N_GROUPS = 4
EXPERTS_PER_GROUP = 8
N_EXPERTS = N_GROUPS * EXPERTS_PER_GROUP
TOP_K = 2
LN_EPS = 1e-5

WKV_CHUNK = 64
ATT_BLK = 128
PROJ_ROWS = 512
MIX_ROWS = 256
MOE_ROWS = 256
OUT_ROWS = 256
ROUTER_LANES = 128
VMEM_LIMIT = 56 * 1024 * 1024


def _params(sem):
    return pltpu.CompilerParams(dimension_semantics=sem, vmem_limit_bytes=VMEM_LIMIT)


def _dot(a, b):
    return jnp.dot(a.astype(BF16), b.astype(BF16), preferred_element_type=F32)


def _dot_nt(a, b):
    return lax.dot_general(a.astype(BF16), b.astype(BF16), (((1,), (1,)), ((), ())),
                           preferred_element_type=F32)


def _dot_tn(a, b):
    return lax.dot_general(a.astype(BF16), b.astype(BF16), (((0,), (0,)), ((), ())),
                           preferred_element_type=F32)


def _split(x):
    hi = x.astype(BF16)
    lo = (x - hi.astype(F32)).astype(BF16)
    return hi, lo


def _dot_exact_rhs(x, w01):
    hi, lo = _split(x)
    return (jnp.dot(hi, w01, preferred_element_type=F32)
            + jnp.dot(lo, w01, preferred_element_type=F32))


def _dot_exact_lhs(w01, x):
    hi, lo = _split(x)
    return (jnp.dot(w01, hi, preferred_element_type=F32)
            + jnp.dot(w01, lo, preferred_element_type=F32))


def _sigmoid(x):
    return 1.0 / (1.0 + jnp.exp(-x))


def _shift_rows(t, prev_row):
    rolled = pltpu.roll(t, 1, 0)
    row = lax.broadcasted_iota(jnp.int32, t.shape, 0)
    return jnp.where(row == 0, prev_row, rolled)


def _layer_norm(z, g, b):
    mu = jnp.mean(z, axis=-1, keepdims=True)
    zc = z - mu
    var = jnp.mean(zc * zc, axis=-1, keepdims=True)
    return zc * lax.rsqrt(var + LN_EPS) * g + b


def _proj_kernel(x_ref, win_ref, w1_ref, a1_ref, g1_ref, mul_ref, w2_ref, a2_ref, g2_ref,
                 w0_ref, a0_ref, rkv_ref, qkv_ref, lw_ref, ai_ref, g_ref, prev_ref, *, d_rkv):
    @pl.when(pl.program_id(1) == 0)
    def _():
        prev_ref[...] = jnp.zeros_like(prev_ref)

    h = x_ref[0]
    rows = h.shape[0]
    hprev = _shift_rows(h, prev_ref[0:1, :])
    prev_ref[0:1, :] = h[rows - 1:rows, :]

    proj = _dot(h, win_ref[...])
    rkv_ref[0] = proj[:, :d_rkv]
    qkv_ref[0] = proj[:, d_rkv:].astype(BF16)

    dh = hprev - h
    xw = h + dh * mul_ref[0:1, :]
    xa = h + dh * mul_ref[1:2, :]
    xg = h + dh * mul_ref[2:3, :]
    wl = w0_ref[...] + _dot(jnp.tanh(_dot(xw, w1_ref[...])), w2_ref[...])
    z = -wl
    softplus = jnp.maximum(z, 0.0) + jnp.log(1.0 + jnp.exp(-jnp.abs(z)))
    lw_ref[0] = -jnp.exp(-softplus - 0.5)
    ai_ref[0] = _sigmoid(a0_ref[...] + _dot(_dot(xa, a1_ref[...]), a2_ref[...]))
    g_ref[0] = _dot(_sigmoid(_dot(xg, g1_ref[...])), g2_ref[...])


def _stage_proj(x, w_in, w1, a1, g1, mu_lora, w2, a2, g2, w0, a0):
    B, S, D = x.shape
    d_in = w_in.shape[1]
    d_r = w2.shape[1]
    d_rkv = 3 * d_r
    d_att = d_in - d_rkv
    rows = min(PROJ_ROWS, S)
    const = lambda shape: pl.BlockSpec(shape, lambda b, j: (0,) * len(shape))
    tile = lambda c: pl.BlockSpec((1, rows, c), lambda b, j: (b, j, 0))
    return pl.pallas_call(
        functools.partial(_proj_kernel, d_rkv=d_rkv),
        grid=(B, S // rows),
        in_specs=[tile(D), const((D, d_in)), const(w1.shape), const(a1.shape), const(g1.shape),
                  const(mu_lora.shape), const(w2.shape), const(a2.shape), const(g2.shape),
                  const((1, d_r)), const((1, d_r))],
        out_specs=[tile(d_rkv), tile(d_att), tile(d_r), tile(d_r), tile(d_r)],
        out_shape=[jax.ShapeDtypeStruct((B, S, d_rkv), F32),
                   jax.ShapeDtypeStruct((B, S, d_att), BF16),
                   jax.ShapeDtypeStruct((B, S, d_r), F32),
                   jax.ShapeDtypeStruct((B, S, d_r), F32),
                   jax.ShapeDtypeStruct((B, S, d_r), F32)],
        scratch_shapes=[pltpu.VMEM((8, D), F32)],
        compiler_params=_params(("parallel", "arbitrary")),
        name="proj",
    )(x, w_in.astype(BF16), w1.astype(BF16), a1.astype(BF16), g1.astype(BF16), mu_lora,
      w2.astype(BF16), a2.astype(BF16), g2.astype(BF16), w0.reshape(1, d_r), a0.reshape(1, d_r))


def _wkv_kernel(rkv_ref, lw_ref, ai_ref, g_ref, mu_ref, kk_ref, ka_ref, rk_ref, lng_ref, lnb_ref,
                seg_ref, tril_ref, y_ref, s_ref, prev_ref, *, d_r):
    C = WKV_CHUNK

    @pl.when(pl.program_id(1) == 0)
    def _():
        s_ref[...] = jnp.zeros_like(s_ref)
        prev_ref[...] = jnp.zeros_like(prev_ref)

    rkv = rkv_ref[0]
    prev = _shift_rows(rkv, prev_ref[0:1, :])
    prev_ref[0:1, :] = rkv[C - 1:C, :]
    mixed = rkv + (prev - rkv) * mu_ref[...]
    r = mixed[:, :d_r]
    k = mixed[:, d_r:2 * d_r]
    v = mixed[:, 2 * d_r:]
    lw = lw_ref[0]
    a = ai_ref[0]
    seg = seg_ref[...]

    kk = k * kk_ref[...]
    kk = kk * lax.rsqrt(jnp.maximum(_dot_exact_rhs(kk * kk, seg), 1e-24))
    k = k * (1.0 + (a - 1.0) * ka_ref[...])
    aa = -kk
    bb = kk * a

    c = _dot_exact_lhs(tril_ref[...], lw)
    c_end = c[C - 1:C, :]
    e_neg = jnp.exp(-c)
    e_end = jnp.exp(c_end - c)
    a_t = aa * jnp.exp(c - lw)
    b_t = bb * e_neg
    k_t = k * e_neg
    r_t = r * jnp.exp(c)
    b_e = bb * e_end
    k_e = k * e_end
    p_end = jnp.exp(c_end)

    lane = lax.broadcasted_iota(jnp.int32, (1, LANES), 1)
    head0 = lane < HEAD_DIM
    ri = lax.broadcasted_iota(jnp.int32, (2 * C, 2 * C), 0)
    ci = lax.broadcasted_iota(jnp.int32, (2 * C, 2 * C), 1)
    strict = (ri % C) > (ci % C)
    incl = (ri % C) >= (ci % C)
    eye = (ri == ci).astype(F32)

    def stack(t):
        return jnp.concatenate([jnp.where(head0, t, 0.0), jnp.where(head0, 0.0, t)],
                               axis=0).astype(BF16)

    ys = []
    for p in range(d_r // LANES):
        sl = slice(p * LANES, (p + 1) * LANES)
        am, bm, km, rm, vm = stack(a_t[:, sl]), stack(b_t[:, sl]), stack(k_t[:, sl]), \
            stack(r_t[:, sl]), stack(v[:, sl])
        n = jnp.where(strict, _dot_nt(am, bm), 0.0)
        a_ak = jnp.where(strict, _dot_nt(am, km), 0.0)
        a_rb = jnp.where(incl, _dot_nt(rm, bm), 0.0)
        a_rk = jnp.where(incl, _dot_nt(rm, km), 0.0)
        t_inv = eye + n
        m = n
        for _ in range(int(math.log2(C)) - 1):
            m = _dot(m, m)
            t_inv = t_inv + _dot(t_inv, m)
        s_old = s_ref[p]
        s_bf = s_old.astype(BF16)
        u = _dot(t_inv, _dot_nt(am, s_bf) + _dot(a_ak, vm))
        yd = _dot_nt(rm, s_bf) + _dot(a_rb, u) + _dot(a_rk, vm)
        ys.append(yd[:C] + yd[C:])
        s_ref[p] = (s_old * p_end[:, sl] + _dot_tn(u, stack(b_e[:, sl]))
                    + _dot_tn(vm, stack(k_e[:, sl])))
    y = jnp.concatenate(ys, axis=1)

    inv = 1.0 / HEAD_DIM
    mu = _dot_exact_rhs(y, seg) * inv
    yc = y - mu
    var = _dot_exact_rhs(yc * yc, seg) * inv
    yn = yc * lax.rsqrt(var + LNX_EPS) * lng_ref[...] + lnb_ref[...]
    bonus = _dot_exact_rhs(r * k * rk_ref[...], seg) * v
    y_ref[0] = ((yn + bonus) * g_ref[0]).astype(y_ref.dtype)


def _stage_wkv(rkv, lw, ai, g, mu_rkv, k_k, k_a, r_k, lnx_g, lnx_b):
    B, S, d_r = lw.shape
    C = WKV_CHUNK
    head = np.arange(d_r) // HEAD_DIM
    seg = jnp.asarray(head[:, None] == head[None, :], BF16)
    tril = jnp.asarray(np.tril(np.ones((C, C))), BF16)
    const = lambda shape: pl.BlockSpec(shape, lambda b, j: (0,) * len(shape))
    tile = lambda c: pl.BlockSpec((1, C, c), lambda b, j: (b, j, 0))
    row = lambda t: t.reshape(1, -1)
    return pl.pallas_call(
        functools.partial(_wkv_kernel, d_r=d_r),
        grid=(B, S // C),
        in_specs=[tile(3 * d_r), tile(d_r), tile(d_r), tile(d_r), const((1, 3 * d_r)),
                  const((1, d_r)), const((1, d_r)), const((1, d_r)), const((1, d_r)),
                  const((1, d_r)), const((d_r, d_r)), const((C, C))],
        out_specs=tile(d_r),
        out_shape=jax.ShapeDtypeStruct((B, S, d_r), BF16),
        scratch_shapes=[pltpu.VMEM((d_r // LANES, LANES, LANES), F32),
                        pltpu.VMEM((8, 3 * d_r), F32)],
        compiler_params=_params(("parallel", "arbitrary")),
        name="wkv",
    )(rkv, lw, ai, g, row(mu_rkv), row(k_k), row(k_a), row(r_k), row(lnx_g), row(lnx_b), seg, tril)


def _t5_bucket(n):
    exact = N_BUCKETS // 2
    nf = np.maximum(n, 1).astype(np.float32)
    large = exact + (np.log(nf / exact) / math.log(MAX_DISTANCE / exact)
                     * (N_BUCKETS - exact)).astype(np.int32)
    large = np.minimum(large, N_BUCKETS - 1)
    return np.where(n < exact, n, large).astype(np.int32)


def _band_bias(rel_bias, dil):
    blk = ATT_BLK
    rel = np.arange(blk)[:, None] + blk - np.arange(2 * blk)[None, :]
    valid = (rel >= 0) & (rel <= blk)
    bias = jnp.transpose(rel_bias[_t5_bucket(np.clip(rel, 0, None) * dil)], (2, 0, 1)).astype(F32)
    return jnp.where(valid[None], bias, NEG_INF)


def _attn_block(q, kcat, vcat, bias_ref, col0, scale):
    nk = kcat.shape[0]
    lane = lax.broadcasted_iota(jnp.int32, (1, LANES), 1)
    head0 = lane < HEAD_DIM
    outs, lses = [], []
    for p in range(q.shape[1] // LANES):
        sl = slice(p * LANES, (p + 1) * LANES)
        qp, kp, vp = q[:, sl], kcat[:, sl], vcat[:, sl]
        o_h, l_h = [], []
        for h in range(2):
            qh = jnp.where(head0 if h == 0 else jnp.logical_not(head0), qp, jnp.zeros_like(qp))
            s = _dot_nt(qh, kp) * scale + bias_ref[2 * p + h, :, col0:col0 + nk]
            m = jnp.max(s, axis=-1, keepdims=True)
            e = jnp.exp(s - m)
            l = jnp.sum(e, axis=-1, keepdims=True)
            o_h.append(_dot(e, vp) / l)
            l_h.append(m + jnp.log(l))
        outs.append(jnp.where(head0, o_h[0], o_h[1]))
        lses.append(jnp.where(head0, l_h[0], l_h[1]))
    return jnp.concatenate(outs, axis=1), jnp.concatenate(lses, axis=1)


def _attn_kernel(q_ref, k_ref, v_ref, bias_ref, o_ref, l_ref, *, nb, scale):
    blk = ATT_BLK
    o, l = _attn_block(q_ref[0, 0:blk, :], k_ref[0, 0:blk, :], v_ref[0, 0:blk, :], bias_ref, blk,
                       scale)
    o_ref[0, 0:blk, :] = o
    l_ref[0, 0:blk, :] = l

    def body(n, carry):
        q0 = pl.multiple_of(n * blk, blk)
        k0 = pl.multiple_of((n - 1) * blk, blk)
        o, l = _attn_block(q_ref[0, pl.ds(q0, blk), :], k_ref[0, pl.ds(k0, 2 * blk), :],
                           v_ref[0, pl.ds(k0, 2 * blk), :], bias_ref, 0, scale)
        o_ref[0, pl.ds(q0, blk), :] = o
        l_ref[0, pl.ds(q0, blk), :] = l
        return carry

    if nb > 1:
        lax.fori_loop(1, nb, body, 0)


def _stage_attn(qkv, rel_bias, window, dil):
    B, S, d3 = qkv.shape
    d_att = d3 // 3
    L = S // dil
    nb = L // ATT_BLK
    assert window // dil == ATT_BLK and nb * ATT_BLK == L
    bias = _band_bias(rel_bias, dil)
    qkv_v = qkv.reshape(B, L, dil * d3)
    part = lambda i: pl.BlockSpec((1, L, d_att), lambda b, r: (b, 0, 3 * r + i))
    out = pl.BlockSpec((1, L, d_att), lambda b, r: (b, 0, r))
    o, l = pl.pallas_call(
        functools.partial(_attn_kernel, nb=nb, scale=HEAD_DIM ** -0.5),
        grid=(B, dil),
        in_specs=[part(0), part(1), part(2),
                  pl.BlockSpec(bias.shape, lambda b, r: (0, 0, 0))],
        out_specs=[out, out],
        out_shape=[jax.ShapeDtypeStruct((B, L, dil * d_att), F32)] * 2,
        compiler_params=_params(("parallel", "parallel")),
        name=f"attn_d{dil}",
    )(qkv_v, qkv_v, qkv_v, bias)
    return o.reshape(B * S, d_att), l.reshape(B * S, d_att)


def _mix_kernel(x_ref, yr_ref, o1_ref, o2_ref, o3_ref, l1_ref, l2_ref, l3_ref, p_ref, wor_ref,
                woa_ref, g_ref, b_ref, rhi_ref, rlo_ref, rb_ref, pg_ref, pp_ref,
                x1_ref, lg_ref, ple_ref, *, alpha):
    l1, l2, l3 = l1_ref[...], l2_ref[...], l3_ref[...]
    m = jnp.maximum(jnp.maximum(l1, l2), l3)
    e1, e2, e3 = jnp.exp(l1 - m), jnp.exp(l2 - m), jnp.exp(l3 - m)
    att = (e1 * o1_ref[...] + e2 * o2_ref[...] + e3 * o3_ref[...]) / (e1 + e2 + e3)
    mix = _dot(yr_ref[...], wor_ref[...]) + _dot(att, woa_ref[...])
    x1 = _layer_norm(alpha * x_ref[...] + mix, g_ref[...], b_ref[...])
    x1_ref[...] = x1
    hi, lo = _split(x1)
    lg_ref[...] = (jnp.dot(hi, rhi_ref[...], preferred_element_type=F32)
                   + jnp.dot(lo, rhi_ref[...], preferred_element_type=F32)
                   + jnp.dot(hi, rlo_ref[...], preferred_element_type=F32) + rb_ref[...])
    ple_ref[...] = _sigmoid(_dot(hi, pg_ref[...])) * _dot(p_ref[...], pp_ref[...])


def _stage_mix(x2, yr, att, p2, w_o, ln_g, ln_b, router_g, router_g_b, router_e, router_e_b,
               ple_gate, ple_proj, alpha):
    N, D = x2.shape
    d_r = yr.shape[1]
    rows = min(MIX_ROWS, N)
    (o1, l1), (o2, l2), (o3, l3) = att
    d_att = o1.shape[1]
    n_log = N_GROUPS + N_EXPERTS
    rw = jnp.zeros((D, ROUTER_LANES), F32).at[:, :N_GROUPS].set(router_g)
    rw = rw.at[:, N_GROUPS:n_log].set(router_e)
    rb = jnp.zeros((1, ROUTER_LANES), F32).at[0, :N_GROUPS].set(router_g_b)
    rb = rb.at[0, N_GROUPS:n_log].set(router_e_b)
    rhi, rlo = _split(rw)
    const = lambda shape: pl.BlockSpec(shape, lambda i: (0,) * len(shape))
    tile = lambda c: pl.BlockSpec((rows, c), lambda i: (i, 0))
    return pl.pallas_call(
        functools.partial(_mix_kernel, alpha=alpha),
        grid=(N // rows,),
        in_specs=[tile(D), tile(d_r)] + [tile(d_att)] * 6 + [tile(p2.shape[1]),
                  const((d_r, D)), const((d_att, D)), const((1, D)), const((1, D)),
                  const((D, ROUTER_LANES)), const((D, ROUTER_LANES)), const((1, ROUTER_LANES)),
                  const((D, D)), const(ple_proj.shape)],
        out_specs=[tile(D), tile(ROUTER_LANES), tile(D)],
        out_shape=[jax.ShapeDtypeStruct((N, D), F32),
                   jax.ShapeDtypeStruct((N, ROUTER_LANES), F32),
                   jax.ShapeDtypeStruct((N, D), F32)],
        compiler_params=_params(("parallel",)),
        name="mix",
    )(x2, yr, o1, o2, o3, l1, l2, l3, p2, w_o[:d_r].astype(BF16), w_o[d_r:].astype(BF16),
      ln_g.reshape(1, D), ln_b.reshape(1, D), rhi, rlo, rb, ple_gate.astype(BF16),
      ple_proj.astype(BF16))


def _route(logits, rows):
    N = logits.shape[0]
    lg = logits[:, :N_GROUPS]
    le = logits[:, N_GROUPS:N_GROUPS + N_EXPERTS]
    pg = jax.nn.softmax(lg, -1)
    grp = jnp.argmax(lg, -1)
    wg = jnp.take_along_axis(pg, grp[:, None], axis=-1)
    le = jnp.take_along_axis(le.reshape(N, N_GROUPS, EXPERTS_PER_GROUP), grp[:, None, None],
                             axis=1)[:, 0]
    top_v, top_i = lax.top_k(le, TOP_K)
    we = jax.nn.softmax(top_v, -1) * wg
    eid = grp[:, None] * EXPERTS_PER_GROUP + top_i
    A = N * TOP_K
    e_flat = eid.reshape(A).astype(jnp.int32)
    order = jnp.argsort(e_flat).astype(jnp.int32)
    counts = jnp.sum((e_flat[:, None] == jnp.arange(N_EXPERTS)[None, :]).astype(jnp.int32), axis=0)
    start = jnp.cumsum(counts) - counts
    padded = (counts + rows - 1) // rows * rows
    pend = jnp.cumsum(padded)
    pstart = pend - padded
    nblk = -(-A // rows) + N_EXPERTS
    nused = (pend[-1] // rows).astype(jnp.int32)
    blk = jnp.arange(nblk, dtype=jnp.int32)
    blk_e = jnp.minimum(jnp.searchsorted(pend, blk * rows, side='right'), N_EXPERTS - 1)
    blk_e = jnp.where(blk < nused, blk_e, blk_e[nused - 1]).astype(jnp.int32)
    q = jnp.arange(nblk * rows, dtype=jnp.int32)
    e_q = jnp.repeat(blk_e, rows)
    off = q - pstart[e_q]
    row_asg = order[jnp.clip(start[e_q] + off, 0, A - 1)]
    nvalid = jnp.clip(counts[blk_e] - (blk * rows - pstart[blk_e]), 0, rows)
    nvalid = jnp.where(blk < nused, nvalid, 0).astype(jnp.int32)
    return row_asg.reshape(nblk, 1, rows), blk_e, nvalid, nused.reshape(1), we


def _moe_kernel(blk_e_ref, nvalid_ref, nused_ref, first_ref, cur_ref, nxt_ref, x_hbm, wg_ref,
                wu_ref, wd_ref, y_hbm, xbuf, ybuf, gsem, ssem, *, rows):
    i = pl.program_id(0)
    nused = nused_ref[0]
    slot = i % 2

    def gather_row(tab_ref, sl, r):
        tok = lax.shift_right_logical(tab_ref[0, 0, r], TOP_K.bit_length() - 1)
        return pltpu.make_async_copy(x_hbm.at[pl.ds(tok, 1)], xbuf.at[sl, pl.ds(r, 1)],
                                     gsem.at[sl])

    def scatter_row(sl, r):
        return pltpu.make_async_copy(ybuf.at[sl, pl.ds(r, 1)],
                                     y_hbm.at[pl.ds(cur_ref[0, 0, r], 1)], ssem.at[sl])

    def for_rows(fn, n=None):
        def body(r, c):
            fn(r)
            return c
        if n is None:
            lax.fori_loop(0, rows, body, 0, unroll=8)
        else:
            lax.fori_loop(0, n, body, 0)

    def for_real_rows(blk, fn):
        nv = nvalid_ref[blk]

        @pl.when(nv == rows)
        def _():
            for_rows(fn)

        @pl.when(nv < rows)
        def _():
            for_rows(fn, nv)

    @pl.when(i == 0)
    def _():
        for_rows(lambda r: gather_row(first_ref, 0, r).start())

    @pl.when(i < nused)
    def _():
        @pl.when(i + 1 < nused)
        def _():
            for_rows(lambda r: gather_row(nxt_ref, 1 - slot, r).start())

        for_rows(lambda r: gather_row(cur_ref, slot, r).wait())

        @pl.when(i >= 2)
        def _():
            for_real_rows(i - 2, lambda r: scatter_row(slot, r).wait())

        xb = xbuf[slot].astype(BF16)
        hidden = _dot(xb, wg_ref[0])
        hidden = hidden * _sigmoid(hidden) * _dot(xb, wu_ref[0])
        ybuf[slot] = _dot(hidden, wd_ref[0])
        for_real_rows(i, lambda r: scatter_row(slot, r).start())

    @pl.when(i == pl.num_programs(0) - 1)
    def _():
        @pl.when(nused >= 2)
        def _():
            for_real_rows(nused - 2, lambda r: scatter_row(nused % 2, r).wait())
        for_real_rows(nused - 1, lambda r: scatter_row((nused - 1) % 2, r).wait())


def _stage_moe(x1, row_asg, blk_e, nvalid, nused, w_gate, w_up, w_down):
    N, D = x1.shape
    nblk, _, rows = row_asg.shape
    d_e = w_gate.shape[2]
    wspec = lambda shape: pl.BlockSpec((1,) + shape, lambda i, be, nv, nu: (be[i], 0, 0))
    table = lambda fn: pl.BlockSpec((1, 1, rows), lambda i, be, nv, nu: (fn(i), 0, 0),
                                    memory_space=pltpu.SMEM)
    grid_spec = pltpu.PrefetchScalarGridSpec(
        num_scalar_prefetch=3,
        grid=(nblk,),
        in_specs=[table(lambda i: 0), table(lambda i: i),
                  table(lambda i: jnp.minimum(i + 1, nblk - 1)),
                  pl.BlockSpec(memory_space=pl.ANY), wspec((D, d_e)), wspec((D, d_e)),
                  wspec((d_e, D))],
        out_specs=pl.BlockSpec(memory_space=pl.ANY),
        scratch_shapes=[pltpu.VMEM((2, rows, D), F32), pltpu.VMEM((2, rows, D), F32),
                        pltpu.SemaphoreType.DMA((2,)), pltpu.SemaphoreType.DMA((2,))],
    )
    return pl.pallas_call(
        functools.partial(_moe_kernel, rows=rows),
        grid_spec=grid_spec,
        out_shape=jax.ShapeDtypeStruct((N * TOP_K, D), F32),
        compiler_params=_params(("arbitrary",)),
        name="moe",
    )(blk_e, nvalid, nused, row_asg, row_asg, row_asg, x1, w_gate, w_up, w_down)


def _out_kernel(x1_ref, yb_ref, we_ref, ple_ref, g_ref, b_ref, o_ref, *, alpha):
    D = x1_ref.shape[1]
    yb = yb_ref[...]
    we = we_ref[...]
    moe = we[:, 0:1] * yb[:, :D] + we[:, 1:2] * yb[:, D:]
    o_ref[...] = _layer_norm(alpha * x1_ref[...] + moe + ple_ref[...], g_ref[...], b_ref[...])


def _stage_out(x1, yb, we, ple, ln_g, ln_b, alpha):
    N, D = x1.shape
    rows = min(OUT_ROWS, N)
    yb2 = yb.reshape(yb.shape[0] // TOP_K, TOP_K * D)
    const = lambda shape: pl.BlockSpec(shape, lambda i: (0,) * len(shape))
    tile = lambda c: pl.BlockSpec((rows, c), lambda i: (i, 0))
    return pl.pallas_call(
        functools.partial(_out_kernel, alpha=alpha),
        grid=(N // rows,),
        in_specs=[tile(D), tile(TOP_K * D), tile(TOP_K), tile(D), const((1, D)), const((1, D))],
        out_specs=tile(D),
        out_shape=jax.ShapeDtypeStruct((N, D), F32),
        compiler_params=_params(("parallel",)),
        name="out",
    )(x1, yb2, we, ple, ln_g.reshape(1, D), ln_b.reshape(1, D))


def kernel(x, p, w_in, mu_rkv, mu_lora, w0, w_lora1, w_lora2, a0, a_lora1, a_lora2, g_lora1, g_lora2, k_k, k_a, r_k, lnx_g, lnx_b, rel_bias, w_o, ln1_g, ln1_b, router_g, router_g_b, router_e, router_e_b, w_gate, w_up, w_down, ple_gate, ple_proj, ln2_g, ln2_b):
    B, S, D = x.shape
    depth = w_in.shape[0]
    alpha = (2 * depth) ** 0.25
    for i in range(depth):
        rkv, qkv, lw, ai, g = _stage_proj(x, w_in[i], w_lora1[i], a_lora1[i], g_lora1[i],
                                          mu_lora[i], w_lora2[i], a_lora2[i], g_lora2[i],
                                          w0[i], a0[i])
        yr = _stage_wkv(rkv, lw, ai, g, mu_rkv[i], k_k[i], k_a[i], r_k[i], lnx_g[i], lnx_b[i])
        att = [_stage_attn(qkv, rel_bias, window, dil) for window, dil in DIL_PATTERNS]
        x1, logits, ple = _stage_mix(x.reshape(B * S, D), yr.reshape(B * S, -1), att,
                                     p[i].reshape(B * S, -1), w_o[i], ln1_g[i], ln1_b[i],
                                     router_g[i], router_g_b[i], router_e[i], router_e_b[i],
                                     ple_gate[i], ple_proj[i], alpha)
        row_asg, blk_e, nvalid, nused, we = _route(logits, MOE_ROWS)
        yb = _stage_moe(x1, row_asg, blk_e, nvalid, nused, w_gate[i], w_up[i], w_down[i])
        x = _stage_out(x1, yb, we, ple, ln2_g[i], ln2_b[i], alpha).reshape(B, S, D)
    return x
```

```python
import functools
import math

import jax
import jax.numpy as jnp
import numpy as np
from jax import lax
from jax.experimental import pallas as pl
from jax.experimental.pallas import tpu as pltpu

F32 = jnp.float32
BF16 = jnp.bfloat16

HEAD_DIM = 64
LANES = 128
DECAY_LORA = 64
ICLR_LORA = 64
GATE_LORA = 128
LNX_EPS = 64e-5
DIL_PATTERNS = ((128, 1), (512, 4), (2048, 16))
NEG_INF = -1e30
N_BUCKETS = 32
MAX_DISTANCE = 2048
N_GROUPS = 4
EXPERTS_PER_GROUP = 8
N_EXPERTS = N_GROUPS * EXPERTS_PER_GROUP
TOP_K = 2
LN_EPS = 1e-5

WKV_CHUNK = 64
WKV_CHUNKS_PER_STEP = 2
ATT_BLK = 128
PROJ_ROWS = 512
MIX_ROWS = 256
MOE_ROWS = 256
OUT_ROWS = 256
ROUTER_LANES = 128
VMEM_LIMIT = 56 * 1024 * 1024


def _params(sem):
    return pltpu.CompilerParams(dimension_semantics=sem, vmem_limit_bytes=VMEM_LIMIT)


def _dot(a, b):
    return jnp.dot(a.astype(BF16), b.astype(BF16), preferred_element_type=F32)


def _dot_nt(a, b):
    return lax.dot_general(a.astype(BF16), b.astype(BF16), (((1,), (1,)), ((), ())),
                           preferred_element_type=F32)


def _dot_tn(a, b):
    return lax.dot_general(a.astype(BF16), b.astype(BF16), (((0,), (0,)), ((), ())),
                           preferred_element_type=F32)


def _split(x):
    hi = x.astype(BF16)
    lo = (x - hi.astype(F32)).astype(BF16)
    return hi, lo


def _dot_exact_rhs(x, w01):
    hi, lo = _split(x)
    return (jnp.dot(hi, w01, preferred_element_type=F32)
            + jnp.dot(lo, w01, preferred_element_type=F32))


def _dot_exact_lhs(w01, x):
    hi, lo = _split(x)
    return (jnp.dot(w01, hi, preferred_element_type=F32)
            + jnp.dot(w01, lo, preferred_element_type=F32))


def _sigmoid(x):
    return 1.0 / (1.0 + jnp.exp(-x))


def _shift_rows(t, prev_row):
    rolled = pltpu.roll(t, 1, 0)
    row = lax.broadcasted_iota(jnp.int32, t.shape, 0)
    return jnp.where(row == 0, prev_row, rolled)


def _layer_norm(z, g, b):
    mu = jnp.mean(z, axis=-1, keepdims=True)
    zc = z - mu
    var = jnp.mean(zc * zc, axis=-1, keepdims=True)
    return zc * lax.rsqrt(var + LN_EPS) * g + b


def _proj_kernel(x_ref, win_ref, w1_ref, a1_ref, g1_ref, mul_ref, w2_ref, a2_ref, g2_ref,
                 w0_ref, a0_ref, rkv_ref, qkv_ref, lw_ref, ai_ref, g_ref, prev_ref, *, d_rkv):
    @pl.when(pl.program_id(1) == 0)
    def _():
        prev_ref[...] = jnp.zeros_like(prev_ref)

    h = x_ref[0]
    rows = h.shape[0]
    hprev = _shift_rows(h, prev_ref[0:1, :])
    prev_ref[0:1, :] = h[rows - 1:rows, :]

    proj = _dot(h, win_ref[...])
    rkv_ref[0] = proj[:, :d_rkv]
    qkv_ref[0] = proj[:, d_rkv:].astype(BF16)

    dh = hprev - h
    xw = h + dh * mul_ref[0:1, :]
    xa = h + dh * mul_ref[1:2, :]
    xg = h + dh * mul_ref[2:3, :]
    wl = w0_ref[...] + _dot(jnp.tanh(_dot(xw, w1_ref[...])), w2_ref[...])
    z = -wl
    softplus = jnp.maximum(z, 0.0) + jnp.log(1.0 + jnp.exp(-jnp.abs(z)))
    lw_ref[0] = -jnp.exp(-softplus - 0.5)
    ai_ref[0] = _sigmoid(a0_ref[...] + _dot(_dot(xa, a1_ref[...]), a2_ref[...]))
    g_ref[0] = _dot(_sigmoid(_dot(xg, g1_ref[...])), g2_ref[...])


def _stage_proj(x, w_in, w1, a1, g1, mu_lora, w2, a2, g2, w0, a0):
    B, S, D = x.shape
    d_in = w_in.shape[1]
    d_r = w2.shape[1]
    d_rkv = 3 * d_r
    d_att = d_in - d_rkv
    rows = min(PROJ_ROWS, S)
    const = lambda shape: pl.BlockSpec(shape, lambda b, j: (0,) * len(shape))
    tile = lambda c: pl.BlockSpec((1, rows, c), lambda b, j: (b, j, 0))
    return pl.pallas_call(
        functools.partial(_proj_kernel, d_rkv=d_rkv),
        grid=(B, S // rows),
        in_specs=[tile(D), const((D, d_in)), const(w1.shape), const(a1.shape), const(g1.shape),
                  const(mu_lora.shape), const(w2.shape), const(a2.shape), const(g2.shape),
                  const((1, d_r)), const((1, d_r))],
        out_specs=[tile(d_rkv), tile(d_att), tile(d_r), tile(d_r), tile(d_r)],
        out_shape=[jax.ShapeDtypeStruct((B, S, d_rkv), F32),
                   jax.ShapeDtypeStruct((B, S, d_att), BF16),
                   jax.ShapeDtypeStruct((B, S, d_r), F32),
                   jax.ShapeDtypeStruct((B, S, d_r), F32),
                   jax.ShapeDtypeStruct((B, S, d_r), F32)],
        scratch_shapes=[pltpu.VMEM((8, D), F32)],
        compiler_params=_params(("parallel", "arbitrary")),
        name="proj",
    )(x, w_in.astype(BF16), w1.astype(BF16), a1.astype(BF16), g1.astype(BF16), mu_lora,
      w2.astype(BF16), a2.astype(BF16), g2.astype(BF16), w0.reshape(1, d_r), a0.reshape(1, d_r))


def _wkv_kernel(rkv_ref, lw_ref, ai_ref, g_ref, mu_ref, kk_ref, ka_ref, rk_ref, lng_ref, lnb_ref,
                seg_ref, tril_ref, y_ref, s_ref, prev_ref, *, d_r, n_chunks):
    C = WKV_CHUNK
    rows = C * n_chunks

    @pl.when(pl.program_id(1) == 0)
    def _():
        s_ref[...] = jnp.zeros_like(s_ref)
        prev_ref[...] = jnp.zeros_like(prev_ref)

    rkv = rkv_ref[0]
    prev = _shift_rows(rkv, prev_ref[0:1, :])
    prev_ref[0:1, :] = rkv[rows - 1:rows, :]
    mixed = rkv + (prev - rkv) * mu_ref[...]
    r = mixed[:, :d_r]
    k = mixed[:, d_r:2 * d_r]
    v = mixed[:, 2 * d_r:]
    lw = lw_ref[0]
    a = ai_ref[0]
    seg = seg_ref[...]

    kk = k * kk_ref[...]
    kk = kk * lax.rsqrt(jnp.maximum(_dot_exact_rhs(kk * kk, seg), 1e-24))
    k = k * (1.0 + (a - 1.0) * ka_ref[...])
    aa = -kk
    bb = kk * a
    c = _dot_exact_lhs(tril_ref[...], lw)
    e_neg = jnp.exp(-c)
    a_t = aa * jnp.exp(c - lw)
    b_t = bb * e_neg
    k_t = k * e_neg
    r_t = r * jnp.exp(c)

    lane = lax.broadcasted_iota(jnp.int32, (1, LANES), 1)
    head0 = lane < HEAD_DIM
    ri = lax.broadcasted_iota(jnp.int32, (2 * C, 2 * C), 0)
    ci = lax.broadcasted_iota(jnp.int32, (2 * C, 2 * C), 1)
    strict = (ri % C) > (ci % C)
    incl = (ri % C) >= (ci % C)
    eye = (ri == ci).astype(F32)
    zero_bf = jnp.zeros((2 * C, LANES), BF16)

    def stack(t):
        return jnp.concatenate([jnp.where(head0, t, 0.0), jnp.where(head0, 0.0, t)],
                               axis=0).astype(BF16)

    def fold(t):
        return t[:C] + t[C:]

    n_pairs = d_r // LANES
    folds = [(q, p) for q in range(n_chunks) for p in range(n_pairs)]
    each = lambda fn, *cols: [fn(*args) for args in zip(*cols)]
    cat0 = lambda *ts: jnp.concatenate(ts, axis=0)
    cat1 = lambda *ts: jnp.concatenate(ts, axis=1)
    left = lambda t: t[:, :2 * C]
    right = lambda t: t[:, 2 * C:]

    p_ends, b_es, k_es = [], [], []
    for q in range(n_chunks):
        rs = slice(q * C, (q + 1) * C)
        c_end = c[rs][C - 1:C, :]
        e_end = jnp.exp(c_end - c[rs])
        b_es.append(bb[rs] * e_end)
        k_es.append(k[rs] * e_end)
        p_ends.append(jnp.exp(c_end))

    def stacks(t_of_q):
        return [stack(t_of_q(q)[:, p * LANES:(p + 1) * LANES]) for q, p in folds]

    chunk = lambda t: (lambda q: t[q * C:(q + 1) * C])
    am, bm, km, rm, vm = (stacks(chunk(t)) for t in (a_t, b_t, k_t, r_t, v))
    bem = stacks(lambda q: b_es[q])
    kem = stacks(lambda q: k_es[q])

    x = each(lambda a_, r_, b_, k_: _dot_nt(cat0(a_, r_), cat0(b_, k_)), am, rm, bm, km)
    n = each(lambda t: jnp.where(strict, t[:2 * C, :2 * C], 0.0), x)
    a_ak = each(lambda t: jnp.where(strict, t[:2 * C, 2 * C:], 0.0).astype(BF16), x)
    a_rb = each(lambda t: jnp.where(incl, t[2 * C:, :2 * C], 0.0).astype(BF16), x)
    a_rk = each(lambda t: jnp.where(incl, t[2 * C:, 2 * C:], 0.0).astype(BF16), x)
    t_inv = each(lambda t: eye + t, n)
    m = each(lambda t: _dot(t, t), n)
    av = each(lambda a_, v_: _dot(a_, v_).astype(BF16), a_ak, vm)
    for _ in range(int(math.log2(C)) - 2):
        z = each(lambda m_, t_: _dot(m_, cat1(m_.astype(BF16), t_.astype(BF16))), m, t_inv)
        t_inv = each(lambda t_, z_: t_ + right(z_), t_inv, z)
        m = each(left, z)
    t_inv = each(lambda t_, m_: t_ + _dot(m_, t_), t_inv, m)
    z = each(lambda t_, a_, av_: _dot(t_, cat1(a_, av_)), t_inv, am, av)
    ap = each(lambda z_: left(z_).astype(BF16), z)
    u0 = each(lambda z_: right(z_).astype(BF16), z)
    z = each(lambda rb_, rk_, ap_, u_, v_: _dot(cat1(rb_, rk_),
                                               cat0(cat1(ap_, u_), cat1(zero_bf, v_))),
             a_rb, a_rk, ap, u0, vm)
    g_m = each(lambda ap_, be_: _dot_tn(ap_, be_).astype(BF16), ap, bem)
    d0 = each(lambda u_, v_, be_, ke_: _dot_tn(cat0(u_, v_), cat0(be_, ke_)), u0, vm, bem, kem)
    rp = each(lambda r_, z_: (r_.astype(F32) + left(z_)).astype(BF16), rm, z)
    y0 = each(lambda z_: fold(right(z_)), z)

    s = [s_ref[p] for p in range(n_pairs)]
    y_cols = []
    for q in range(n_chunks):
        s_bf = [t.astype(BF16) for t in s]
        f0 = q * n_pairs
        y_cols.append([fold(_dot_nt(rp[f0 + p], s_bf[p])) + y0[f0 + p] for p in range(n_pairs)])
        s = [s[p] * p_ends[q][:, p * LANES:(p + 1) * LANES] + _dot(s_bf[p], g_m[f0 + p])
             + d0[f0 + p] for p in range(n_pairs)]
    for p in range(n_pairs):
        s_ref[p] = s[p]
    y = cat0(*[cat1(*cols) for cols in y_cols])

    inv = 1.0 / HEAD_DIM
    mu = _dot_exact_rhs(y, seg) * inv
    yc = y - mu
    var = _dot_exact_rhs(yc * yc, seg) * inv
    yn = yc * lax.rsqrt(var + LNX_EPS) * lng_ref[...] + lnb_ref[...]
    bonus = _dot_exact_rhs(r * k * rk_ref[...], seg) * v
    y_ref[0] = ((yn + bonus) * g_ref[0]).astype(y_ref.dtype)


def _stage_wkv(rkv, lw, ai, g, mu_rkv, k_k, k_a, r_k, lnx_g, lnx_b):
    B, S, d_r = lw.shape
    n_chunks = WKV_CHUNKS_PER_STEP
    rows = WKV_CHUNK * n_chunks
    head = np.arange(d_r) // HEAD_DIM
    seg = jnp.asarray(head[:, None] == head[None, :], BF16)
    t = np.arange(rows)
    tril = jnp.asarray((t[:, None] >= t[None, :])
                       & (t[:, None] // WKV_CHUNK == t[None, :] // WKV_CHUNK), BF16)
    const = lambda shape: pl.BlockSpec(shape, lambda b, j: (0,) * len(shape))
    tile = lambda c: pl.BlockSpec((1, rows, c), lambda b, j: (b, j, 0))
    row = lambda t: t.reshape(1, -1)
    return pl.pallas_call(
        functools.partial(_wkv_kernel, d_r=d_r, n_chunks=n_chunks),
        grid=(B, S // rows),
        in_specs=[tile(3 * d_r), tile(d_r), tile(d_r), tile(d_r), const((1, 3 * d_r)),
                  const((1, d_r)), const((1, d_r)), const((1, d_r)), const((1, d_r)),
                  const((1, d_r)), const((d_r, d_r)), const((rows, rows))],
        out_specs=tile(d_r),
        out_shape=jax.ShapeDtypeStruct((B, S, d_r), BF16),
        scratch_shapes=[pltpu.VMEM((d_r // LANES, LANES, LANES), F32),
                        pltpu.VMEM((8, 3 * d_r), F32)],
        compiler_params=_params(("parallel", "arbitrary")),
        name="wkv",
    )(rkv, lw, ai, g, row(mu_rkv), row(k_k), row(k_a), row(r_k), row(lnx_g), row(lnx_b), seg, tril)


def _t5_bucket(n):
    exact = N_BUCKETS // 2
    nf = np.maximum(n, 1).astype(np.float32)
    large = exact + (np.log(nf / exact) / math.log(MAX_DISTANCE / exact)
                     * (N_BUCKETS - exact)).astype(np.int32)
    large = np.minimum(large, N_BUCKETS - 1)
    return np.where(n < exact, n, large).astype(np.int32)


def _band_bias(rel_bias, dil):
    blk = ATT_BLK
    rel = np.arange(blk)[:, None] + blk - np.arange(2 * blk)[None, :]
    valid = (rel >= 0) & (rel <= blk)
    bias = jnp.transpose(rel_bias[_t5_bucket(np.clip(rel, 0, None) * dil)], (2, 0, 1)).astype(F32)
    return jnp.where(valid[None], bias, NEG_INF)


def _attn_block(q, kcat, vcat, bias_ref, col0, scale):
    nk = kcat.shape[0]
    lane = lax.broadcasted_iota(jnp.int32, (1, LANES), 1)
    head0 = lane < HEAD_DIM
    n_pairs = q.shape[1] // LANES
    heads = [(p, h) for p in range(n_pairs) for h in range(2)]
    pair = lambda t, p: t[:, p * LANES:(p + 1) * LANES]
    s = []
    for p, h in heads:
        qp = pair(q, p)
        qh = jnp.where(head0 if h == 0 else jnp.logical_not(head0), qp, jnp.zeros_like(qp))
        s.append(_dot_nt(qh, pair(kcat, p)) * scale + bias_ref[2 * p + h, :, col0:col0 + nk])
    m = [jnp.max(t, axis=-1, keepdims=True) for t in s]
    e = [jnp.exp(t - m_) for t, m_ in zip(s, m)]
    l = [jnp.sum(t, axis=-1, keepdims=True) for t in e]
    o = [_dot(t, pair(vcat, p)) / l_ for t, l_, (p, h) in zip(e, l, heads)]
    lse = [m_ + jnp.log(l_) for m_, l_ in zip(m, l)]
    outs = [jnp.where(head0, o[2 * p], o[2 * p + 1]) for p in range(n_pairs)]
    lses = [jnp.where(head0, lse[2 * p], lse[2 * p + 1]) for p in range(n_pairs)]
    return jnp.concatenate(outs, axis=1), jnp.concatenate(lses, axis=1)


def _attn_kernel(q_ref, k_ref, v_ref, bias_ref, o_ref, l_ref, *, nb, scale):
    blk = ATT_BLK
    o, l = _attn_block(q_ref[0, 0:blk, :], k_ref[0, 0:blk, :], v_ref[0, 0:blk, :], bias_ref, blk,
                       scale)
    o_ref[0, 0:blk, :] = o
    l_ref[0, 0:blk, :] = l

    def body(n, carry):
        q0 = pl.multiple_of(n * blk, blk)
        k0 = pl.multiple_of((n - 1) * blk, blk)
        o, l = _attn_block(q_ref[0, pl.ds(q0, blk), :], k_ref[0, pl.ds(k0, 2 * blk), :],
                           v_ref[0, pl.ds(k0, 2 * blk), :], bias_ref, 0, scale)
        o_ref[0, pl.ds(q0, blk), :] = o
        l_ref[0, pl.ds(q0, blk), :] = l
        return carry

    if nb > 1:
        lax.fori_loop(1, nb, body, 0)


def _stage_attn(qkv, rel_bias, window, dil):
    B, S, d3 = qkv.shape
    d_att = d3 // 3
    L = S // dil
    nb = L // ATT_BLK
    assert window // dil == ATT_BLK and nb * ATT_BLK == L
    bias = _band_bias(rel_bias, dil)
    qkv_v = qkv.reshape(B, L, dil * d3)
    part = lambda i: pl.BlockSpec((1, L, d_att), lambda b, r: (b, 0, 3 * r + i))
    out = pl.BlockSpec((1, L, d_att), lambda b, r: (b, 0, r))
    o, l = pl.pallas_call(
        functools.partial(_attn_kernel, nb=nb, scale=HEAD_DIM ** -0.5),
        grid=(B, dil),
        in_specs=[part(0), part(1), part(2),
                  pl.BlockSpec(bias.shape, lambda b, r: (0, 0, 0))],
        out_specs=[out, out],
        out_shape=[jax.ShapeDtypeStruct((B, L, dil * d_att), F32)] * 2,
        compiler_params=_params(("parallel", "parallel")),
        name=f"attn_d{dil}",
    )(qkv_v, qkv_v, qkv_v, bias)
    return o.reshape(B * S, d_att), l.reshape(B * S, d_att)


def _mix_kernel(x_ref, yr_ref, o1_ref, o2_ref, o3_ref, l1_ref, l2_ref, l3_ref, p_ref, wor_ref,
                woa_ref, g_ref, b_ref, rhi_ref, rlo_ref, rb_ref, pg_ref, pp_ref,
                x1_ref, lg_ref, ple_ref, *, alpha):
    l1, l2, l3 = l1_ref[...], l2_ref[...], l3_ref[...]
    m = jnp.maximum(jnp.maximum(l1, l2), l3)
    e1, e2, e3 = jnp.exp(l1 - m), jnp.exp(l2 - m), jnp.exp(l3 - m)
    att = (e1 * o1_ref[...] + e2 * o2_ref[...] + e3 * o3_ref[...]) / (e1 + e2 + e3)
    mix = _dot(yr_ref[...], wor_ref[...]) + _dot(att, woa_ref[...])
    x1 = _layer_norm(alpha * x_ref[...] + mix, g_ref[...], b_ref[...])
    x1_ref[...] = x1
    hi, lo = _split(x1)
    lg_ref[...] = (jnp.dot(hi, rhi_ref[...], preferred_element_type=F32)
                   + jnp.dot(lo, rhi_ref[...], preferred_element_type=F32)
                   + jnp.dot(hi, rlo_ref[...], preferred_element_type=F32) + rb_ref[...])
    ple_ref[...] = _sigmoid(_dot(hi, pg_ref[...])) * _dot(p_ref[...], pp_ref[...])


def _stage_mix(x2, yr, att, p2, w_o, ln_g, ln_b, router_g, router_g_b, router_e, router_e_b,
               ple_gate, ple_proj, alpha):
    N, D = x2.shape
    d_r = yr.shape[1]
    rows = min(MIX_ROWS, N)
    (o1, l1), (o2, l2), (o3, l3) = att
    d_att = o1.shape[1]
    n_log = N_GROUPS + N_EXPERTS
    rw = jnp.zeros((D, ROUTER_LANES), F32).at[:, :N_GROUPS].set(router_g)
    rw = rw.at[:, N_GROUPS:n_log].set(router_e)
    rb = jnp.zeros((1, ROUTER_LANES), F32).at[0, :N_GROUPS].set(router_g_b)
    rb = rb.at[0, N_GROUPS:n_log].set(router_e_b)
    rhi, rlo = _split(rw)
    const = lambda shape: pl.BlockSpec(shape, lambda i: (0,) * len(shape))
    tile = lambda c: pl.BlockSpec((rows, c), lambda i: (i, 0))
    return pl.pallas_call(
        functools.partial(_mix_kernel, alpha=alpha),
        grid=(N // rows,),
        in_specs=[tile(D), tile(d_r)] + [tile(d_att)] * 6 + [tile(p2.shape[1]),
                  const((d_r, D)), const((d_att, D)), const((1, D)), const((1, D)),
                  const((D, ROUTER_LANES)), const((D, ROUTER_LANES)), const((1, ROUTER_LANES)),
                  const((D, D)), const(ple_proj.shape)],
        out_specs=[tile(D), tile(ROUTER_LANES), tile(D)],
        out_shape=[jax.ShapeDtypeStruct((N, D), F32),
                   jax.ShapeDtypeStruct((N, ROUTER_LANES), F32),
                   jax.ShapeDtypeStruct((N, D), F32)],
        compiler_params=_params(("parallel",)),
        name="mix",
    )(x2, yr, o1, o2, o3, l1, l2, l3, p2, w_o[:d_r].astype(BF16), w_o[d_r:].astype(BF16),
      ln_g.reshape(1, D), ln_b.reshape(1, D), rhi, rlo, rb, ple_gate.astype(BF16),
      ple_proj.astype(BF16))


def _route(logits, rows):
    N = logits.shape[0]
    lg = logits[:, :N_GROUPS]
    le = logits[:, N_GROUPS:N_GROUPS + N_EXPERTS]
    pg = jax.nn.softmax(lg, -1)
    grp = jnp.argmax(lg, -1)
    wg = jnp.take_along_axis(pg, grp[:, None], axis=-1)
    le = jnp.take_along_axis(le.reshape(N, N_GROUPS, EXPERTS_PER_GROUP), grp[:, None, None],
                             axis=1)[:, 0]
    top_v, top_i = lax.top_k(le, TOP_K)
    we = jax.nn.softmax(top_v, -1) * wg
    eid = grp[:, None] * EXPERTS_PER_GROUP + top_i
    A = N * TOP_K
    e_flat = eid.reshape(A).astype(jnp.int32)
    order = jnp.argsort(e_flat).astype(jnp.int32)
    counts = jnp.sum((e_flat[:, None] == jnp.arange(N_EXPERTS)[None, :]).astype(jnp.int32), axis=0)
    start = jnp.cumsum(counts) - counts
    padded = (counts + rows - 1) // rows * rows
    pend = jnp.cumsum(padded)
    pstart = pend - padded
    nblk = -(-A // rows) + N_EXPERTS
    nused = (pend[-1] // rows).astype(jnp.int32)
    blk = jnp.arange(nblk, dtype=jnp.int32)
    blk_e = jnp.minimum(jnp.searchsorted(pend, blk * rows, side='right'), N_EXPERTS - 1)
    blk_e = jnp.where(blk < nused, blk_e, blk_e[nused - 1]).astype(jnp.int32)
    q = jnp.arange(nblk * rows, dtype=jnp.int32)
    e_q = jnp.repeat(blk_e, rows)
    off = q - pstart[e_q]
    row_asg = order[jnp.clip(start[e_q] + off, 0, A - 1)]
    nvalid = jnp.clip(counts[blk_e] - (blk * rows - pstart[blk_e]), 0, rows)
    nvalid = jnp.where(blk < nused, nvalid, 0).astype(jnp.int32)
    return row_asg.reshape(nblk, 1, rows), blk_e, nvalid, nused.reshape(1), we


def _moe_kernel(blk_e_ref, nvalid_ref, nused_ref, first_ref, cur_ref, nxt_ref, x_hbm, wg_ref,
                wu_ref, wd_ref, y_hbm, xbuf, ybuf, gsem, ssem, *, rows):
    i = pl.program_id(0)
    nused = nused_ref[0]
    slot = i % 2

    def gather_row(tab_ref, sl, r):
        tok = lax.shift_right_logical(tab_ref[0, 0, r], TOP_K.bit_length() - 1)
        return pltpu.make_async_copy(x_hbm.at[pl.ds(tok, 1)], xbuf.at[sl, pl.ds(r, 1)],
                                     gsem.at[sl])

    def scatter_row(sl, r):
        return pltpu.make_async_copy(ybuf.at[sl, pl.ds(r, 1)],
                                     y_hbm.at[pl.ds(cur_ref[0, 0, r], 1)], ssem.at[sl])

    def for_rows(fn, n=None):
        def body(r, c):
            fn(r)
            return c
        if n is None:
            lax.fori_loop(0, rows, body, 0, unroll=8)
        else:
            lax.fori_loop(0, n, body, 0)

    def for_real_rows(blk, fn):
        nv = nvalid_ref[blk]

        @pl.when(nv == rows)
        def _():
            for_rows(fn)

        @pl.when(nv < rows)
        def _():
            for_rows(fn, nv)

    @pl.when(i == 0)
    def _():
        for_rows(lambda r: gather_row(first_ref, 0, r).start())

    @pl.when(i < nused)
    def _():
        @pl.when(i + 1 < nused)
        def _():
            for_rows(lambda r: gather_row(nxt_ref, 1 - slot, r).start())

        for_rows(lambda r: gather_row(cur_ref, slot, r).wait())

        @pl.when(i >= 2)
        def _():
            for_real_rows(i - 2, lambda r: scatter_row(slot, r).wait())

        xb = xbuf[slot].astype(BF16)
        hidden = _dot(xb, wg_ref[0])
        hidden = hidden * _sigmoid(hidden) * _dot(xb, wu_ref[0])
        ybuf[slot] = _dot(hidden, wd_ref[0])
        for_real_rows(i, lambda r: scatter_row(slot, r).start())

    @pl.when(i == pl.num_programs(0) - 1)
    def _():
        @pl.when(nused >= 2)
        def _():
            for_real_rows(nused - 2, lambda r: scatter_row(nused % 2, r).wait())
        for_real_rows(nused - 1, lambda r: scatter_row((nused - 1) % 2, r).wait())


def _stage_moe(x1, row_asg, blk_e, nvalid, nused, w_gate, w_up, w_down):
    N, D = x1.shape
    nblk, _, rows = row_asg.shape
    d_e = w_gate.shape[2]
    wspec = lambda shape: pl.BlockSpec((1,) + shape, lambda i, be, nv, nu: (be[i], 0, 0))
    table = lambda fn: pl.BlockSpec((1, 1, rows), lambda i, be, nv, nu: (fn(i), 0, 0),
                                    memory_space=pltpu.SMEM)
    grid_spec = pltpu.PrefetchScalarGridSpec(
        num_scalar_prefetch=3,
        grid=(nblk,),
        in_specs=[table(lambda i: 0), table(lambda i: i),
                  table(lambda i: jnp.minimum(i + 1, nblk - 1)),
                  pl.BlockSpec(memory_space=pl.ANY), wspec((D, d_e)), wspec((D, d_e)),
                  wspec((d_e, D))],
        out_specs=pl.BlockSpec(memory_space=pl.ANY),
        scratch_shapes=[pltpu.VMEM((2, rows, D), F32), pltpu.VMEM((2, rows, D), F32),
                        pltpu.SemaphoreType.DMA((2,)), pltpu.SemaphoreType.DMA((2,))],
    )
    return pl.pallas_call(
        functools.partial(_moe_kernel, rows=rows),
        grid_spec=grid_spec,
        out_shape=jax.ShapeDtypeStruct((N * TOP_K, D), F32),
        compiler_params=_params(("arbitrary",)),
        name="moe",
    )(blk_e, nvalid, nused, row_asg, row_asg, row_asg, x1, w_gate, w_up, w_down)


def _out_kernel(x1_ref, yb_ref, we_ref, ple_ref, g_ref, b_ref, o_ref, *, alpha):
    D = x1_ref.shape[1]
    yb = yb_ref[...]
    we = we_ref[...]
    moe = we[:, 0:1] * yb[:, :D] + we[:, 1:2] * yb[:, D:]
    o_ref[...] = _layer_norm(alpha * x1_ref[...] + moe + ple_ref[...], g_ref[...], b_ref[...])


def _stage_out(x1, yb, we, ple, ln_g, ln_b, alpha):
    N, D = x1.shape
    rows = min(OUT_ROWS, N)
    yb2 = yb.reshape(yb.shape[0] // TOP_K, TOP_K * D)
    const = lambda shape: pl.BlockSpec(shape, lambda i: (0,) * len(shape))
    tile = lambda c: pl.BlockSpec((rows, c), lambda i: (i, 0))
    return pl.pallas_call(
        functools.partial(_out_kernel, alpha=alpha),
        grid=(N // rows,),
        in_specs=[tile(D), tile(TOP_K * D), tile(TOP_K), tile(D), const((1, D)), const((1, D))],
        out_specs=tile(D),
        out_shape=jax.ShapeDtypeStruct((N, D), F32),
        compiler_params=_params(("parallel",)),
        name="out",
    )(x1, yb2, we, ple, ln_g.reshape(1, D), ln_b.reshape(1, D))


def kernel(x, p, w_in, mu_rkv, mu_lora, w0, w_lora1, w_lora2, a0, a_lora1, a_lora2, g_lora1, g_lora2, k_k, k_a, r_k, lnx_g, lnx_b, rel_bias, w_o, ln1_g, ln1_b, router_g, router_g_b, router_e, router_e_b, w_gate, w_up, w_down, ple_gate, ple_proj, ln2_g, ln2_b):
    B, S, D = x.shape
    depth = w_in.shape[0]
    alpha = (2 * depth) ** 0.25
    for i in range(depth):
        rkv, qkv, lw, ai, g = _stage_proj(x, w_in[i], w_lora1[i], a_lora1[i], g_lora1[i],
                                          mu_lora[i], w_lora2[i], a_lora2[i], g_lora2[i],
                                          w0[i], a0[i])
        yr = _stage_wkv(rkv, lw, ai, g, mu_rkv[i], k_k[i], k_a[i], r_k[i], lnx_g[i], lnx_b[i])
        att = [_stage_attn(qkv, rel_bias, window, dil) for window, dil in DIL_PATTERNS]
        x1, logits, ple = _stage_mix(x.reshape(B * S, D), yr.reshape(B * S, -1), att,
                                     p[i].reshape(B * S, -1), w_o[i], ln1_g[i], ln1_b[i],
                                     router_g[i], router_g_b[i], router_e[i], router_e_b[i],
                                     ple_gate[i], ple_proj[i], alpha)
        row_asg, blk_e, nvalid, nused, we = _route(logits, MOE_ROWS)
        yb = _stage_moe(x1, row_asg, blk_e, nvalid, nused, w_gate[i], w_up[i], w_down[i])
        x = _stage_out(x1, yb, we, ple, ln2_g[i], ln2_b[i], alpha).reshape(B, S, D)
    return x
```

```python
import functools
import math

import jax
import jax.numpy as jnp
import numpy as np
from jax import lax
from jax.experimental import pallas as pl
from jax.experimental.pallas import tpu as pltpu

F32 = jnp.float32
BF16 = jnp.bfloat16

HEAD_DIM = 64
LANES = 128
DECAY_LORA = 64
ICLR_LORA = 64
GATE_LORA = 128
LNX_EPS = 64e-5
DIL_PATTERNS = ((128, 1), (512, 4), (2048, 16))
NEG_INF = -1e30
N_BUCKETS = 32
MAX_DISTANCE = 2048
N_GROUPS = 4
EXPERTS_PER_GROUP = 8
N_EXPERTS = N_GROUPS * EXPERTS_PER_GROUP
TOP_K = 2
LN_EPS = 1e-5

WKV_CHUNK = 64
WKV_CHUNKS_PER_STEP = 2
ATT_BLK = 128
PROJ_ROWS = 512
MIX_ROWS = 256
MOE_ROWS = 256
OUT_ROWS = 256
ROUTER_LANES = 128
VMEM_LIMIT = 56 * 1024 * 1024


def _params(sem):
    return pltpu.CompilerParams(dimension_semantics=sem, vmem_limit_bytes=VMEM_LIMIT)


def _dot(a, b):
    return jnp.dot(a.astype(BF16), b.astype(BF16), preferred_element_type=F32)


def _dot_nt(a, b):
    return lax.dot_general(a.astype(BF16), b.astype(BF16), (((1,), (1,)), ((), ())),
                           preferred_element_type=F32)


def _dot_tn(a, b):
    return lax.dot_general(a.astype(BF16), b.astype(BF16), (((0,), (0,)), ((), ())),
                           preferred_element_type=F32)


def _split(x):
    hi = x.astype(BF16)
    lo = (x - hi.astype(F32)).astype(BF16)
    return hi, lo


def _dot_exact_rhs(x, w01):
    hi, lo = _split(x)
    return (jnp.dot(hi, w01, preferred_element_type=F32)
            + jnp.dot(lo, w01, preferred_element_type=F32))


def _dot_exact_lhs(w01, x):
    hi, lo = _split(x)
    return (jnp.dot(w01, hi, preferred_element_type=F32)
            + jnp.dot(w01, lo, preferred_element_type=F32))


def _sigmoid(x):
    return 1.0 / (1.0 + jnp.exp(-x))


def _shift_rows(t, prev_row):
    rolled = pltpu.roll(t, 1, 0)
    row = lax.broadcasted_iota(jnp.int32, t.shape, 0)
    return jnp.where(row == 0, prev_row, rolled)


def _layer_norm(z, g, b):
    mu = jnp.mean(z, axis=-1, keepdims=True)
    zc = z - mu
    var = jnp.mean(zc * zc, axis=-1, keepdims=True)
    return zc * lax.rsqrt(var + LN_EPS) * g + b


def _proj_kernel(x_ref, win_ref, w1_ref, a1_ref, g1_ref, mul_ref, w2_ref, a2_ref, g2_ref,
                 w0_ref, a0_ref, rkv_ref, qkv_ref, lw_ref, ai_ref, g_ref, prev_ref, *, d_rkv):
    @pl.when(pl.program_id(1) == 0)
    def _():
        prev_ref[...] = jnp.zeros_like(prev_ref)

    h = x_ref[0]
    rows = h.shape[0]
    hprev = _shift_rows(h, prev_ref[0:1, :])
    prev_ref[0:1, :] = h[rows - 1:rows, :]

    proj = _dot(h, win_ref[...])
    rkv_ref[0] = proj[:, :d_rkv]
    qkv_ref[0] = proj[:, d_rkv:].astype(BF16)

    dh = hprev - h
    xw = h + dh * mul_ref[0:1, :]
    xa = h + dh * mul_ref[1:2, :]
    xg = h + dh * mul_ref[2:3, :]
    wl = w0_ref[...] + _dot(jnp.tanh(_dot(xw, w1_ref[...])), w2_ref[...])
    z = -wl
    softplus = jnp.maximum(z, 0.0) + jnp.log(1.0 + jnp.exp(-jnp.abs(z)))
    lw_ref[0] = -jnp.exp(-softplus - 0.5)
    ai_ref[0] = _sigmoid(a0_ref[...] + _dot(_dot(xa, a1_ref[...]), a2_ref[...]))
    g_ref[0] = _dot(_sigmoid(_dot(xg, g1_ref[...])), g2_ref[...])


def _stage_proj(x, w_in, w1, a1, g1, mu_lora, w2, a2, g2, w0, a0):
    B, S, D = x.shape
    d_in = w_in.shape[1]
    d_r = w2.shape[1]
    d_rkv = 3 * d_r
    d_att = d_in - d_rkv
    rows = min(PROJ_ROWS, S)
    const = lambda shape: pl.BlockSpec(shape, lambda b, j: (0,) * len(shape))
    tile = lambda c: pl.BlockSpec((1, rows, c), lambda b, j: (b, j, 0))
    return pl.pallas_call(
        functools.partial(_proj_kernel, d_rkv=d_rkv),
        grid=(B, S // rows),
        in_specs=[tile(D), const((D, d_in)), const(w1.shape), const(a1.shape), const(g1.shape),
                  const(mu_lora.shape), const(w2.shape), const(a2.shape), const(g2.shape),
                  const((1, d_r)), const((1, d_r))],
        out_specs=[tile(d_rkv), tile(d_att), tile(d_r), tile(d_r), tile(d_r)],
        out_shape=[jax.ShapeDtypeStruct((B, S, d_rkv), F32),
                   jax.ShapeDtypeStruct((B, S, d_att), BF16),
                   jax.ShapeDtypeStruct((B, S, d_r), F32),
                   jax.ShapeDtypeStruct((B, S, d_r), F32),
                   jax.ShapeDtypeStruct((B, S, d_r), F32)],
        scratch_shapes=[pltpu.VMEM((8, D), F32)],
        compiler_params=_params(("parallel", "arbitrary")),
        name="proj",
    )(x, w_in.astype(BF16), w1.astype(BF16), a1.astype(BF16), g1.astype(BF16), mu_lora,
      w2.astype(BF16), a2.astype(BF16), g2.astype(BF16), w0.reshape(1, d_r), a0.reshape(1, d_r))


def _wkv_kernel(rkv_ref, lw_ref, ai_ref, g_ref, mu_ref, kk_ref, ka_ref, rk_ref, lng_ref, lnb_ref,
                seg_ref, tril_ref, y_ref, s_ref, prev_ref, *, d_r, n_chunks):
    C = WKV_CHUNK
    rows = C * n_chunks

    @pl.when(pl.program_id(1) == 0)
    def _():
        s_ref[...] = jnp.zeros_like(s_ref)
        prev_ref[...] = jnp.zeros_like(prev_ref)

    rkv = rkv_ref[0]
    prev = _shift_rows(rkv, prev_ref[0:1, :])
    prev_ref[0:1, :] = rkv[rows - 1:rows, :]
    mixed = rkv + (prev - rkv) * mu_ref[...]
    r = mixed[:, :d_r]
    k = mixed[:, d_r:2 * d_r]
    v = mixed[:, 2 * d_r:]
    lw = lw_ref[0]
    a = ai_ref[0]
    seg = seg_ref[...]

    kk = k * kk_ref[...]
    kk = kk * lax.rsqrt(jnp.maximum(_dot_exact_rhs(kk * kk, seg), 1e-24))
    k = k * (1.0 + (a - 1.0) * ka_ref[...])
    aa = -kk
    bb = kk * a
    c = _dot_exact_lhs(tril_ref[...], lw)
    e_neg = jnp.exp(-c)
    a_t = aa * jnp.exp(c - lw)
    b_t = bb * e_neg
    k_t = k * e_neg
    r_t = r * jnp.exp(c)

    lane = lax.broadcasted_iota(jnp.int32, (1, LANES), 1)
    head0 = lane < HEAD_DIM
    ri = lax.broadcasted_iota(jnp.int32, (2 * C, 2 * C), 0)
    ci = lax.broadcasted_iota(jnp.int32, (2 * C, 2 * C), 1)
    strict = (ri % C) > (ci % C)
    incl = (ri % C) >= (ci % C)
    eye = (ri == ci).astype(F32)
    zero_bf = jnp.zeros((2 * C, LANES), BF16)

    def stack(t):
        return jnp.concatenate([jnp.where(head0, t, 0.0), jnp.where(head0, 0.0, t)],
                               axis=0).astype(BF16)

    def fold(t):
        return t[:C] + t[C:]

    n_pairs = d_r // LANES
    folds = [(q, p) for q in range(n_chunks) for p in range(n_pairs)]
    each = lambda fn, *cols: [fn(*args) for args in zip(*cols)]
    cat0 = lambda *ts: jnp.concatenate(ts, axis=0)
    cat1 = lambda *ts: jnp.concatenate(ts, axis=1)
    left = lambda t: t[:, :2 * C]
    right = lambda t: t[:, 2 * C:]

    p_ends, b_es, k_es = [], [], []
    for q in range(n_chunks):
        rs = slice(q * C, (q + 1) * C)
        c_end = c[rs][C - 1:C, :]
        e_end = jnp.exp(c_end - c[rs])
        b_es.append(bb[rs] * e_end)
        k_es.append(k[rs] * e_end)
        p_ends.append(jnp.exp(c_end))

    def stacks(t_of_q):
        return [stack(t_of_q(q)[:, p * LANES:(p + 1) * LANES]) for q, p in folds]

    chunk = lambda t: (lambda q: t[q * C:(q + 1) * C])
    am, bm, km, rm, vm = (stacks(chunk(t)) for t in (a_t, b_t, k_t, r_t, v))
    bem = stacks(lambda q: b_es[q])
    kem = stacks(lambda q: k_es[q])

    x = each(lambda a_, r_, b_, k_: _dot_nt(cat0(a_, r_), cat0(b_, k_)), am, rm, bm, km)
    n = each(lambda t: jnp.where(strict, t[:2 * C, :2 * C], 0.0), x)
    a_ak = each(lambda t: jnp.where(strict, t[:2 * C, 2 * C:], 0.0).astype(BF16), x)
    a_rb = each(lambda t: jnp.where(incl, t[2 * C:, :2 * C], 0.0).astype(BF16), x)
    a_rk = each(lambda t: jnp.where(incl, t[2 * C:, 2 * C:], 0.0).astype(BF16), x)
    t_inv = each(lambda t: eye + t, n)
    m = each(lambda t: _dot(t, t), n)
    av = each(lambda a_, v_: _dot(a_, v_).astype(BF16), a_ak, vm)
    for _ in range(int(math.log2(C)) - 2):
        z = each(lambda m_, t_: _dot(m_, cat1(m_.astype(BF16), t_.astype(BF16))), m, t_inv)
        t_inv = each(lambda t_, z_: t_ + right(z_), t_inv, z)
        m = each(left, z)
    t_inv = each(lambda t_, m_: t_ + _dot(m_, t_), t_inv, m)
    z = each(lambda t_, a_, av_: _dot(t_, cat1(a_, av_)), t_inv, am, av)
    ap = each(lambda z_: left(z_).astype(BF16), z)
    u0 = each(lambda z_: right(z_).astype(BF16), z)
    z = each(lambda rb_, rk_, ap_, u_, v_: _dot(cat1(rb_, rk_),
                                               cat0(cat1(ap_, u_), cat1(zero_bf, v_))),
             a_rb, a_rk, ap, u0, vm)
    g_m = each(lambda ap_, be_: _dot_tn(ap_, be_).astype(BF16), ap, bem)
    d0 = each(lambda u_, v_, be_, ke_: _dot_tn(cat0(u_, v_), cat0(be_, ke_)), u0, vm, bem, kem)
    rp = each(lambda r_, z_: (r_.astype(F32) + left(z_)).astype(BF16), rm, z)
    y0 = each(lambda z_: fold(right(z_)), z)

    s = [s_ref[p] for p in range(n_pairs)]
    y_cols = []
    for q in range(n_chunks):
        s_bf = [t.astype(BF16) for t in s]
        f0 = q * n_pairs
        y_cols.append([fold(_dot_nt(rp[f0 + p], s_bf[p])) + y0[f0 + p] for p in range(n_pairs)])
        s = [s[p] * p_ends[q][:, p * LANES:(p + 1) * LANES] + _dot(s_bf[p], g_m[f0 + p])
             + d0[f0 + p] for p in range(n_pairs)]
    for p in range(n_pairs):
        s_ref[p] = s[p]
    y = cat0(*[cat1(*cols) for cols in y_cols])

    inv = 1.0 / HEAD_DIM
    mu = _dot_exact_rhs(y, seg) * inv
    yc = y - mu
    var = _dot_exact_rhs(yc * yc, seg) * inv
    yn = yc * lax.rsqrt(var + LNX_EPS) * lng_ref[...] + lnb_ref[...]
    bonus = _dot_exact_rhs(r * k * rk_ref[...], seg) * v
    y_ref[0] = ((yn + bonus) * g_ref[0]).astype(y_ref.dtype)


def _stage_wkv(rkv, lw, ai, g, mu_rkv, k_k, k_a, r_k, lnx_g, lnx_b):
    B, S, d_r = lw.shape
    n_chunks = WKV_CHUNKS_PER_STEP
    rows = WKV_CHUNK * n_chunks
    head = np.arange(d_r) // HEAD_DIM
    seg = jnp.asarray(head[:, None] == head[None, :], BF16)
    t = np.arange(rows)
    tril = jnp.asarray((t[:, None] >= t[None, :])
                       & (t[:, None] // WKV_CHUNK == t[None, :] // WKV_CHUNK), BF16)
    const = lambda shape: pl.BlockSpec(shape, lambda b, j: (0,) * len(shape))
    tile = lambda c: pl.BlockSpec((1, rows, c), lambda b, j: (b, j, 0))
    row = lambda t: t.reshape(1, -1)
    return pl.pallas_call(
        functools.partial(_wkv_kernel, d_r=d_r, n_chunks=n_chunks),
        grid=(B, S // rows),
        in_specs=[tile(3 * d_r), tile(d_r), tile(d_r), tile(d_r), const((1, 3 * d_r)),
                  const((1, d_r)), const((1, d_r)), const((1, d_r)), const((1, d_r)),
                  const((1, d_r)), const((d_r, d_r)), const((rows, rows))],
        out_specs=tile(d_r),
        out_shape=jax.ShapeDtypeStruct((B, S, d_r), BF16),
        scratch_shapes=[pltpu.VMEM((d_r // LANES, LANES, LANES), F32),
                        pltpu.VMEM((8, 3 * d_r), F32)],
        compiler_params=_params(("parallel", "arbitrary")),
        name="wkv",
    )(rkv, lw, ai, g, row(mu_rkv), row(k_k), row(k_a), row(r_k), row(lnx_g), row(lnx_b), seg, tril)


def _t5_bucket(n):
    exact = N_BUCKETS // 2
    nf = np.maximum(n, 1).astype(np.float32)
    large = exact + (np.log(nf / exact) / math.log(MAX_DISTANCE / exact)
                     * (N_BUCKETS - exact)).astype(np.int32)
    large = np.minimum(large, N_BUCKETS - 1)
    return np.where(n < exact, n, large).astype(np.int32)


def _band_bias(rel_bias, dil):
    blk = ATT_BLK
    rel = np.arange(blk)[:, None] + blk - np.arange(2 * blk)[None, :]
    valid = (rel >= 0) & (rel <= blk)
    bias = jnp.transpose(rel_bias[_t5_bucket(np.clip(rel, 0, None) * dil)], (2, 0, 1)).astype(F32)
    return jnp.where(valid[None], bias, NEG_INF)


def _attn_block(q, kcat, vcat, bias_ref, col0, scale):
    nk = kcat.shape[0]
    lane = lax.broadcasted_iota(jnp.int32, (1, LANES), 1)
    head0 = lane < HEAD_DIM
    n_pairs = q.shape[1] // LANES
    heads = [(p, h) for p in range(n_pairs) for h in range(2)]
    pair = lambda t, p: t[:, p * LANES:(p + 1) * LANES]
    s = []
    for p, h in heads:
        qp = pair(q, p)
        qh = jnp.where(head0 if h == 0 else jnp.logical_not(head0), qp, jnp.zeros_like(qp))
        s.append(_dot_nt(qh, pair(kcat, p)) * scale + bias_ref[2 * p + h, :, col0:col0 + nk])
    m = [jnp.max(t, axis=-1, keepdims=True) for t in s]
    e = [jnp.exp(t - m_) for t, m_ in zip(s, m)]
    l = [jnp.sum(t, axis=-1, keepdims=True) for t in e]
    o = [_dot(t, pair(vcat, p)) / l_ for t, l_, (p, h) in zip(e, l, heads)]
    lse = [m_ + jnp.log(l_) for m_, l_ in zip(m, l)]
    outs = [jnp.where(head0, o[2 * p], o[2 * p + 1]) for p in range(n_pairs)]
    lses = [jnp.where(head0, lse[2 * p], lse[2 * p + 1]) for p in range(n_pairs)]
    return jnp.concatenate(outs, axis=1), jnp.concatenate(lses, axis=1)


def _attn_kernel(q_ref, k_ref, v_ref, bias_ref, o_ref, l_ref, *, nb, scale):
    blk = ATT_BLK
    o, l = _attn_block(q_ref[0, 0:blk, :], k_ref[0, 0:blk, :], v_ref[0, 0:blk, :], bias_ref, blk,
                       scale)
    o_ref[0, 0:blk, :] = o
    l_ref[0, 0:blk, :] = l

    def body(n, carry):
        q0 = pl.multiple_of(n * blk, blk)
        k0 = pl.multiple_of((n - 1) * blk, blk)
        o, l = _attn_block(q_ref[0, pl.ds(q0, blk), :], k_ref[0, pl.ds(k0, 2 * blk), :],
                           v_ref[0, pl.ds(k0, 2 * blk), :], bias_ref, 0, scale)
        o_ref[0, pl.ds(q0, blk), :] = o
        l_ref[0, pl.ds(q0, blk), :] = l
        return carry

    if nb > 1:
        lax.fori_loop(1, nb, body, 0)


def _stage_attn(qkv, rel_bias, window, dil):
    B, S, d3 = qkv.shape
    d_att = d3 // 3
    L = S // dil
    nb = L // ATT_BLK
    assert window // dil == ATT_BLK and nb * ATT_BLK == L
    bias = _band_bias(rel_bias, dil)
    qkv_v = qkv.reshape(B, L, dil * d3)
    part = lambda i: pl.BlockSpec((1, L, d_att), lambda b, r: (b, 0, 3 * r + i))
    out = pl.BlockSpec((1, L, d_att), lambda b, r: (b, 0, r))
    o, l = pl.pallas_call(
        functools.partial(_attn_kernel, nb=nb, scale=HEAD_DIM ** -0.5),
        grid=(B, dil),
        in_specs=[part(0), part(1), part(2),
                  pl.BlockSpec(bias.shape, lambda b, r: (0, 0, 0))],
        out_specs=[out, out],
        out_shape=[jax.ShapeDtypeStruct((B, L, dil * d_att), F32)] * 2,
        compiler_params=_params(("parallel", "parallel")),
        name=f"attn_d{dil}",
    )(qkv_v, qkv_v, qkv_v, bias)
    return o.reshape(B * S, d_att), l.reshape(B * S, d_att)


def _mix_kernel(x_ref, yr_ref, o1_ref, o2_ref, o3_ref, l1_ref, l2_ref, l3_ref, p_ref, wor_ref,
                woa_ref, g_ref, b_ref, rhi_ref, rlo_ref, rb_ref, pg_ref, pp_ref,
                x1_ref, lg_ref, ple_ref, *, alpha):
    l1, l2, l3 = l1_ref[...], l2_ref[...], l3_ref[...]
    m = jnp.maximum(jnp.maximum(l1, l2), l3)
    e1, e2, e3 = jnp.exp(l1 - m), jnp.exp(l2 - m), jnp.exp(l3 - m)
    att = (e1 * o1_ref[...] + e2 * o2_ref[...] + e3 * o3_ref[...]) / (e1 + e2 + e3)
    mix = _dot(yr_ref[...], wor_ref[...]) + _dot(att, woa_ref[...])
    x1 = _layer_norm(alpha * x_ref[...] + mix, g_ref[...], b_ref[...])
    x1_ref[...] = x1
    hi, lo = _split(x1)
    lg_ref[...] = (jnp.dot(hi, rhi_ref[...], preferred_element_type=F32)
                   + jnp.dot(lo, rhi_ref[...], preferred_element_type=F32)
                   + jnp.dot(hi, rlo_ref[...], preferred_element_type=F32) + rb_ref[...])
    ple_ref[...] = _sigmoid(_dot(hi, pg_ref[...])) * _dot(p_ref[...], pp_ref[...])


def _stage_mix(x2, yr, att, p2, w_o, ln_g, ln_b, router_g, router_g_b, router_e, router_e_b,
               ple_gate, ple_proj, alpha):
    N, D = x2.shape
    d_r = yr.shape[1]
    rows = min(MIX_ROWS, N)
    (o1, l1), (o2, l2), (o3, l3) = att
    d_att = o1.shape[1]
    n_log = N_GROUPS + N_EXPERTS
    rw = jnp.zeros((D, ROUTER_LANES), F32).at[:, :N_GROUPS].set(router_g)
    rw = rw.at[:, N_GROUPS:n_log].set(router_e)
    rb = jnp.zeros((1, ROUTER_LANES), F32).at[0, :N_GROUPS].set(router_g_b)
    rb = rb.at[0, N_GROUPS:n_log].set(router_e_b)
    rhi, rlo = _split(rw)
    const = lambda shape: pl.BlockSpec(shape, lambda i: (0,) * len(shape))
    tile = lambda c: pl.BlockSpec((rows, c), lambda i: (i, 0))
    return pl.pallas_call(
        functools.partial(_mix_kernel, alpha=alpha),
        grid=(N // rows,),
        in_specs=[tile(D), tile(d_r)] + [tile(d_att)] * 6 + [tile(p2.shape[1]),
                  const((d_r, D)), const((d_att, D)), const((1, D)), const((1, D)),
                  const((D, ROUTER_LANES)), const((D, ROUTER_LANES)), const((1, ROUTER_LANES)),
                  const((D, D)), const(ple_proj.shape)],
        out_specs=[tile(D), tile(ROUTER_LANES), tile(D)],
        out_shape=[jax.ShapeDtypeStruct((N, D), F32),
                   jax.ShapeDtypeStruct((N, ROUTER_LANES), F32),
                   jax.ShapeDtypeStruct((N, D), F32)],
        compiler_params=_params(("parallel",)),
        name="mix",
    )(x2, yr, o1, o2, o3, l1, l2, l3, p2, w_o[:d_r].astype(BF16), w_o[d_r:].astype(BF16),
      ln_g.reshape(1, D), ln_b.reshape(1, D), rhi, rlo, rb, ple_gate.astype(BF16),
      ple_proj.astype(BF16))


def _route(logits, rows):
    N = logits.shape[0]
    lg = logits[:, :N_GROUPS]
    le = logits[:, N_GROUPS:N_GROUPS + N_EXPERTS]
    pg = jax.nn.softmax(lg, -1)
    grp = jnp.argmax(lg, -1)
    wg = jnp.take_along_axis(pg, grp[:, None], axis=-1)
    le = jnp.take_along_axis(le.reshape(N, N_GROUPS, EXPERTS_PER_GROUP), grp[:, None, None],
                             axis=1)[:, 0]
    top_v, top_i = lax.top_k(le, TOP_K)
    we = jax.nn.softmax(top_v, -1) * wg
    eid = grp[:, None] * EXPERTS_PER_GROUP + top_i
    A = N * TOP_K
    e_flat = eid.T.reshape(A).astype(jnp.int32)
    order = jnp.argsort(e_flat).astype(jnp.int32)
    counts = jnp.sum((e_flat[:, None] == jnp.arange(N_EXPERTS)[None, :]).astype(jnp.int32), axis=0)
    start = jnp.cumsum(counts) - counts
    padded = (counts + rows - 1) // rows * rows
    pend = jnp.cumsum(padded)
    pstart = pend - padded
    nblk = -(-A // rows) + N_EXPERTS
    nused = (pend[-1] // rows).astype(jnp.int32)
    blk = jnp.arange(nblk, dtype=jnp.int32)
    blk_e = jnp.sum((pend[None, :] <= (blk * rows)[:, None]).astype(jnp.int32), axis=1)
    blk_e = jnp.minimum(blk_e, N_EXPERTS - 1)
    blk_e = jnp.where(blk < nused, blk_e, blk_e[nused - 1]).astype(jnp.int32)
    r = jnp.arange(rows, dtype=jnp.int32)[None, :]
    off = (blk * rows - pstart[blk_e])[:, None] + r
    real = (off < counts[blk_e][:, None]) & (blk < nused)[:, None]
    asg = order[jnp.clip(start[blk_e][:, None] + off, 0, A - 1)]
    row_tok = jnp.where(real, asg % N, r).astype(jnp.int32)
    row_dst = jnp.where(real, asg, A + r).astype(jnp.int32)
    return (row_tok.reshape(nblk, 1, rows), row_dst.reshape(nblk, 1, rows), blk_e,
            nused.reshape(1), we)


def _moe_kernel(blk_e_ref, nused_ref, tok0_ref, tokn_ref, dstp_ref, x_hbm, wg_ref, wu_ref, wd_ref,
                y_hbm, xbuf, ybuf, gsem, ssem, *, rows, n_real):
    i = pl.program_id(0)
    nused = nused_ref[0]
    slot = i % 2

    def gather_row(tab_ref, sl, r):
        return pltpu.make_async_copy(x_hbm.at[pl.ds(tab_ref[0, 0, r], 1)],
                                     xbuf.at[sl, pl.ds(r, 1)], gsem.at[sl])

    def scatter_row(sl, r):
        t = dstp_ref[0, 0, r]
        dst = jnp.where(t >= n_real, t + sl * rows, t)
        return pltpu.make_async_copy(ybuf.at[sl, pl.ds(r, 1)], y_hbm.at[pl.ds(dst, 1)],
                                     ssem.at[sl])

    def gather_all(sl):
        return pltpu.make_async_copy(x_hbm.at[pl.ds(0, rows)], xbuf.at[sl], gsem.at[sl])

    def scatter_all(sl, first_row=0):
        return pltpu.make_async_copy(ybuf.at[sl], y_hbm.at[pl.ds(first_row, rows)], ssem.at[sl])

    def rolled(fn):
        def body(r, c):
            fn(r)
            return c
        lax.fori_loop(0, rows, body, 0, unroll=8)

    def inline(fn):
        for r in range(rows):
            fn(r)

    def ffn(sl):
        xb = xbuf[sl].astype(BF16)
        gate = _dot(xb, wg_ref[0])
        up = _dot(xb, wu_ref[0])
        ybuf[sl] = _dot(gate * _sigmoid(gate) * up, wd_ref[0])

    @pl.when(i == 0)
    def _():
        ybuf[...] = jnp.zeros_like(ybuf)
        for sl in range(2):
            scatter_all(sl, n_real + sl * rows).start()
        for sl in range(2):
            scatter_all(sl, n_real + sl * rows).wait()
        rolled(lambda r: gather_row(tok0_ref, 0, r).start())

    @pl.when(i < nused)
    def _():
        gather_all(slot).wait()

    @pl.when((i >= 2) & (i <= nused))
    def _():
        scatter_all(slot).wait()

    @pl.when(i == 0)
    def _():
        inline(lambda r: gather_row(tokn_ref, 1, r).start())
        ffn(0)

    @pl.when((i >= 1) & (i < nused))
    def _():
        inline(lambda r: gather_row(tokn_ref, 1 - slot, r).start())
        inline(lambda r: scatter_row(1 - slot, r).start())
        ffn(slot)

    @pl.when(i == nused)
    def _():
        rolled(lambda r: scatter_row(1 - slot, r).start())
        scatter_all(1 - slot).wait()
        gather_all(slot).wait()


def _stage_moe(x1, row_tok, row_dst, blk_e, nused, w_gate, w_up, w_down):
    N, D = x1.shape
    nblk, _, rows = row_tok.shape
    d_e = w_gate.shape[2]
    n_real = N * TOP_K
    last = nblk - 1
    wspec = lambda shape: pl.BlockSpec((1,) + shape,
                                       lambda i, be, nu: (be[jnp.minimum(i, last)], 0, 0))
    table = lambda fn: pl.BlockSpec((1, 1, rows), lambda i, be, nu: (fn(i), 0, 0),
                                    memory_space=pltpu.SMEM)
    grid_spec = pltpu.PrefetchScalarGridSpec(
        num_scalar_prefetch=2,
        grid=(nblk + 1,),
        in_specs=[table(lambda i: 0), table(lambda i: jnp.minimum(i + 1, last)),
                  table(lambda i: jnp.clip(i - 1, 0, last)),
                  pl.BlockSpec(memory_space=pl.ANY), wspec((D, d_e)), wspec((D, d_e)),
                  wspec((d_e, D))],
        out_specs=pl.BlockSpec(memory_space=pl.ANY),
        scratch_shapes=[pltpu.VMEM((2, rows, D), F32), pltpu.VMEM((2, rows, D), F32),
                        pltpu.SemaphoreType.DMA((2,)), pltpu.SemaphoreType.DMA((2,))],
    )
    return pl.pallas_call(
        functools.partial(_moe_kernel, rows=rows, n_real=n_real),
        grid_spec=grid_spec,
        out_shape=jax.ShapeDtypeStruct((n_real + 2 * rows, D), F32),
        compiler_params=_params(("arbitrary",)),
        name="moe",
    )(blk_e, nused, row_tok, row_tok, row_dst, x1, w_gate, w_up, w_down)


def _out_kernel(x1_ref, y0_ref, y1_ref, we_ref, ple_ref, g_ref, b_ref, o_ref, *, alpha):
    we = we_ref[...]
    moe = we[:, 0:1] * y0_ref[...] + we[:, 1:2] * y1_ref[...]
    o_ref[...] = _layer_norm(alpha * x1_ref[...] + moe + ple_ref[...], g_ref[...], b_ref[...])


def _stage_out(x1, yb, we, ple, ln_g, ln_b, alpha):
    N, D = x1.shape
    rows = min(OUT_ROWS, N)
    const = lambda shape: pl.BlockSpec(shape, lambda i: (0,) * len(shape))
    tile = lambda c: pl.BlockSpec((rows, c), lambda i: (i, 0))
    slot = lambda j: pl.BlockSpec((rows, D), lambda i: (i + j * (N // rows), 0))
    return pl.pallas_call(
        functools.partial(_out_kernel, alpha=alpha),
        grid=(N // rows,),
        in_specs=[tile(D), slot(0), slot(1), tile(TOP_K), tile(D), const((1, D)), const((1, D))],
        out_specs=tile(D),
        out_shape=jax.ShapeDtypeStruct((N, D), F32),
        compiler_params=_params(("parallel",)),
        name="out",
    )(x1, yb, yb, we, ple, ln_g.reshape(1, D), ln_b.reshape(1, D))


def kernel(x, p, w_in, mu_rkv, mu_lora, w0, w_lora1, w_lora2, a0, a_lora1, a_lora2, g_lora1, g_lora2, k_k, k_a, r_k, lnx_g, lnx_b, rel_bias, w_o, ln1_g, ln1_b, router_g, router_g_b, router_e, router_e_b, w_gate, w_up, w_down, ple_gate, ple_proj, ln2_g, ln2_b):
    B, S, D = x.shape
    depth = w_in.shape[0]
    alpha = (2 * depth) ** 0.25
    for i in range(depth):
        rkv, qkv, lw, ai, g = _stage_proj(x, w_in[i], w_lora1[i], a_lora1[i], g_lora1[i],
                                          mu_lora[i], w_lora2[i], a_lora2[i], g_lora2[i],
                                          w0[i], a0[i])
        yr = _stage_wkv(rkv, lw, ai, g, mu_rkv[i], k_k[i], k_a[i], r_k[i], lnx_g[i], lnx_b[i])
        att = [_stage_attn(qkv, rel_bias, window, dil) for window, dil in DIL_PATTERNS]
        x1, logits, ple = _stage_mix(x.reshape(B * S, D), yr.reshape(B * S, -1), att,
                                     p[i].reshape(B * S, -1), w_o[i], ln1_g[i], ln1_b[i],
                                     router_g[i], router_g_b[i], router_e[i], router_e_b[i],
                                     ple_gate[i], ple_proj[i], alpha)
        row_tok, row_dst, blk_e, nused, we = _route(logits, MOE_ROWS)
        yb = _stage_moe(x1, row_tok, row_dst, blk_e, nused, w_gate[i], w_up[i], w_down[i])
        x = _stage_out(x1, yb, we, ple, ln2_g[i], ln2_b[i], alpha).reshape(B, S, D)
    return x
```

```python
import functools
import math

import jax
import jax.numpy as jnp
import numpy as np
from jax import lax
from jax.experimental import pallas as pl
from jax.experimental.pallas import tpu as pltpu

F32 = jnp.float32
BF16 = jnp.bfloat16

HEAD_DIM = 64
LANES = 128
DECAY_LORA = 64
ICLR_LORA = 64
GATE_LORA = 128
LNX_EPS = 64e-5
DIL_PATTERNS = ((128, 1), (512, 4), (2048, 16))
NEG_INF = -1e30
N_BUCKETS = 32
MAX_DISTANCE = 2048
N_GROUPS = 4
EXPERTS_PER_GROUP = 8
N_EXPERTS = N_GROUPS * EXPERTS_PER_GROUP
TOP_K = 2
LN_EPS = 1e-5

WKV_CHUNK = 64
WKV_CHUNKS_PER_STEP = 2
ATT_BLK = 128
PROJ_ROWS = 512
MIX_ROWS = 256
MOE_ROWS = 256
OUT_ROWS = 256
ROUTER_LANES = 128
VMEM_LIMIT = 56 * 1024 * 1024


def _params(sem):
    return pltpu.CompilerParams(dimension_semantics=sem, vmem_limit_bytes=VMEM_LIMIT)


def _dot(a, b):
    return jnp.dot(a.astype(BF16), b.astype(BF16), preferred_element_type=F32)


def _dot_nt(a, b):
    return lax.dot_general(a.astype(BF16), b.astype(BF16), (((1,), (1,)), ((), ())),
                           preferred_element_type=F32)


def _dot_tn(a, b):
    return lax.dot_general(a.astype(BF16), b.astype(BF16), (((0,), (0,)), ((), ())),
                           preferred_element_type=F32)


def _split(x):
    hi = x.astype(BF16)
    lo = (x - hi.astype(F32)).astype(BF16)
    return hi, lo


def _dot_exact_rhs(x, w01):
    hi, lo = _split(x)
    return (jnp.dot(hi, w01, preferred_element_type=F32)
            + jnp.dot(lo, w01, preferred_element_type=F32))


def _dot_exact_lhs(w01, x):
    hi, lo = _split(x)
    return (jnp.dot(w01, hi, preferred_element_type=F32)
            + jnp.dot(w01, lo, preferred_element_type=F32))


def _sigmoid(x):
    return 1.0 / (1.0 + jnp.exp(-x))


def _shift_rows(t, prev_row):
    rolled = pltpu.roll(t, 1, 0)
    row = lax.broadcasted_iota(jnp.int32, t.shape, 0)
    return jnp.where(row == 0, prev_row, rolled)


def _layer_norm(z, g, b):
    mu = jnp.mean(z, axis=-1, keepdims=True)
    zc = z - mu
    var = jnp.mean(zc * zc, axis=-1, keepdims=True)
    return zc * lax.rsqrt(var + LN_EPS) * g + b


def _proj_kernel(x_ref, win_ref, w1_ref, a1_ref, g1_ref, mul_ref, w2_ref, a2_ref, g2_ref,
                 w0_ref, a0_ref, rkv_ref, qkv_ref, lw_ref, ai_ref, g_ref, prev_ref, *, d_rkv):
    @pl.when(pl.program_id(1) == 0)
    def _():
        prev_ref[...] = jnp.zeros_like(prev_ref)

    h = x_ref[0]
    rows = h.shape[0]
    hprev = _shift_rows(h, prev_ref[0:1, :])
    prev_ref[0:1, :] = h[rows - 1:rows, :]

    proj = _dot(h, win_ref[...])
    rkv_ref[0] = proj[:, :d_rkv]
    qkv_ref[0] = proj[:, d_rkv:]

    dh = hprev - h
    xw = h + dh * mul_ref[0:1, :]
    xa = h + dh * mul_ref[1:2, :]
    xg = h + dh * mul_ref[2:3, :]
    wl = w0_ref[...] + _dot(jnp.tanh(_dot(xw, w1_ref[...])), w2_ref[...])
    z = -wl
    softplus = jnp.maximum(z, 0.0) + jnp.log(1.0 + jnp.exp(-jnp.abs(z)))
    lw_ref[0] = -jnp.exp(-softplus - 0.5)
    ai_ref[0] = _sigmoid(a0_ref[...] + _dot(_dot(xa, a1_ref[...]), a2_ref[...]))
    g_ref[0] = _dot(_sigmoid(_dot(xg, g1_ref[...])), g2_ref[...])


def _stage_proj(x, w_in, w1, a1, g1, mu_lora, w2, a2, g2, w0, a0):
    B, S, D = x.shape
    d_in = w_in.shape[1]
    d_r = w2.shape[1]
    d_rkv = 3 * d_r
    d_att = d_in - d_rkv
    rows = min(PROJ_ROWS, S)
    const = lambda shape: pl.BlockSpec(shape, lambda b, j: (0,) * len(shape))
    tile = lambda c: pl.BlockSpec((1, rows, c), lambda b, j: (b, j, 0))
    return pl.pallas_call(
        functools.partial(_proj_kernel, d_rkv=d_rkv),
        grid=(B, S // rows),
        in_specs=[tile(D), const((D, d_in)), const(w1.shape), const(a1.shape), const(g1.shape),
                  const(mu_lora.shape), const(w2.shape), const(a2.shape), const(g2.shape),
                  const((1, d_r)), const((1, d_r))],
        out_specs=[tile(d_rkv), tile(d_att), tile(d_r), tile(d_r), tile(d_r)],
        out_shape=[jax.ShapeDtypeStruct((B, S, d_rkv), F32),
                   jax.ShapeDtypeStruct((B, S, d_att), F32),
                   jax.ShapeDtypeStruct((B, S, d_r), F32),
                   jax.ShapeDtypeStruct((B, S, d_r), F32),
                   jax.ShapeDtypeStruct((B, S, d_r), F32)],
        scratch_shapes=[pltpu.VMEM((8, D), F32)],
        compiler_params=_params(("parallel", "arbitrary")),
        name="proj",
    )(x, w_in.astype(BF16), w1.astype(BF16), a1.astype(BF16), g1.astype(BF16), mu_lora,
      w2.astype(BF16), a2.astype(BF16), g2.astype(BF16), w0.reshape(1, d_r), a0.reshape(1, d_r))


def _wkv_kernel(rkv_ref, lw_ref, ai_ref, g_ref, mu_ref, kk_ref, ka_ref, rk_ref, lng_ref, lnb_ref,
                seg_ref, tril_ref, y_ref, s_ref, prev_ref, *, d_r, n_chunks):
    C = WKV_CHUNK
    rows = C * n_chunks

    @pl.when(pl.program_id(1) == 0)
    def _():
        s_ref[...] = jnp.zeros_like(s_ref)
        prev_ref[...] = jnp.zeros_like(prev_ref)

    rkv = rkv_ref[0]
    prev = _shift_rows(rkv, prev_ref[0:1, :])
    prev_ref[0:1, :] = rkv[rows - 1:rows, :]
    mixed = rkv + (prev - rkv) * mu_ref[...]
    r = mixed[:, :d_r]
    k = mixed[:, d_r:2 * d_r]
    v = mixed[:, 2 * d_r:]
    lw = lw_ref[0]
    a = ai_ref[0]
    seg = seg_ref[...]

    kk = k * kk_ref[...]
    kk = kk * lax.rsqrt(jnp.maximum(_dot_exact_rhs(kk * kk, seg), 1e-24))
    k = k * (1.0 + (a - 1.0) * ka_ref[...])
    aa = -kk
    bb = kk * a
    c = _dot_exact_lhs(tril_ref[...], lw)
    e_neg = jnp.exp(-c)
    a_t = aa * jnp.exp(c - lw)
    b_t = bb * e_neg
    k_t = k * e_neg
    r_t = r * jnp.exp(c)

    lane = lax.broadcasted_iota(jnp.int32, (1, LANES), 1)
    head0 = lane < HEAD_DIM
    ri = lax.broadcasted_iota(jnp.int32, (2 * C, 2 * C), 0)
    ci = lax.broadcasted_iota(jnp.int32, (2 * C, 2 * C), 1)
    strict = (ri % C) > (ci % C)
    incl = (ri % C) >= (ci % C)
    eye = (ri == ci).astype(F32)
    zero_bf = jnp.zeros((2 * C, LANES), BF16)

    def stack(t):
        return jnp.concatenate([jnp.where(head0, t, 0.0), jnp.where(head0, 0.0, t)],
                               axis=0).astype(BF16)

    def fold(t):
        return t[:C] + t[C:]

    n_pairs = d_r // LANES
    folds = [(q, p) for q in range(n_chunks) for p in range(n_pairs)]
    each = lambda fn, *cols: [fn(*args) for args in zip(*cols)]
    cat0 = lambda *ts: jnp.concatenate(ts, axis=0)
    cat1 = lambda *ts: jnp.concatenate(ts, axis=1)
    left = lambda t: t[:, :2 * C]
    right = lambda t: t[:, 2 * C:]

    p_ends, b_es, k_es = [], [], []
    for q in range(n_chunks):
        rs = slice(q * C, (q + 1) * C)
        c_end = c[rs][C - 1:C, :]
        e_end = jnp.exp(c_end - c[rs])
        b_es.append(bb[rs] * e_end)
        k_es.append(k[rs] * e_end)
        p_ends.append(jnp.exp(c_end))

    def stacks(t_of_q):
        return [stack(t_of_q(q)[:, p * LANES:(p + 1) * LANES]) for q, p in folds]

    chunk = lambda t: (lambda q: t[q * C:(q + 1) * C])
    am, bm, km, rm, vm = (stacks(chunk(t)) for t in (a_t, b_t, k_t, r_t, v))
    bem = stacks(lambda q: b_es[q])
    kem = stacks(lambda q: k_es[q])

    x = each(lambda a_, r_, b_, k_: _dot_nt(cat0(a_, r_), cat0(b_, k_)), am, rm, bm, km)
    n = each(lambda t: jnp.where(strict, t[:2 * C, :2 * C], 0.0), x)
    a_ak = each(lambda t: jnp.where(strict, t[:2 * C, 2 * C:], 0.0).astype(BF16), x)
    a_rb = each(lambda t: jnp.where(incl, t[2 * C:, :2 * C], 0.0).astype(BF16), x)
    a_rk = each(lambda t: jnp.where(incl, t[2 * C:, 2 * C:], 0.0).astype(BF16), x)
    t_inv = each(lambda t: eye + t, n)
    m = each(lambda t: _dot(t, t), n)
    av = each(lambda a_, v_: _dot(a_, v_).astype(BF16), a_ak, vm)
    for _ in range(int(math.log2(C)) - 2):
        z = each(lambda m_, t_: _dot(m_, cat1(m_.astype(BF16), t_.astype(BF16))), m, t_inv)
        t_inv = each(lambda t_, z_: t_ + right(z_), t_inv, z)
        m = each(left, z)
    t_inv = each(lambda t_, m_: t_ + _dot(m_, t_), t_inv, m)
    z = each(lambda t_, a_, av_: _dot(t_, cat1(a_, av_)), t_inv, am, av)
    ap = each(lambda z_: left(z_).astype(BF16), z)
    u0 = each(lambda z_: right(z_).astype(BF16), z)
    z = each(lambda rb_, rk_, ap_, u_, v_: _dot(cat1(rb_, rk_),
                                               cat0(cat1(ap_, u_), cat1(zero_bf, v_))),
             a_rb, a_rk, ap, u0, vm)
    g_m = each(lambda ap_, be_: _dot_tn(ap_, be_).astype(BF16), ap, bem)
    d0 = each(lambda u_, v_, be_, ke_: _dot_tn(cat0(u_, v_), cat0(be_, ke_)), u0, vm, bem, kem)
    rp = each(lambda r_, z_: (r_.astype(F32) + left(z_)).astype(BF16), rm, z)
    y0 = each(lambda z_: fold(right(z_)), z)

    s = [s_ref[p] for p in range(n_pairs)]
    y_cols = []
    for q in range(n_chunks):
        s_bf = [t.astype(BF16) for t in s]
        f0 = q * n_pairs
        y_cols.append([fold(_dot_nt(rp[f0 + p], s_bf[p])) + y0[f0 + p] for p in range(n_pairs)])
        s = [s[p] * p_ends[q][:, p * LANES:(p + 1) * LANES] + _dot(s_bf[p], g_m[f0 + p])
             + d0[f0 + p] for p in range(n_pairs)]
    for p in range(n_pairs):
        s_ref[p] = s[p]
    y = cat0(*[cat1(*cols) for cols in y_cols])

    inv = 1.0 / HEAD_DIM
    mu = _dot_exact_rhs(y, seg) * inv
    yc = y - mu
    var = _dot_exact_rhs(yc * yc, seg) * inv
    yn = yc * lax.rsqrt(var + LNX_EPS) * lng_ref[...] + lnb_ref[...]
    bonus = _dot_exact_rhs(r * k * rk_ref[...], seg) * v
    y_ref[0] = ((yn + bonus) * g_ref[0]).astype(y_ref.dtype)


def _stage_wkv(rkv, lw, ai, g, mu_rkv, k_k, k_a, r_k, lnx_g, lnx_b):
    B, S, d_r = lw.shape
    n_chunks = WKV_CHUNKS_PER_STEP
    rows = WKV_CHUNK * n_chunks
    head = np.arange(d_r) // HEAD_DIM
    seg = jnp.asarray(head[:, None] == head[None, :], BF16)
    t = np.arange(rows)
    tril = jnp.asarray((t[:, None] >= t[None, :])
                       & (t[:, None] // WKV_CHUNK == t[None, :] // WKV_CHUNK), BF16)
    const = lambda shape: pl.BlockSpec(shape, lambda b, j: (0,) * len(shape))
    tile = lambda c: pl.BlockSpec((1, rows, c), lambda b, j: (b, j, 0))
    row = lambda t: t.reshape(1, -1)
    return pl.pallas_call(
        functools.partial(_wkv_kernel, d_r=d_r, n_chunks=n_chunks),
        grid=(B, S // rows),
        in_specs=[tile(3 * d_r), tile(d_r), tile(d_r), tile(d_r), const((1, 3 * d_r)),
                  const((1, d_r)), const((1, d_r)), const((1, d_r)), const((1, d_r)),
                  const((1, d_r)), const((d_r, d_r)), const((rows, rows))],
        out_specs=tile(d_r),
        out_shape=jax.ShapeDtypeStruct((B, S, d_r), BF16),
        scratch_shapes=[pltpu.VMEM((d_r // LANES, LANES, LANES), F32),
                        pltpu.VMEM((8, 3 * d_r), F32)],
        compiler_params=_params(("parallel", "arbitrary")),
        name="wkv",
    )(rkv, lw, ai, g, row(mu_rkv), row(k_k), row(k_a), row(r_k), row(lnx_g), row(lnx_b), seg, tril)


def _t5_bucket(n):
    exact = N_BUCKETS // 2
    nf = np.maximum(n, 1).astype(np.float32)
    large = exact + (np.log(nf / exact) / math.log(MAX_DISTANCE / exact)
                     * (N_BUCKETS - exact)).astype(np.int32)
    large = np.minimum(large, N_BUCKETS - 1)
    return np.where(n < exact, n, large).astype(np.int32)


def _band_bias(rel_bias, dil):
    blk = ATT_BLK
    rel = np.arange(blk)[:, None] + blk - np.arange(2 * blk)[None, :]
    valid = (rel >= 0) & (rel <= blk)
    bias = jnp.transpose(rel_bias[_t5_bucket(np.clip(rel, 0, None) * dil)], (2, 0, 1)).astype(F32)
    return jnp.where(valid[None], bias, NEG_INF)


def _attend(units, bias_ref, scale):
    lane = lax.broadcasted_iota(jnp.int32, (1, LANES), 1)
    head0 = lane < HEAD_DIM
    jobs = [(u, h) for u in range(len(units)) for h in range(2)]
    s = []
    for u, h in jobs:
        q, k, _, col0 = units[u]
        qh = jnp.where(head0 if h == 0 else jnp.logical_not(head0), q * scale, 0.0)
        s.append(_dot_nt(qh, k) + bias_ref[h, :, col0:col0 + k.shape[0]])
    m = [jnp.max(t, axis=-1, keepdims=True) for t in s]
    e = [jnp.exp(t - m_) for t, m_ in zip(s, m)]
    l = [jnp.sum(t, axis=-1, keepdims=True) for t in e]
    o = [_dot(t, units[u][2]) / l_ for t, l_, (u, h) in zip(e, l, jobs)]
    lse = [m_ + jnp.log(l_) for m_, l_ in zip(m, l)]
    return [(jnp.where(head0, o[2 * u], o[2 * u + 1]), jnp.where(head0, lse[2 * u], lse[2 * u + 1]))
            for u in range(len(units))]


def _attn_kernel(q_ref, k_ref, v_ref, b1_ref, b4_ref, b16_ref, o_ref, acc_o, acc_l, *, seq, scale):
    blk = ATT_BLK
    group = 4

    def rows(ref, start, n, dil):
        if dil == 1:
            return ref[0, pl.ds(start, n), :]
        return ref[0, pl.ds(start, n, stride=dil), :]

    def unit(start, dil, has_prev):
        q = rows(q_ref, start, blk, dil)
        if has_prev:
            k0 = start - blk * dil
            return (q, rows(k_ref, k0, 2 * blk, dil), rows(v_ref, k0, 2 * blk, dil), 0)
        return (q, rows(k_ref, start, blk, dil), rows(v_ref, start, blk, dil), blk)

    def acc_rows(ref, start, dil):
        if dil == 1:
            return ref.at[pl.ds(start, blk), :]
        return ref.at[pl.ds(start, blk, stride=dil), :]

    def run(starts, dil, prevs, bias_ref, first, last):
        units = [unit(s, dil, hp) for s, hp in zip(starts, prevs)]
        for s, (o, lse) in zip(starts, _attend(units, bias_ref, scale)):
            ao, al = acc_rows(acc_o, s, dil), acc_rows(acc_l, s, dil)
            if first:
                ao[...] = o
                al[...] = lse
            else:
                o_old, l_old = ao[...], al[...]
                m = jnp.maximum(l_old, lse)
                e_old, e_new = jnp.exp(l_old - m), jnp.exp(lse - m)
                ao[...] = (e_old * o_old + e_new * o) / (e_old + e_new)
                if not last:
                    al[...] = m + jnp.log(e_old + e_new)

    run([n * blk for n in range(group)], 1, [n > 0 for n in range(group)], b1_ref, True, False)

    def body1(g, c):
        base = pl.multiple_of(g * (group * blk), group * blk)
        run([base + n * blk for n in range(group)], 1, [True] * group, b1_ref, True, False)
        return c
    lax.fori_loop(1, seq // (group * blk), body1, 0)

    nb4 = seq // 4 // blk

    def body4(r, c):
        run([r + n * blk * 4 for n in range(nb4)], 4, [n > 0 for n in range(nb4)], b4_ref,
            False, False)
        return c
    lax.fori_loop(0, 4, body4, 0)

    def body16(g, c):
        run([g * group + j for j in range(group)], 16, [False] * group, b16_ref, False, True)
        return c
    lax.fori_loop(0, 16 // group, body16, 0)

    o_ref[0] = acc_o[...].astype(o_ref.dtype)


def _stage_attn(qkv, rel_bias):
    B, S, d3 = qkv.shape
    d_att = d3 // 3
    n_pairs = d_att // LANES
    assert DIL_PATTERNS == ((ATT_BLK, 1), (4 * ATT_BLK, 4), (16 * ATT_BLK, 16))
    assert S == 16 * ATT_BLK
    biases = [_band_bias(rel_bias, dil) for _, dil in DIL_PATTERNS]
    part = lambda i: pl.BlockSpec((1, S, LANES), lambda b, p: (b, 0, i * n_pairs + p))
    bias_spec = pl.BlockSpec((2, ATT_BLK, 2 * ATT_BLK), lambda b, p: (p, 0, 0))
    return pl.pallas_call(
        functools.partial(_attn_kernel, seq=S, scale=HEAD_DIM ** -0.5),
        grid=(B, n_pairs),
        in_specs=[part(0), part(1), part(2), bias_spec, bias_spec, bias_spec],
        out_specs=pl.BlockSpec((1, S, LANES), lambda b, p: (b, 0, p)),
        out_shape=jax.ShapeDtypeStruct((B, S, d_att), BF16),
        scratch_shapes=[pltpu.VMEM((S, LANES), F32), pltpu.VMEM((S, LANES), F32)],
        compiler_params=_params(("parallel", "parallel")),
        name="attn",
    )(qkv, qkv, qkv, *biases)


def _mix_kernel(x_ref, yr_ref, att_ref, p_ref, wor_ref, woa_ref, g_ref, b_ref, rhi_ref, rlo_ref,
                rb_ref, pg_ref, pp_ref, x1_ref, lg_ref, ple_ref, *, alpha):
    mix = _dot(yr_ref[...], wor_ref[...]) + _dot(att_ref[...], woa_ref[...])
    x1 = _layer_norm(alpha * x_ref[...] + mix, g_ref[...], b_ref[...])
    x1_ref[...] = x1
    hi, lo = _split(x1)
    lg_ref[...] = (jnp.dot(hi, rhi_ref[...], preferred_element_type=F32)
                   + jnp.dot(lo, rhi_ref[...], preferred_element_type=F32)
                   + jnp.dot(hi, rlo_ref[...], preferred_element_type=F32) + rb_ref[...])
    ple_ref[...] = _sigmoid(_dot(hi, pg_ref[...])) * _dot(p_ref[...], pp_ref[...])


def _stage_mix(x2, yr, att, p2, w_o, ln_g, ln_b, router_g, router_g_b, router_e, router_e_b,
               ple_gate, ple_proj, alpha):
    N, D = x2.shape
    d_r = yr.shape[1]
    rows = min(MIX_ROWS, N)
    d_att = att.shape[1]
    n_log = N_GROUPS + N_EXPERTS
    rw = jnp.zeros((D, ROUTER_LANES), F32).at[:, :N_GROUPS].set(router_g)
    rw = rw.at[:, N_GROUPS:n_log].set(router_e)
    rb = jnp.zeros((1, ROUTER_LANES), F32).at[0, :N_GROUPS].set(router_g_b)
    rb = rb.at[0, N_GROUPS:n_log].set(router_e_b)
    rhi, rlo = _split(rw)
    const = lambda shape: pl.BlockSpec(shape, lambda i: (0,) * len(shape))
    tile = lambda c: pl.BlockSpec((rows, c), lambda i: (i, 0))
    return pl.pallas_call(
        functools.partial(_mix_kernel, alpha=alpha),
        grid=(N // rows,),
        in_specs=[tile(D), tile(d_r), tile(d_att), tile(p2.shape[1]),
                  const((d_r, D)), const((d_att, D)), const((1, D)), const((1, D)),
                  const((D, ROUTER_LANES)), const((D, ROUTER_LANES)), const((1, ROUTER_LANES)),
                  const((D, D)), const(ple_proj.shape)],
        out_specs=[tile(D), tile(ROUTER_LANES), tile(D)],
        out_shape=[jax.ShapeDtypeStruct((N, D), F32),
                   jax.ShapeDtypeStruct((N, ROUTER_LANES), F32),
                   jax.ShapeDtypeStruct((N, D), F32)],
        compiler_params=_params(("parallel",)),
        name="mix",
    )(x2, yr, att, p2, w_o[:d_r].astype(BF16), w_o[d_r:].astype(BF16),
      ln_g.reshape(1, D), ln_b.reshape(1, D), rhi, rlo, rb, ple_gate.astype(BF16),
      ple_proj.astype(BF16))


def _route(logits, rows):
    N = logits.shape[0]
    lg = logits[:, :N_GROUPS]
    le = logits[:, N_GROUPS:N_GROUPS + N_EXPERTS]
    pg = jax.nn.softmax(lg, -1)
    grp = jnp.argmax(lg, -1)
    chosen = grp[:, None] == jnp.arange(N_GROUPS)[None, :]
    wg = jnp.sum(jnp.where(chosen, pg, 0.0), axis=-1, keepdims=True)
    le = jnp.sum(jnp.where(chosen[:, :, None], le.reshape(N, N_GROUPS, EXPERTS_PER_GROUP), 0.0),
                 axis=1)
    top_v, top_i = lax.top_k(le, TOP_K)
    we = jax.nn.softmax(top_v, -1) * wg
    eid = grp[:, None] * EXPERTS_PER_GROUP + top_i
    A = N * TOP_K
    e_flat = eid.T.reshape(A).astype(jnp.int32)
    order = jnp.argsort(e_flat).astype(jnp.int32)
    counts = jnp.sum((e_flat[:, None] == jnp.arange(N_EXPERTS)[None, :]).astype(jnp.int32), axis=0)
    start = jnp.cumsum(counts) - counts
    padded = (counts + rows - 1) // rows * rows
    pend = jnp.cumsum(padded)
    pstart = pend - padded
    nblk = -(-A // rows) + N_EXPERTS
    nused = (pend[-1] // rows).astype(jnp.int32)
    blk = jnp.arange(nblk, dtype=jnp.int32)
    blk_e = jnp.sum((pend[None, :] <= (blk * rows)[:, None]).astype(jnp.int32), axis=1)
    blk_e = jnp.minimum(blk_e, N_EXPERTS - 1)
    blk_e = jnp.where(blk < nused, blk_e, blk_e[nused - 1]).astype(jnp.int32)
    r = jnp.arange(rows, dtype=jnp.int32)[None, :]
    off = (blk * rows - pstart[blk_e])[:, None] + r
    real = (off < counts[blk_e][:, None]) & (blk < nused)[:, None]
    asg = order[jnp.clip(start[blk_e][:, None] + off, 0, A - 1)]
    row_tok = jnp.where(real, asg % N, r).astype(jnp.int32)
    row_dst = jnp.where(real, asg, A + r).astype(jnp.int32)
    return (row_tok.reshape(nblk, 1, rows), row_dst.reshape(nblk, 1, rows), blk_e,
            nused.reshape(1), we)


def _moe_kernel(blk_e_ref, nused_ref, tok0_ref, tokn_ref, dstp_ref, x_hbm, wg_ref, wu_ref, wd_ref,
                y_hbm, xbuf, ybuf, gsem, ssem, *, rows, n_real):
    i = pl.program_id(0)
    nused = nused_ref[0]
    slot = i % 2

    def gather_row(tab_ref, sl, r):
        return pltpu.make_async_copy(x_hbm.at[pl.ds(tab_ref[0, 0, r], 1)],
                                     xbuf.at[sl, pl.ds(r, 1)], gsem.at[sl])

    def scatter_row(sl, r):
        t = dstp_ref[0, 0, r]
        dst = jnp.where(t >= n_real, t + sl * rows, t)
        return pltpu.make_async_copy(ybuf.at[sl, pl.ds(r, 1)], y_hbm.at[pl.ds(dst, 1)],
                                     ssem.at[sl])

    def gather_all(sl):
        return pltpu.make_async_copy(x_hbm.at[pl.ds(0, rows)], xbuf.at[sl], gsem.at[sl])

    def scatter_all(sl, first_row=0):
        return pltpu.make_async_copy(ybuf.at[sl], y_hbm.at[pl.ds(first_row, rows)], ssem.at[sl])

    def rolled(fn):
        def body(r, c):
            fn(r)
            return c
        lax.fori_loop(0, rows, body, 0, unroll=8)

    def inline(fn):
        for r in range(rows):
            fn(r)

    def ffn(sl):
        xb = xbuf[sl].astype(BF16)
        gate = _dot(xb, wg_ref[0])
        up = _dot(xb, wu_ref[0])
        ybuf[sl] = _dot(gate * _sigmoid(gate) * up, wd_ref[0])

    @pl.when(i == 0)
    def _():
        ybuf[...] = jnp.zeros_like(ybuf)
        for sl in range(2):
            scatter_all(sl, n_real + sl * rows).start()
        for sl in range(2):
            scatter_all(sl, n_real + sl * rows).wait()
        rolled(lambda r: gather_row(tok0_ref, 0, r).start())

    @pl.when(i < nused)
    def _():
        gather_all(slot).wait()

    @pl.when((i >= 2) & (i <= nused))
    def _():
        scatter_all(slot).wait()

    @pl.when(i == 0)
    def _():
        inline(lambda r: gather_row(tokn_ref, 1, r).start())
        ffn(0)

    @pl.when((i >= 1) & (i < nused))
    def _():
        inline(lambda r: gather_row(tokn_ref, 1 - slot, r).start())
        inline(lambda r: scatter_row(1 - slot, r).start())
        ffn(slot)

    @pl.when(i == nused)
    def _():
        rolled(lambda r: scatter_row(1 - slot, r).start())
        scatter_all(1 - slot).wait()
        gather_all(slot).wait()


def _stage_moe(x1, row_tok, row_dst, blk_e, nused, w_gate, w_up, w_down):
    N, D = x1.shape
    nblk, _, rows = row_tok.shape
    d_e = w_gate.shape[2]
    n_real = N * TOP_K
    last = nblk - 1
    wspec = lambda shape: pl.BlockSpec((1,) + shape,
                                       lambda i, be, nu: (be[jnp.minimum(i, last)], 0, 0))
    table = lambda fn: pl.BlockSpec((1, 1, rows), lambda i, be, nu: (fn(i), 0, 0),
                                    memory_space=pltpu.SMEM)
    grid_spec = pltpu.PrefetchScalarGridSpec(
        num_scalar_prefetch=2,
        grid=(nblk + 1,),
        in_specs=[table(lambda i: 0), table(lambda i: jnp.minimum(i + 1, last)),
                  table(lambda i: jnp.clip(i - 1, 0, last)),
                  pl.BlockSpec(memory_space=pl.ANY), wspec((D, d_e)), wspec((D, d_e)),
                  wspec((d_e, D))],
        out_specs=pl.BlockSpec(memory_space=pl.ANY),
        scratch_shapes=[pltpu.VMEM((2, rows, D), F32), pltpu.VMEM((2, rows, D), F32),
                        pltpu.SemaphoreType.DMA((2,)), pltpu.SemaphoreType.DMA((2,))],
    )
    return pl.pallas_call(
        functools.partial(_moe_kernel, rows=rows, n_real=n_real),
        grid_spec=grid_spec,
        out_shape=jax.ShapeDtypeStruct((n_real + 2 * rows, D), F32),
        compiler_params=_params(("arbitrary",)),
        name="moe",
    )(blk_e, nused, row_tok, row_tok, row_dst, x1, w_gate, w_up, w_down)


def _out_kernel(x1_ref, y0_ref, y1_ref, we_ref, ple_ref, g_ref, b_ref, o_ref, *, alpha):
    we = we_ref[...]
    moe = we[:, 0:1] * y0_ref[...] + we[:, 1:2] * y1_ref[...]
    o_ref[...] = _layer_norm(alpha * x1_ref[...] + moe + ple_ref[...], g_ref[...], b_ref[...])


def _stage_out(x1, yb, we, ple, ln_g, ln_b, alpha):
    N, D = x1.shape
    rows = min(OUT_ROWS, N)
    const = lambda shape: pl.BlockSpec(shape, lambda i: (0,) * len(shape))
    tile = lambda c: pl.BlockSpec((rows, c), lambda i: (i, 0))
    slot = lambda j: pl.BlockSpec((rows, D), lambda i: (i + j * (N // rows), 0))
    return pl.pallas_call(
        functools.partial(_out_kernel, alpha=alpha),
        grid=(N // rows,),
        in_specs=[tile(D), slot(0), slot(1), tile(TOP_K), tile(D), const((1, D)), const((1, D))],
        out_specs=tile(D),
        out_shape=jax.ShapeDtypeStruct((N, D), F32),
        compiler_params=_params(("parallel",)),
        name="out",
    )(x1, yb, yb, we, ple, ln_g.reshape(1, D), ln_b.reshape(1, D))


def kernel(x, p, w_in, mu_rkv, mu_lora, w0, w_lora1, w_lora2, a0, a_lora1, a_lora2, g_lora1, g_lora2, k_k, k_a, r_k, lnx_g, lnx_b, rel_bias, w_o, ln1_g, ln1_b, router_g, router_g_b, router_e, router_e_b, w_gate, w_up, w_down, ple_gate, ple_proj, ln2_g, ln2_b):
    B, S, D = x.shape
    depth = w_in.shape[0]
    alpha = (2 * depth) ** 0.25
    for i in range(depth):
        rkv, qkv, lw, ai, g = _stage_proj(x, w_in[i], w_lora1[i], a_lora1[i], g_lora1[i],
                                          mu_lora[i], w_lora2[i], a_lora2[i], g_lora2[i],
                                          w0[i], a0[i])
        yr = _stage_wkv(rkv, lw, ai, g, mu_rkv[i], k_k[i], k_a[i], r_k[i], lnx_g[i], lnx_b[i])
        att = _stage_attn(qkv, rel_bias).reshape(B * S, -1)
        x1, logits, ple = _stage_mix(x.reshape(B * S, D), yr.reshape(B * S, -1), att,
                                     p[i].reshape(B * S, -1), w_o[i], ln1_g[i], ln1_b[i],
                                     router_g[i], router_g_b[i], router_e[i], router_e_b[i],
                                     ple_gate[i], ple_proj[i], alpha)
        row_tok, row_dst, blk_e, nused, we = _route(logits, MOE_ROWS)
        yb = _stage_moe(x1, row_tok, row_dst, blk_e, nused, w_gate[i], w_up[i], w_down[i])
        x = _stage_out(x1, yb, we, ple, ln2_g[i], ln2_b[i], alpha).reshape(B, S, D)
    return x
```

```python
import functools
import math

import jax
import jax.numpy as jnp
import numpy as np
from jax import lax
from jax.experimental import pallas as pl
from jax.experimental.pallas import tpu as pltpu

F32 = jnp.float32
BF16 = jnp.bfloat16

HEAD_DIM = 64
LANES = 128
DECAY_LORA = 64
ICLR_LORA = 64
GATE_LORA = 128
LNX_EPS = 64e-5
DIL_PATTERNS = ((128, 1), (512, 4), (2048, 16))
NEG_INF = -1e30
N_BUCKETS = 32
MAX_DISTANCE = 2048
N_GROUPS = 4
EXPERTS_PER_GROUP = 8
N_EXPERTS = N_GROUPS * EXPERTS_PER_GROUP
TOP_K = 2
LN_EPS = 1e-5

WKV_CHUNK = 64
WKV_CHUNKS_PER_STEP = 4
ATT_BLK = 128
PROJ_ROWS = 512
MIX_ROWS = 256
MOE_ROWS = 256
OUT_ROWS = 256
ROUTER_LANES = 128
VMEM_LIMIT = 56 * 1024 * 1024


def _params(sem):
    return pltpu.CompilerParams(dimension_semantics=sem, vmem_limit_bytes=VMEM_LIMIT)


def _dot(a, b):
    return jnp.dot(a.astype(BF16), b.astype(BF16), preferred_element_type=F32)


def _dot_nt(a, b):
    return lax.dot_general(a.astype(BF16), b.astype(BF16), (((1,), (1,)), ((), ())),
                           preferred_element_type=F32)


def _dot_tn(a, b):
    return lax.dot_general(a.astype(BF16), b.astype(BF16), (((0,), (0,)), ((), ())),
                           preferred_element_type=F32)


def _split(x):
    hi = x.astype(BF16)
    lo = (x - hi.astype(F32)).astype(BF16)
    return hi, lo


def _dot_exact_lhs(w01, x):
    hi, lo = _split(x)
    return (jnp.dot(w01, hi, preferred_element_type=F32)
            + jnp.dot(w01, lo, preferred_element_type=F32))


def _sigmoid(x):
    return 1.0 / (1.0 + jnp.exp(-x))


def _shift_rows(t, prev_row):
    rolled = pltpu.roll(t, 1, 0)
    row = lax.broadcasted_iota(jnp.int32, t.shape, 0)
    return jnp.where(row == 0, prev_row, rolled)


def _layer_norm(z, g, b):
    mu = jnp.mean(z, axis=-1, keepdims=True)
    zc = z - mu
    var = jnp.mean(zc * zc, axis=-1, keepdims=True)
    return zc * lax.rsqrt(var + LN_EPS) * g + b


def _proj_kernel(x_ref, win_ref, w1_ref, a1_ref, g1_ref, mul_ref, w2_ref, a2_ref, g2_ref,
                 w0_ref, a0_ref, rkv_ref, qkv_ref, lw_ref, ai_ref, g_ref, prev_ref, *, d_rkv):
    @pl.when(pl.program_id(1) == 0)
    def _():
        prev_ref[...] = jnp.zeros_like(prev_ref)

    h = x_ref[0]
    rows = h.shape[0]
    hprev = _shift_rows(h, prev_ref[0:1, :])
    prev_ref[0:1, :] = h[rows - 1:rows, :]

    proj = _dot(h, win_ref[...])
    rkv_ref[0] = proj[:, :d_rkv]
    qkv_ref[0] = proj[:, d_rkv:]

    dh = hprev - h
    xw = h + dh * mul_ref[0:1, :]
    xa = h + dh * mul_ref[1:2, :]
    xg = h + dh * mul_ref[2:3, :]
    wl = w0_ref[...] + _dot(jnp.tanh(_dot(xw, w1_ref[...])), w2_ref[...])
    z = -wl
    softplus = jnp.maximum(z, 0.0) + jnp.log(1.0 + jnp.exp(-jnp.abs(z)))
    lw_ref[0] = -jnp.exp(-softplus - 0.5)
    ai_ref[0] = _sigmoid(a0_ref[...] + _dot(_dot(xa, a1_ref[...]), a2_ref[...]))
    g_ref[0] = _dot(_sigmoid(_dot(xg, g1_ref[...])), g2_ref[...])


def _stage_proj(x, w_in, w1, a1, g1, mu_lora, w2, a2, g2, w0, a0):
    B, S, D = x.shape
    d_in = w_in.shape[1]
    d_r = w2.shape[1]
    d_rkv = 3 * d_r
    d_att = d_in - d_rkv
    rows = min(PROJ_ROWS, S)
    const = lambda shape: pl.BlockSpec(shape, lambda b, j: (0,) * len(shape))
    tile = lambda c: pl.BlockSpec((1, rows, c), lambda b, j: (b, j, 0))
    return pl.pallas_call(
        functools.partial(_proj_kernel, d_rkv=d_rkv),
        grid=(B, S // rows),
        in_specs=[tile(D), const((D, d_in)), const(w1.shape), const(a1.shape), const(g1.shape),
                  const(mu_lora.shape), const(w2.shape), const(a2.shape), const(g2.shape),
                  const((1, d_r)), const((1, d_r))],
        out_specs=[tile(d_rkv), tile(d_att), tile(d_r), tile(d_r), tile(d_r)],
        out_shape=[jax.ShapeDtypeStruct((B, S, d_rkv), F32),
                   jax.ShapeDtypeStruct((B, S, d_att), F32),
                   jax.ShapeDtypeStruct((B, S, d_r), F32),
                   jax.ShapeDtypeStruct((B, S, d_r), F32),
                   jax.ShapeDtypeStruct((B, S, d_r), F32)],
        scratch_shapes=[pltpu.VMEM((8, D), F32)],
        compiler_params=_params(("parallel", "arbitrary")),
        name="proj",
    )(x, w_in.astype(BF16), w1.astype(BF16), a1.astype(BF16), g1.astype(BF16), mu_lora,
      w2.astype(BF16), a2.astype(BF16), g2.astype(BF16), w0.reshape(1, d_r), a0.reshape(1, d_r))


def _wkv_kernel(rkv_ref, lw_ref, ai_ref, g_ref, mu_ref, kk_ref, ka_ref, rk_ref, lng_ref, lnb_ref,
                seg_ref, tril_ref, y_ref, s_ref, prev_ref, *, d_r, n_chunks):
    C = WKV_CHUNK
    rows = C * n_chunks

    @pl.when(pl.program_id(1) == 0)
    def _():
        s_ref[...] = jnp.zeros_like(s_ref)
        prev_ref[...] = jnp.zeros_like(prev_ref)

    rkv = rkv_ref[0]
    prev = _shift_rows(rkv, prev_ref[0:1, :])
    prev_ref[0:1, :] = rkv[rows - 1:rows, :]
    mixed = rkv + (prev - rkv) * mu_ref[...]
    r = mixed[:, :d_r]
    k = mixed[:, d_r:2 * d_r]
    v = mixed[:, 2 * d_r:]
    lw = lw_ref[0]
    a = ai_ref[0]
    seg = seg_ref[...]

    kk = k * kk_ref[...]
    kk = kk * lax.rsqrt(jnp.maximum(_dot(kk * kk, seg), 1e-24))
    k = k * (1.0 + (a - 1.0) * ka_ref[...])
    aa = -kk
    bb = kk * a
    c = _dot_exact_lhs(tril_ref[...], lw)
    e_neg = jnp.exp(-c)
    a_t = aa * jnp.exp(c - lw)
    b_t = bb * e_neg
    k_t = k * e_neg
    r_t = r * jnp.exp(c)

    lane = lax.broadcasted_iota(jnp.int32, (1, LANES), 1)
    head0 = lane < HEAD_DIM
    ri = lax.broadcasted_iota(jnp.int32, (2 * C, 2 * C), 0)
    ci = lax.broadcasted_iota(jnp.int32, (2 * C, 2 * C), 1)
    strict = (ri % C) > (ci % C)
    incl = (ri % C) >= (ci % C)
    eye = (ri == ci).astype(F32)
    zero_bf = jnp.zeros((2 * C, LANES), BF16)

    def stack(t):
        return jnp.concatenate([jnp.where(head0, t, 0.0), jnp.where(head0, 0.0, t)],
                               axis=0).astype(BF16)

    def fold(t):
        return t[:C] + t[C:]

    n_pairs = d_r // LANES
    folds = [(q, p) for q in range(n_chunks) for p in range(n_pairs)]
    each = lambda fn, *cols: [fn(*args) for args in zip(*cols)]
    cat0 = lambda *ts: jnp.concatenate(ts, axis=0)
    cat1 = lambda *ts: jnp.concatenate(ts, axis=1)
    left = lambda t: t[:, :2 * C]
    right = lambda t: t[:, 2 * C:]

    p_ends, b_es, k_es = [], [], []
    for q in range(n_chunks):
        rs = slice(q * C, (q + 1) * C)
        c_end = c[rs][C - 1:C, :]
        e_end = jnp.exp(c_end - c[rs])
        b_es.append(bb[rs] * e_end)
        k_es.append(k[rs] * e_end)
        p_ends.append(jnp.exp(c_end))

    def stacks(t_of_q):
        return [stack(t_of_q(q)[:, p * LANES:(p + 1) * LANES]) for q, p in folds]

    chunk = lambda t: (lambda q: t[q * C:(q + 1) * C])
    am, bm, km, rm, vm = (stacks(chunk(t)) for t in (a_t, b_t, k_t, r_t, v))
    bem = stacks(lambda q: b_es[q])
    kem = stacks(lambda q: k_es[q])

    x = each(lambda a_, r_, b_, k_: _dot_nt(cat0(a_, r_), cat0(b_, k_)), am, rm, bm, km)
    n = each(lambda t: jnp.where(strict, t[:2 * C, :2 * C], 0.0), x)
    a_ak = each(lambda t: jnp.where(strict, t[:2 * C, 2 * C:], 0.0).astype(BF16), x)
    a_rb = each(lambda t: jnp.where(incl, t[2 * C:, :2 * C], 0.0).astype(BF16), x)
    a_rk = each(lambda t: jnp.where(incl, t[2 * C:, 2 * C:], 0.0).astype(BF16), x)
    t_inv = each(lambda t: eye + t, n)
    m = each(lambda t: _dot(t, t), n)
    av = each(lambda a_, v_: _dot(a_, v_).astype(BF16), a_ak, vm)
    for _ in range(int(math.log2(C)) - 2):
        z = each(lambda m_, t_: _dot(m_, cat1(m_.astype(BF16), t_.astype(BF16))), m, t_inv)
        t_inv = each(lambda t_, z_: t_ + right(z_), t_inv, z)
        m = each(left, z)
    t_inv = each(lambda t_, m_: t_ + _dot(m_, t_), t_inv, m)
    z = each(lambda t_, a_, av_: _dot(t_, cat1(a_, av_)), t_inv, am, av)
    ap = each(lambda z_: left(z_).astype(BF16), z)
    u0 = each(lambda z_: right(z_).astype(BF16), z)
    z = each(lambda rb_, rk_, ap_, u_, v_: _dot(cat1(rb_, rk_),
                                               cat0(cat1(ap_, u_), cat1(zero_bf, v_))),
             a_rb, a_rk, ap, u0, vm)
    g_m = each(lambda ap_, be_: _dot_tn(ap_, be_).astype(BF16), ap, bem)
    d0 = each(lambda u_, v_, be_, ke_: _dot_tn(cat0(u_, v_), cat0(be_, ke_)), u0, vm, bem, kem)
    rp = each(lambda r_, z_: (r_.astype(F32) + left(z_)).astype(BF16), rm, z)
    y0 = each(lambda z_: fold(right(z_)), z)

    s = [s_ref[p] for p in range(n_pairs)]
    y_cols = []
    for q in range(n_chunks):
        s_bf = [t.astype(BF16) for t in s]
        f0 = q * n_pairs
        y_cols.append([fold(_dot_nt(rp[f0 + p], s_bf[p])) + y0[f0 + p] for p in range(n_pairs)])
        s = [s[p] * p_ends[q][:, p * LANES:(p + 1) * LANES] + _dot(s_bf[p], g_m[f0 + p])
             + d0[f0 + p] for p in range(n_pairs)]
    for p in range(n_pairs):
        s_ref[p] = s[p]
    y = cat0(*[cat1(*cols) for cols in y_cols])

    inv = 1.0 / HEAD_DIM
    mu = _dot(y, seg) * inv
    yc = y - mu
    var = _dot(yc * yc, seg) * inv
    yn = yc * lax.rsqrt(var + LNX_EPS) * lng_ref[...] + lnb_ref[...]
    bonus = _dot(r * k * rk_ref[...], seg) * v
    y_ref[0] = ((yn + bonus) * g_ref[0]).astype(y_ref.dtype)


def _stage_wkv(rkv, lw, ai, g, mu_rkv, k_k, k_a, r_k, lnx_g, lnx_b):
    B, S, d_r = lw.shape
    n_chunks = WKV_CHUNKS_PER_STEP
    rows = WKV_CHUNK * n_chunks
    head = np.arange(d_r) // HEAD_DIM
    seg = jnp.asarray(head[:, None] == head[None, :], BF16)
    t = np.arange(rows)
    tril = jnp.asarray((t[:, None] >= t[None, :])
                       & (t[:, None] // WKV_CHUNK == t[None, :] // WKV_CHUNK), BF16)
    const = lambda shape: pl.BlockSpec(shape, lambda b, j: (0,) * len(shape))
    tile = lambda c: pl.BlockSpec((1, rows, c), lambda b, j: (b, j, 0))
    row = lambda t: t.reshape(1, -1)
    return pl.pallas_call(
        functools.partial(_wkv_kernel, d_r=d_r, n_chunks=n_chunks),
        grid=(B, S // rows),
        in_specs=[tile(3 * d_r), tile(d_r), tile(d_r), tile(d_r), const((1, 3 * d_r)),
                  const((1, d_r)), const((1, d_r)), const((1, d_r)), const((1, d_r)),
                  const((1, d_r)), const((d_r, d_r)), const((rows, rows))],
        out_specs=tile(d_r),
        out_shape=jax.ShapeDtypeStruct((B, S, d_r), BF16),
        scratch_shapes=[pltpu.VMEM((d_r // LANES, LANES, LANES), F32),
                        pltpu.VMEM((8, 3 * d_r), F32)],
        compiler_params=_params(("parallel", "arbitrary")),
        name="wkv",
    )(rkv, lw, ai, g, row(mu_rkv), row(k_k), row(k_a), row(r_k), row(lnx_g), row(lnx_b), seg, tril)


def _t5_bucket(n):
    exact = N_BUCKETS // 2
    nf = np.maximum(n, 1).astype(np.float32)
    large = exact + (np.log(nf / exact) / math.log(MAX_DISTANCE / exact)
                     * (N_BUCKETS - exact)).astype(np.int32)
    large = np.minimum(large, N_BUCKETS - 1)
    return np.where(n < exact, n, large).astype(np.int32)


def _band_bias(rel_bias, dil):
    blk = ATT_BLK
    L = 3 * blk
    rel = (2 * blk - 1) - np.arange(L)
    valid = (rel >= 0) & (rel <= blk)
    base = rel_bias[_t5_bucket(np.clip(rel, 0, None) * dil)].astype(F32).T
    base = jnp.where(valid[None], base, NEG_INF)
    skew = jnp.tile(base, (1, blk))[:, :blk * (L - 1)].reshape(-1, blk, L - 1)
    return skew[:, :, blk - 1:3 * blk - 1]


def _attend(units, bias_ref, scale):
    lane = lax.broadcasted_iota(jnp.int32, (1, LANES), 1)
    head0 = lane < HEAD_DIM
    jobs = [(u, h) for u in range(len(units)) for h in range(2)]
    s = []
    for u, h in jobs:
        q, k, _, col0 = units[u]
        qh = jnp.where(head0 if h == 0 else jnp.logical_not(head0), q * scale, 0.0)
        s.append(_dot_nt(qh, k) + bias_ref[h, :, col0:col0 + k.shape[0]])
    m = [jnp.max(t, axis=-1, keepdims=True) for t in s]
    e = [jnp.exp(t - m_) for t, m_ in zip(s, m)]
    l = [jnp.sum(t, axis=-1, keepdims=True) for t in e]
    o = [_dot(t, units[u][2]) / l_ for t, l_, (u, h) in zip(e, l, jobs)]
    lse = [m_ + jnp.log(l_) for m_, l_ in zip(m, l)]
    return [(jnp.where(head0, o[2 * u], o[2 * u + 1]), jnp.where(head0, lse[2 * u], lse[2 * u + 1]))
            for u in range(len(units))]


def _attn_kernel(q_ref, k_ref, v_ref, b1_ref, b4_ref, b16_ref, o_ref, acc_o, acc_l, *, seq, scale):
    blk = ATT_BLK
    group = 4

    def rows(ref, start, n, dil):
        if dil == 1:
            return ref[0, pl.ds(start, n), :]
        return ref[0, pl.ds(start, n, stride=dil), :]

    def unit(start, dil, has_prev):
        q = rows(q_ref, start, blk, dil)
        if has_prev:
            k0 = start - blk * dil
            return (q, rows(k_ref, k0, 2 * blk, dil), rows(v_ref, k0, 2 * blk, dil), 0)
        return (q, rows(k_ref, start, blk, dil), rows(v_ref, start, blk, dil), blk)

    def acc_rows(ref, start, dil):
        if dil == 1:
            return ref.at[pl.ds(start, blk), :]
        return ref.at[pl.ds(start, blk, stride=dil), :]

    def run(starts, dil, prevs, bias_ref, first, last):
        units = [unit(s, dil, hp) for s, hp in zip(starts, prevs)]
        for s, (o, lse) in zip(starts, _attend(units, bias_ref, scale)):
            ao, al = acc_rows(acc_o, s, dil), acc_rows(acc_l, s, dil)
            if first:
                ao[...] = o
                al[...] = lse
            else:
                o_old, l_old = ao[...], al[...]
                m = jnp.maximum(l_old, lse)
                e_old, e_new = jnp.exp(l_old - m), jnp.exp(lse - m)
                ao[...] = (e_old * o_old + e_new * o) / (e_old + e_new)
                if not last:
                    al[...] = m + jnp.log(e_old + e_new)

    run([n * blk for n in range(group)], 1, [n > 0 for n in range(group)], b1_ref, True, False)

    def body1(g, c):
        base = pl.multiple_of(g * (group * blk), group * blk)
        run([base + n * blk for n in range(group)], 1, [True] * group, b1_ref, True, False)
        return c
    lax.fori_loop(1, seq // (group * blk), body1, 0)

    nb4 = seq // 4 // blk

    def body4(r, c):
        run([r + n * blk * 4 for n in range(nb4)], 4, [n > 0 for n in range(nb4)], b4_ref,
            False, False)
        return c
    lax.fori_loop(0, 4, body4, 0)

    def body16(g, c):
        run([g * group + j for j in range(group)], 16, [False] * group, b16_ref, False, True)
        return c
    lax.fori_loop(0, 16 // group, body16, 0)

    o_ref[0] = acc_o[...].astype(o_ref.dtype)


def _stage_attn(qkv, rel_bias):
    B, S, d3 = qkv.shape
    d_att = d3 // 3
    n_pairs = d_att // LANES
    assert DIL_PATTERNS == ((ATT_BLK, 1), (4 * ATT_BLK, 4), (16 * ATT_BLK, 16))
    assert S == 16 * ATT_BLK
    biases = [_band_bias(rel_bias, dil) for _, dil in DIL_PATTERNS]
    part = lambda i: pl.BlockSpec((1, S, LANES), lambda b, p: (b, 0, i * n_pairs + p))
    bias_spec = pl.BlockSpec((2, ATT_BLK, 2 * ATT_BLK), lambda b, p: (p, 0, 0))
    return pl.pallas_call(
        functools.partial(_attn_kernel, seq=S, scale=HEAD_DIM ** -0.5),
        grid=(B, n_pairs),
        in_specs=[part(0), part(1), part(2), bias_spec, bias_spec, bias_spec],
        out_specs=pl.BlockSpec((1, S, LANES), lambda b, p: (b, 0, p)),
        out_shape=jax.ShapeDtypeStruct((B, S, d_att), BF16),
        scratch_shapes=[pltpu.VMEM((S, LANES), F32), pltpu.VMEM((S, LANES), F32)],
        compiler_params=_params(("parallel", "parallel")),
        name="attn",
    )(qkv, qkv, qkv, *biases)


def _mix_kernel(x_ref, yr_ref, att_ref, p_ref, wor_ref, woa_ref, g_ref, b_ref, rhi_ref, rlo_ref,
                rb_ref, pg_ref, pp_ref, x1_ref, lg_ref, ple_ref, *, alpha):
    mix = _dot(yr_ref[...], wor_ref[...]) + _dot(att_ref[...], woa_ref[...])
    x1 = _layer_norm(alpha * x_ref[...] + mix, g_ref[...], b_ref[...])
    x1_ref[...] = x1
    hi, lo = _split(x1)
    lg_ref[...] = (jnp.dot(hi, rhi_ref[...], preferred_element_type=F32)
                   + jnp.dot(lo, rhi_ref[...], preferred_element_type=F32)
                   + jnp.dot(hi, rlo_ref[...], preferred_element_type=F32) + rb_ref[...])
    ple_ref[...] = _sigmoid(_dot(hi, pg_ref[...])) * _dot(p_ref[...], pp_ref[...])


def _stage_mix(x2, yr, att, p2, w_o, ln_g, ln_b, router_g, router_g_b, router_e, router_e_b,
               ple_gate, ple_proj, alpha):
    N, D = x2.shape
    d_r = yr.shape[1]
    rows = min(MIX_ROWS, N)
    d_att = att.shape[1]
    n_log = N_GROUPS + N_EXPERTS
    rw = jnp.zeros((D, ROUTER_LANES), F32).at[:, :N_GROUPS].set(router_g)
    rw = rw.at[:, N_GROUPS:n_log].set(router_e)
    rb = jnp.zeros((1, ROUTER_LANES), F32).at[0, :N_GROUPS].set(router_g_b)
    rb = rb.at[0, N_GROUPS:n_log].set(router_e_b)
    rhi, rlo = _split(rw)
    const = lambda shape: pl.BlockSpec(shape, lambda i: (0,) * len(shape))
    tile = lambda c: pl.BlockSpec((rows, c), lambda i: (i, 0))
    return pl.pallas_call(
        functools.partial(_mix_kernel, alpha=alpha),
        grid=(N // rows,),
        in_specs=[tile(D), tile(d_r), tile(d_att), tile(p2.shape[1]),
                  const((d_r, D)), const((d_att, D)), const((1, D)), const((1, D)),
                  const((D, ROUTER_LANES)), const((D, ROUTER_LANES)), const((1, ROUTER_LANES)),
                  const((D, D)), const(ple_proj.shape)],
        out_specs=[tile(D), tile(ROUTER_LANES), tile(D)],
        out_shape=[jax.ShapeDtypeStruct((N, D), F32),
                   jax.ShapeDtypeStruct((N, ROUTER_LANES), F32),
                   jax.ShapeDtypeStruct((N, D), F32)],
        compiler_params=_params(("parallel",)),
        name="mix",
    )(x2, yr, att, p2, w_o[:d_r].astype(BF16), w_o[d_r:].astype(BF16),
      ln_g.reshape(1, D), ln_b.reshape(1, D), rhi, rlo, rb, ple_gate.astype(BF16),
      ple_proj.astype(BF16))


def _route(logits, rows):
    N = logits.shape[0]
    lg = logits[:, :N_GROUPS]
    le = logits[:, N_GROUPS:N_GROUPS + N_EXPERTS]
    pg = jax.nn.softmax(lg, -1)
    grp = jnp.argmax(lg, -1)
    chosen = grp[:, None] == jnp.arange(N_GROUPS)[None, :]
    wg = jnp.sum(jnp.where(chosen, pg, 0.0), axis=-1, keepdims=True)
    le = jnp.sum(jnp.where(chosen[:, :, None], le.reshape(N, N_GROUPS, EXPERTS_PER_GROUP), 0.0),
                 axis=1)
    top_v, top_i = lax.top_k(le, TOP_K)
    we = jax.nn.softmax(top_v, -1) * wg
    eid = grp[:, None] * EXPERTS_PER_GROUP + top_i
    A = N * TOP_K
    e_flat = eid.T.reshape(A).astype(jnp.int32)
    order = jnp.argsort(e_flat).astype(jnp.int32)
    counts = jnp.sum((e_flat[:, None] == jnp.arange(N_EXPERTS)[None, :]).astype(jnp.int32), axis=0)
    start = jnp.cumsum(counts) - counts
    padded = (counts + rows - 1) // rows * rows
    pend = jnp.cumsum(padded)
    pstart = pend - padded
    nblk = -(-A // rows) + N_EXPERTS
    nused = (pend[-1] // rows).astype(jnp.int32)
    blk = jnp.arange(nblk, dtype=jnp.int32)
    blk_e = jnp.sum((pend[None, :] <= (blk * rows)[:, None]).astype(jnp.int32), axis=1)
    blk_e = jnp.minimum(blk_e, N_EXPERTS - 1)
    blk_e = jnp.where(blk < nused, blk_e, blk_e[nused - 1]).astype(jnp.int32)
    r = jnp.arange(rows, dtype=jnp.int32)[None, :]
    off = (blk * rows - pstart[blk_e])[:, None] + r
    real = (off < counts[blk_e][:, None]) & (blk < nused)[:, None]
    asg = order[jnp.clip(start[blk_e][:, None] + off, 0, A - 1)]
    row_tok = jnp.where(real, asg % N, r).astype(jnp.int32)
    row_dst = jnp.where(real, asg, A + r).astype(jnp.int32)
    return (row_tok.reshape(nblk, 1, rows), row_dst.reshape(nblk, 1, rows), blk_e,
            nused.reshape(1), we)


def _moe_kernel(blk_e_ref, nused_ref, tok0_ref, tokn_ref, dstp_ref, x_hbm, wg_ref, wu_ref, wd_ref,
                y_hbm, xbuf, ybuf, gsem, ssem, *, rows, n_real):
    i = pl.program_id(0)
    nused = nused_ref[0]
    slot = i % 2

    def gather_row(tab_ref, sl, r):
        return pltpu.make_async_copy(x_hbm.at[pl.ds(tab_ref[0, 0, r], 1)],
                                     xbuf.at[sl, pl.ds(r, 1)], gsem.at[sl])

    def scatter_row(sl, r):
        t = dstp_ref[0, 0, r]
        dst = jnp.where(t >= n_real, t + sl * rows, t)
        return pltpu.make_async_copy(ybuf.at[sl, pl.ds(r, 1)], y_hbm.at[pl.ds(dst, 1)],
                                     ssem.at[sl])

    def gather_all(sl):
        return pltpu.make_async_copy(x_hbm.at[pl.ds(0, rows)], xbuf.at[sl], gsem.at[sl])

    def scatter_all(sl, first_row=0):
        return pltpu.make_async_copy(ybuf.at[sl], y_hbm.at[pl.ds(first_row, rows)], ssem.at[sl])

    def rolled(fn):
        def body(r, c):
            fn(r)
            return c
        lax.fori_loop(0, rows, body, 0, unroll=8)

    def inline(fn):
        for r in range(rows):
            fn(r)

    def ffn(sl):
        xb = xbuf[sl].astype(BF16)
        gate = _dot(xb, wg_ref[0])
        up = _dot(xb, wu_ref[0])
        ybuf[sl] = _dot(gate * _sigmoid(gate) * up, wd_ref[0])

    @pl.when(i == 0)
    def _():
        ybuf[...] = jnp.zeros_like(ybuf)
        for sl in range(2):
            scatter_all(sl, n_real + sl * rows).start()
        for sl in range(2):
            scatter_all(sl, n_real + sl * rows).wait()
        rolled(lambda r: gather_row(tok0_ref, 0, r).start())

    @pl.when(i < nused)
    def _():
        gather_all(slot).wait()

    @pl.when((i >= 2) & (i <= nused))
    def _():
        scatter_all(slot).wait()

    @pl.when(i == 0)
    def _():
        inline(lambda r: gather_row(tokn_ref, 1, r).start())
        ffn(0)

    @pl.when((i >= 1) & (i < nused))
    def _():
        inline(lambda r: gather_row(tokn_ref, 1 - slot, r).start(priority=r % 2))
        inline(lambda r: scatter_row(1 - slot, r).start(priority=r % 2))
        ffn(slot)

    @pl.when(i == nused)
    def _():
        rolled(lambda r: scatter_row(1 - slot, r).start())
        scatter_all(1 - slot).wait()
        gather_all(slot).wait()


def _stage_moe(x1, row_tok, row_dst, blk_e, nused, w_gate, w_up, w_down):
    N, D = x1.shape
    nblk, _, rows = row_tok.shape
    d_e = w_gate.shape[2]
    n_real = N * TOP_K
    last = nblk - 1
    wspec = lambda shape: pl.BlockSpec((1,) + shape,
                                       lambda i, be, nu: (be[jnp.minimum(i, last)], 0, 0))
    table = lambda fn: pl.BlockSpec((1, 1, rows), lambda i, be, nu: (fn(i), 0, 0),
                                    memory_space=pltpu.SMEM)
    grid_spec = pltpu.PrefetchScalarGridSpec(
        num_scalar_prefetch=2,
        grid=(nblk + 1,),
        in_specs=[table(lambda i: 0), table(lambda i: jnp.minimum(i + 1, last)),
                  table(lambda i: jnp.clip(i - 1, 0, last)),
                  pl.BlockSpec(memory_space=pl.ANY), wspec((D, d_e)), wspec((D, d_e)),
                  wspec((d_e, D))],
        out_specs=pl.BlockSpec(memory_space=pl.ANY),
        scratch_shapes=[pltpu.VMEM((2, rows, D), F32), pltpu.VMEM((2, rows, D), F32),
                        pltpu.SemaphoreType.DMA((2,)), pltpu.SemaphoreType.DMA((2,))],
    )
    return pl.pallas_call(
        functools.partial(_moe_kernel, rows=rows, n_real=n_real),
        grid_spec=grid_spec,
        out_shape=jax.ShapeDtypeStruct((n_real + 2 * rows, D), F32),
        compiler_params=_params(("arbitrary",)),
        name="moe",
    )(blk_e, nused, row_tok, row_tok, row_dst, x1, w_gate, w_up, w_down)


def _out_kernel(x1_ref, y0_ref, y1_ref, we_ref, ple_ref, g_ref, b_ref, o_ref, *, alpha):
    we = we_ref[...]
    moe = we[:, 0:1] * y0_ref[...] + we[:, 1:2] * y1_ref[...]
    o_ref[...] = _layer_norm(alpha * x1_ref[...] + moe + ple_ref[...], g_ref[...], b_ref[...])


def _stage_out(x1, yb, we, ple, ln_g, ln_b, alpha):
    N, D = x1.shape
    rows = min(OUT_ROWS, N)
    const = lambda shape: pl.BlockSpec(shape, lambda i: (0,) * len(shape))
    tile = lambda c: pl.BlockSpec((rows, c), lambda i: (i, 0))
    slot = lambda j: pl.BlockSpec((rows, D), lambda i: (i + j * (N // rows), 0))
    return pl.pallas_call(
        functools.partial(_out_kernel, alpha=alpha),
        grid=(N // rows,),
        in_specs=[tile(D), slot(0), slot(1), tile(TOP_K), tile(D), const((1, D)), const((1, D))],
        out_specs=tile(D),
        out_shape=jax.ShapeDtypeStruct((N, D), F32),
        compiler_params=_params(("parallel",)),
        name="out",
    )(x1, yb, yb, we, ple, ln_g.reshape(1, D), ln_b.reshape(1, D))


def kernel(x, p, w_in, mu_rkv, mu_lora, w0, w_lora1, w_lora2, a0, a_lora1, a_lora2, g_lora1, g_lora2, k_k, k_a, r_k, lnx_g, lnx_b, rel_bias, w_o, ln1_g, ln1_b, router_g, router_g_b, router_e, router_e_b, w_gate, w_up, w_down, ple_gate, ple_proj, ln2_g, ln2_b):
    B, S, D = x.shape
    depth = w_in.shape[0]
    alpha = (2 * depth) ** 0.25
    for i in range(depth):
        rkv, qkv, lw, ai, g = _stage_proj(x, w_in[i], w_lora1[i], a_lora1[i], g_lora1[i],
                                          mu_lora[i], w_lora2[i], a_lora2[i], g_lora2[i],
                                          w0[i], a0[i])
        yr = _stage_wkv(rkv, lw, ai, g, mu_rkv[i], k_k[i], k_a[i], r_k[i], lnx_g[i], lnx_b[i])
        att = _stage_attn(qkv, rel_bias).reshape(B * S, -1)
        x1, logits, ple = _stage_mix(x.reshape(B * S, D), yr.reshape(B * S, -1), att,
                                     p[i].reshape(B * S, -1), w_o[i], ln1_g[i], ln1_b[i],
                                     router_g[i], router_g_b[i], router_e[i], router_e_b[i],
                                     ple_gate[i], ple_proj[i], alpha)
        row_tok, row_dst, blk_e, nused, we = _route(logits, MOE_ROWS)
        yb = _stage_moe(x1, row_tok, row_dst, blk_e, nused, w_gate[i], w_up[i], w_down[i])
        x = _stage_out(x1, yb, we, ple, ln2_g[i], ln2_b[i], alpha).reshape(B, S, D)
    return x
```

```python
import functools
import math

import jax
import jax.numpy as jnp
import numpy as np
from jax import lax
from jax.experimental import pallas as pl
from jax.experimental.pallas import tpu as pltpu

F32 = jnp.float32
BF16 = jnp.bfloat16

HEAD_DIM = 64
LANES = 128
DECAY_LORA = 64
ICLR_LORA = 64
GATE_LORA = 128
LNX_EPS = 64e-5
DIL_PATTERNS = ((128, 1), (512, 4), (2048, 16))
NEG_INF = -1e30
N_BUCKETS = 32
MAX_DISTANCE = 2048
N_GROUPS = 4
EXPERTS_PER_GROUP = 8
N_EXPERTS = N_GROUPS * EXPERTS_PER_GROUP
TOP_K = 2
LN_EPS = 1e-5

WKV_CHUNK = 64
WKV_CHUNKS_PER_STEP = 4
ATT_BLK = 128
PROJ_ROWS = 512
MIX_ROWS = 256
MOE_ROWS = 256
OUT_ROWS = 256
ROUTER_LANES = 128
VMEM_LIMIT = 56 * 1024 * 1024


def _params(sem):
    return pltpu.CompilerParams(dimension_semantics=sem, vmem_limit_bytes=VMEM_LIMIT)


def _dot(a, b):
    return jnp.dot(a.astype(BF16), b.astype(BF16), preferred_element_type=F32)


def _dot_nt(a, b):
    return lax.dot_general(a.astype(BF16), b.astype(BF16), (((1,), (1,)), ((), ())),
                           preferred_element_type=F32)


def _dot_tn(a, b):
    return lax.dot_general(a.astype(BF16), b.astype(BF16), (((0,), (0,)), ((), ())),
                           preferred_element_type=F32)


def _split(x):
    hi = x.astype(BF16)
    lo = (x - hi.astype(F32)).astype(BF16)
    return hi, lo


def _dot_exact_lhs(w01, x):
    hi, lo = _split(x)
    return (jnp.dot(w01, hi, preferred_element_type=F32)
            + jnp.dot(w01, lo, preferred_element_type=F32))


def _sigmoid(x):
    return 1.0 / (1.0 + jnp.exp(-x))


def _shift_rows(t, prev_row):
    rolled = pltpu.roll(t, 1, 0)
    row = lax.broadcasted_iota(jnp.int32, t.shape, 0)
    return jnp.where(row == 0, prev_row, rolled)


def _layer_norm(z, g, b):
    mu = jnp.mean(z, axis=-1, keepdims=True)
    zc = z - mu
    var = jnp.mean(zc * zc, axis=-1, keepdims=True)
    return zc * lax.rsqrt(var + LN_EPS) * g + b


def _proj_kernel(x_ref, win_ref, w1_ref, a1_ref, g1_ref, mul_ref, w2_ref, a2_ref, g2_ref,
                 w0_ref, a0_ref, rkv_ref, qkv_ref, lw_ref, ai_ref, g_ref, prev_ref, *, d_rkv):
    @pl.when(pl.program_id(1) == 0)
    def _():
        prev_ref[...] = jnp.zeros_like(prev_ref)

    h = x_ref[0]
    rows = h.shape[0]
    hprev = _shift_rows(h, prev_ref[0:1, :])
    prev_ref[0:1, :] = h[rows - 1:rows, :]

    proj = _dot(h, win_ref[...])
    rkv_ref[0] = proj[:, :d_rkv]
    qkv_ref[0] = proj[:, d_rkv:]

    dh = hprev - h
    xw = h + dh * mul_ref[0:1, :]
    xa = h + dh * mul_ref[1:2, :]
    xg = h + dh * mul_ref[2:3, :]
    wl = w0_ref[...] + _dot(jnp.tanh(_dot(xw, w1_ref[...])), w2_ref[...])
    z = -wl
    softplus = jnp.maximum(z, 0.0) + jnp.log(1.0 + jnp.exp(-jnp.abs(z)))
    lw_ref[0] = -jnp.exp(-softplus - 0.5)
    ai_ref[0] = _sigmoid(a0_ref[...] + _dot(_dot(xa, a1_ref[...]), a2_ref[...]))
    g_ref[0] = _dot(_sigmoid(_dot(xg, g1_ref[...])), g2_ref[...])


def _stage_proj(x, w_in, w1, a1, g1, mu_lora, w2, a2, g2, w0, a0):
    B, S, D = x.shape
    d_in = w_in.shape[1]
    d_r = w2.shape[1]
    d_rkv = 3 * d_r
    d_att = d_in - d_rkv
    rows = min(PROJ_ROWS, S)
    const = lambda shape: pl.BlockSpec(shape, lambda b, j: (0,) * len(shape))
    tile = lambda c: pl.BlockSpec((1, rows, c), lambda b, j: (b, j, 0))
    return pl.pallas_call(
        functools.partial(_proj_kernel, d_rkv=d_rkv),
        grid=(B, S // rows),
        in_specs=[tile(D), const((D, d_in)), const(w1.shape), const(a1.shape), const(g1.shape),
                  const(mu_lora.shape), const(w2.shape), const(a2.shape), const(g2.shape),
                  const((1, d_r)), const((1, d_r))],
        out_specs=[tile(d_rkv), tile(d_att), tile(d_r), tile(d_r), tile(d_r)],
        out_shape=[jax.ShapeDtypeStruct((B, S, d_rkv), F32),
                   jax.ShapeDtypeStruct((B, S, d_att), F32),
                   jax.ShapeDtypeStruct((B, S, d_r), F32),
                   jax.ShapeDtypeStruct((B, S, d_r), F32),
                   jax.ShapeDtypeStruct((B, S, d_r), F32)],
        scratch_shapes=[pltpu.VMEM((8, D), F32)],
        compiler_params=_params(("parallel", "arbitrary")),
        name="proj",
    )(x, w_in.astype(BF16), w1.astype(BF16), a1.astype(BF16), g1.astype(BF16), mu_lora,
      w2.astype(BF16), a2.astype(BF16), g2.astype(BF16), w0.reshape(1, d_r), a0.reshape(1, d_r))


def _wkv_kernel(rkv_ref, lw_ref, ai_ref, g_ref, mu_ref, kk_ref, ka_ref, rk_ref, lng_ref, lnb_ref,
                seg_ref, tril_ref, y_ref, s_ref, prev_ref, *, d_r, n_chunks):
    C = WKV_CHUNK
    rows = C * n_chunks

    @pl.when(pl.program_id(1) == 0)
    def _():
        s_ref[...] = jnp.zeros_like(s_ref)
        prev_ref[...] = jnp.zeros_like(prev_ref)

    rkv = rkv_ref[0]
    prev = _shift_rows(rkv, prev_ref[0:1, :])
    prev_ref[0:1, :] = rkv[rows - 1:rows, :]
    mixed = rkv + (prev - rkv) * mu_ref[...]
    r = mixed[:, :d_r]
    k = mixed[:, d_r:2 * d_r]
    v = mixed[:, 2 * d_r:]
    lw = lw_ref[0]
    a = ai_ref[0]
    seg = seg_ref[...]

    kk = k * kk_ref[...]
    kk = kk * lax.rsqrt(jnp.maximum(_dot(kk * kk, seg), 1e-24))
    k = k * (1.0 + (a - 1.0) * ka_ref[...])
    aa = -kk
    bb = kk * a
    c = _dot_exact_lhs(tril_ref[...], lw)
    e_neg = jnp.exp(-c)
    a_t = aa * jnp.exp(c - lw)
    b_t = bb * e_neg
    k_t = k * e_neg
    r_t = r * jnp.exp(c)

    lane = lax.broadcasted_iota(jnp.int32, (1, LANES), 1)
    head0 = lane < HEAD_DIM
    ri = lax.broadcasted_iota(jnp.int32, (2 * C, 2 * C), 0)
    ci = lax.broadcasted_iota(jnp.int32, (2 * C, 2 * C), 1)
    strict = (ri % C) > (ci % C)
    incl = (ri % C) >= (ci % C)
    eye = (ri == ci).astype(F32)
    zero_bf = jnp.zeros((2 * C, LANES), BF16)

    def stack(t):
        return jnp.concatenate([jnp.where(head0, t, 0.0), jnp.where(head0, 0.0, t)],
                               axis=0).astype(BF16)

    def fold(t):
        return t[:C] + t[C:]

    n_pairs = d_r // LANES
    folds = [(q, p) for q in range(n_chunks) for p in range(n_pairs)]
    each = lambda fn, *cols: [fn(*args) for args in zip(*cols)]
    cat0 = lambda *ts: jnp.concatenate(ts, axis=0)
    cat1 = lambda *ts: jnp.concatenate(ts, axis=1)
    left = lambda t: t[:, :2 * C]
    right = lambda t: t[:, 2 * C:]

    p_ends, b_es, k_es = [], [], []
    for q in range(n_chunks):
        rs = slice(q * C, (q + 1) * C)
        c_end = c[rs][C - 1:C, :]
        e_end = jnp.exp(c_end - c[rs])
        b_es.append(bb[rs] * e_end)
        k_es.append(k[rs] * e_end)
        p_ends.append(jnp.exp(c_end))

    def stacks(t_of_q):
        return [stack(t_of_q(q)[:, p * LANES:(p + 1) * LANES]) for q, p in folds]

    chunk = lambda t: (lambda q: t[q * C:(q + 1) * C])
    am, bm, km, rm, vm = (stacks(chunk(t)) for t in (a_t, b_t, k_t, r_t, v))
    bem = stacks(lambda q: b_es[q])
    kem = stacks(lambda q: k_es[q])

    x = each(lambda a_, r_, b_, k_: _dot_nt(cat0(a_, r_), cat0(b_, k_)), am, rm, bm, km)
    n = each(lambda t: jnp.where(strict, t[:2 * C, :2 * C], 0.0), x)
    a_ak = each(lambda t: jnp.where(strict, t[:2 * C, 2 * C:], 0.0).astype(BF16), x)
    a_rb = each(lambda t: jnp.where(incl, t[2 * C:, :2 * C], 0.0).astype(BF16), x)
    a_rk = each(lambda t: jnp.where(incl, t[2 * C:, 2 * C:], 0.0).astype(BF16), x)
    t_inv = each(lambda t: eye + t, n)
    m = each(lambda t: _dot(t, t), n)
    av = each(lambda a_, v_: _dot(a_, v_).astype(BF16), a_ak, vm)
    for _ in range(int(math.log2(C)) - 2):
        z = each(lambda m_, t_: _dot(m_, cat1(m_.astype(BF16), t_.astype(BF16))), m, t_inv)
        t_inv = each(lambda t_, z_: t_ + right(z_), t_inv, z)
        m = each(left, z)
    t_inv = each(lambda t_, m_: t_ + _dot(m_, t_), t_inv, m)
    z = each(lambda t_, a_, av_: _dot(t_, cat1(a_, av_)), t_inv, am, av)
    ap = each(lambda z_: left(z_).astype(BF16), z)
    u0 = each(lambda z_: right(z_).astype(BF16), z)
    z = each(lambda rb_, rk_, ap_, u_, v_: _dot(cat1(rb_, rk_),
                                               cat0(cat1(ap_, u_), cat1(zero_bf, v_))),
             a_rb, a_rk, ap, u0, vm)
    g_m = each(lambda ap_, be_: _dot_tn(ap_, be_).astype(BF16), ap, bem)
    d0 = each(lambda u_, v_, be_, ke_: _dot_tn(cat0(u_, v_), cat0(be_, ke_)), u0, vm, bem, kem)
    rp = each(lambda r_, z_: (r_.astype(F32) + left(z_)).astype(BF16), rm, z)
    y0 = each(lambda z_: fold(right(z_)), z)

    s = [s_ref[p] for p in range(n_pairs)]
    y_cols = []
    for q in range(n_chunks):
        s_bf = [t.astype(BF16) for t in s]
        f0 = q * n_pairs
        y_cols.append([fold(_dot_nt(rp[f0 + p], s_bf[p])) + y0[f0 + p] for p in range(n_pairs)])
        s = [s[p] * p_ends[q][:, p * LANES:(p + 1) * LANES] + _dot(s_bf[p], g_m[f0 + p])
             + d0[f0 + p] for p in range(n_pairs)]
    for p in range(n_pairs):
        s_ref[p] = s[p]
    y = cat0(*[cat1(*cols) for cols in y_cols])

    inv = 1.0 / HEAD_DIM
    mu = _dot(y, seg) * inv
    yc = y - mu
    var = _dot(yc * yc, seg) * inv
    yn = yc * lax.rsqrt(var + LNX_EPS) * lng_ref[...] + lnb_ref[...]
    bonus = _dot(r * k * rk_ref[...], seg) * v
    y_ref[0] = ((yn + bonus) * g_ref[0]).astype(y_ref.dtype)


def _stage_wkv(rkv, lw, ai, g, mu_rkv, k_k, k_a, r_k, lnx_g, lnx_b):
    B, S, d_r = lw.shape
    n_chunks = WKV_CHUNKS_PER_STEP
    rows = WKV_CHUNK * n_chunks
    head = np.arange(d_r) // HEAD_DIM
    seg = jnp.asarray(head[:, None] == head[None, :], BF16)
    t = np.arange(rows)
    tril = jnp.asarray((t[:, None] >= t[None, :])
                       & (t[:, None] // WKV_CHUNK == t[None, :] // WKV_CHUNK), BF16)
    const = lambda shape: pl.BlockSpec(shape, lambda b, j: (0,) * len(shape))
    tile = lambda c: pl.BlockSpec((1, rows, c), lambda b, j: (b, j, 0))
    row = lambda t: t.reshape(1, -1)
    return pl.pallas_call(
        functools.partial(_wkv_kernel, d_r=d_r, n_chunks=n_chunks),
        grid=(B, S // rows),
        in_specs=[tile(3 * d_r), tile(d_r), tile(d_r), tile(d_r), const((1, 3 * d_r)),
                  const((1, d_r)), const((1, d_r)), const((1, d_r)), const((1, d_r)),
                  const((1, d_r)), const((d_r, d_r)), const((rows, rows))],
        out_specs=tile(d_r),
        out_shape=jax.ShapeDtypeStruct((B, S, d_r), BF16),
        scratch_shapes=[pltpu.VMEM((d_r // LANES, LANES, LANES), F32),
                        pltpu.VMEM((8, 3 * d_r), F32)],
        compiler_params=_params(("parallel", "arbitrary")),
        name="wkv",
    )(rkv, lw, ai, g, row(mu_rkv), row(k_k), row(k_a), row(r_k), row(lnx_g), row(lnx_b), seg, tril)


def _t5_bucket(n):
    exact = N_BUCKETS // 2
    nf = np.maximum(n, 1).astype(np.float32)
    large = exact + (np.log(nf / exact) / math.log(MAX_DISTANCE / exact)
                     * (N_BUCKETS - exact)).astype(np.int32)
    large = np.minimum(large, N_BUCKETS - 1)
    return np.where(n < exact, n, large).astype(np.int32)


def _band_bias(rel_bias, dil):
    blk = ATT_BLK
    L = 3 * blk
    rel = (2 * blk - 1) - np.arange(L)
    valid = (rel >= 0) & (rel <= blk)
    base = rel_bias[_t5_bucket(np.clip(rel, 0, None) * dil)].astype(F32).T
    base = jnp.where(valid[None], base, NEG_INF)
    skew = jnp.tile(base, (1, blk))[:, :blk * (L - 1)].reshape(-1, blk, L - 1)
    return skew[:, :, blk - 1:3 * blk - 1]


def _attend(units, bias_ref, scale):
    lane = lax.broadcasted_iota(jnp.int32, (1, LANES), 1)
    head0 = lane < HEAD_DIM
    jobs = [(u, h) for u in range(len(units)) for h in range(2)]
    s = []
    for u, h in jobs:
        q, k, _, col0 = units[u]
        qh = jnp.where(head0 if h == 0 else jnp.logical_not(head0), q * scale, 0.0)
        s.append(_dot_nt(qh, k) + bias_ref[h, :, col0:col0 + k.shape[0]])
    m = [jnp.max(t, axis=-1, keepdims=True) for t in s]
    e = [jnp.exp(t - m_) for t, m_ in zip(s, m)]
    l = [jnp.sum(t, axis=-1, keepdims=True) for t in e]
    o = [_dot(t, units[u][2]) / l_ for t, l_, (u, h) in zip(e, l, jobs)]
    lse = [m_ + jnp.log(l_) for m_, l_ in zip(m, l)]
    return [(jnp.where(head0, o[2 * u], o[2 * u + 1]), jnp.where(head0, lse[2 * u], lse[2 * u + 1]))
            for u in range(len(units))]


def _attn_kernel(q_ref, k_ref, v_ref, b1_ref, b4_ref, b16_ref, o_ref, acc_o, acc_l, *, seq, scale):
    blk = ATT_BLK
    group = 4

    def rows(ref, start, n, dil):
        if dil == 1:
            return ref[0, pl.ds(start, n), :]
        return ref[0, pl.ds(start, n, stride=dil), :]

    def unit(start, dil, has_prev):
        q = rows(q_ref, start, blk, dil)
        if has_prev:
            k0 = start - blk * dil
            return (q, rows(k_ref, k0, 2 * blk, dil), rows(v_ref, k0, 2 * blk, dil), 0)
        return (q, rows(k_ref, start, blk, dil), rows(v_ref, start, blk, dil), blk)

    def acc_rows(ref, start, dil):
        if dil == 1:
            return ref.at[pl.ds(start, blk), :]
        return ref.at[pl.ds(start, blk, stride=dil), :]

    def run(starts, dil, prevs, bias_ref, first, last):
        units = [unit(s, dil, hp) for s, hp in zip(starts, prevs)]
        for s, (o, lse) in zip(starts, _attend(units, bias_ref, scale)):
            ao, al = acc_rows(acc_o, s, dil), acc_rows(acc_l, s, dil)
            if first:
                ao[...] = o
                al[...] = lse
            else:
                o_old, l_old = ao[...], al[...]
                m = jnp.maximum(l_old, lse)
                e_old, e_new = jnp.exp(l_old - m), jnp.exp(lse - m)
                ao[...] = (e_old * o_old + e_new * o) / (e_old + e_new)
                if not last:
                    al[...] = m + jnp.log(e_old + e_new)

    run([n * blk for n in range(group)], 1, [n > 0 for n in range(group)], b1_ref, True, False)

    def body1(g, c):
        base = pl.multiple_of(g * (group * blk), group * blk)
        run([base + n * blk for n in range(group)], 1, [True] * group, b1_ref, True, False)
        return c
    lax.fori_loop(1, seq // (group * blk), body1, 0)

    nb4 = seq // 4 // blk

    def body4(r, c):
        run([r + n * blk * 4 for n in range(nb4)], 4, [n > 0 for n in range(nb4)], b4_ref,
            False, False)
        return c
    lax.fori_loop(0, 4, body4, 0)

    def body16(g, c):
        run([g * group + j for j in range(group)], 16, [False] * group, b16_ref, False, True)
        return c
    lax.fori_loop(0, 16 // group, body16, 0)

    o_ref[0] = acc_o[...].astype(o_ref.dtype)


def _stage_attn(qkv, rel_bias):
    B, S, d3 = qkv.shape
    d_att = d3 // 3
    n_pairs = d_att // LANES
    assert DIL_PATTERNS == ((ATT_BLK, 1), (4 * ATT_BLK, 4), (16 * ATT_BLK, 16))
    assert S == 16 * ATT_BLK
    biases = [_band_bias(rel_bias, dil) for _, dil in DIL_PATTERNS]
    part = lambda i: pl.BlockSpec((1, S, LANES), lambda b, p: (b, 0, i * n_pairs + p))
    bias_spec = pl.BlockSpec((2, ATT_BLK, 2 * ATT_BLK), lambda b, p: (p, 0, 0))
    return pl.pallas_call(
        functools.partial(_attn_kernel, seq=S, scale=HEAD_DIM ** -0.5),
        grid=(B, n_pairs),
        in_specs=[part(0), part(1), part(2), bias_spec, bias_spec, bias_spec],
        out_specs=pl.BlockSpec((1, S, LANES), lambda b, p: (b, 0, p)),
        out_shape=jax.ShapeDtypeStruct((B, S, d_att), BF16),
        scratch_shapes=[pltpu.VMEM((S, LANES), F32), pltpu.VMEM((S, LANES), F32)],
        compiler_params=_params(("parallel", "parallel")),
        name="attn",
    )(qkv, qkv, qkv, *biases)


def _mix_kernel(x_ref, yr_ref, att_ref, p_ref, wor_ref, woa_ref, g_ref, b_ref, rhi_ref, rlo_ref,
                rb_ref, pg_ref, pp_ref, x1_ref, lg_ref, ple_ref, *, alpha):
    mix = _dot(yr_ref[...], wor_ref[...]) + _dot(att_ref[...], woa_ref[...])
    x1 = _layer_norm(alpha * x_ref[...] + mix, g_ref[...], b_ref[...])
    x1_ref[...] = x1
    hi, lo = _split(x1)
    logits = (jnp.dot(hi, rhi_ref[...], preferred_element_type=F32)
              + jnp.dot(lo, rhi_ref[...], preferred_element_type=F32)
              + jnp.dot(hi, rlo_ref[...], preferred_element_type=F32) + rb_ref[...])
    ple = _sigmoid(_dot(hi, pg_ref[...])) * _dot(p_ref[...], pp_ref[...])
    ple_ref[...] = ple.astype(ple_ref.dtype)

    lane = lax.broadcasted_iota(jnp.int32, logits.shape, 1).astype(F32)
    big = float(ROUTER_LANES)
    rmax = lambda t: jnp.max(t, axis=-1, keepdims=True)
    first = lambda hit: jnp.min(jnp.where(hit, lane, big), axis=-1, keepdims=True)
    off = float("-inf")
    is_grp = lane < N_GROUPS
    lg = jnp.where(is_grp, logits, off)
    mg = rmax(lg)
    wg = 1.0 / jnp.sum(jnp.where(is_grp, jnp.exp(lg - mg), 0.0), axis=-1, keepdims=True)
    lo_lane = N_GROUPS + EXPERTS_PER_GROUP * first(lg == mg)
    le = jnp.where((lane >= lo_lane) & (lane < lo_lane + EXPERTS_PER_GROUP), logits, off)
    v1 = rmax(le)
    i1 = first(le == v1)
    le = jnp.where(lane == i1, off, le)
    v2 = rmax(le)
    i2 = first(le == v2)
    e2 = jnp.exp(v2 - v1)
    w1 = wg / (1.0 + e2)
    route = jnp.where(lane == 0, w1, jnp.where(lane == 1, w1 * e2, jnp.where(
        lane == 2, i1 - N_GROUPS, jnp.where(lane == 3, i2 - N_GROUPS, 0.0))))
    lg_ref[...] = route


def _stage_mix(x2, yr, att, p2, w_o, ln_g, ln_b, router_g, router_g_b, router_e, router_e_b,
               ple_gate, ple_proj, alpha):
    N, D = x2.shape
    d_r = yr.shape[1]
    rows = min(MIX_ROWS, N)
    d_att = att.shape[1]
    n_log = N_GROUPS + N_EXPERTS
    rw = jnp.zeros((D, ROUTER_LANES), F32).at[:, :N_GROUPS].set(router_g)
    rw = rw.at[:, N_GROUPS:n_log].set(router_e)
    rb = jnp.zeros((1, ROUTER_LANES), F32).at[0, :N_GROUPS].set(router_g_b)
    rb = rb.at[0, N_GROUPS:n_log].set(router_e_b)
    rhi, rlo = _split(rw)
    const = lambda shape: pl.BlockSpec(shape, lambda i: (0,) * len(shape))
    tile = lambda c: pl.BlockSpec((rows, c), lambda i: (i, 0))
    return pl.pallas_call(
        functools.partial(_mix_kernel, alpha=alpha),
        grid=(N // rows,),
        in_specs=[tile(D), tile(d_r), tile(d_att), tile(p2.shape[1]),
                  const((d_r, D)), const((d_att, D)), const((1, D)), const((1, D)),
                  const((D, ROUTER_LANES)), const((D, ROUTER_LANES)), const((1, ROUTER_LANES)),
                  const((D, D)), const(ple_proj.shape)],
        out_specs=[tile(D), tile(ROUTER_LANES), tile(D)],
        out_shape=[jax.ShapeDtypeStruct((N, D), F32),
                   jax.ShapeDtypeStruct((N, ROUTER_LANES), F32),
                   jax.ShapeDtypeStruct((N, D), BF16)],
        compiler_params=_params(("parallel",)),
        name="mix",
    )(x2, yr, att, p2, w_o[:d_r].astype(BF16), w_o[d_r:].astype(BF16),
      ln_g.reshape(1, D), ln_b.reshape(1, D), rhi, rlo, rb, ple_gate.astype(BF16),
      ple_proj.astype(BF16))


def _route(route, rows):
    N = route.shape[0]
    we = route[:, :TOP_K]
    eid = route[:, TOP_K:2 * TOP_K].astype(jnp.int32)
    A = N * TOP_K
    e_flat = eid.T.reshape(A).astype(jnp.int32)
    order = jnp.argsort(e_flat).astype(jnp.int32)
    counts = jnp.sum((e_flat[:, None] == jnp.arange(N_EXPERTS)[None, :]).astype(jnp.int32), axis=0)
    start = jnp.cumsum(counts) - counts
    padded = (counts + rows - 1) // rows * rows
    pend = jnp.cumsum(padded)
    pstart = pend - padded
    nblk = -(-A // rows) + N_EXPERTS
    nused = (pend[-1] // rows).astype(jnp.int32)
    blk = jnp.arange(nblk, dtype=jnp.int32)
    blk_e = jnp.sum((pend[None, :] <= (blk * rows)[:, None]).astype(jnp.int32), axis=1)
    blk_e = jnp.minimum(blk_e, N_EXPERTS - 1)
    blk_e = jnp.where(blk < nused, blk_e, blk_e[nused - 1]).astype(jnp.int32)
    r = jnp.arange(rows, dtype=jnp.int32)[None, :]
    off = (blk * rows - pstart[blk_e])[:, None] + r
    real = (off < counts[blk_e][:, None]) & (blk < nused)[:, None]
    asg = order[jnp.clip(start[blk_e][:, None] + off, 0, A - 1)]
    row_tok = jnp.where(real, asg % N, r).astype(jnp.int32)
    row_dst = jnp.where(real, asg, A + (blk % 2)[:, None] * rows + r).astype(jnp.int32)
    return (row_tok.reshape(nblk, 1, rows), row_dst.reshape(nblk, 1, rows), blk_e,
            nused.reshape(1), we)


def _moe_kernel(blk_e_ref, nused_ref, tok0_ref, tokn_ref, dstp_ref, x_hbm, wg_ref, wu_ref, wd_ref,
                y_hbm, xbuf, ybuf, gsem, ssem, *, rows, n_real):
    i = pl.program_id(0)
    nused = nused_ref[0]
    slot = i % 2

    def gather_row(tab_ref, sl, r):
        return pltpu.make_async_copy(x_hbm.at[pl.ds(tab_ref[0, 0, r], 1)],
                                     xbuf.at[sl, pl.ds(r, 1)], gsem.at[sl])

    def scatter_row(sl, r):
        return pltpu.make_async_copy(ybuf.at[sl, pl.ds(r, 1)],
                                     y_hbm.at[pl.ds(dstp_ref[0, 0, r], 1)], ssem.at[sl])

    def gather_all(sl):
        return pltpu.make_async_copy(x_hbm.at[pl.ds(0, rows)], xbuf.at[sl], gsem.at[sl])

    def scatter_all(sl, first_row=0):
        return pltpu.make_async_copy(ybuf.at[sl], y_hbm.at[pl.ds(first_row, rows)], ssem.at[sl])

    def rolled(fn):
        def body(r, c):
            fn(r)
            return c
        lax.fori_loop(0, rows, body, 0, unroll=8)

    def inline(fn):
        for r in range(rows):
            fn(r)

    def ffn(sl):
        xb = xbuf[sl].astype(BF16)
        gate = _dot(xb, wg_ref[0])
        up = _dot(xb, wu_ref[0])
        ybuf[sl] = _dot(gate * _sigmoid(gate) * up, wd_ref[0])

    @pl.when(i == 0)
    def _():
        ybuf[...] = jnp.zeros_like(ybuf)
        for sl in range(2):
            scatter_all(sl, n_real + sl * rows).start()
        for sl in range(2):
            scatter_all(sl, n_real + sl * rows).wait()
        rolled(lambda r: gather_row(tok0_ref, 0, r).start())

    @pl.when(i < nused)
    def _():
        gather_all(slot).wait()

    @pl.when((i >= 2) & (i <= nused))
    def _():
        scatter_all(slot).wait()

    @pl.when(i == 0)
    def _():
        inline(lambda r: gather_row(tokn_ref, 1, r).start())
        ffn(0)

    for s in range(2):
        @pl.when((i >= 1) & (i < nused) & (slot == s))
        def _():
            inline(lambda r: gather_row(tokn_ref, 1 - s, r).start())
            inline(lambda r: scatter_row(1 - s, r).start())
            ffn(s)

    @pl.when(i == nused)
    def _():
        rolled(lambda r: scatter_row(1 - slot, r).start())
        scatter_all(1 - slot).wait()
        gather_all(slot).wait()


def _stage_moe(x1, row_tok, row_dst, blk_e, nused, w_gate, w_up, w_down):
    N, D = x1.shape
    nblk, _, rows = row_tok.shape
    d_e = w_gate.shape[2]
    n_real = N * TOP_K
    last = nblk - 1
    wspec = lambda shape: pl.BlockSpec((1,) + shape,
                                       lambda i, be, nu: (be[jnp.minimum(i, last)], 0, 0))
    table = lambda fn: pl.BlockSpec((1, 1, rows), lambda i, be, nu: (fn(i), 0, 0),
                                    memory_space=pltpu.SMEM)
    grid_spec = pltpu.PrefetchScalarGridSpec(
        num_scalar_prefetch=2,
        grid=(nblk + 1,),
        in_specs=[table(lambda i: 0), table(lambda i: jnp.minimum(i + 1, last)),
                  table(lambda i: jnp.clip(i - 1, 0, last)),
                  pl.BlockSpec(memory_space=pl.ANY), wspec((D, d_e)), wspec((D, d_e)),
                  wspec((d_e, D))],
        out_specs=pl.BlockSpec(memory_space=pl.ANY),
        scratch_shapes=[pltpu.VMEM((2, rows, D), F32), pltpu.VMEM((2, rows, D), F32),
                        pltpu.SemaphoreType.DMA((2,)), pltpu.SemaphoreType.DMA((2,))],
    )
    return pl.pallas_call(
        functools.partial(_moe_kernel, rows=rows, n_real=n_real),
        grid_spec=grid_spec,
        out_shape=jax.ShapeDtypeStruct((n_real + 2 * rows, D), F32),
        compiler_params=_params(("arbitrary",)),
        name="moe",
    )(blk_e, nused, row_tok, row_tok, row_dst, x1, w_gate, w_up, w_down)


def _out_kernel(x1_ref, y0_ref, y1_ref, we_ref, ple_ref, g_ref, b_ref, o_ref, *, alpha):
    we = we_ref[...]
    moe = we[:, 0:1] * y0_ref[...] + we[:, 1:2] * y1_ref[...]
    o_ref[...] = _layer_norm(alpha * x1_ref[...] + moe + ple_ref[...], g_ref[...], b_ref[...])


def _stage_out(x1, yb, we, ple, ln_g, ln_b, alpha):
    N, D = x1.shape
    rows = min(OUT_ROWS, N)
    const = lambda shape: pl.BlockSpec(shape, lambda i: (0,) * len(shape))
    tile = lambda c: pl.BlockSpec((rows, c), lambda i: (i, 0))
    slot = lambda j: pl.BlockSpec((rows, D), lambda i: (i + j * (N // rows), 0))
    return pl.pallas_call(
        functools.partial(_out_kernel, alpha=alpha),
        grid=(N // rows,),
        in_specs=[tile(D), slot(0), slot(1), tile(TOP_K), tile(D), const((1, D)), const((1, D))],
        out_specs=tile(D),
        out_shape=jax.ShapeDtypeStruct((N, D), F32),
        compiler_params=_params(("parallel",)),
        name="out",
    )(x1, yb, yb, we, ple, ln_g.reshape(1, D), ln_b.reshape(1, D))


def kernel(x, p, w_in, mu_rkv, mu_lora, w0, w_lora1, w_lora2, a0, a_lora1, a_lora2, g_lora1, g_lora2, k_k, k_a, r_k, lnx_g, lnx_b, rel_bias, w_o, ln1_g, ln1_b, router_g, router_g_b, router_e, router_e_b, w_gate, w_up, w_down, ple_gate, ple_proj, ln2_g, ln2_b):
    B, S, D = x.shape
    depth = w_in.shape[0]
    alpha = (2 * depth) ** 0.25
    for i in range(depth):
        rkv, qkv, lw, ai, g = _stage_proj(x, w_in[i], w_lora1[i], a_lora1[i], g_lora1[i],
                                          mu_lora[i], w_lora2[i], a_lora2[i], g_lora2[i],
                                          w0[i], a0[i])
        yr = _stage_wkv(rkv, lw, ai, g, mu_rkv[i], k_k[i], k_a[i], r_k[i], lnx_g[i], lnx_b[i])
        att = _stage_attn(qkv, rel_bias).reshape(B * S, -1)
        x1, route, ple = _stage_mix(x.reshape(B * S, D), yr.reshape(B * S, -1), att,
                                     p[i].reshape(B * S, -1), w_o[i], ln1_g[i], ln1_b[i],
                                     router_g[i], router_g_b[i], router_e[i], router_e_b[i],
                                     ple_gate[i], ple_proj[i], alpha)
        row_tok, row_dst, blk_e, nused, we = _route(route, MOE_ROWS)
        yb = _stage_moe(x1, row_tok, row_dst, blk_e, nused, w_gate[i], w_up[i], w_down[i])
        x = _stage_out(x1, yb, we, ple, ln2_g[i], ln2_b[i], alpha).reshape(B, S, D)
    return x
```

```python
import functools
import math

import jax
import jax.numpy as jnp
import numpy as np
from jax import lax
from jax.experimental import pallas as pl
from jax.experimental.pallas import tpu as pltpu

F32 = jnp.float32
BF16 = jnp.bfloat16

HEAD_DIM = 64
LANES = 128
MXU_DEPTH = 256
DECAY_LORA = 64
ICLR_LORA = 64
GATE_LORA = 128
LNX_EPS = 64e-5
DIL_PATTERNS = ((128, 1), (512, 4), (2048, 16))
NEG_INF = -1e30
N_BUCKETS = 32
MAX_DISTANCE = 2048
N_GROUPS = 4
EXPERTS_PER_GROUP = 8
N_EXPERTS = N_GROUPS * EXPERTS_PER_GROUP
TOP_K = 2
LN_EPS = 1e-5

WKV_CHUNK = 64
WKV_CHUNKS_PER_STEP = 4
ATT_BLK = 128
PROJ_ROWS = 512
MIX_ROWS = 256
MOE_ROWS = 256
OUT_ROWS = 256
ROUTER_LANES = 128
VMEM_LIMIT = 56 * 1024 * 1024


def _params(sem):
    return pltpu.CompilerParams(dimension_semantics=sem, vmem_limit_bytes=VMEM_LIMIT)


def _dot(a, b):
    return jnp.dot(a.astype(BF16), b.astype(BF16), preferred_element_type=F32)


def _dot_nt(a, b):
    return lax.dot_general(a.astype(BF16), b.astype(BF16), (((1,), (1,)), ((), ())),
                           preferred_element_type=F32)


def _dot_tn(a, b):
    return lax.dot_general(a.astype(BF16), b.astype(BF16), (((0,), (0,)), ((), ())),
                           preferred_element_type=F32)


def _split(x):
    hi = x.astype(BF16)
    lo = (x - hi.astype(F32)).astype(BF16)
    return hi, lo


def _dot_exact_lhs(w01, x):
    hi, lo = _split(x)
    return (jnp.dot(w01, hi, preferred_element_type=F32)
            + jnp.dot(w01, lo, preferred_element_type=F32))


def _sigmoid(x):
    return 1.0 / (1.0 + jnp.exp(-x))


def _shift_rows(t, prev_row):
    rolled = pltpu.roll(t, 1, 0)
    row = lax.broadcasted_iota(jnp.int32, t.shape, 0)
    return jnp.where(row == 0, prev_row, rolled)


def _layer_norm(z, g, b):
    mu = jnp.mean(z, axis=-1, keepdims=True)
    zc = z - mu
    var = jnp.mean(zc * zc, axis=-1, keepdims=True)
    return zc * lax.rsqrt(var + LN_EPS) * g + b


def _proj_kernel(x_ref, win_ref, w1_ref, a1_ref, g1_ref, mul_ref, w2_ref, a2_ref, g2_ref,
                 w0_ref, a0_ref, rkv_ref, qkv_ref, lw_ref, ai_ref, g_ref, prev_ref, *, d_rkv):
    @pl.when(pl.program_id(1) == 0)
    def _():
        prev_ref[...] = jnp.zeros_like(prev_ref)

    h = x_ref[0]
    rows = h.shape[0]
    hprev = _shift_rows(h, prev_ref[0:1, :])
    prev_ref[0:1, :] = h[rows - 1:rows, :]

    dh = hprev - h
    hb = h.astype(BF16)
    low_w = _dot(h + dh * mul_ref[0:1, :], w1_ref[...])
    low_a = _dot(h + dh * mul_ref[1:2, :], a1_ref[...])
    low_g = _dot(h + dh * mul_ref[2:3, :], g1_ref[...])
    rkv_ref[0] = _dot(hb, win_ref[:, :d_rkv])
    wl = w0_ref[...] + _dot(jnp.tanh(low_w), w2_ref[...])
    al = a0_ref[...] + _dot(low_a, a2_ref[...])
    g_ref[0] = _dot(_sigmoid(low_g), g2_ref[...])
    qkv_ref[0] = _dot(hb, win_ref[:, d_rkv:])
    z = -wl
    softplus = jnp.maximum(z, 0.0) + jnp.log(1.0 + jnp.exp(-jnp.abs(z)))
    lw_ref[0] = -jnp.exp(-softplus - 0.5)
    ai_ref[0] = _sigmoid(al)


def _stage_proj(x, w_in, w1, a1, g1, mu_lora, w2, a2, g2, w0, a0):
    B, S, D = x.shape
    d_in = w_in.shape[1]
    d_r = w2.shape[1]
    d_rkv = 3 * d_r
    d_att = d_in - d_rkv
    rows = min(PROJ_ROWS, S)
    const = lambda shape: pl.BlockSpec(shape, lambda b, j: (0,) * len(shape))
    tile = lambda c: pl.BlockSpec((1, rows, c), lambda b, j: (b, j, 0))
    return pl.pallas_call(
        functools.partial(_proj_kernel, d_rkv=d_rkv),
        grid=(B, S // rows),
        in_specs=[tile(D), const((D, d_in)), const(w1.shape), const(a1.shape), const(g1.shape),
                  const(mu_lora.shape), const(w2.shape), const(a2.shape), const(g2.shape),
                  const((1, d_r)), const((1, d_r))],
        out_specs=[tile(d_rkv), tile(d_att), tile(d_r), tile(d_r), tile(d_r)],
        out_shape=[jax.ShapeDtypeStruct((B, S, d_rkv), F32),
                   jax.ShapeDtypeStruct((B, S, d_att), F32),
                   jax.ShapeDtypeStruct((B, S, d_r), F32),
                   jax.ShapeDtypeStruct((B, S, d_r), F32),
                   jax.ShapeDtypeStruct((B, S, d_r), F32)],
        scratch_shapes=[pltpu.VMEM((8, D), F32)],
        compiler_params=_params(("parallel", "arbitrary")),
        name="proj",
    )(x, w_in.astype(BF16), w1.astype(BF16), a1.astype(BF16), g1.astype(BF16), mu_lora,
      w2.astype(BF16), a2.astype(BF16), g2.astype(BF16), w0.reshape(1, d_r), a0.reshape(1, d_r))


def _wkv_kernel(rkv_ref, lw_ref, ai_ref, g_ref, mu_ref, kk_ref, ka_ref, rk_ref, lng_ref, lnb_ref,
                seg_ref, tril_ref, y_ref, s_ref, prev_ref, *, d_r, n_chunks):
    C = WKV_CHUNK
    rows = C * n_chunks

    @pl.when(pl.program_id(1) == 0)
    def _():
        s_ref[...] = jnp.zeros_like(s_ref)
        prev_ref[...] = jnp.zeros_like(prev_ref)

    rkv = rkv_ref[0]
    prev = _shift_rows(rkv, prev_ref[0:1, :])
    prev_ref[0:1, :] = rkv[rows - 1:rows, :]
    mixed = rkv + (prev - rkv) * mu_ref[...]
    r = mixed[:, :d_r]
    k = mixed[:, d_r:2 * d_r]
    v = mixed[:, 2 * d_r:]
    lw = lw_ref[0]
    a = ai_ref[0]
    seg = seg_ref[...]
    seg_w = seg.shape[0]

    def head_sum(t):
        return jnp.concatenate([_dot(t[:, j:j + seg_w], seg) for j in range(0, d_r, seg_w)],
                               axis=1)

    kk = k * kk_ref[...]
    kk = kk * lax.rsqrt(jnp.maximum(head_sum(kk * kk), 1e-24))
    k = k * (1.0 + (a - 1.0) * ka_ref[...])
    aa = -kk
    bb = kk * a
    c = _dot_exact_lhs(tril_ref[...], lw)
    e_neg = jnp.exp(-c)
    a_t = aa * jnp.exp(c - lw)
    b_t = bb * e_neg
    k_t = k * e_neg
    r_t = r * jnp.exp(c)

    lane = lax.broadcasted_iota(jnp.int32, (1, LANES), 1)
    head0 = lane < HEAD_DIM
    ri = lax.broadcasted_iota(jnp.int32, (2 * C, 2 * C), 0)
    ci = lax.broadcasted_iota(jnp.int32, (2 * C, 2 * C), 1)
    strict = (ri % C) > (ci % C)
    incl = (ri % C) >= (ci % C)
    eye = (ri == ci).astype(F32)
    zero_bf = jnp.zeros((2 * C, LANES), BF16)

    def stack(t):
        return jnp.concatenate([jnp.where(head0, t, 0.0), jnp.where(head0, 0.0, t)],
                               axis=0).astype(BF16)

    def fold(t):
        return t[:C] + t[C:]

    n_pairs = d_r // LANES
    folds = [(q, p) for q in range(n_chunks) for p in range(n_pairs)]
    each = lambda fn, *cols: [fn(*args) for args in zip(*cols)]
    cat0 = lambda *ts: jnp.concatenate(ts, axis=0)
    cat1 = lambda *ts: jnp.concatenate(ts, axis=1)
    left = lambda t: t[:, :2 * C]
    right = lambda t: t[:, 2 * C:]

    p_ends, b_es, k_es = [], [], []
    for q in range(n_chunks):
        rs = slice(q * C, (q + 1) * C)
        c_end = c[rs][C - 1:C, :]
        e_end = jnp.exp(c_end - c[rs])
        b_es.append(bb[rs] * e_end)
        k_es.append(k[rs] * e_end)
        p_ends.append(jnp.exp(c_end))

    def stacks(t_of_q):
        return [stack(t_of_q(q)[:, p * LANES:(p + 1) * LANES]) for q, p in folds]

    chunk = lambda t: (lambda q: t[q * C:(q + 1) * C])
    am, bm, km, rm, vm = (stacks(chunk(t)) for t in (a_t, b_t, k_t, r_t, v))
    bem = stacks(lambda q: b_es[q])
    kem = stacks(lambda q: k_es[q])

    x = each(lambda a_, r_, b_, k_: _dot_nt(cat0(a_, r_), cat0(b_, k_)), am, rm, bm, km)
    n = each(lambda t: jnp.where(strict, t[:2 * C, :2 * C], 0.0), x)
    a_ak = each(lambda t: jnp.where(strict, t[:2 * C, 2 * C:], 0.0).astype(BF16), x)
    a_rb = each(lambda t: jnp.where(incl, t[2 * C:, :2 * C], 0.0).astype(BF16), x)
    a_rk = each(lambda t: jnp.where(incl, t[2 * C:, 2 * C:], 0.0).astype(BF16), x)
    t_inv = each(lambda t: eye + t, n)
    m = each(lambda t: _dot(t, t), n)
    av = each(lambda a_, v_: _dot(a_, v_).astype(BF16), a_ak, vm)
    for _ in range(int(math.log2(C)) - 2):
        z = each(lambda m_, t_: _dot(m_, cat1(m_.astype(BF16), t_.astype(BF16))), m, t_inv)
        t_inv = each(lambda t_, z_: t_ + right(z_), t_inv, z)
        m = each(left, z)
    t_inv = each(lambda t_, m_: t_ + _dot(m_, t_), t_inv, m)
    z = each(lambda t_, a_, av_: _dot(t_, cat1(a_, av_)), t_inv, am, av)
    ap = each(lambda z_: left(z_).astype(BF16), z)
    u0 = each(lambda z_: right(z_).astype(BF16), z)
    z = each(lambda rb_, rk_, ap_, u_, v_: _dot(cat1(rb_, rk_),
                                               cat0(cat1(ap_, u_), cat1(zero_bf, v_))),
             a_rb, a_rk, ap, u0, vm)
    g_m = each(lambda ap_, be_: _dot_tn(ap_, be_).astype(BF16), ap, bem)
    d0 = each(lambda u_, v_, be_, ke_: _dot_tn(cat0(u_, v_), cat0(be_, ke_)), u0, vm, bem, kem)
    rp = each(lambda r_, z_: (r_.astype(F32) + left(z_)).astype(BF16), rm, z)
    y0 = each(lambda z_: fold(right(z_)), z)

    s = [s_ref[p] for p in range(n_pairs)]
    y_cols = []
    for q in range(n_chunks):
        s_bf = [t.astype(BF16) for t in s]
        f0 = q * n_pairs
        y_cols.append([fold(_dot_nt(rp[f0 + p], s_bf[p])) + y0[f0 + p] for p in range(n_pairs)])
        s = [s[p] * p_ends[q][:, p * LANES:(p + 1) * LANES] + _dot(s_bf[p], g_m[f0 + p])
             + d0[f0 + p] for p in range(n_pairs)]
    for p in range(n_pairs):
        s_ref[p] = s[p]
    y = cat0(*[cat1(*cols) for cols in y_cols])

    inv = 1.0 / HEAD_DIM
    mu = head_sum(y) * inv
    yc = y - mu
    var = head_sum(yc * yc) * inv
    yn = yc * lax.rsqrt(var + LNX_EPS) * lng_ref[...] + lnb_ref[...]
    bonus = head_sum(r * k * rk_ref[...]) * v
    y_ref[0] = ((yn + bonus) * g_ref[0]).astype(y_ref.dtype)


def _stage_wkv(rkv, lw, ai, g, mu_rkv, k_k, k_a, r_k, lnx_g, lnx_b):
    B, S, d_r = lw.shape
    n_chunks = WKV_CHUNKS_PER_STEP
    rows = WKV_CHUNK * n_chunks
    head = np.arange(MXU_DEPTH) // HEAD_DIM
    seg = jnp.asarray(head[:, None] == head[None, :], BF16)
    t = np.arange(rows)
    tril = jnp.asarray((t[:, None] >= t[None, :])
                       & (t[:, None] // WKV_CHUNK == t[None, :] // WKV_CHUNK), BF16)
    const = lambda shape: pl.BlockSpec(shape, lambda b, j: (0,) * len(shape))
    tile = lambda c: pl.BlockSpec((1, rows, c), lambda b, j: (b, j, 0))
    row = lambda t: t.reshape(1, -1)
    return pl.pallas_call(
        functools.partial(_wkv_kernel, d_r=d_r, n_chunks=n_chunks),
        grid=(B, S // rows),
        in_specs=[tile(3 * d_r), tile(d_r), tile(d_r), tile(d_r), const((1, 3 * d_r)),
                  const((1, d_r)), const((1, d_r)), const((1, d_r)), const((1, d_r)),
                  const((1, d_r)), const((MXU_DEPTH, MXU_DEPTH)), const((rows, rows))],
        out_specs=tile(d_r),
        out_shape=jax.ShapeDtypeStruct((B, S, d_r), BF16),
        scratch_shapes=[pltpu.VMEM((d_r // LANES, LANES, LANES), F32),
                        pltpu.VMEM((8, 3 * d_r), F32)],
        compiler_params=_params(("parallel", "arbitrary")),
        name="wkv",
    )(rkv, lw, ai, g, row(mu_rkv), row(k_k), row(k_a), row(r_k), row(lnx_g), row(lnx_b), seg, tril)


def _t5_bucket(n):
    exact = N_BUCKETS // 2
    nf = np.maximum(n, 1).astype(np.float32)
    large = exact + (np.log(nf / exact) / math.log(MAX_DISTANCE / exact)
                     * (N_BUCKETS - exact)).astype(np.int32)
    large = np.minimum(large, N_BUCKETS - 1)
    return np.where(n < exact, n, large).astype(np.int32)


def _band_bias(rel_bias, dil):
    blk = ATT_BLK
    L = 3 * blk
    rel = (2 * blk - 1) - np.arange(L)
    valid = (rel >= 0) & (rel <= blk)
    base = rel_bias[_t5_bucket(np.clip(rel, 0, None) * dil)].astype(F32).T
    base = jnp.where(valid[None], base, NEG_INF)
    skew = jnp.tile(base, (1, blk))[:, :blk * (L - 1)].reshape(-1, blk, L - 1)
    return skew[:, :, blk - 1:3 * blk - 1]


def _attend(units, bias_ref, scale):
    lane = lax.broadcasted_iota(jnp.int32, (1, LANES), 1)
    head0 = lane < HEAD_DIM
    jobs = [(u, h) for u in range(len(units)) for h in range(2)]
    s = []
    for u, h in jobs:
        q, k, _, col0 = units[u]
        qh = jnp.where(head0 if h == 0 else jnp.logical_not(head0), q * scale, 0.0)
        s.append(_dot_nt(qh, k) + bias_ref[h, :, col0:col0 + k.shape[0]])
    m = [jnp.max(t, axis=-1, keepdims=True) for t in s]
    e = [jnp.exp2(t - m_) for t, m_ in zip(s, m)]
    l = [jnp.sum(t, axis=-1, keepdims=True) for t in e]
    o = [_dot(t, units[u][2]) / l_ for t, l_, (u, h) in zip(e, l, jobs)]
    lse = [m_ + jnp.log2(l_) for m_, l_ in zip(m, l)]
    return [(jnp.where(head0, o[2 * u], o[2 * u + 1]), jnp.where(head0, lse[2 * u], lse[2 * u + 1]))
            for u in range(len(units))]


def _attn_kernel(q_ref, k_ref, v_ref, b1_ref, b4_ref, b16_ref, o_ref, acc_o, acc_l, *, seq, scale):
    blk = ATT_BLK
    group = 4

    def rows(ref, start, n, dil):
        if dil == 1:
            return ref[0, pl.ds(start, n), :]
        return ref[0, pl.ds(start, n, stride=dil), :]

    def unit(start, dil, has_prev):
        q = rows(q_ref, start, blk, dil)
        if has_prev:
            k0 = start - blk * dil
            return (q, rows(k_ref, k0, 2 * blk, dil), rows(v_ref, k0, 2 * blk, dil), 0)
        return (q, rows(k_ref, start, blk, dil), rows(v_ref, start, blk, dil), blk)

    def acc_rows(ref, start, dil):
        if dil == 1:
            return ref.at[pl.ds(start, blk), :]
        return ref.at[pl.ds(start, blk, stride=dil), :]

    def run(starts, dil, prevs, bias_ref, first, last):
        units = [unit(s, dil, hp) for s, hp in zip(starts, prevs)]
        for s, (o, lse) in zip(starts, _attend(units, bias_ref, scale)):
            ao, al = acc_rows(acc_o, s, dil), acc_rows(acc_l, s, dil)
            if first:
                ao[...] = o
                al[...] = lse
            else:
                o_old, l_old = ao[...], al[...]
                m = jnp.maximum(l_old, lse)
                e_old, e_new = jnp.exp2(l_old - m), jnp.exp2(lse - m)
                ao[...] = (e_old * o_old + e_new * o) / (e_old + e_new)
                if not last:
                    al[...] = m + jnp.log2(e_old + e_new)

    run([n * blk for n in range(group)], 1, [n > 0 for n in range(group)], b1_ref, True, False)

    def body1(g, c):
        base = pl.multiple_of(g * (group * blk), group * blk)
        run([base + n * blk for n in range(group)], 1, [True] * group, b1_ref, True, False)
        return c
    lax.fori_loop(1, seq // (group * blk), body1, 0)

    nb4 = seq // 4 // blk

    def body4(r, c):
        run([r + n * blk * 4 for n in range(nb4)], 4, [n > 0 for n in range(nb4)], b4_ref,
            False, False)
        return c
    lax.fori_loop(0, 4, body4, 0)

    def body16(g, c):
        run([g * group + j for j in range(group)], 16, [False] * group, b16_ref, False, True)
        return c
    lax.fori_loop(0, 16 // group, body16, 0)

    o_ref[0] = acc_o[...].astype(o_ref.dtype)


def _stage_attn(qkv, rel_bias):
    B, S, d3 = qkv.shape
    d_att = d3 // 3
    n_pairs = d_att // LANES
    assert DIL_PATTERNS == ((ATT_BLK, 1), (4 * ATT_BLK, 4), (16 * ATT_BLK, 16))
    assert S == 16 * ATT_BLK
    log2e = math.log2(math.e)
    biases = [_band_bias(rel_bias, dil) * log2e for _, dil in DIL_PATTERNS]
    part = lambda i: pl.BlockSpec((1, S, LANES), lambda b, p: (b, 0, i * n_pairs + p))
    bias_spec = pl.BlockSpec((2, ATT_BLK, 2 * ATT_BLK), lambda b, p: (p, 0, 0))
    return pl.pallas_call(
        functools.partial(_attn_kernel, seq=S, scale=HEAD_DIM ** -0.5 * log2e),
        grid=(B, n_pairs),
        in_specs=[part(0), part(1), part(2), bias_spec, bias_spec, bias_spec],
        out_specs=pl.BlockSpec((1, S, LANES), lambda b, p: (b, 0, p)),
        out_shape=jax.ShapeDtypeStruct((B, S, d_att), BF16),
        scratch_shapes=[pltpu.VMEM((S, LANES), F32), pltpu.VMEM((S, LANES), F32)],
        compiler_params=_params(("parallel", "parallel")),
        name="attn",
    )(qkv, qkv, qkv, *biases)


def _mix_kernel(x_ref, yr_ref, att_ref, p_ref, wor_ref, woa_ref, g_ref, b_ref, rhi_ref, rlo_ref,
                rb_ref, pg_ref, pp_ref, x1_ref, lg_ref, ple_ref, *, alpha):
    mix = _dot(yr_ref[...], wor_ref[...]) + _dot(att_ref[...], woa_ref[...])
    emb = _dot(p_ref[...], pp_ref[...])
    x1 = _layer_norm(alpha * x_ref[...] + mix, g_ref[...], b_ref[...])
    x1_ref[...] = x1
    hi, lo = _split(x1)
    logits = (jnp.dot(hi, rhi_ref[...], preferred_element_type=F32)
              + jnp.dot(lo, rhi_ref[...], preferred_element_type=F32)
              + jnp.dot(hi, rlo_ref[...], preferred_element_type=F32) + rb_ref[...])
    ple_ref[...] = (_sigmoid(_dot(hi, pg_ref[...])) * emb).astype(ple_ref.dtype)

    lane = lax.broadcasted_iota(jnp.int32, logits.shape, 1).astype(F32)
    big = float(ROUTER_LANES)
    rmax = lambda t: jnp.max(t, axis=-1, keepdims=True)
    first = lambda hit: jnp.min(jnp.where(hit, lane, big), axis=-1, keepdims=True)
    off = float("-inf")
    is_grp = lane < N_GROUPS
    lg = jnp.where(is_grp, logits, off)
    mg = rmax(lg)
    wg = 1.0 / jnp.sum(jnp.where(is_grp, jnp.exp(lg - mg), 0.0), axis=-1, keepdims=True)
    lo_lane = N_GROUPS + EXPERTS_PER_GROUP * first(lg == mg)
    le = jnp.where((lane >= lo_lane) & (lane < lo_lane + EXPERTS_PER_GROUP), logits, off)
    v1 = rmax(le)
    i1 = first(le == v1)
    le = jnp.where(lane == i1, off, le)
    v2 = rmax(le)
    i2 = first(le == v2)
    e2 = jnp.exp(v2 - v1)
    w1 = wg / (1.0 + e2)
    route = jnp.where(lane == 0, w1, jnp.where(lane == 1, w1 * e2, jnp.where(
        lane == 2, i1 - N_GROUPS, jnp.where(lane == 3, i2 - N_GROUPS, 0.0))))
    lg_ref[...] = route


def _stage_mix(x2, yr, att, p2, w_o, ln_g, ln_b, router_g, router_g_b, router_e, router_e_b,
               ple_gate, ple_proj, alpha):
    N, D = x2.shape
    d_r = yr.shape[1]
    rows = min(MIX_ROWS, N)
    d_att = att.shape[1]
    n_log = N_GROUPS + N_EXPERTS
    rw = jnp.zeros((D, ROUTER_LANES), F32).at[:, :N_GROUPS].set(router_g)
    rw = rw.at[:, N_GROUPS:n_log].set(router_e)
    rb = jnp.zeros((1, ROUTER_LANES), F32).at[0, :N_GROUPS].set(router_g_b)
    rb = rb.at[0, N_GROUPS:n_log].set(router_e_b)
    rhi, rlo = _split(rw)
    const = lambda shape: pl.BlockSpec(shape, lambda i: (0,) * len(shape))
    tile = lambda c: pl.BlockSpec((rows, c), lambda i: (i, 0))
    return pl.pallas_call(
        functools.partial(_mix_kernel, alpha=alpha),
        grid=(N // rows,),
        in_specs=[tile(D), tile(d_r), tile(d_att), tile(p2.shape[1]),
                  const((d_r, D)), const((d_att, D)), const((1, D)), const((1, D)),
                  const((D, ROUTER_LANES)), const((D, ROUTER_LANES)), const((1, ROUTER_LANES)),
                  const((D, D)), const(ple_proj.shape)],
        out_specs=[tile(D), tile(ROUTER_LANES), tile(D)],
        out_shape=[jax.ShapeDtypeStruct((N, D), F32),
                   jax.ShapeDtypeStruct((N, ROUTER_LANES), F32),
                   jax.ShapeDtypeStruct((N, D), BF16)],
        compiler_params=_params(("parallel",)),
        name="mix",
    )(x2, yr, att, p2, w_o[:d_r].astype(BF16), w_o[d_r:].astype(BF16),
      ln_g.reshape(1, D), ln_b.reshape(1, D), rhi, rlo, rb, ple_gate.astype(BF16),
      ple_proj.astype(BF16))


def _route(route, rows):
    N = route.shape[0]
    we = route[:, :TOP_K]
    eid = route[:, TOP_K:2 * TOP_K].astype(jnp.int32)
    A = N * TOP_K
    e_flat = eid.T.reshape(A).astype(jnp.int32)
    order = jnp.argsort(e_flat).astype(jnp.int32)
    counts = jnp.sum((e_flat[:, None] == jnp.arange(N_EXPERTS)[None, :]).astype(jnp.int32), axis=0)
    start = jnp.cumsum(counts) - counts
    padded = (counts + rows - 1) // rows * rows
    pend = jnp.cumsum(padded)
    pstart = pend - padded
    nblk = -(-A // rows) + N_EXPERTS
    nused = (pend[-1] // rows).astype(jnp.int32)
    blk = jnp.arange(nblk, dtype=jnp.int32)
    blk_e = jnp.sum((pend[None, :] <= (blk * rows)[:, None]).astype(jnp.int32), axis=1)
    blk_e = jnp.minimum(blk_e, N_EXPERTS - 1)
    blk_e = jnp.where(blk < nused, blk_e, blk_e[nused - 1]).astype(jnp.int32)
    r = jnp.arange(rows, dtype=jnp.int32)[None, :]
    off = (blk * rows - pstart[blk_e])[:, None] + r
    real = (off < counts[blk_e][:, None]) & (blk < nused)[:, None]
    asg = order[jnp.clip(start[blk_e][:, None] + off, 0, A - 1)]
    row_tok = jnp.where(real, asg % N, r).astype(jnp.int32)
    row_dst = jnp.where(real, asg, A + (blk % 2)[:, None] * rows + r).astype(jnp.int32)
    return (row_tok.reshape(nblk, 1, rows), row_dst.reshape(nblk, 1, rows), blk_e,
            nused.reshape(1), we)


def _moe_kernel(blk_e_ref, nused_ref, tok0_ref, tokn_ref, dstp_ref, x_hbm, wg_ref, wu_ref, wd_ref,
                y_hbm, xbuf, ybuf, gsem, ssem, *, rows, n_real):
    i = pl.program_id(0)
    nused = nused_ref[0]
    slot = i % 2

    def gather_row(tab_ref, sl, r):
        return pltpu.make_async_copy(x_hbm.at[pl.ds(tab_ref[0, 0, r], 1)],
                                     xbuf.at[sl, pl.ds(r, 1)], gsem.at[sl])

    def scatter_row(sl, r):
        return pltpu.make_async_copy(ybuf.at[sl, pl.ds(r, 1)],
                                     y_hbm.at[pl.ds(dstp_ref[0, 0, r], 1)], ssem.at[sl])

    def gather_all(sl):
        return pltpu.make_async_copy(x_hbm.at[pl.ds(0, rows)], xbuf.at[sl], gsem.at[sl])

    def scatter_all(sl, first_row=0):
        return pltpu.make_async_copy(ybuf.at[sl], y_hbm.at[pl.ds(first_row, rows)], ssem.at[sl])

    def rolled(fn):
        def body(r, c):
            fn(r)
            return c
        lax.fori_loop(0, rows, body, 0, unroll=8)

    def inline(fn):
        for r in range(rows):
            fn(r)

    def ffn(sl):
        xb = xbuf[sl].astype(BF16)
        gate = _dot(xb, wg_ref[0])
        up = _dot(xb, wu_ref[0])
        ybuf[sl] = _dot(gate * _sigmoid(gate) * up, wd_ref[0])

    @pl.when(i == 0)
    def _():
        ybuf[...] = jnp.zeros_like(ybuf)
        for sl in range(2):
            scatter_all(sl, n_real + sl * rows).start()
        for sl in range(2):
            scatter_all(sl, n_real + sl * rows).wait()
        rolled(lambda r: gather_row(tok0_ref, 0, r).start())

    @pl.when(i < nused)
    def _():
        gather_all(slot).wait()

    @pl.when((i >= 2) & (i <= nused))
    def _():
        scatter_all(slot).wait()

    @pl.when(i == 0)
    def _():
        inline(lambda r: gather_row(tokn_ref, 1, r).start())
        ffn(0)

    for s in range(2):
        @pl.when((i >= 1) & (i < nused) & (slot == s))
        def _():
            inline(lambda r: gather_row(tokn_ref, 1 - s, r).start())
            inline(lambda r: scatter_row(1 - s, r).start())
            ffn(s)

    @pl.when(i == nused)
    def _():
        rolled(lambda r: scatter_row(1 - slot, r).start())
        scatter_all(1 - slot).wait()
        gather_all(slot).wait()


def _stage_moe(x1, row_tok, row_dst, blk_e, nused, w_gate, w_up, w_down):
    N, D = x1.shape
    nblk, _, rows = row_tok.shape
    d_e = w_gate.shape[2]
    n_real = N * TOP_K
    last = nblk - 1
    wspec = lambda shape: pl.BlockSpec((1,) + shape,
                                       lambda i, be, nu: (be[jnp.minimum(i, last)], 0, 0))
    table = lambda fn: pl.BlockSpec((1, 1, rows), lambda i, be, nu: (fn(i), 0, 0),
                                    memory_space=pltpu.SMEM)
    grid_spec = pltpu.PrefetchScalarGridSpec(
        num_scalar_prefetch=2,
        grid=(nblk + 1,),
        in_specs=[table(lambda i: 0), table(lambda i: jnp.minimum(i + 1, last)),
                  table(lambda i: jnp.clip(i - 1, 0, last)),
                  pl.BlockSpec(memory_space=pl.ANY), wspec((D, d_e)), wspec((D, d_e)),
                  wspec((d_e, D))],
        out_specs=pl.BlockSpec(memory_space=pl.ANY),
        scratch_shapes=[pltpu.VMEM((2, rows, D), F32), pltpu.VMEM((2, rows, D), F32),
                        pltpu.SemaphoreType.DMA((2,)), pltpu.SemaphoreType.DMA((2,))],
    )
    return pl.pallas_call(
        functools.partial(_moe_kernel, rows=rows, n_real=n_real),
        grid_spec=grid_spec,
        out_shape=jax.ShapeDtypeStruct((n_real + 2 * rows, D), F32),
        compiler_params=_params(("arbitrary",)),
        name="moe",
    )(blk_e, nused, row_tok, row_tok, row_dst, x1, w_gate, w_up, w_down)


def _out_kernel(x1_ref, y0_ref, y1_ref, we_ref, ple_ref, g_ref, b_ref, o_ref, *, alpha):
    we = we_ref[...]
    moe = we[:, 0:1] * y0_ref[...] + we[:, 1:2] * y1_ref[...]
    o_ref[...] = _layer_norm(alpha * x1_ref[...] + moe + ple_ref[...], g_ref[...], b_ref[...])


def _stage_out(x1, yb, we, ple, ln_g, ln_b, alpha):
    N, D = x1.shape
    rows = min(OUT_ROWS, N)
    const = lambda shape: pl.BlockSpec(shape, lambda i: (0,) * len(shape))
    tile = lambda c: pl.BlockSpec((rows, c), lambda i: (i, 0))
    slot = lambda j: pl.BlockSpec((rows, D), lambda i: (i + j * (N // rows), 0))
    return pl.pallas_call(
        functools.partial(_out_kernel, alpha=alpha),
        grid=(N // rows,),
        in_specs=[tile(D), slot(0), slot(1), tile(TOP_K), tile(D), const((1, D)), const((1, D))],
        out_specs=tile(D),
        out_shape=jax.ShapeDtypeStruct((N, D), F32),
        compiler_params=_params(("parallel",)),
        name="out",
    )(x1, yb, yb, we, ple, ln_g.reshape(1, D), ln_b.reshape(1, D))


def kernel(x, p, w_in, mu_rkv, mu_lora, w0, w_lora1, w_lora2, a0, a_lora1, a_lora2, g_lora1, g_lora2, k_k, k_a, r_k, lnx_g, lnx_b, rel_bias, w_o, ln1_g, ln1_b, router_g, router_g_b, router_e, router_e_b, w_gate, w_up, w_down, ple_gate, ple_proj, ln2_g, ln2_b):
    B, S, D = x.shape
    depth = w_in.shape[0]
    alpha = (2 * depth) ** 0.25
    for i in range(depth):
        rkv, qkv, lw, ai, g = _stage_proj(x, w_in[i], w_lora1[i], a_lora1[i], g_lora1[i],
                                          mu_lora[i], w_lora2[i], a_lora2[i], g_lora2[i],
                                          w0[i], a0[i])
        yr = _stage_wkv(rkv, lw, ai, g, mu_rkv[i], k_k[i], k_a[i], r_k[i], lnx_g[i], lnx_b[i])
        att = _stage_attn(qkv, rel_bias).reshape(B * S, -1)
        x1, route, ple = _stage_mix(x.reshape(B * S, D), yr.reshape(B * S, -1), att,
                                     p[i].reshape(B * S, -1), w_o[i], ln1_g[i], ln1_b[i],
                                     router_g[i], router_g_b[i], router_e[i], router_e_b[i],
                                     ple_gate[i], ple_proj[i], alpha)
        row_tok, row_dst, blk_e, nused, we = _route(route, MOE_ROWS)
        yb = _stage_moe(x1, row_tok, row_dst, blk_e, nused, w_gate[i], w_up[i], w_down[i])
        x = _stage_out(x1, yb, we, ple, ln2_g[i], ln2_b[i], alpha).reshape(B, S, D)
    return x
```

```python
import functools
import math

import jax
import jax.numpy as jnp
import numpy as np
from jax import lax
from jax.experimental import pallas as pl
from jax.experimental.pallas import tpu as pltpu

F32 = jnp.float32
BF16 = jnp.bfloat16

HEAD_DIM = 64
LANES = 128
SUBLANES = 8
MXU_DEPTH = 256
DECAY_LORA = 64
ICLR_LORA = 64
GATE_LORA = 128
LNX_EPS = 64e-5
DIL_PATTERNS = ((128, 1), (512, 4), (2048, 16))
NEG_INF = -1e30
N_BUCKETS = 32
MAX_DISTANCE = 2048
N_GROUPS = 4
EXPERTS_PER_GROUP = 8
N_EXPERTS = N_GROUPS * EXPERTS_PER_GROUP
TOP_K = 2
LN_EPS = 1e-5

WKV_CHUNK = 64
WKV_CHUNKS_PER_STEP = 4
ATT_BLK = 128
PROJ_ROWS = 512
MIX_ROWS = 256
MOE_ROWS = 256
OUT_ROWS = 256
ROUTER_LANES = 128
VMEM_LIMIT = 56 * 1024 * 1024


def _params(sem):
    return pltpu.CompilerParams(dimension_semantics=sem, vmem_limit_bytes=VMEM_LIMIT)


def _dot(a, b):
    return jnp.dot(a.astype(BF16), b.astype(BF16), preferred_element_type=F32)


def _dot_nt(a, b):
    return lax.dot_general(a.astype(BF16), b.astype(BF16), (((1,), (1,)), ((), ())),
                           preferred_element_type=F32)


def _dot_tn(a, b):
    return lax.dot_general(a.astype(BF16), b.astype(BF16), (((0,), (0,)), ((), ())),
                           preferred_element_type=F32)


def _split(x):
    hi = x.astype(BF16)
    lo = (x - hi.astype(F32)).astype(BF16)
    return hi, lo


def _dot_exact_lhs(w01, x):
    hi, lo = _split(x)
    return (jnp.dot(w01, hi, preferred_element_type=F32)
            + jnp.dot(w01, lo, preferred_element_type=F32))


def _sigmoid(x):
    return 1.0 / (1.0 + jnp.exp(-x))


def _shift_rows(t, prev_row):
    rolled = pltpu.roll(t, 1, 0)
    row = lax.broadcasted_iota(jnp.int32, t.shape, 0)
    return jnp.where(row == 0, prev_row, rolled)


def _store_token_tiles(ref, t):
    rows = t.shape[0]
    for s in range(SUBLANES):
        ref[pl.ds(s, rows, stride=SUBLANES), :] = t[:, s * LANES:(s + 1) * LANES]


def _load_token_tiles(ref):
    rows = ref.shape[0] // SUBLANES
    return jnp.concatenate([ref[pl.ds(s, rows, stride=SUBLANES), :] for s in range(SUBLANES)],
                           axis=1)


def _layer_norm(z, g, b):
    mu = jnp.mean(z, axis=-1, keepdims=True)
    zc = z - mu
    var = jnp.mean(zc * zc, axis=-1, keepdims=True)
    return zc * lax.rsqrt(var + LN_EPS) * g + b


def _proj_kernel(x_ref, win_ref, w1_ref, a1_ref, g1_ref, mul_ref, w2_ref, a2_ref, g2_ref,
                 w0_ref, a0_ref, rkv_ref, qkv_ref, lw_ref, ai_ref, g_ref, prev_ref, *, d_rkv):
    @pl.when(pl.program_id(1) == 0)
    def _():
        prev_ref[...] = jnp.zeros_like(prev_ref)

    h = x_ref[0]
    rows = h.shape[0]
    hprev = _shift_rows(h, prev_ref[0:1, :])
    prev_ref[0:1, :] = h[rows - 1:rows, :]

    dh = hprev - h
    hb = h.astype(BF16)
    low_w = _dot(h + dh * mul_ref[0:1, :], w1_ref[...])
    low_a = _dot(h + dh * mul_ref[1:2, :], a1_ref[...])
    low_g = _dot(h + dh * mul_ref[2:3, :], g1_ref[...])
    rkv_ref[0] = _dot(hb, win_ref[:, :d_rkv])
    wl = w0_ref[...] + _dot(jnp.tanh(low_w), w2_ref[...])
    al = a0_ref[...] + _dot(low_a, a2_ref[...])
    g_ref[0] = _dot(_sigmoid(low_g), g2_ref[...])
    qkv_ref[0] = _dot(hb, win_ref[:, d_rkv:])
    z = -wl
    softplus = jnp.maximum(z, 0.0) + jnp.log(1.0 + jnp.exp(-jnp.abs(z)))
    lw_ref[0] = -jnp.exp(-softplus - 0.5)
    ai_ref[0] = _sigmoid(al)


def _stage_proj(x, w_in, w1, a1, g1, mu_lora, w2, a2, g2, w0, a0):
    B, S, D = x.shape
    d_in = w_in.shape[1]
    d_r = w2.shape[1]
    d_rkv = 3 * d_r
    d_att = d_in - d_rkv
    rows = min(PROJ_ROWS, S)
    const = lambda shape: pl.BlockSpec(shape, lambda b, j: (0,) * len(shape))
    tile = lambda c: pl.BlockSpec((1, rows, c), lambda b, j: (b, j, 0))
    return pl.pallas_call(
        functools.partial(_proj_kernel, d_rkv=d_rkv),
        grid=(B, S // rows),
        in_specs=[tile(D), const((D, d_in)), const(w1.shape), const(a1.shape), const(g1.shape),
                  const(mu_lora.shape), const(w2.shape), const(a2.shape), const(g2.shape),
                  const((1, d_r)), const((1, d_r))],
        out_specs=[tile(d_rkv), tile(d_att), tile(d_r), tile(d_r), tile(d_r)],
        out_shape=[jax.ShapeDtypeStruct((B, S, d_rkv), F32),
                   jax.ShapeDtypeStruct((B, S, d_att), F32),
                   jax.ShapeDtypeStruct((B, S, d_r), F32),
                   jax.ShapeDtypeStruct((B, S, d_r), F32),
                   jax.ShapeDtypeStruct((B, S, d_r), F32)],
        scratch_shapes=[pltpu.VMEM((8, D), F32)],
        compiler_params=_params(("parallel", "arbitrary")),
        name="proj",
    )(x, w_in.astype(BF16), w1.astype(BF16), a1.astype(BF16), g1.astype(BF16), mu_lora,
      w2.astype(BF16), a2.astype(BF16), g2.astype(BF16), w0.reshape(1, d_r), a0.reshape(1, d_r))


def _wkv_kernel(rkv_ref, lw_ref, ai_ref, g_ref, mu_ref, kk_ref, ka_ref, rk_ref, lng_ref, lnb_ref,
                seg_ref, tril_ref, y_ref, s_ref, prev_ref, *, d_r, n_chunks):
    C = WKV_CHUNK
    rows = C * n_chunks

    @pl.when(pl.program_id(1) == 0)
    def _():
        s_ref[...] = jnp.zeros_like(s_ref)
        prev_ref[...] = jnp.zeros_like(prev_ref)

    rkv = rkv_ref[0]
    prev = _shift_rows(rkv, prev_ref[0:1, :])
    prev_ref[0:1, :] = rkv[rows - 1:rows, :]
    mixed = rkv + (prev - rkv) * mu_ref[...]
    r = mixed[:, :d_r]
    k = mixed[:, d_r:2 * d_r]
    v = mixed[:, 2 * d_r:]
    lw = lw_ref[0]
    a = ai_ref[0]
    seg = seg_ref[...]
    seg_w = seg.shape[0]

    def head_sum(t):
        return jnp.concatenate([_dot(t[:, j:j + seg_w], seg) for j in range(0, d_r, seg_w)],
                               axis=1)

    kk = k * kk_ref[...]
    kk = kk * lax.rsqrt(jnp.maximum(head_sum(kk * kk), 1e-24))
    k = k * (1.0 + (a - 1.0) * ka_ref[...])
    aa = -kk
    bb = kk * a
    c = _dot_exact_lhs(tril_ref[...], lw)
    e_neg = jnp.exp(-c)
    a_t = aa * jnp.exp(c - lw)
    b_t = bb * e_neg
    k_t = k * e_neg
    r_t = r * jnp.exp(c)

    lane = lax.broadcasted_iota(jnp.int32, (1, LANES), 1)
    head0 = lane < HEAD_DIM
    ri = lax.broadcasted_iota(jnp.int32, (2 * C, 2 * C), 0)
    ci = lax.broadcasted_iota(jnp.int32, (2 * C, 2 * C), 1)
    strict = (ri % C) > (ci % C)
    incl = (ri % C) >= (ci % C)
    eye = (ri == ci).astype(F32)
    zero_bf = jnp.zeros((2 * C, LANES), BF16)

    def stack(t):
        return jnp.concatenate([jnp.where(head0, t, 0.0), jnp.where(head0, 0.0, t)],
                               axis=0).astype(BF16)

    def fold(t):
        return t[:C] + t[C:]

    n_pairs = d_r // LANES
    folds = [(q, p) for q in range(n_chunks) for p in range(n_pairs)]
    each = lambda fn, *cols: [fn(*args) for args in zip(*cols)]
    cat0 = lambda *ts: jnp.concatenate(ts, axis=0)
    cat1 = lambda *ts: jnp.concatenate(ts, axis=1)
    left = lambda t: t[:, :2 * C]
    right = lambda t: t[:, 2 * C:]

    p_ends, b_es, k_es = [], [], []
    for q in range(n_chunks):
        rs = slice(q * C, (q + 1) * C)
        c_end = c[rs][C - 1:C, :]
        e_end = jnp.exp(c_end - c[rs])
        b_es.append(bb[rs] * e_end)
        k_es.append(k[rs] * e_end)
        p_ends.append(jnp.exp(c_end))

    def stacks(t_of_q):
        return [stack(t_of_q(q)[:, p * LANES:(p + 1) * LANES]) for q, p in folds]

    chunk = lambda t: (lambda q: t[q * C:(q + 1) * C])
    am, bm, km, rm, vm = (stacks(chunk(t)) for t in (a_t, b_t, k_t, r_t, v))
    bem = stacks(lambda q: b_es[q])
    kem = stacks(lambda q: k_es[q])

    x = each(lambda a_, r_, b_, k_: _dot_nt(cat0(a_, r_), cat0(b_, k_)), am, rm, bm, km)
    n = each(lambda t: jnp.where(strict, t[:2 * C, :2 * C], 0.0), x)
    a_ak = each(lambda t: jnp.where(strict, t[:2 * C, 2 * C:], 0.0).astype(BF16), x)
    a_rb = each(lambda t: jnp.where(incl, t[2 * C:, :2 * C], 0.0).astype(BF16), x)
    a_rk = each(lambda t: jnp.where(incl, t[2 * C:, 2 * C:], 0.0).astype(BF16), x)
    t_inv = each(lambda t: eye + t, n)
    m = each(lambda t: _dot(t, t), n)
    av = each(lambda a_, v_: _dot(a_, v_).astype(BF16), a_ak, vm)
    for _ in range(int(math.log2(C)) - 2):
        z = each(lambda m_, t_: _dot(m_, cat1(m_.astype(BF16), t_.astype(BF16))), m, t_inv)
        t_inv = each(lambda t_, z_: t_ + right(z_), t_inv, z)
        m = each(left, z)
    t_inv = each(lambda t_, m_: t_ + _dot(m_, t_), t_inv, m)
    z = each(lambda t_, a_, av_: _dot(t_, cat1(a_, av_)), t_inv, am, av)
    ap = each(lambda z_: left(z_).astype(BF16), z)
    u0 = each(lambda z_: right(z_).astype(BF16), z)
    z = each(lambda rb_, rk_, ap_, u_, v_: _dot(cat1(rb_, rk_),
                                               cat0(cat1(ap_, u_), cat1(zero_bf, v_))),
             a_rb, a_rk, ap, u0, vm)
    g_m = each(lambda ap_, be_: _dot_tn(ap_, be_).astype(BF16), ap, bem)
    d0 = each(lambda u_, v_, be_, ke_: _dot_tn(cat0(u_, v_), cat0(be_, ke_)), u0, vm, bem, kem)
    rp = each(lambda r_, z_: (r_.astype(F32) + left(z_)).astype(BF16), rm, z)
    y0 = each(lambda z_: fold(right(z_)), z)

    s = [s_ref[p] for p in range(n_pairs)]
    y_cols = []
    for q in range(n_chunks):
        s_bf = [t.astype(BF16) for t in s]
        f0 = q * n_pairs
        y_cols.append([fold(_dot_nt(rp[f0 + p], s_bf[p])) + y0[f0 + p] for p in range(n_pairs)])
        s = [s[p] * p_ends[q][:, p * LANES:(p + 1) * LANES] + _dot(s_bf[p], g_m[f0 + p])
             + d0[f0 + p] for p in range(n_pairs)]
    for p in range(n_pairs):
        s_ref[p] = s[p]
    y = cat0(*[cat1(*cols) for cols in y_cols])

    inv = 1.0 / HEAD_DIM
    mu = head_sum(y) * inv
    yc = y - mu
    var = head_sum(yc * yc) * inv
    yn = yc * lax.rsqrt(var + LNX_EPS) * lng_ref[...] + lnb_ref[...]
    bonus = head_sum(r * k * rk_ref[...]) * v
    y_ref[0] = ((yn + bonus) * g_ref[0]).astype(y_ref.dtype)


def _stage_wkv(rkv, lw, ai, g, mu_rkv, k_k, k_a, r_k, lnx_g, lnx_b):
    B, S, d_r = lw.shape
    n_chunks = WKV_CHUNKS_PER_STEP
    rows = WKV_CHUNK * n_chunks
    head = np.arange(MXU_DEPTH) // HEAD_DIM
    seg = jnp.asarray(head[:, None] == head[None, :], BF16)
    t = np.arange(rows)
    tril = jnp.asarray((t[:, None] >= t[None, :])
                       & (t[:, None] // WKV_CHUNK == t[None, :] // WKV_CHUNK), BF16)
    const = lambda shape: pl.BlockSpec(shape, lambda b, j: (0,) * len(shape))
    tile = lambda c: pl.BlockSpec((1, rows, c), lambda b, j: (b, j, 0))
    row = lambda t: t.reshape(1, -1)
    return pl.pallas_call(
        functools.partial(_wkv_kernel, d_r=d_r, n_chunks=n_chunks),
        grid=(B, S // rows),
        in_specs=[tile(3 * d_r), tile(d_r), tile(d_r), tile(d_r), const((1, 3 * d_r)),
                  const((1, d_r)), const((1, d_r)), const((1, d_r)), const((1, d_r)),
                  const((1, d_r)), const((MXU_DEPTH, MXU_DEPTH)), const((rows, rows))],
        out_specs=tile(d_r),
        out_shape=jax.ShapeDtypeStruct((B, S, d_r), BF16),
        scratch_shapes=[pltpu.VMEM((d_r // LANES, LANES, LANES), F32),
                        pltpu.VMEM((8, 3 * d_r), F32)],
        compiler_params=_params(("parallel", "arbitrary")),
        name="wkv",
    )(rkv, lw, ai, g, row(mu_rkv), row(k_k), row(k_a), row(r_k), row(lnx_g), row(lnx_b), seg, tril)


def _t5_bucket(n):
    exact = N_BUCKETS // 2
    nf = np.maximum(n, 1).astype(np.float32)
    large = exact + (np.log(nf / exact) / math.log(MAX_DISTANCE / exact)
                     * (N_BUCKETS - exact)).astype(np.int32)
    large = np.minimum(large, N_BUCKETS - 1)
    return np.where(n < exact, n, large).astype(np.int32)


def _band_bias(rel_bias, dil):
    blk = ATT_BLK
    L = 3 * blk
    rel = (2 * blk - 1) - np.arange(L)
    valid = (rel >= 0) & (rel <= blk)
    base = rel_bias[_t5_bucket(np.clip(rel, 0, None) * dil)].astype(F32).T
    base = jnp.where(valid[None], base, NEG_INF)
    skew = jnp.tile(base, (1, blk))[:, :blk * (L - 1)].reshape(-1, blk, L - 1)
    return skew[:, :, blk - 1:3 * blk - 1]


def _attend(units, bias_ref, scale):
    lane = lax.broadcasted_iota(jnp.int32, (1, LANES), 1)
    head0 = lane < HEAD_DIM
    jobs = [(u, h) for u in range(len(units)) for h in range(2)]
    s = []
    for u, h in jobs:
        q, k, _, col0 = units[u]
        qh = jnp.where(head0 if h == 0 else jnp.logical_not(head0), q * scale, 0.0)
        s.append(_dot_nt(qh, k) + bias_ref[h, :, col0:col0 + k.shape[0]])
    m = [jnp.max(t, axis=-1, keepdims=True) for t in s]
    e = [jnp.exp2(t - m_) for t, m_ in zip(s, m)]
    l = [jnp.sum(t, axis=-1, keepdims=True) for t in e]
    o = [_dot(t, units[u][2]) / l_ for t, l_, (u, h) in zip(e, l, jobs)]
    lse = [m_ + jnp.log2(l_) for m_, l_ in zip(m, l)]
    return [(jnp.where(head0, o[2 * u], o[2 * u + 1]), jnp.where(head0, lse[2 * u], lse[2 * u + 1]))
            for u in range(len(units))]


def _attn_kernel(q_ref, k_ref, v_ref, b1_ref, b4_ref, b16_ref, o_ref, acc_o, acc_l, *, seq, scale):
    blk = ATT_BLK
    group = 4

    def rows(ref, start, n, dil):
        if dil == 1:
            return ref[0, pl.ds(start, n), :]
        return ref[0, pl.ds(start, n, stride=dil), :]

    def unit(start, dil, has_prev):
        q = rows(q_ref, start, blk, dil)
        if has_prev:
            k0 = start - blk * dil
            return (q, rows(k_ref, k0, 2 * blk, dil), rows(v_ref, k0, 2 * blk, dil), 0)
        return (q, rows(k_ref, start, blk, dil), rows(v_ref, start, blk, dil), blk)

    def acc_rows(ref, start, dil):
        if dil == 1:
            return ref.at[pl.ds(start, blk), :]
        return ref.at[pl.ds(start, blk, stride=dil), :]

    def run(starts, dil, prevs, bias_ref, first, last):
        units = [unit(s, dil, hp) for s, hp in zip(starts, prevs)]
        for s, (o, lse) in zip(starts, _attend(units, bias_ref, scale)):
            ao, al = acc_rows(acc_o, s, dil), acc_rows(acc_l, s, dil)
            if first:
                ao[...] = o
                al[...] = lse
            else:
                o_old, l_old = ao[...], al[...]
                m = jnp.maximum(l_old, lse)
                e_old, e_new = jnp.exp2(l_old - m), jnp.exp2(lse - m)
                ao[...] = (e_old * o_old + e_new * o) / (e_old + e_new)
                if not last:
                    al[...] = m + jnp.log2(e_old + e_new)

    run([n * blk for n in range(group)], 1, [n > 0 for n in range(group)], b1_ref, True, False)

    def body1(g, c):
        base = pl.multiple_of(g * (group * blk), group * blk)
        run([base + n * blk for n in range(group)], 1, [True] * group, b1_ref, True, False)
        return c
    lax.fori_loop(1, seq // (group * blk), body1, 0)

    nb4 = seq // 4 // blk

    def body4(r, c):
        run([r + n * blk * 4 for n in range(nb4)], 4, [n > 0 for n in range(nb4)], b4_ref,
            False, False)
        return c
    lax.fori_loop(0, 4, body4, 0)

    def body16(g, c):
        run([g * group + j for j in range(group)], 16, [False] * group, b16_ref, False, True)
        return c
    lax.fori_loop(0, 16 // group, body16, 0)

    o_ref[0] = acc_o[...].astype(o_ref.dtype)


def _stage_attn(qkv, rel_bias):
    B, S, d3 = qkv.shape
    d_att = d3 // 3
    n_pairs = d_att // LANES
    assert DIL_PATTERNS == ((ATT_BLK, 1), (4 * ATT_BLK, 4), (16 * ATT_BLK, 16))
    assert S == 16 * ATT_BLK
    log2e = math.log2(math.e)
    biases = [_band_bias(rel_bias, dil) * log2e for _, dil in DIL_PATTERNS]
    part = lambda i: pl.BlockSpec((1, S, LANES), lambda b, p: (b, 0, i * n_pairs + p))
    bias_spec = pl.BlockSpec((2, ATT_BLK, 2 * ATT_BLK), lambda b, p: (p, 0, 0))
    return pl.pallas_call(
        functools.partial(_attn_kernel, seq=S, scale=HEAD_DIM ** -0.5 * log2e),
        grid=(B, n_pairs),
        in_specs=[part(0), part(1), part(2), bias_spec, bias_spec, bias_spec],
        out_specs=pl.BlockSpec((1, S, LANES), lambda b, p: (b, 0, p)),
        out_shape=jax.ShapeDtypeStruct((B, S, d_att), BF16),
        scratch_shapes=[pltpu.VMEM((S, LANES), F32), pltpu.VMEM((S, LANES), F32)],
        compiler_params=_params(("parallel", "parallel")),
        name="attn",
    )(qkv, qkv, qkv, *biases)


def _mix_kernel(x_ref, yr_ref, att_ref, p_ref, wor_ref, woa_ref, g_ref, b_ref, rhi_ref, rlo_ref,
                rb_ref, pg_ref, pp_ref, x1_ref, x1t_ref, lg_ref, ple_ref, *, alpha):
    mix = _dot(yr_ref[...], wor_ref[...]) + _dot(att_ref[...], woa_ref[...])
    emb = _dot(p_ref[...], pp_ref[...])
    x1 = _layer_norm(alpha * x_ref[...] + mix, g_ref[...], b_ref[...])
    x1_ref[...] = x1
    _store_token_tiles(x1t_ref, x1)
    hi, lo = _split(x1)
    logits = (jnp.dot(hi, rhi_ref[...], preferred_element_type=F32)
              + jnp.dot(lo, rhi_ref[...], preferred_element_type=F32)
              + jnp.dot(hi, rlo_ref[...], preferred_element_type=F32) + rb_ref[...])
    ple_ref[...] = (_sigmoid(_dot(hi, pg_ref[...])) * emb).astype(ple_ref.dtype)

    lane = lax.broadcasted_iota(jnp.int32, logits.shape, 1).astype(F32)
    big = float(ROUTER_LANES)
    rmax = lambda t: jnp.max(t, axis=-1, keepdims=True)
    first = lambda hit: jnp.min(jnp.where(hit, lane, big), axis=-1, keepdims=True)
    off = float("-inf")
    is_grp = lane < N_GROUPS
    lg = jnp.where(is_grp, logits, off)
    mg = rmax(lg)
    wg = 1.0 / jnp.sum(jnp.where(is_grp, jnp.exp(lg - mg), 0.0), axis=-1, keepdims=True)
    lo_lane = N_GROUPS + EXPERTS_PER_GROUP * first(lg == mg)
    le = jnp.where((lane >= lo_lane) & (lane < lo_lane + EXPERTS_PER_GROUP), logits, off)
    v1 = rmax(le)
    i1 = first(le == v1)
    le = jnp.where(lane == i1, off, le)
    v2 = rmax(le)
    i2 = first(le == v2)
    e2 = jnp.exp(v2 - v1)
    w1 = wg / (1.0 + e2)
    route = jnp.where(lane == 0, w1, jnp.where(lane == 1, w1 * e2, jnp.where(
        lane == 2, i1 - N_GROUPS, jnp.where(lane == 3, i2 - N_GROUPS, 0.0))))
    lg_ref[...] = route


def _stage_mix(x2, yr, att, p2, w_o, ln_g, ln_b, router_g, router_g_b, router_e, router_e_b,
               ple_gate, ple_proj, alpha):
    N, D = x2.shape
    d_r = yr.shape[1]
    rows = min(MIX_ROWS, N)
    d_att = att.shape[1]
    n_log = N_GROUPS + N_EXPERTS
    rw = jnp.zeros((D, ROUTER_LANES), F32).at[:, :N_GROUPS].set(router_g)
    rw = rw.at[:, N_GROUPS:n_log].set(router_e)
    rb = jnp.zeros((1, ROUTER_LANES), F32).at[0, :N_GROUPS].set(router_g_b)
    rb = rb.at[0, N_GROUPS:n_log].set(router_e_b)
    rhi, rlo = _split(rw)
    const = lambda shape: pl.BlockSpec(shape, lambda i: (0,) * len(shape))
    tile = lambda c: pl.BlockSpec((rows, c), lambda i: (i, 0))
    return pl.pallas_call(
        functools.partial(_mix_kernel, alpha=alpha),
        grid=(N // rows,),
        in_specs=[tile(D), tile(d_r), tile(d_att), tile(p2.shape[1]),
                  const((d_r, D)), const((d_att, D)), const((1, D)), const((1, D)),
                  const((D, ROUTER_LANES)), const((D, ROUTER_LANES)), const((1, ROUTER_LANES)),
                  const((D, D)), const(ple_proj.shape)],
        out_specs=[tile(D), pl.BlockSpec((rows * SUBLANES, LANES), lambda i: (i, 0)),
                   tile(ROUTER_LANES), tile(D)],
        out_shape=[jax.ShapeDtypeStruct((N, D), F32),
                   jax.ShapeDtypeStruct((N * SUBLANES, LANES), F32),
                   jax.ShapeDtypeStruct((N, ROUTER_LANES), F32),
                   jax.ShapeDtypeStruct((N, D), BF16)],
        compiler_params=_params(("parallel",)),
        name="mix",
    )(x2, yr, att, p2, w_o[:d_r].astype(BF16), w_o[d_r:].astype(BF16),
      ln_g.reshape(1, D), ln_b.reshape(1, D), rhi, rlo, rb, ple_gate.astype(BF16),
      ple_proj.astype(BF16))


def _route(route, rows):
    N = route.shape[0]
    we = route[:, :TOP_K]
    eid = route[:, TOP_K:2 * TOP_K].astype(jnp.int32)
    A = N * TOP_K
    e_flat = eid.T.reshape(A).astype(jnp.int32)
    order = jnp.argsort(e_flat).astype(jnp.int32)
    counts = jnp.sum((e_flat[:, None] == jnp.arange(N_EXPERTS)[None, :]).astype(jnp.int32), axis=0)
    start = jnp.cumsum(counts) - counts
    padded = (counts + rows - 1) // rows * rows
    pend = jnp.cumsum(padded)
    pstart = pend - padded
    nblk = -(-A // rows) + N_EXPERTS
    nused = (pend[-1] // rows).astype(jnp.int32)
    blk = jnp.arange(nblk, dtype=jnp.int32)
    blk_e = jnp.sum((pend[None, :] <= (blk * rows)[:, None]).astype(jnp.int32), axis=1)
    blk_e = jnp.minimum(blk_e, N_EXPERTS - 1)
    blk_e = jnp.where(blk < nused, blk_e, blk_e[nused - 1]).astype(jnp.int32)
    r = jnp.arange(rows, dtype=jnp.int32)[None, :]
    off = (blk * rows - pstart[blk_e])[:, None] + r
    real = (off < counts[blk_e][:, None]) & (blk < nused)[:, None]
    asg = order[jnp.clip(start[blk_e][:, None] + off, 0, A - 1)]
    row_tok = jnp.where(real, asg % N, r).astype(jnp.int32)
    row_dst = jnp.where(real, asg, A + (blk % 2)[:, None] * rows + r).astype(jnp.int32)
    return (row_tok.reshape(nblk, 1, rows), row_dst.reshape(nblk, 1, rows), blk_e,
            nused.reshape(1), we)


def _moe_kernel(blk_e_ref, nused_ref, tok0_ref, tokn_ref, dstp_ref, x_hbm, wg_ref, wu_ref, wd_ref,
                y_hbm, xbuf, ybuf, gsem, ssem, *, rows, n_real):
    i = pl.program_id(0)
    nused = nused_ref[0]
    slot = i % 2
    tile_rows = lambda first: pl.ds(pl.multiple_of(first, SUBLANES), SUBLANES)

    def gather_row(tab_ref, sl, r):
        return pltpu.make_async_copy(x_hbm.at[tile_rows(tab_ref[0, 0, r])],
                                     xbuf.at[sl, pl.ds(r * SUBLANES, SUBLANES)], gsem.at[sl])

    def scatter_row(sl, r):
        return pltpu.make_async_copy(ybuf.at[sl, pl.ds(r * SUBLANES, SUBLANES)],
                                     y_hbm.at[tile_rows(dstp_ref[0, 0, r])], ssem.at[sl])

    def gather_all(sl):
        return pltpu.make_async_copy(x_hbm.at[pl.ds(0, rows * SUBLANES)], xbuf.at[sl],
                                     gsem.at[sl])

    def scatter_all(sl, first_row=0):
        return pltpu.make_async_copy(
            ybuf.at[sl], y_hbm.at[pl.ds(first_row * SUBLANES, rows * SUBLANES)], ssem.at[sl])

    def rolled(fn):
        def body(r, c):
            fn(r)
            return c
        lax.fori_loop(0, rows, body, 0, unroll=8)

    def inline(fn):
        for r in range(rows):
            fn(r)

    def ffn(sl):
        xb = _load_token_tiles(xbuf.at[sl]).astype(BF16)
        gate = _dot(xb, wg_ref[0])
        up = _dot(xb, wu_ref[0])
        _store_token_tiles(ybuf.at[sl], _dot(gate * _sigmoid(gate) * up, wd_ref[0]))

    @pl.when(i == 0)
    def _():
        ybuf[...] = jnp.zeros_like(ybuf)
        for sl in range(2):
            scatter_all(sl, n_real + sl * rows).start()
        for sl in range(2):
            scatter_all(sl, n_real + sl * rows).wait()
        rolled(lambda r: gather_row(tok0_ref, 0, r).start())

    @pl.when(i < nused)
    def _():
        gather_all(slot).wait()

    @pl.when((i >= 2) & (i <= nused))
    def _():
        scatter_all(slot).wait()

    @pl.when(i == 0)
    def _():
        inline(lambda r: gather_row(tokn_ref, 1, r).start())
        ffn(0)

    for s in range(2):
        @pl.when((i >= 1) & (i < nused) & (slot == s))
        def _():
            inline(lambda r: gather_row(tokn_ref, 1 - s, r).start())
            inline(lambda r: scatter_row(1 - s, r).start())
            ffn(s)

    @pl.when(i == nused)
    def _():
        rolled(lambda r: scatter_row(1 - slot, r).start())
        scatter_all(1 - slot).wait()
        gather_all(slot).wait()


def _stage_moe(x1t, row_tok, row_dst, blk_e, nused, w_gate, w_up, w_down):
    N = x1t.shape[0] // SUBLANES
    D = SUBLANES * LANES
    nblk, _, rows = row_tok.shape
    d_e = w_gate.shape[2]
    n_real = N * TOP_K
    last = nblk - 1
    wspec = lambda shape: pl.BlockSpec((1,) + shape,
                                       lambda i, be, nu: (be[jnp.minimum(i, last)], 0, 0))
    table = lambda fn: pl.BlockSpec((1, 1, rows), lambda i, be, nu: (fn(i), 0, 0),
                                    memory_space=pltpu.SMEM)
    grid_spec = pltpu.PrefetchScalarGridSpec(
        num_scalar_prefetch=2,
        grid=(nblk + 1,),
        in_specs=[table(lambda i: 0), table(lambda i: jnp.minimum(i + 1, last)),
                  table(lambda i: jnp.clip(i - 1, 0, last)),
                  pl.BlockSpec(memory_space=pl.ANY), wspec((D, d_e)), wspec((D, d_e)),
                  wspec((d_e, D))],
        out_specs=pl.BlockSpec(memory_space=pl.ANY),
        scratch_shapes=[pltpu.VMEM((2, rows * SUBLANES, LANES), F32),
                        pltpu.VMEM((2, rows * SUBLANES, LANES), F32),
                        pltpu.SemaphoreType.DMA((2,)), pltpu.SemaphoreType.DMA((2,))],
    )
    return pl.pallas_call(
        functools.partial(_moe_kernel, rows=rows, n_real=n_real),
        grid_spec=grid_spec,
        out_shape=jax.ShapeDtypeStruct(((n_real + 2 * rows) * SUBLANES, LANES), F32),
        compiler_params=_params(("arbitrary",)),
        name="moe",
    )(blk_e, nused, row_tok * SUBLANES, row_tok * SUBLANES, row_dst * SUBLANES, x1t, w_gate, w_up,
      w_down)


def _out_kernel(x1_ref, y0_ref, y1_ref, we_ref, ple_ref, g_ref, b_ref, o_ref, *, alpha):
    we = we_ref[...]
    moe = we[:, 0:1] * _load_token_tiles(y0_ref) + we[:, 1:2] * _load_token_tiles(y1_ref)
    o_ref[...] = _layer_norm(alpha * x1_ref[...] + moe + ple_ref[...], g_ref[...], b_ref[...])


def _stage_out(x1, yb, we, ple, ln_g, ln_b, alpha):
    N, D = x1.shape
    rows = min(OUT_ROWS, N)
    const = lambda shape: pl.BlockSpec(shape, lambda i: (0,) * len(shape))
    tile = lambda c: pl.BlockSpec((rows, c), lambda i: (i, 0))
    slot = lambda j: pl.BlockSpec((rows * SUBLANES, LANES), lambda i: (i + j * (N // rows), 0))
    return pl.pallas_call(
        functools.partial(_out_kernel, alpha=alpha),
        grid=(N // rows,),
        in_specs=[tile(D), slot(0), slot(1), tile(TOP_K), tile(D), const((1, D)), const((1, D))],
        out_specs=tile(D),
        out_shape=jax.ShapeDtypeStruct((N, D), F32),
        compiler_params=_params(("parallel",)),
        name="out",
    )(x1, yb, yb, we, ple, ln_g.reshape(1, D), ln_b.reshape(1, D))


def kernel(x, p, w_in, mu_rkv, mu_lora, w0, w_lora1, w_lora2, a0, a_lora1, a_lora2, g_lora1, g_lora2, k_k, k_a, r_k, lnx_g, lnx_b, rel_bias, w_o, ln1_g, ln1_b, router_g, router_g_b, router_e, router_e_b, w_gate, w_up, w_down, ple_gate, ple_proj, ln2_g, ln2_b):
    B, S, D = x.shape
    depth = w_in.shape[0]
    alpha = (2 * depth) ** 0.25
    for i in range(depth):
        rkv, qkv, lw, ai, g = _stage_proj(x, w_in[i], w_lora1[i], a_lora1[i], g_lora1[i],
                                          mu_lora[i], w_lora2[i], a_lora2[i], g_lora2[i],
                                          w0[i], a0[i])
        yr = _stage_wkv(rkv, lw, ai, g, mu_rkv[i], k_k[i], k_a[i], r_k[i], lnx_g[i], lnx_b[i])
        att = _stage_attn(qkv, rel_bias).reshape(B * S, -1)
        x1, x1t, route, ple = _stage_mix(x.reshape(B * S, D), yr.reshape(B * S, -1), att,
                                          p[i].reshape(B * S, -1), w_o[i], ln1_g[i], ln1_b[i],
                                          router_g[i], router_g_b[i], router_e[i],
                                          router_e_b[i], ple_gate[i], ple_proj[i], alpha)
        row_tok, row_dst, blk_e, nused, we = _route(route, MOE_ROWS)
        yb = _stage_moe(x1t, row_tok, row_dst, blk_e, nused, w_gate[i], w_up[i], w_down[i])
        x = _stage_out(x1, yb, we, ple, ln2_g[i], ln2_b[i], alpha).reshape(B, S, D)
    return x
```

```python
import functools
import math

import jax
import jax.numpy as jnp
import numpy as np
from jax import lax
from jax.experimental import pallas as pl
from jax.experimental.pallas import tpu as pltpu

F32 = jnp.float32
BF16 = jnp.bfloat16

HEAD_DIM = 64
LANES = 128
SUBLANES = 8
MXU_DEPTH = 256
DECAY_LORA = 64
ICLR_LORA = 64
GATE_LORA = 128
LNX_EPS = 64e-5
DIL_PATTERNS = ((128, 1), (512, 4), (2048, 16))
NEG_INF = -1e30
N_BUCKETS = 32
MAX_DISTANCE = 2048
N_GROUPS = 4
EXPERTS_PER_GROUP = 8
N_EXPERTS = N_GROUPS * EXPERTS_PER_GROUP
TOP_K = 2
LN_EPS = 1e-5

WKV_CHUNK = 64
WKV_CHUNKS_PER_STEP = 4
ATT_BLK = 128
PROJ_ROWS = 512
MIX_ROWS = 512
MOE_ROWS = 256
OUT_ROWS = 512
ROUTER_LANES = 128
VMEM_LIMIT = 56 * 1024 * 1024


def _params(sem):
    return pltpu.CompilerParams(dimension_semantics=sem, vmem_limit_bytes=VMEM_LIMIT)


def _dot(a, b):
    return jnp.dot(a.astype(BF16), b.astype(BF16), preferred_element_type=F32)


def _dot_nt(a, b):
    return lax.dot_general(a.astype(BF16), b.astype(BF16), (((1,), (1,)), ((), ())),
                           preferred_element_type=F32)


def _dot_tn(a, b):
    return lax.dot_general(a.astype(BF16), b.astype(BF16), (((0,), (0,)), ((), ())),
                           preferred_element_type=F32)


def _split(x):
    hi = x.astype(BF16)
    lo = (x - hi.astype(F32)).astype(BF16)
    return hi, lo


def _dot_exact_lhs(w01, x):
    hi, lo = _split(x)
    return (jnp.dot(w01, hi, preferred_element_type=F32)
            + jnp.dot(w01, lo, preferred_element_type=F32))


def _sigmoid(x):
    return 1.0 / (1.0 + jnp.exp(-x))


def _shift_rows(t, prev_row):
    rolled = pltpu.roll(t, 1, 0)
    row = lax.broadcasted_iota(jnp.int32, t.shape, 0)
    return jnp.where(row == 0, prev_row, rolled)


def _store_token_tiles(ref, t):
    rows = t.shape[0]
    for s in range(SUBLANES):
        ref[pl.ds(s, rows, stride=SUBLANES), :] = t[:, s * LANES:(s + 1) * LANES]


def _load_token_tiles(ref):
    rows = ref.shape[0] // SUBLANES
    return jnp.concatenate([ref[pl.ds(s, rows, stride=SUBLANES), :] for s in range(SUBLANES)],
                           axis=1)


def _layer_norm(z, g, b):
    mu = jnp.mean(z, axis=-1, keepdims=True)
    zc = z - mu
    var = jnp.mean(zc * zc, axis=-1, keepdims=True)
    return zc * lax.rsqrt(var + LN_EPS) * g + b


def _proj_kernel(x_ref, win_ref, w1_ref, a1_ref, g1_ref, mul_ref, w2_ref, a2_ref, g2_ref,
                 w0_ref, a0_ref, rkv_ref, qkv_ref, lw_ref, ai_ref, g_ref, prev_ref, *, d_rkv):
    @pl.when(pl.program_id(1) == 0)
    def _():
        prev_ref[...] = jnp.zeros_like(prev_ref)

    h = x_ref[0]
    rows = h.shape[0]
    hprev = _shift_rows(h, prev_ref[0:1, :])
    prev_ref[0:1, :] = h[rows - 1:rows, :]

    dh = hprev - h
    hb = h.astype(BF16)
    low_w = _dot(h + dh * mul_ref[0:1, :], w1_ref[...])
    low_a = _dot(h + dh * mul_ref[1:2, :], a1_ref[...])
    low_g = _dot(h + dh * mul_ref[2:3, :], g1_ref[...])
    rkv_ref[0] = _dot(hb, win_ref[:, :d_rkv])
    wl = w0_ref[...] + _dot(jnp.tanh(low_w), w2_ref[...])
    al = a0_ref[...] + _dot(low_a, a2_ref[...])
    g_ref[0] = _dot(_sigmoid(low_g), g2_ref[...])
    qkv_ref[0] = _dot(hb, win_ref[:, d_rkv:])
    z = -wl
    softplus = jnp.maximum(z, 0.0) + jnp.log(1.0 + jnp.exp(-jnp.abs(z)))
    lw_ref[0] = -jnp.exp(-softplus - 0.5)
    ai_ref[0] = _sigmoid(al)


def _stage_proj(x, w_in, w1, a1, g1, mu_lora, w2, a2, g2, w0, a0):
    B, S, D = x.shape
    d_in = w_in.shape[1]
    d_r = w2.shape[1]
    d_rkv = 3 * d_r
    d_att = d_in - d_rkv
    rows = min(PROJ_ROWS, S)
    const = lambda shape: pl.BlockSpec(shape, lambda b, j: (0,) * len(shape))
    tile = lambda c: pl.BlockSpec((1, rows, c), lambda b, j: (b, j, 0))
    return pl.pallas_call(
        functools.partial(_proj_kernel, d_rkv=d_rkv),
        grid=(B, S // rows),
        in_specs=[tile(D), const((D, d_in)), const(w1.shape), const(a1.shape), const(g1.shape),
                  const(mu_lora.shape), const(w2.shape), const(a2.shape), const(g2.shape),
                  const((1, d_r)), const((1, d_r))],
        out_specs=[tile(d_rkv), tile(d_att), tile(d_r), tile(d_r), tile(d_r)],
        out_shape=[jax.ShapeDtypeStruct((B, S, d_rkv), F32),
                   jax.ShapeDtypeStruct((B, S, d_att), F32),
                   jax.ShapeDtypeStruct((B, S, d_r), F32),
                   jax.ShapeDtypeStruct((B, S, d_r), F32),
                   jax.ShapeDtypeStruct((B, S, d_r), F32)],
        scratch_shapes=[pltpu.VMEM((8, D), F32)],
        compiler_params=_params(("parallel", "arbitrary")),
        name="proj",
    )(x, w_in.astype(BF16), w1.astype(BF16), a1.astype(BF16), g1.astype(BF16), mu_lora,
      w2.astype(BF16), a2.astype(BF16), g2.astype(BF16), w0.reshape(1, d_r), a0.reshape(1, d_r))


def _wkv_kernel(rkv_ref, lw_ref, ai_ref, g_ref, mu_ref, kk_ref, ka_ref, rk_ref, lng_ref, lnb_ref,
                seg_ref, tril_ref, y_ref, s_ref, prev_ref, *, d_r, n_chunks):
    C = WKV_CHUNK
    rows = C * n_chunks

    @pl.when(pl.program_id(1) == 0)
    def _():
        s_ref[...] = jnp.zeros_like(s_ref)
        prev_ref[...] = jnp.zeros_like(prev_ref)

    rkv = rkv_ref[0]
    prev = _shift_rows(rkv, prev_ref[0:1, :])
    prev_ref[0:1, :] = rkv[rows - 1:rows, :]
    mixed = rkv + (prev - rkv) * mu_ref[...]
    r = mixed[:, :d_r]
    k = mixed[:, d_r:2 * d_r]
    v = mixed[:, 2 * d_r:]
    lw = lw_ref[0]
    a = ai_ref[0]
    seg = seg_ref[...]
    seg_w = seg.shape[0]

    def head_sum(t):
        return jnp.concatenate([_dot(t[:, j:j + seg_w], seg) for j in range(0, d_r, seg_w)],
                               axis=1)

    kk = k * kk_ref[...]
    kk = kk * lax.rsqrt(jnp.maximum(head_sum(kk * kk), 1e-24))
    k = k * (1.0 + (a - 1.0) * ka_ref[...])
    aa = -kk
    bb = kk * a
    c = _dot_exact_lhs(tril_ref[...], lw)
    e_neg = jnp.exp(-c)
    a_t = aa * jnp.exp(c - lw)
    b_t = bb * e_neg
    k_t = k * e_neg
    r_t = r * jnp.exp(c)

    lane = lax.broadcasted_iota(jnp.int32, (1, LANES), 1)
    head0 = lane < HEAD_DIM
    ri = lax.broadcasted_iota(jnp.int32, (2 * C, 2 * C), 0)
    ci = lax.broadcasted_iota(jnp.int32, (2 * C, 2 * C), 1)
    strict = (ri % C) > (ci % C)
    incl = (ri % C) >= (ci % C)
    eye = (ri == ci).astype(F32)
    zero_bf = jnp.zeros((2 * C, LANES), BF16)

    def stack(t):
        return jnp.concatenate([jnp.where(head0, t, 0.0), jnp.where(head0, 0.0, t)],
                               axis=0).astype(BF16)

    def fold(t):
        return t[:C] + t[C:]

    n_pairs = d_r // LANES
    folds = [(q, p) for q in range(n_chunks) for p in range(n_pairs)]
    each = lambda fn, *cols: [fn(*args) for args in zip(*cols)]
    cat0 = lambda *ts: jnp.concatenate(ts, axis=0)
    cat1 = lambda *ts: jnp.concatenate(ts, axis=1)
    left = lambda t: t[:, :2 * C]
    right = lambda t: t[:, 2 * C:]

    p_ends, b_es, k_es = [], [], []
    for q in range(n_chunks):
        rs = slice(q * C, (q + 1) * C)
        c_end = c[rs][C - 1:C, :]
        e_end = jnp.exp(c_end - c[rs])
        b_es.append(bb[rs] * e_end)
        k_es.append(k[rs] * e_end)
        p_ends.append(jnp.exp(c_end))

    def stacks(t_of_q):
        return [stack(t_of_q(q)[:, p * LANES:(p + 1) * LANES]) for q, p in folds]

    chunk = lambda t: (lambda q: t[q * C:(q + 1) * C])
    am, bm, km, rm, vm = (stacks(chunk(t)) for t in (a_t, b_t, k_t, r_t, v))
    bem = stacks(lambda q: b_es[q])
    kem = stacks(lambda q: k_es[q])

    x = each(lambda a_, r_, b_, k_: _dot_nt(cat0(a_, r_), cat0(b_, k_)), am, rm, bm, km)
    n = each(lambda t: jnp.where(strict, t[:2 * C, :2 * C], 0.0), x)
    a_ak = each(lambda t: jnp.where(strict, t[:2 * C, 2 * C:], 0.0).astype(BF16), x)
    a_rb = each(lambda t: jnp.where(incl, t[2 * C:, :2 * C], 0.0).astype(BF16), x)
    a_rk = each(lambda t: jnp.where(incl, t[2 * C:, 2 * C:], 0.0).astype(BF16), x)
    t_inv = each(lambda t: eye + t, n)
    m = each(lambda t: _dot(t, t), n)
    av = each(lambda a_, v_: _dot(a_, v_).astype(BF16), a_ak, vm)
    for _ in range(int(math.log2(C)) - 2):
        z = each(lambda m_, t_: _dot(m_, cat1(m_.astype(BF16), t_.astype(BF16))), m, t_inv)
        t_inv = each(lambda t_, z_: t_ + right(z_), t_inv, z)
        m = each(left, z)
    t_inv = each(lambda t_, m_: t_ + _dot(m_, t_), t_inv, m)
    z = each(lambda t_, a_, av_: _dot(t_, cat1(a_, av_)), t_inv, am, av)
    ap = each(lambda z_: left(z_).astype(BF16), z)
    u0 = each(lambda z_: right(z_).astype(BF16), z)
    z = each(lambda rb_, rk_, ap_, u_, v_: _dot(cat1(rb_, rk_),
                                               cat0(cat1(ap_, u_), cat1(zero_bf, v_))),
             a_rb, a_rk, ap, u0, vm)
    g_m = each(lambda ap_, be_: _dot_tn(ap_, be_).astype(BF16), ap, bem)
    d0 = each(lambda u_, v_, be_, ke_: _dot_tn(cat0(u_, v_), cat0(be_, ke_)), u0, vm, bem, kem)
    rp = each(lambda r_, z_: (r_.astype(F32) + left(z_)).astype(BF16), rm, z)
    y0 = each(lambda z_: fold(right(z_)), z)

    s = [s_ref[p] for p in range(n_pairs)]
    y_cols = []
    for q in range(n_chunks):
        s_bf = [t.astype(BF16) for t in s]
        f0 = q * n_pairs
        y_cols.append([fold(_dot_nt(rp[f0 + p], s_bf[p])) + y0[f0 + p] for p in range(n_pairs)])
        s = [s[p] * p_ends[q][:, p * LANES:(p + 1) * LANES] + _dot(s_bf[p], g_m[f0 + p])
             + d0[f0 + p] for p in range(n_pairs)]
    for p in range(n_pairs):
        s_ref[p] = s[p]
    y = cat0(*[cat1(*cols) for cols in y_cols])

    inv = 1.0 / HEAD_DIM
    mu = head_sum(y) * inv
    yc = y - mu
    var = head_sum(yc * yc) * inv
    yn = yc * lax.rsqrt(var + LNX_EPS) * lng_ref[...] + lnb_ref[...]
    bonus = head_sum(r * k * rk_ref[...]) * v
    y_ref[0] = ((yn + bonus) * g_ref[0]).astype(y_ref.dtype)


def _stage_wkv(rkv, lw, ai, g, mu_rkv, k_k, k_a, r_k, lnx_g, lnx_b):
    B, S, d_r = lw.shape
    n_chunks = WKV_CHUNKS_PER_STEP
    rows = WKV_CHUNK * n_chunks
    head = np.arange(MXU_DEPTH) // HEAD_DIM
    seg = jnp.asarray(head[:, None] == head[None, :], BF16)
    t = np.arange(rows)
    tril = jnp.asarray((t[:, None] >= t[None, :])
                       & (t[:, None] // WKV_CHUNK == t[None, :] // WKV_CHUNK), BF16)
    const = lambda shape: pl.BlockSpec(shape, lambda b, j: (0,) * len(shape))
    tile = lambda c: pl.BlockSpec((1, rows, c), lambda b, j: (b, j, 0))
    row = lambda t: t.reshape(1, -1)
    return pl.pallas_call(
        functools.partial(_wkv_kernel, d_r=d_r, n_chunks=n_chunks),
        grid=(B, S // rows),
        in_specs=[tile(3 * d_r), tile(d_r), tile(d_r), tile(d_r), const((1, 3 * d_r)),
                  const((1, d_r)), const((1, d_r)), const((1, d_r)), const((1, d_r)),
                  const((1, d_r)), const((MXU_DEPTH, MXU_DEPTH)), const((rows, rows))],
        out_specs=tile(d_r),
        out_shape=jax.ShapeDtypeStruct((B, S, d_r), BF16),
        scratch_shapes=[pltpu.VMEM((d_r // LANES, LANES, LANES), F32),
                        pltpu.VMEM((8, 3 * d_r), F32)],
        compiler_params=_params(("parallel", "arbitrary")),
        name="wkv",
    )(rkv, lw, ai, g, row(mu_rkv), row(k_k), row(k_a), row(r_k), row(lnx_g), row(lnx_b), seg, tril)


def _t5_bucket(n):
    exact = N_BUCKETS // 2
    nf = np.maximum(n, 1).astype(np.float32)
    large = exact + (np.log(nf / exact) / math.log(MAX_DISTANCE / exact)
                     * (N_BUCKETS - exact)).astype(np.int32)
    large = np.minimum(large, N_BUCKETS - 1)
    return np.where(n < exact, n, large).astype(np.int32)


def _band_bias(rel_bias, dil):
    blk = ATT_BLK
    L = 3 * blk
    rel = (2 * blk - 1) - np.arange(L)
    valid = (rel >= 0) & (rel <= blk)
    base = rel_bias[_t5_bucket(np.clip(rel, 0, None) * dil)].astype(F32).T
    base = jnp.where(valid[None], base, NEG_INF)
    skew = jnp.tile(base, (1, blk))[:, :blk * (L - 1)].reshape(-1, blk, L - 1)
    return skew[:, :, blk - 1:3 * blk - 1]


def _attend(units, bias_ref, scale):
    lane = lax.broadcasted_iota(jnp.int32, (1, LANES), 1)
    head0 = lane < HEAD_DIM
    jobs = [(u, h) for u in range(len(units)) for h in range(2)]
    s = []
    for u, h in jobs:
        q, k, _, col0 = units[u]
        qh = jnp.where(head0 if h == 0 else jnp.logical_not(head0), q * scale, 0.0)
        s.append(_dot_nt(qh, k) + bias_ref[h, :, col0:col0 + k.shape[0]])
    m = [jnp.max(t, axis=-1, keepdims=True) for t in s]
    e = [jnp.exp2(t - m_) for t, m_ in zip(s, m)]
    l = [jnp.sum(t, axis=-1, keepdims=True) for t in e]
    o = [_dot(t, units[u][2]) for t, (u, h) in zip(e, jobs)]
    both = lambda t, u: jnp.where(head0, t[2 * u], t[2 * u + 1])
    return [(both(o, u), both(m, u), both(l, u)) for u in range(len(units))]


def _attn_kernel(q_ref, k_ref, v_ref, b1_ref, b4_ref, b16_ref, o_ref, acc_o, acc_m, acc_l, *, seq,
                 scale):
    blk = ATT_BLK
    group = 4

    def rows(ref, start, n, dil):
        if dil == 1:
            return ref[0, pl.ds(start, n), :]
        return ref[0, pl.ds(start, n, stride=dil), :]

    def unit(start, dil, has_prev):
        q = rows(q_ref, start, blk, dil)
        if has_prev:
            k0 = start - blk * dil
            return (q, rows(k_ref, k0, 2 * blk, dil), rows(v_ref, k0, 2 * blk, dil), 0)
        return (q, rows(k_ref, start, blk, dil), rows(v_ref, start, blk, dil), blk)

    def acc_rows(ref, start, dil):
        if dil == 1:
            return ref.at[pl.ds(start, blk), :]
        return ref.at[pl.ds(start, blk, stride=dil), :]

    def run(starts, dil, prevs, bias_ref, first, last):
        units = [unit(s, dil, hp) for s, hp in zip(starts, prevs)]
        for s, (o, m, l) in zip(starts, _attend(units, bias_ref, scale)):
            ao, am, al = (acc_rows(ref, s, dil) for ref in (acc_o, acc_m, acc_l))
            if first:
                ao[...] = o
                am[...] = m
                al[...] = l
            else:
                m_old = am[...]
                m_new = jnp.maximum(m_old, m)
                w_old, w_new = jnp.exp2(m_old - m_new), jnp.exp2(m - m_new)
                ao[...] = w_old * ao[...] + w_new * o
                al[...] = w_old * al[...] + w_new * l
                if not last:
                    am[...] = m_new

    run([n * blk for n in range(group)], 1, [n > 0 for n in range(group)], b1_ref, True, False)

    def body1(g, c):
        base = pl.multiple_of(g * (group * blk), group * blk)
        run([base + n * blk for n in range(group)], 1, [True] * group, b1_ref, True, False)
        return c
    lax.fori_loop(1, seq // (group * blk), body1, 0)

    nb4 = seq // 4 // blk

    def body4(r, c):
        run([r + n * blk * 4 for n in range(nb4)], 4, [n > 0 for n in range(nb4)], b4_ref,
            False, False)
        return c
    lax.fori_loop(0, 4, body4, 0)

    def body16(g, c):
        run([g * group + j for j in range(group)], 16, [False] * group, b16_ref, False, True)
        return c
    lax.fori_loop(0, 16 // group, body16, 0)

    o_ref[0] = (acc_o[...] / acc_l[...]).astype(o_ref.dtype)


def _stage_attn(qkv, rel_bias):
    B, S, d3 = qkv.shape
    d_att = d3 // 3
    n_pairs = d_att // LANES
    assert DIL_PATTERNS == ((ATT_BLK, 1), (4 * ATT_BLK, 4), (16 * ATT_BLK, 16))
    assert S == 16 * ATT_BLK
    log2e = math.log2(math.e)
    biases = [_band_bias(rel_bias, dil) * log2e for _, dil in DIL_PATTERNS]
    part = lambda i: pl.BlockSpec((1, S, LANES), lambda b, p: (b, 0, i * n_pairs + p))
    bias_spec = pl.BlockSpec((2, ATT_BLK, 2 * ATT_BLK), lambda b, p: (p, 0, 0))
    return pl.pallas_call(
        functools.partial(_attn_kernel, seq=S, scale=HEAD_DIM ** -0.5 * log2e),
        grid=(B, n_pairs),
        in_specs=[part(0), part(1), part(2), bias_spec, bias_spec, bias_spec],
        out_specs=pl.BlockSpec((1, S, LANES), lambda b, p: (b, 0, p)),
        out_shape=jax.ShapeDtypeStruct((B, S, d_att), BF16),
        scratch_shapes=[pltpu.VMEM((S, LANES), F32)] * 3,
        compiler_params=_params(("parallel", "parallel")),
        name="attn",
    )(qkv, qkv, qkv, *biases)


def _mix_kernel(x_ref, yr_ref, att_ref, p_ref, wor_ref, woa_ref, g_ref, b_ref, rhi_ref, rlo_ref,
                rb_ref, pg_ref, pp_ref, x1_ref, x1t_ref, lg_ref, ple_ref, *, alpha):
    mix = _dot(yr_ref[...], wor_ref[...]) + _dot(att_ref[...], woa_ref[...])
    emb = _dot(p_ref[...], pp_ref[...])
    x1 = _layer_norm(alpha * x_ref[...] + mix, g_ref[...], b_ref[...])
    x1_ref[...] = x1
    _store_token_tiles(x1t_ref, x1)
    hi, lo = _split(x1)
    logits = (jnp.dot(hi, rhi_ref[...], preferred_element_type=F32)
              + jnp.dot(lo, rhi_ref[...], preferred_element_type=F32)
              + jnp.dot(hi, rlo_ref[...], preferred_element_type=F32) + rb_ref[...])
    ple_ref[...] = (_sigmoid(_dot(hi, pg_ref[...])) * emb).astype(ple_ref.dtype)

    lane = lax.broadcasted_iota(jnp.int32, logits.shape, 1).astype(F32)
    big = float(ROUTER_LANES)
    rmax = lambda t: jnp.max(t, axis=-1, keepdims=True)
    first = lambda hit: jnp.min(jnp.where(hit, lane, big), axis=-1, keepdims=True)
    off = float("-inf")
    is_grp = lane < N_GROUPS
    lg = jnp.where(is_grp, logits, off)
    mg = rmax(lg)
    wg = 1.0 / jnp.sum(jnp.where(is_grp, jnp.exp(lg - mg), 0.0), axis=-1, keepdims=True)
    lo_lane = N_GROUPS + EXPERTS_PER_GROUP * first(lg == mg)
    le = jnp.where((lane >= lo_lane) & (lane < lo_lane + EXPERTS_PER_GROUP), logits, off)
    v1 = rmax(le)
    i1 = first(le == v1)
    le = jnp.where(lane == i1, off, le)
    v2 = rmax(le)
    i2 = first(le == v2)
    e2 = jnp.exp(v2 - v1)
    w1 = wg / (1.0 + e2)
    route = jnp.where(lane == 0, w1, jnp.where(lane == 1, w1 * e2, jnp.where(
        lane == 2, i1 - N_GROUPS, jnp.where(lane == 3, i2 - N_GROUPS, 0.0))))
    lg_ref[...] = route


def _stage_mix(x2, yr, att, p2, w_o, ln_g, ln_b, router_g, router_g_b, router_e, router_e_b,
               ple_gate, ple_proj, alpha):
    N, D = x2.shape
    d_r = yr.shape[1]
    rows = min(MIX_ROWS, N)
    d_att = att.shape[1]
    n_log = N_GROUPS + N_EXPERTS
    rw = jnp.zeros((D, ROUTER_LANES), F32).at[:, :N_GROUPS].set(router_g)
    rw = rw.at[:, N_GROUPS:n_log].set(router_e)
    rb = jnp.zeros((1, ROUTER_LANES), F32).at[0, :N_GROUPS].set(router_g_b)
    rb = rb.at[0, N_GROUPS:n_log].set(router_e_b)
    rhi, rlo = _split(rw)
    const = lambda shape: pl.BlockSpec(shape, lambda i: (0,) * len(shape))
    tile = lambda c: pl.BlockSpec((rows, c), lambda i: (i, 0))
    return pl.pallas_call(
        functools.partial(_mix_kernel, alpha=alpha),
        grid=(N // rows,),
        in_specs=[tile(D), tile(d_r), tile(d_att), tile(p2.shape[1]),
                  const((d_r, D)), const((d_att, D)), const((1, D)), const((1, D)),
                  const((D, ROUTER_LANES)), const((D, ROUTER_LANES)), const((1, ROUTER_LANES)),
                  const((D, D)), const(ple_proj.shape)],
        out_specs=[tile(D), pl.BlockSpec((rows * SUBLANES, LANES), lambda i: (i, 0)),
                   tile(ROUTER_LANES), tile(D)],
        out_shape=[jax.ShapeDtypeStruct((N, D), F32),
                   jax.ShapeDtypeStruct((N * SUBLANES, LANES), F32),
                   jax.ShapeDtypeStruct((N, ROUTER_LANES), F32),
                   jax.ShapeDtypeStruct((N, D), BF16)],
        compiler_params=_params(("parallel",)),
        name="mix",
    )(x2, yr, att, p2, w_o[:d_r].astype(BF16), w_o[d_r:].astype(BF16),
      ln_g.reshape(1, D), ln_b.reshape(1, D), rhi, rlo, rb, ple_gate.astype(BF16),
      ple_proj.astype(BF16))


def _route(route, rows):
    N = route.shape[0]
    we = route[:, :TOP_K]
    eid = route[:, TOP_K:2 * TOP_K].astype(jnp.int32)
    A = N * TOP_K
    e_flat = eid.T.reshape(A).astype(jnp.int32)
    order = jnp.argsort(e_flat).astype(jnp.int32)
    counts = jnp.sum((e_flat[:, None] == jnp.arange(N_EXPERTS)[None, :]).astype(jnp.int32), axis=0)
    start = jnp.cumsum(counts) - counts
    padded = (counts + rows - 1) // rows * rows
    pend = jnp.cumsum(padded)
    pstart = pend - padded
    nblk = -(-A // rows) + N_EXPERTS
    nused = (pend[-1] // rows).astype(jnp.int32)
    blk = jnp.arange(nblk, dtype=jnp.int32)
    blk_e = jnp.sum((pend[None, :] <= (blk * rows)[:, None]).astype(jnp.int32), axis=1)
    blk_e = jnp.minimum(blk_e, N_EXPERTS - 1)
    blk_e = jnp.where(blk < nused, blk_e, blk_e[nused - 1]).astype(jnp.int32)
    r = jnp.arange(rows, dtype=jnp.int32)[None, :]
    off = (blk * rows - pstart[blk_e])[:, None] + r
    real = (off < counts[blk_e][:, None]) & (blk < nused)[:, None]
    asg = order[jnp.clip(start[blk_e][:, None] + off, 0, A - 1)]
    row_tok = jnp.where(real, asg % N, r).astype(jnp.int32)
    row_dst = jnp.where(real, asg, A + (blk % 2)[:, None] * rows + r).astype(jnp.int32)
    return (row_tok.reshape(nblk, 1, rows), row_dst.reshape(nblk, 1, rows), blk_e,
            nused.reshape(1), we)


def _moe_kernel(blk_e_ref, nused_ref, tok0_ref, tokn_ref, dstp_ref, x_hbm, wg_ref, wu_ref, wd_ref,
                y_hbm, xbuf, ybuf, gsem, ssem, *, rows, n_real):
    i = pl.program_id(0)
    nused = nused_ref[0]
    slot = i % 2
    tile_rows = lambda first: pl.ds(pl.multiple_of(first, SUBLANES), SUBLANES)

    def gather_row(tab_ref, sl, r):
        return pltpu.make_async_copy(x_hbm.at[tile_rows(tab_ref[0, 0, r])],
                                     xbuf.at[sl, pl.ds(r * SUBLANES, SUBLANES)], gsem.at[sl])

    def scatter_row(sl, r):
        return pltpu.make_async_copy(ybuf.at[sl, pl.ds(r * SUBLANES, SUBLANES)],
                                     y_hbm.at[tile_rows(dstp_ref[0, 0, r])], ssem.at[sl])

    def gather_all(sl):
        return pltpu.make_async_copy(x_hbm.at[pl.ds(0, rows * SUBLANES)], xbuf.at[sl],
                                     gsem.at[sl])

    def scatter_all(sl, first_row=0):
        return pltpu.make_async_copy(
            ybuf.at[sl], y_hbm.at[pl.ds(first_row * SUBLANES, rows * SUBLANES)], ssem.at[sl])

    def rolled(fn):
        def body(r, c):
            fn(r)
            return c
        lax.fori_loop(0, rows, body, 0, unroll=8)

    def inline(fn):
        for r in range(rows):
            fn(r)

    def ffn(sl):
        xb = _load_token_tiles(xbuf.at[sl]).astype(BF16)
        gate = _dot(xb, wg_ref[0])
        up = _dot(xb, wu_ref[0])
        _store_token_tiles(ybuf.at[sl], _dot(gate * _sigmoid(gate) * up, wd_ref[0]))

    @pl.when(i == 0)
    def _():
        ybuf[...] = jnp.zeros_like(ybuf)
        for sl in range(2):
            scatter_all(sl, n_real + sl * rows).start()
        for sl in range(2):
            scatter_all(sl, n_real + sl * rows).wait()
        rolled(lambda r: gather_row(tok0_ref, 0, r).start())

    @pl.when(i < nused)
    def _():
        gather_all(slot).wait()

    @pl.when((i >= 2) & (i <= nused))
    def _():
        scatter_all(slot).wait()

    @pl.when(i == 0)
    def _():
        inline(lambda r: gather_row(tokn_ref, 1, r).start())
        ffn(0)

    for s in range(2):
        @pl.when((i >= 1) & (i < nused) & (slot == s))
        def _():
            inline(lambda r: gather_row(tokn_ref, 1 - s, r).start(priority=r % 2))
            inline(lambda r: scatter_row(1 - s, r).start(priority=r % 2))
            ffn(s)

    @pl.when(i == nused)
    def _():
        rolled(lambda r: scatter_row(1 - slot, r).start())
        scatter_all(1 - slot).wait()
        gather_all(slot).wait()


def _stage_moe(x1t, row_tok, row_dst, blk_e, nused, w_gate, w_up, w_down):
    N = x1t.shape[0] // SUBLANES
    D = SUBLANES * LANES
    nblk, _, rows = row_tok.shape
    d_e = w_gate.shape[2]
    n_real = N * TOP_K
    last = nblk - 1
    wspec = lambda shape: pl.BlockSpec((1,) + shape,
                                       lambda i, be, nu: (be[jnp.minimum(i, last)], 0, 0))
    table = lambda fn: pl.BlockSpec((1, 1, rows), lambda i, be, nu: (fn(i), 0, 0),
                                    memory_space=pltpu.SMEM)
    grid_spec = pltpu.PrefetchScalarGridSpec(
        num_scalar_prefetch=2,
        grid=(nblk + 1,),
        in_specs=[table(lambda i: 0), table(lambda i: jnp.minimum(i + 1, last)),
                  table(lambda i: jnp.clip(i - 1, 0, last)),
                  pl.BlockSpec(memory_space=pl.ANY), wspec((D, d_e)), wspec((D, d_e)),
                  wspec((d_e, D))],
        out_specs=pl.BlockSpec(memory_space=pl.ANY),
        scratch_shapes=[pltpu.VMEM((2, rows * SUBLANES, LANES), F32),
                        pltpu.VMEM((2, rows * SUBLANES, LANES), F32),
                        pltpu.SemaphoreType.DMA((2,)), pltpu.SemaphoreType.DMA((2,))],
    )
    return pl.pallas_call(
        functools.partial(_moe_kernel, rows=rows, n_real=n_real),
        grid_spec=grid_spec,
        out_shape=jax.ShapeDtypeStruct(((n_real + 2 * rows) * SUBLANES, LANES), F32),
        compiler_params=_params(("arbitrary",)),
        name="moe",
    )(blk_e, nused, row_tok * SUBLANES, row_tok * SUBLANES, row_dst * SUBLANES, x1t, w_gate, w_up,
      w_down)


def _out_kernel(x1_ref, y0_ref, y1_ref, we_ref, ple_ref, g_ref, b_ref, o_ref, *, alpha):
    we = we_ref[...]
    moe = we[:, 0:1] * _load_token_tiles(y0_ref) + we[:, 1:2] * _load_token_tiles(y1_ref)
    o_ref[...] = _layer_norm(alpha * x1_ref[...] + moe + ple_ref[...], g_ref[...], b_ref[...])


def _stage_out(x1, yb, we, ple, ln_g, ln_b, alpha):
    N, D = x1.shape
    rows = min(OUT_ROWS, N)
    const = lambda shape: pl.BlockSpec(shape, lambda i: (0,) * len(shape))
    tile = lambda c: pl.BlockSpec((rows, c), lambda i: (i, 0))
    slot = lambda j: pl.BlockSpec((rows * SUBLANES, LANES), lambda i: (i + j * (N // rows), 0))
    return pl.pallas_call(
        functools.partial(_out_kernel, alpha=alpha),
        grid=(N // rows,),
        in_specs=[tile(D), slot(0), slot(1), tile(TOP_K), tile(D), const((1, D)), const((1, D))],
        out_specs=tile(D),
        out_shape=jax.ShapeDtypeStruct((N, D), F32),
        compiler_params=_params(("parallel",)),
        name="out",
    )(x1, yb, yb, we, ple, ln_g.reshape(1, D), ln_b.reshape(1, D))


def kernel(x, p, w_in, mu_rkv, mu_lora, w0, w_lora1, w_lora2, a0, a_lora1, a_lora2, g_lora1, g_lora2, k_k, k_a, r_k, lnx_g, lnx_b, rel_bias, w_o, ln1_g, ln1_b, router_g, router_g_b, router_e, router_e_b, w_gate, w_up, w_down, ple_gate, ple_proj, ln2_g, ln2_b):
    B, S, D = x.shape
    depth = w_in.shape[0]
    alpha = (2 * depth) ** 0.25
    for i in range(depth):
        rkv, qkv, lw, ai, g = _stage_proj(x, w_in[i], w_lora1[i], a_lora1[i], g_lora1[i],
                                          mu_lora[i], w_lora2[i], a_lora2[i], g_lora2[i],
                                          w0[i], a0[i])
        yr = _stage_wkv(rkv, lw, ai, g, mu_rkv[i], k_k[i], k_a[i], r_k[i], lnx_g[i], lnx_b[i])
        att = _stage_attn(qkv, rel_bias).reshape(B * S, -1)
        x1, x1t, route, ple = _stage_mix(x.reshape(B * S, D), yr.reshape(B * S, -1), att,
                                          p[i].reshape(B * S, -1), w_o[i], ln1_g[i], ln1_b[i],
                                          router_g[i], router_g_b[i], router_e[i],
                                          router_e_b[i], ple_gate[i], ple_proj[i], alpha)
        row_tok, row_dst, blk_e, nused, we = _route(route, MOE_ROWS)
        yb = _stage_moe(x1t, row_tok, row_dst, blk_e, nused, w_gate[i], w_up[i], w_down[i])
        x = _stage_out(x1, yb, we, ple, ln2_g[i], ln2_b[i], alpha).reshape(B, S, D)
    return x
```

```python
import functools
import math

import jax
import jax.numpy as jnp
import numpy as np
from jax import lax
from jax.experimental import pallas as pl
from jax.experimental.pallas import tpu as pltpu

F32 = jnp.float32
BF16 = jnp.bfloat16

HEAD_DIM = 64
LANES = 128
SUBLANES = 8
MXU_DEPTH = 256
DECAY_LORA = 64
ICLR_LORA = 64
GATE_LORA = 128
LNX_EPS = 64e-5
DIL_PATTERNS = ((128, 1), (512, 4), (2048, 16))
NEG_INF = -1e30
N_BUCKETS = 32
MAX_DISTANCE = 2048
N_GROUPS = 4
EXPERTS_PER_GROUP = 8
N_EXPERTS = N_GROUPS * EXPERTS_PER_GROUP
TOP_K = 2
LN_EPS = 1e-5

WKV_CHUNK = 64
WKV_CHUNKS_PER_STEP = 4
ATT_BLK = 128
PROJ_ROWS = 512
MIX_ROWS = 512
MOE_ROWS = 256
OUT_ROWS = 1024
ROUTER_LANES = 128
VMEM_LIMIT = 56 * 1024 * 1024


def _params(sem):
    return pltpu.CompilerParams(dimension_semantics=sem, vmem_limit_bytes=VMEM_LIMIT)


def _dot(a, b):
    return jnp.dot(a.astype(BF16), b.astype(BF16), preferred_element_type=F32)


def _dot_nt(a, b):
    return lax.dot_general(a.astype(BF16), b.astype(BF16), (((1,), (1,)), ((), ())),
                           preferred_element_type=F32)


def _dot_tn(a, b):
    return lax.dot_general(a.astype(BF16), b.astype(BF16), (((0,), (0,)), ((), ())),
                           preferred_element_type=F32)


def _split(x):
    hi = x.astype(BF16)
    lo = (x - hi.astype(F32)).astype(BF16)
    return hi, lo


def _dot_exact_lhs(w01, x):
    hi, lo = _split(x)
    return (jnp.dot(w01, hi, preferred_element_type=F32)
            + jnp.dot(w01, lo, preferred_element_type=F32))


def _sigmoid(x):
    return 1.0 / (1.0 + jnp.exp(-x))


def _shift_rows(t, prev_row):
    rolled = pltpu.roll(t, 1, 0)
    row = lax.broadcasted_iota(jnp.int32, t.shape, 0)
    return jnp.where(row == 0, prev_row, rolled)


def _store_token_tiles(ref, t):
    rows = t.shape[0]
    for s in range(SUBLANES):
        ref[pl.ds(s, rows, stride=SUBLANES), :] = t[:, s * LANES:(s + 1) * LANES]


def _load_token_tiles(ref):
    rows = ref.shape[0] // SUBLANES
    return jnp.concatenate([ref[pl.ds(s, rows, stride=SUBLANES), :] for s in range(SUBLANES)],
                           axis=1)


def _layer_norm(z, g, b):
    mu = jnp.mean(z, axis=-1, keepdims=True)
    zc = z - mu
    var = jnp.mean(zc * zc, axis=-1, keepdims=True)
    return zc * lax.rsqrt(var + LN_EPS) * g + b


def _proj_kernel(x_ref, win_ref, w1_ref, a1_ref, g1_ref, mul_ref, w2_ref, a2_ref, g2_ref,
                 w0_ref, a0_ref, rkv_ref, qkv_ref, lw_ref, ai_ref, g_ref, prev_ref, *, d_rkv):
    @pl.when(pl.program_id(1) == 0)
    def _():
        prev_ref[...] = jnp.zeros_like(prev_ref)

    h = x_ref[0]
    rows = h.shape[0]
    hprev = _shift_rows(h, prev_ref[0:1, :])
    prev_ref[0:1, :] = h[rows - 1:rows, :]

    dh = hprev - h
    hb = h.astype(BF16)
    low_w = _dot(h + dh * mul_ref[0:1, :], w1_ref[...])
    low_a = _dot(h + dh * mul_ref[1:2, :], a1_ref[...])
    low_g = _dot(h + dh * mul_ref[2:3, :], g1_ref[...])
    rkv_ref[0] = _dot(hb, win_ref[:, :d_rkv])
    wl = w0_ref[...] + _dot(jnp.tanh(low_w), w2_ref[...])
    al = a0_ref[...] + _dot(low_a, a2_ref[...])
    g_ref[0] = _dot(_sigmoid(low_g), g2_ref[...]).astype(g_ref.dtype)
    qkv_ref[0] = _dot(hb, win_ref[:, d_rkv:])
    z = -wl
    softplus = jnp.maximum(z, 0.0) + jnp.log(1.0 + jnp.exp(-jnp.abs(z)))
    lw_ref[0] = -jnp.exp(-softplus - 0.5)
    ai_ref[0] = _sigmoid(al)


def _stage_proj(x, w_in, w1, a1, g1, mu_lora, w2, a2, g2, w0, a0):
    B, S, D = x.shape
    d_in = w_in.shape[1]
    d_r = w2.shape[1]
    d_rkv = 3 * d_r
    d_att = d_in - d_rkv
    rows = min(PROJ_ROWS, S)
    const = lambda shape: pl.BlockSpec(shape, lambda b, j: (0,) * len(shape))
    tile = lambda c: pl.BlockSpec((1, rows, c), lambda b, j: (b, j, 0))
    return pl.pallas_call(
        functools.partial(_proj_kernel, d_rkv=d_rkv),
        grid=(B, S // rows),
        in_specs=[tile(D), const((D, d_in)), const(w1.shape), const(a1.shape), const(g1.shape),
                  const(mu_lora.shape), const(w2.shape), const(a2.shape), const(g2.shape),
                  const((1, d_r)), const((1, d_r))],
        out_specs=[tile(d_rkv), tile(d_att), tile(d_r), tile(d_r), tile(d_r)],
        out_shape=[jax.ShapeDtypeStruct((B, S, d_rkv), F32),
                   jax.ShapeDtypeStruct((B, S, d_att), F32),
                   jax.ShapeDtypeStruct((B, S, d_r), F32),
                   jax.ShapeDtypeStruct((B, S, d_r), F32),
                   jax.ShapeDtypeStruct((B, S, d_r), BF16)],
        scratch_shapes=[pltpu.VMEM((8, D), F32)],
        compiler_params=_params(("parallel", "arbitrary")),
        name="proj",
    )(x, w_in.astype(BF16), w1.astype(BF16), a1.astype(BF16), g1.astype(BF16), mu_lora,
      w2.astype(BF16), a2.astype(BF16), g2.astype(BF16), w0.reshape(1, d_r), a0.reshape(1, d_r))


def _wkv_kernel(rkv_ref, lw_ref, ai_ref, g_ref, mu_ref, kk_ref, ka_ref, rk_ref, lng_ref, lnb_ref,
                seg_ref, tril_ref, y_ref, s_ref, prev_ref, *, d_r, n_chunks):
    C = WKV_CHUNK
    rows = C * n_chunks

    @pl.when(pl.program_id(1) == 0)
    def _():
        s_ref[...] = jnp.zeros_like(s_ref)
        prev_ref[...] = jnp.zeros_like(prev_ref)

    rkv = rkv_ref[0]
    prev = _shift_rows(rkv, prev_ref[0:1, :])
    prev_ref[0:1, :] = rkv[rows - 1:rows, :]
    mixed = rkv + (prev - rkv) * mu_ref[...]
    r = mixed[:, :d_r]
    k = mixed[:, d_r:2 * d_r]
    v = mixed[:, 2 * d_r:]
    lw = lw_ref[0]
    a = ai_ref[0]
    seg = seg_ref[...]
    seg_w = seg.shape[0]

    def head_sum(t):
        return jnp.concatenate([_dot(t[:, j:j + seg_w], seg) for j in range(0, d_r, seg_w)],
                               axis=1)

    kk = k * kk_ref[...]
    kk = kk * lax.rsqrt(jnp.maximum(head_sum(kk * kk), 1e-24))
    k = k * (1.0 + (a - 1.0) * ka_ref[...])
    aa = -kk
    bb = kk * a
    c = _dot_exact_lhs(tril_ref[...], lw)
    e_neg = jnp.exp(-c)
    a_t = aa * jnp.exp(c - lw)
    b_t = bb * e_neg
    k_t = k * e_neg
    r_t = r * jnp.exp(c)

    lane = lax.broadcasted_iota(jnp.int32, (1, LANES), 1)
    head0 = lane < HEAD_DIM
    ri = lax.broadcasted_iota(jnp.int32, (2 * C, 2 * C), 0)
    ci = lax.broadcasted_iota(jnp.int32, (2 * C, 2 * C), 1)
    strict = (ri % C) > (ci % C)
    incl = (ri % C) >= (ci % C)
    eye = (ri == ci).astype(F32)
    zero_bf = jnp.zeros((2 * C, LANES), BF16)

    def stack(t):
        return jnp.concatenate([jnp.where(head0, t, 0.0), jnp.where(head0, 0.0, t)],
                               axis=0).astype(BF16)

    def fold(t):
        return t[:C] + t[C:]

    n_pairs = d_r // LANES
    folds = [(q, p) for q in range(n_chunks) for p in range(n_pairs)]
    each = lambda fn, *cols: [fn(*args) for args in zip(*cols)]
    cat0 = lambda *ts: jnp.concatenate(ts, axis=0)
    cat1 = lambda *ts: jnp.concatenate(ts, axis=1)
    left = lambda t: t[:, :2 * C]
    right = lambda t: t[:, 2 * C:]

    p_ends, b_es, k_es = [], [], []
    for q in range(n_chunks):
        rs = slice(q * C, (q + 1) * C)
        c_end = c[rs][C - 1:C, :]
        e_end = jnp.exp(c_end - c[rs])
        b_es.append(bb[rs] * e_end)
        k_es.append(k[rs] * e_end)
        p_ends.append(jnp.exp(c_end))

    def stacks(t_of_q):
        return [stack(t_of_q(q)[:, p * LANES:(p + 1) * LANES]) for q, p in folds]

    chunk = lambda t: (lambda q: t[q * C:(q + 1) * C])
    am, bm, km, rm, vm = (stacks(chunk(t)) for t in (a_t, b_t, k_t, r_t, v))
    bem = stacks(lambda q: b_es[q])
    kem = stacks(lambda q: k_es[q])

    x = each(lambda a_, r_, b_, k_: _dot_nt(cat0(a_, r_), cat0(b_, k_)), am, rm, bm, km)
    n = each(lambda t: jnp.where(strict, t[:2 * C, :2 * C], 0.0), x)
    a_ak = each(lambda t: jnp.where(strict, t[:2 * C, 2 * C:], 0.0).astype(BF16), x)
    a_rb = each(lambda t: jnp.where(incl, t[2 * C:, :2 * C], 0.0).astype(BF16), x)
    a_rk = each(lambda t: jnp.where(incl, t[2 * C:, 2 * C:], 0.0).astype(BF16), x)
    t_inv = each(lambda t: eye + t, n)
    m = each(lambda t: _dot(t, t), n)
    av = each(lambda a_, v_: _dot(a_, v_).astype(BF16), a_ak, vm)
    for _ in range(int(math.log2(C)) - 2):
        z = each(lambda m_, t_: _dot(m_, cat1(m_.astype(BF16), t_.astype(BF16))), m, t_inv)
        t_inv = each(lambda t_, z_: t_ + right(z_), t_inv, z)
        m = each(left, z)
    t_inv = each(lambda t_, m_: t_ + _dot(m_, t_), t_inv, m)
    z = each(lambda t_, a_, av_: _dot(t_, cat1(a_, av_)), t_inv, am, av)
    ap = each(lambda z_: left(z_).astype(BF16), z)
    u0 = each(lambda z_: right(z_).astype(BF16), z)
    z = each(lambda rb_, rk_, ap_, u_, v_: _dot(cat1(rb_, rk_),
                                               cat0(cat1(ap_, u_), cat1(zero_bf, v_))),
             a_rb, a_rk, ap, u0, vm)
    g_m = each(lambda ap_, be_: _dot_tn(ap_, be_).astype(BF16), ap, bem)
    d0 = each(lambda u_, v_, be_, ke_: _dot_tn(cat0(u_, v_), cat0(be_, ke_)), u0, vm, bem, kem)
    rp_t = each(lambda r_, z_: jnp.transpose(r_.astype(F32) + left(z_)).astype(BF16), rm, z)
    y0 = each(lambda z_: fold(right(z_)), z)

    s = [s_ref[p] for p in range(n_pairs)]
    y_cols = []
    for q in range(n_chunks):
        f0 = q * n_pairs
        zs = [_dot(s[p], cat1(g_m[f0 + p], rp_t[f0 + p])) for p in range(n_pairs)]
        y_cols.append([fold(jnp.transpose(right(zs[p]))) + y0[f0 + p] for p in range(n_pairs)])
        s = [s[p] * p_ends[q][:, p * LANES:(p + 1) * LANES] + left(zs[p]) + d0[f0 + p]
             for p in range(n_pairs)]
    for p in range(n_pairs):
        s_ref[p] = s[p]
    y = cat0(*[cat1(*cols) for cols in y_cols])

    inv = 1.0 / HEAD_DIM
    mu = head_sum(y) * inv
    yc = y - mu
    var = head_sum(yc * yc) * inv
    yn = yc * lax.rsqrt(var + LNX_EPS) * lng_ref[...] + lnb_ref[...]
    bonus = head_sum(r * k * rk_ref[...]) * v
    y_ref[0] = ((yn + bonus) * g_ref[0]).astype(y_ref.dtype)


def _stage_wkv(rkv, lw, ai, g, mu_rkv, k_k, k_a, r_k, lnx_g, lnx_b):
    B, S, d_r = lw.shape
    n_chunks = WKV_CHUNKS_PER_STEP
    rows = WKV_CHUNK * n_chunks
    head = np.arange(MXU_DEPTH) // HEAD_DIM
    seg = jnp.asarray(head[:, None] == head[None, :], BF16)
    t = np.arange(rows)
    tril = jnp.asarray((t[:, None] >= t[None, :])
                       & (t[:, None] // WKV_CHUNK == t[None, :] // WKV_CHUNK), BF16)
    const = lambda shape: pl.BlockSpec(shape, lambda b, j: (0,) * len(shape))
    tile = lambda c: pl.BlockSpec((1, rows, c), lambda b, j: (b, j, 0))
    row = lambda t: t.reshape(1, -1)
    return pl.pallas_call(
        functools.partial(_wkv_kernel, d_r=d_r, n_chunks=n_chunks),
        grid=(B, S // rows),
        in_specs=[tile(3 * d_r), tile(d_r), tile(d_r), tile(d_r), const((1, 3 * d_r)),
                  const((1, d_r)), const((1, d_r)), const((1, d_r)), const((1, d_r)),
                  const((1, d_r)), const((MXU_DEPTH, MXU_DEPTH)), const((rows, rows))],
        out_specs=tile(d_r),
        out_shape=jax.ShapeDtypeStruct((B, S, d_r), BF16),
        scratch_shapes=[pltpu.VMEM((d_r // LANES, LANES, LANES), F32),
                        pltpu.VMEM((8, 3 * d_r), F32)],
        compiler_params=_params(("parallel", "arbitrary")),
        name="wkv",
    )(rkv, lw, ai, g, row(mu_rkv), row(k_k), row(k_a), row(r_k), row(lnx_g), row(lnx_b), seg, tril)


def _t5_bucket(n):
    exact = N_BUCKETS // 2
    nf = np.maximum(n, 1).astype(np.float32)
    large = exact + (np.log(nf / exact) / math.log(MAX_DISTANCE / exact)
                     * (N_BUCKETS - exact)).astype(np.int32)
    large = np.minimum(large, N_BUCKETS - 1)
    return np.where(n < exact, n, large).astype(np.int32)


def _band_bias(rel_bias, dil):
    blk = ATT_BLK
    L = 3 * blk
    rel = (2 * blk - 1) - np.arange(L)
    valid = (rel >= 0) & (rel <= blk)
    base = rel_bias[_t5_bucket(np.clip(rel, 0, None) * dil)].astype(F32).T
    base = jnp.where(valid[None], base, NEG_INF)
    skew = jnp.tile(base, (1, blk))[:, :blk * (L - 1)].reshape(-1, blk, L - 1)
    return skew[:, :, blk - 1:3 * blk - 1]


def _attend(units, bias_ref, scale):
    lane = lax.broadcasted_iota(jnp.int32, (1, LANES), 1)
    head0 = lane < HEAD_DIM
    jobs = [(u, h) for u in range(len(units)) for h in range(2)]
    s = []
    for u, h in jobs:
        q, k, _, col0 = units[u]
        qh = jnp.where(head0 if h == 0 else jnp.logical_not(head0), q * scale, 0.0)
        s.append(_dot_nt(qh, k) + bias_ref[h, :, col0:col0 + k.shape[0]])
    m = [jnp.max(t, axis=-1, keepdims=True) for t in s]
    e = [jnp.exp2(t - m_) for t, m_ in zip(s, m)]
    ol = [_dot(t, jnp.where(head0 if h == 0 else jnp.logical_not(head0), units[u][2], 1.0))
          for t, (u, h) in zip(e, jobs)]
    both = lambda t, u: jnp.where(head0, t[2 * u], t[2 * u + 1])
    swapped = lambda u: jnp.where(head0, ol[2 * u + 1], ol[2 * u])
    return [(both(ol, u), both(m, u), pltpu.roll(swapped(u), HEAD_DIM, 1))
            for u in range(len(units))]


def _attn_kernel(q_ref, k_ref, v_ref, b1_ref, b4_ref, b16_ref, o_ref, acc_o, acc_m, acc_l, *, seq,
                 scale):
    blk = ATT_BLK
    group = 4

    def rows(ref, start, n, dil):
        if dil == 1:
            return ref[0, pl.ds(start, n), :]
        return ref[0, pl.ds(start, n, stride=dil), :]

    def unit(start, dil, has_prev):
        q = rows(q_ref, start, blk, dil)
        if has_prev:
            k0 = start - blk * dil
            return (q, rows(k_ref, k0, 2 * blk, dil), rows(v_ref, k0, 2 * blk, dil), 0)
        return (q, rows(k_ref, start, blk, dil), rows(v_ref, start, blk, dil), blk)

    def acc_rows(ref, start, dil):
        if dil == 1:
            return ref.at[pl.ds(start, blk), :]
        return ref.at[pl.ds(start, blk, stride=dil), :]

    def run(starts, dil, prevs, bias_ref, first, last):
        units = [unit(s, dil, hp) for s, hp in zip(starts, prevs)]
        for s, (o, m, l) in zip(starts, _attend(units, bias_ref, scale)):
            ao, am, al = (acc_rows(ref, s, dil) for ref in (acc_o, acc_m, acc_l))
            if first:
                ao[...] = o
                am[...] = m
                al[...] = l
            else:
                m_old = am[...]
                m_new = jnp.maximum(m_old, m)
                w_old, w_new = jnp.exp2(m_old - m_new), jnp.exp2(m - m_new)
                ao[...] = w_old * ao[...] + w_new * o
                al[...] = w_old * al[...] + w_new * l
                if not last:
                    am[...] = m_new

    run([n * blk for n in range(group)], 1, [n > 0 for n in range(group)], b1_ref, True, False)

    def body1(g, c):
        base = pl.multiple_of(g * (group * blk), group * blk)
        run([base + n * blk for n in range(group)], 1, [True] * group, b1_ref, True, False)
        return c
    lax.fori_loop(1, seq // (group * blk), body1, 0)

    nb4 = seq // 4 // blk

    def body4(r, c):
        run([r + n * blk * 4 for n in range(nb4)], 4, [n > 0 for n in range(nb4)], b4_ref,
            False, False)
        return c
    lax.fori_loop(0, 4, body4, 0)

    def body16(g, c):
        run([g * group + j for j in range(group)], 16, [False] * group, b16_ref, False, True)
        return c
    lax.fori_loop(0, 16 // group, body16, 0)

    o_ref[0] = (acc_o[...] / acc_l[...]).astype(o_ref.dtype)


def _stage_attn(qkv, rel_bias):
    B, S, d3 = qkv.shape
    d_att = d3 // 3
    n_pairs = d_att // LANES
    assert DIL_PATTERNS == ((ATT_BLK, 1), (4 * ATT_BLK, 4), (16 * ATT_BLK, 16))
    assert S == 16 * ATT_BLK
    log2e = math.log2(math.e)
    biases = [_band_bias(rel_bias, dil) * log2e for _, dil in DIL_PATTERNS]
    part = lambda i: pl.BlockSpec((1, S, LANES), lambda b, p: (b, 0, i * n_pairs + p))
    bias_spec = pl.BlockSpec((2, ATT_BLK, 2 * ATT_BLK), lambda b, p: (p, 0, 0))
    return pl.pallas_call(
        functools.partial(_attn_kernel, seq=S, scale=HEAD_DIM ** -0.5 * log2e),
        grid=(B, n_pairs),
        in_specs=[part(0), part(1), part(2), bias_spec, bias_spec, bias_spec],
        out_specs=pl.BlockSpec((1, S, LANES), lambda b, p: (b, 0, p)),
        out_shape=jax.ShapeDtypeStruct((B, S, d_att), BF16),
        scratch_shapes=[pltpu.VMEM((S, LANES), F32)] * 3,
        compiler_params=_params(("parallel", "parallel")),
        name="attn",
    )(qkv, qkv, qkv, *biases)


def _mix_kernel(x_ref, yr_ref, att_ref, p_ref, wor_ref, woa_ref, g_ref, b_ref, rhi_ref, rlo_ref,
                rb_ref, pg_ref, pp_ref, x1_ref, x1t_ref, lg_ref, ple_ref, *, alpha):
    mix = _dot(yr_ref[...], wor_ref[...]) + _dot(att_ref[...], woa_ref[...])
    emb = _dot(p_ref[...], pp_ref[...])
    x1 = _layer_norm(alpha * x_ref[...] + mix, g_ref[...], b_ref[...])
    x1_ref[...] = x1
    _store_token_tiles(x1t_ref, x1)
    hi, lo = _split(x1)
    logits = (jnp.dot(hi, rhi_ref[...], preferred_element_type=F32)
              + jnp.dot(lo, rhi_ref[...], preferred_element_type=F32)
              + jnp.dot(hi, rlo_ref[...], preferred_element_type=F32) + rb_ref[...])
    ple_ref[...] = (_sigmoid(_dot(hi, pg_ref[...])) * emb).astype(ple_ref.dtype)

    lane = lax.broadcasted_iota(jnp.int32, logits.shape, 1).astype(F32)
    big = float(ROUTER_LANES)
    rmax = lambda t: jnp.max(t, axis=-1, keepdims=True)
    first = lambda hit: jnp.min(jnp.where(hit, lane, big), axis=-1, keepdims=True)
    off = float("-inf")
    is_grp = lane < N_GROUPS
    lg = jnp.where(is_grp, logits, off)
    mg = rmax(lg)
    wg = 1.0 / jnp.sum(jnp.where(is_grp, jnp.exp(lg - mg), 0.0), axis=-1, keepdims=True)
    lo_lane = N_GROUPS + EXPERTS_PER_GROUP * first(lg == mg)
    le = jnp.where((lane >= lo_lane) & (lane < lo_lane + EXPERTS_PER_GROUP), logits, off)
    v1 = rmax(le)
    i1 = first(le == v1)
    le = jnp.where(lane == i1, off, le)
    v2 = rmax(le)
    i2 = first(le == v2)
    e2 = jnp.exp(v2 - v1)
    w1 = wg / (1.0 + e2)
    route = jnp.where(lane == 0, w1, jnp.where(lane == 1, w1 * e2, jnp.where(
        lane == 2, i1 - N_GROUPS, jnp.where(lane == 3, i2 - N_GROUPS, 0.0))))
    lg_ref[...] = route


def _stage_mix(x2, yr, att, p2, w_o, ln_g, ln_b, router_g, router_g_b, router_e, router_e_b,
               ple_gate, ple_proj, alpha):
    N, D = x2.shape
    d_r = yr.shape[1]
    rows = min(MIX_ROWS, N)
    d_att = att.shape[1]
    n_log = N_GROUPS + N_EXPERTS
    rw = jnp.zeros((D, ROUTER_LANES), F32).at[:, :N_GROUPS].set(router_g)
    rw = rw.at[:, N_GROUPS:n_log].set(router_e)
    rb = jnp.zeros((1, ROUTER_LANES), F32).at[0, :N_GROUPS].set(router_g_b)
    rb = rb.at[0, N_GROUPS:n_log].set(router_e_b)
    rhi, rlo = _split(rw)
    const = lambda shape: pl.BlockSpec(shape, lambda i: (0,) * len(shape))
    tile = lambda c: pl.BlockSpec((rows, c), lambda i: (i, 0))
    return pl.pallas_call(
        functools.partial(_mix_kernel, alpha=alpha),
        grid=(N // rows,),
        in_specs=[tile(D), tile(d_r), tile(d_att), tile(p2.shape[1]),
                  const((d_r, D)), const((d_att, D)), const((1, D)), const((1, D)),
                  const((D, ROUTER_LANES)), const((D, ROUTER_LANES)), const((1, ROUTER_LANES)),
                  const((D, D)), const(ple_proj.shape)],
        out_specs=[tile(D), pl.BlockSpec((rows * SUBLANES, LANES), lambda i: (i, 0)),
                   tile(ROUTER_LANES), tile(D)],
        out_shape=[jax.ShapeDtypeStruct((N, D), F32),
                   jax.ShapeDtypeStruct((N * SUBLANES, LANES), F32),
                   jax.ShapeDtypeStruct((N, ROUTER_LANES), F32),
                   jax.ShapeDtypeStruct((N, D), BF16)],
        compiler_params=_params(("parallel",)),
        name="mix",
    )(x2, yr, att, p2, w_o[:d_r].astype(BF16), w_o[d_r:].astype(BF16),
      ln_g.reshape(1, D), ln_b.reshape(1, D), rhi, rlo, rb, ple_gate.astype(BF16),
      ple_proj.astype(BF16))


def _route(route, rows):
    N = route.shape[0]
    we = route[:, :TOP_K]
    eid = route[:, TOP_K:2 * TOP_K].astype(jnp.int32)
    A = N * TOP_K
    e_flat = eid.T.reshape(A).astype(jnp.int32)
    order = jnp.argsort(e_flat).astype(jnp.int32)
    counts = jnp.sum((e_flat[:, None] == jnp.arange(N_EXPERTS)[None, :]).astype(jnp.int32), axis=0)
    start = jnp.cumsum(counts) - counts
    padded = (counts + rows - 1) // rows * rows
    pend = jnp.cumsum(padded)
    pstart = pend - padded
    nblk = -(-A // rows) + N_EXPERTS
    nused = (pend[-1] // rows).astype(jnp.int32)
    blk = jnp.arange(nblk, dtype=jnp.int32)
    blk_e = jnp.sum((pend[None, :] <= (blk * rows)[:, None]).astype(jnp.int32), axis=1)
    blk_e = jnp.minimum(blk_e, N_EXPERTS - 1)
    blk_e = jnp.where(blk < nused, blk_e, blk_e[nused - 1]).astype(jnp.int32)
    r = jnp.arange(rows, dtype=jnp.int32)[None, :]
    off = (blk * rows - pstart[blk_e])[:, None] + r
    real = (off < counts[blk_e][:, None]) & (blk < nused)[:, None]
    asg = order[jnp.clip(start[blk_e][:, None] + off, 0, A - 1)]
    row_tok = jnp.where(real, asg % N, r).astype(jnp.int32)
    row_dst = jnp.where(real, asg, A + (blk % 2)[:, None] * rows + r).astype(jnp.int32)
    return (row_tok.reshape(nblk, 1, rows), row_dst.reshape(nblk, 1, rows), blk_e,
            nused.reshape(1), we)


def _moe_kernel(blk_e_ref, nused_ref, tok0_ref, tokn_ref, dstp_ref, x_hbm, wg_ref, wu_ref, wd_ref,
                y_hbm, xbuf, ybuf, gsem, ssem, *, rows, n_real):
    i = pl.program_id(0)
    nused = nused_ref[0]
    slot = i % 2
    tile_rows = lambda first: pl.ds(pl.multiple_of(first, SUBLANES), SUBLANES)

    def gather_row(tab_ref, sl, r):
        return pltpu.make_async_copy(x_hbm.at[tile_rows(tab_ref[0, 0, r])],
                                     xbuf.at[sl, pl.ds(r * SUBLANES, SUBLANES)], gsem.at[sl])

    def scatter_row(sl, r):
        return pltpu.make_async_copy(ybuf.at[sl, pl.ds(r * SUBLANES, SUBLANES)],
                                     y_hbm.at[tile_rows(dstp_ref[0, 0, r])], ssem.at[sl])

    def gather_all(sl):
        return pltpu.make_async_copy(x_hbm.at[pl.ds(0, rows * SUBLANES)], xbuf.at[sl],
                                     gsem.at[sl])

    def scatter_all(sl, first_row=0):
        return pltpu.make_async_copy(
            ybuf.at[sl], y_hbm.at[pl.ds(first_row * SUBLANES, rows * SUBLANES)], ssem.at[sl])

    def rolled(fn):
        def body(r, c):
            fn(r)
            return c
        lax.fori_loop(0, rows, body, 0, unroll=8)

    def inline(fn):
        for r in range(rows):
            fn(r)

    def ffn(sl):
        xb = _load_token_tiles(xbuf.at[sl]).astype(BF16)
        gate = _dot(xb, wg_ref[0])
        up = _dot(xb, wu_ref[0])
        _store_token_tiles(ybuf.at[sl], _dot(gate * _sigmoid(gate) * up, wd_ref[0]))

    @pl.when(i == 0)
    def _():
        ybuf[...] = jnp.zeros_like(ybuf)
        for sl in range(2):
            scatter_all(sl, n_real + sl * rows).start()
        for sl in range(2):
            scatter_all(sl, n_real + sl * rows).wait()
        rolled(lambda r: gather_row(tok0_ref, 0, r).start())

    @pl.when(i < nused)
    def _():
        gather_all(slot).wait()

    @pl.when((i >= 2) & (i <= nused))
    def _():
        scatter_all(slot).wait()

    @pl.when(i == 0)
    def _():
        inline(lambda r: gather_row(tokn_ref, 1, r).start())
        ffn(0)

    for s in range(2):
        @pl.when((i >= 1) & (i < nused) & (slot == s))
        def _():
            inline(lambda r: gather_row(tokn_ref, 1 - s, r).start(priority=r % 2))
            inline(lambda r: scatter_row(1 - s, r).start(priority=r % 2))
            ffn(s)

    @pl.when(i == nused)
    def _():
        rolled(lambda r: scatter_row(1 - slot, r).start())
        scatter_all(1 - slot).wait()
        gather_all(slot).wait()


def _stage_moe(x1t, row_tok, row_dst, blk_e, nused, w_gate, w_up, w_down):
    N = x1t.shape[0] // SUBLANES
    D = SUBLANES * LANES
    nblk, _, rows = row_tok.shape
    d_e = w_gate.shape[2]
    n_real = N * TOP_K
    last = nblk - 1
    wspec = lambda shape: pl.BlockSpec((1,) + shape,
                                       lambda i, be, nu: (be[jnp.minimum(i, last)], 0, 0))
    table = lambda fn: pl.BlockSpec((1, 1, rows), lambda i, be, nu: (fn(i), 0, 0),
                                    memory_space=pltpu.SMEM)
    grid_spec = pltpu.PrefetchScalarGridSpec(
        num_scalar_prefetch=2,
        grid=(nblk + 1,),
        in_specs=[table(lambda i: 0), table(lambda i: jnp.minimum(i + 1, last)),
                  table(lambda i: jnp.clip(i - 1, 0, last)),
                  pl.BlockSpec(memory_space=pl.ANY), wspec((D, d_e)), wspec((D, d_e)),
                  wspec((d_e, D))],
        out_specs=pl.BlockSpec(memory_space=pl.ANY),
        scratch_shapes=[pltpu.VMEM((2, rows * SUBLANES, LANES), F32),
                        pltpu.VMEM((2, rows * SUBLANES, LANES), F32),
                        pltpu.SemaphoreType.DMA((2,)), pltpu.SemaphoreType.DMA((2,))],
    )
    return pl.pallas_call(
        functools.partial(_moe_kernel, rows=rows, n_real=n_real),
        grid_spec=grid_spec,
        out_shape=jax.ShapeDtypeStruct(((n_real + 2 * rows) * SUBLANES, LANES), F32),
        compiler_params=_params(("arbitrary",)),
        name="moe",
    )(blk_e, nused, row_tok * SUBLANES, row_tok * SUBLANES, row_dst * SUBLANES, x1t, w_gate, w_up,
      w_down)


def _out_kernel(x1_ref, y0_ref, y1_ref, we_ref, ple_ref, g_ref, b_ref, o_ref, *, alpha):
    we = we_ref[...]
    moe = we[:, 0:1] * _load_token_tiles(y0_ref) + we[:, 1:2] * _load_token_tiles(y1_ref)
    o_ref[...] = _layer_norm(alpha * x1_ref[...] + moe + ple_ref[...], g_ref[...], b_ref[...])


def _stage_out(x1, yb, we, ple, ln_g, ln_b, alpha):
    N, D = x1.shape
    rows = min(OUT_ROWS, N)
    const = lambda shape: pl.BlockSpec(shape, lambda i: (0,) * len(shape))
    tile = lambda c: pl.BlockSpec((rows, c), lambda i: (i, 0))
    slot = lambda j: pl.BlockSpec((rows * SUBLANES, LANES), lambda i: (i + j * (N // rows), 0))
    return pl.pallas_call(
        functools.partial(_out_kernel, alpha=alpha),
        grid=(N // rows,),
        in_specs=[tile(D), slot(0), slot(1), tile(TOP_K), tile(D), const((1, D)), const((1, D))],
        out_specs=tile(D),
        out_shape=jax.ShapeDtypeStruct((N, D), F32),
        compiler_params=_params(("parallel",)),
        name="out",
    )(x1, yb, yb, we, ple, ln_g.reshape(1, D), ln_b.reshape(1, D))


def kernel(x, p, w_in, mu_rkv, mu_lora, w0, w_lora1, w_lora2, a0, a_lora1, a_lora2, g_lora1, g_lora2, k_k, k_a, r_k, lnx_g, lnx_b, rel_bias, w_o, ln1_g, ln1_b, router_g, router_g_b, router_e, router_e_b, w_gate, w_up, w_down, ple_gate, ple_proj, ln2_g, ln2_b):
    B, S, D = x.shape
    depth = w_in.shape[0]
    alpha = (2 * depth) ** 0.25
    for i in range(depth):
        rkv, qkv, lw, ai, g = _stage_proj(x, w_in[i], w_lora1[i], a_lora1[i], g_lora1[i],
                                          mu_lora[i], w_lora2[i], a_lora2[i], g_lora2[i],
                                          w0[i], a0[i])
        yr = _stage_wkv(rkv, lw, ai, g, mu_rkv[i], k_k[i], k_a[i], r_k[i], lnx_g[i], lnx_b[i])
        att = _stage_attn(qkv, rel_bias).reshape(B * S, -1)
        x1, x1t, route, ple = _stage_mix(x.reshape(B * S, D), yr.reshape(B * S, -1), att,
                                          p[i].reshape(B * S, -1), w_o[i], ln1_g[i], ln1_b[i],
                                          router_g[i], router_g_b[i], router_e[i],
                                          router_e_b[i], ple_gate[i], ple_proj[i], alpha)
        row_tok, row_dst, blk_e, nused, we = _route(route, MOE_ROWS)
        yb = _stage_moe(x1t, row_tok, row_dst, blk_e, nused, w_gate[i], w_up[i], w_down[i])
        x = _stage_out(x1, yb, we, ple, ln2_g[i], ln2_b[i], alpha).reshape(B, S, D)
    return x
```

```python
import functools
import math

import jax
import jax.numpy as jnp
import numpy as np
from jax import lax
from jax.experimental import pallas as pl
from jax.experimental.pallas import tpu as pltpu

F32 = jnp.float32
BF16 = jnp.bfloat16

HEAD_DIM = 64
LANES = 128
SUBLANES = 8
MXU_DEPTH = 256
DECAY_LORA = 64
ICLR_LORA = 64
GATE_LORA = 128
LNX_EPS = 64e-5
DIL_PATTERNS = ((128, 1), (512, 4), (2048, 16))
NEG_INF = -1e30
N_BUCKETS = 32
MAX_DISTANCE = 2048
N_GROUPS = 4
EXPERTS_PER_GROUP = 8
N_EXPERTS = N_GROUPS * EXPERTS_PER_GROUP
TOP_K = 2
LN_EPS = 1e-5

WKV_CHUNK = 64
WKV_CHUNKS_PER_STEP = 4
ATT_BLK = 128
ATT_UNITS = 8
PROJ_ROWS = 512
MIX_ROWS = 512
MOE_ROWS = 256
OUT_ROWS = 1024
ROUTER_LANES = 128
VMEM_LIMIT = 56 * 1024 * 1024


def _params(sem):
    return pltpu.CompilerParams(dimension_semantics=sem, vmem_limit_bytes=VMEM_LIMIT)


def _dot(a, b):
    return jnp.dot(a.astype(BF16), b.astype(BF16), preferred_element_type=F32)


def _dot_nt(a, b):
    return lax.dot_general(a.astype(BF16), b.astype(BF16), (((1,), (1,)), ((), ())),
                           preferred_element_type=F32)


def _dot_tn(a, b):
    return lax.dot_general(a.astype(BF16), b.astype(BF16), (((0,), (0,)), ((), ())),
                           preferred_element_type=F32)


def _split(x):
    hi = x.astype(BF16)
    lo = (x - hi.astype(F32)).astype(BF16)
    return hi, lo


def _dot_exact_lhs(w01, x):
    hi, lo = _split(x)
    return (jnp.dot(w01, hi, preferred_element_type=F32)
            + jnp.dot(w01, lo, preferred_element_type=F32))


def _sigmoid(x):
    return 1.0 / (1.0 + jnp.exp(-x))


def _shift_rows(t, prev_row):
    rolled = pltpu.roll(t, 1, 0)
    row = lax.broadcasted_iota(jnp.int32, t.shape, 0)
    return jnp.where(row == 0, prev_row, rolled)


def _store_token_tiles(ref, t):
    rows = t.shape[0]
    for s in range(SUBLANES):
        ref[pl.ds(s, rows, stride=SUBLANES), :] = t[:, s * LANES:(s + 1) * LANES]


def _load_token_tiles(ref):
    rows = ref.shape[0] // SUBLANES
    return jnp.concatenate([ref[pl.ds(s, rows, stride=SUBLANES), :] for s in range(SUBLANES)],
                           axis=1)


def _layer_norm(z, g, b):
    mu = jnp.mean(z, axis=-1, keepdims=True)
    zc = z - mu
    var = jnp.mean(zc * zc, axis=-1, keepdims=True)
    return zc * lax.rsqrt(var + LN_EPS) * g + b


def _proj_kernel(x_ref, win_ref, w1_ref, a1_ref, g1_ref, mul_ref, w2_ref, a2_ref, g2_ref,
                 w0_ref, a0_ref, rkv_ref, qkv_ref, lw_ref, ai_ref, g_ref, prev_ref, *, d_rkv):
    @pl.when(pl.program_id(1) == 0)
    def _():
        prev_ref[...] = jnp.zeros_like(prev_ref)

    h = x_ref[0]
    rows = h.shape[0]
    hprev = _shift_rows(h, prev_ref[0:1, :])
    prev_ref[0:1, :] = h[rows - 1:rows, :]

    dh = hprev - h
    hb = h.astype(BF16)
    low_w = _dot(h + dh * mul_ref[0:1, :], w1_ref[...])
    low_a = _dot(h + dh * mul_ref[1:2, :], a1_ref[...])
    low_g = _dot(h + dh * mul_ref[2:3, :], g1_ref[...])
    rkv_ref[0] = _dot(hb, win_ref[:, :d_rkv])
    wl = w0_ref[...] + _dot(jnp.tanh(low_w), w2_ref[...])
    al = a0_ref[...] + _dot(low_a, a2_ref[...])
    g_ref[0] = _dot(_sigmoid(low_g), g2_ref[...]).astype(g_ref.dtype)
    qkv_ref[0] = _dot(hb, win_ref[:, d_rkv:])
    z = -wl
    softplus = jnp.maximum(z, 0.0) + jnp.log(1.0 + jnp.exp(-jnp.abs(z)))
    lw_ref[0] = -jnp.exp(-softplus - 0.5)
    ai_ref[0] = _sigmoid(al)


def _stage_proj(x, w_in, w1, a1, g1, mu_lora, w2, a2, g2, w0, a0):
    B, S, D = x.shape
    d_in = w_in.shape[1]
    d_r = w2.shape[1]
    d_rkv = 3 * d_r
    d_att = d_in - d_rkv
    rows = min(PROJ_ROWS, S)
    const = lambda shape: pl.BlockSpec(shape, lambda b, j: (0,) * len(shape))
    tile = lambda c: pl.BlockSpec((1, rows, c), lambda b, j: (b, j, 0))
    return pl.pallas_call(
        functools.partial(_proj_kernel, d_rkv=d_rkv),
        grid=(B, S // rows),
        in_specs=[tile(D), const((D, d_in)), const(w1.shape), const(a1.shape), const(g1.shape),
                  const(mu_lora.shape), const(w2.shape), const(a2.shape), const(g2.shape),
                  const((1, d_r)), const((1, d_r))],
        out_specs=[tile(d_rkv), tile(d_att), tile(d_r), tile(d_r), tile(d_r)],
        out_shape=[jax.ShapeDtypeStruct((B, S, d_rkv), F32),
                   jax.ShapeDtypeStruct((B, S, d_att), F32),
                   jax.ShapeDtypeStruct((B, S, d_r), F32),
                   jax.ShapeDtypeStruct((B, S, d_r), F32),
                   jax.ShapeDtypeStruct((B, S, d_r), BF16)],
        scratch_shapes=[pltpu.VMEM((8, D), F32)],
        compiler_params=_params(("parallel", "arbitrary")),
        name="proj",
    )(x, w_in.astype(BF16), w1.astype(BF16), a1.astype(BF16), g1.astype(BF16), mu_lora,
      w2.astype(BF16), a2.astype(BF16), g2.astype(BF16), w0.reshape(1, d_r), a0.reshape(1, d_r))


def _wkv_kernel(rkv_ref, lw_ref, ai_ref, g_ref, mu_ref, kk_ref, ka_ref, rk_ref, lng_ref, lnb_ref,
                seg_ref, tril_ref, y_ref, s_ref, prev_ref, *, d_r, n_chunks):
    C = WKV_CHUNK
    rows = C * n_chunks

    @pl.when(pl.program_id(1) == 0)
    def _():
        s_ref[...] = jnp.zeros_like(s_ref)
        prev_ref[...] = jnp.zeros_like(prev_ref)

    rkv = rkv_ref[0]
    prev = _shift_rows(rkv, prev_ref[0:1, :])
    prev_ref[0:1, :] = rkv[rows - 1:rows, :]
    mixed = rkv + (prev - rkv) * mu_ref[...]
    r = mixed[:, :d_r]
    k = mixed[:, d_r:2 * d_r]
    v = mixed[:, 2 * d_r:]
    lw = lw_ref[0]
    a = ai_ref[0]
    seg = seg_ref[...]
    seg_w = seg.shape[0]

    def head_sum(t):
        return jnp.concatenate([_dot(t[:, j:j + seg_w], seg) for j in range(0, d_r, seg_w)],
                               axis=1)

    kk = k * kk_ref[...]
    kk = kk * lax.rsqrt(jnp.maximum(head_sum(kk * kk), 1e-24))
    k = k * (1.0 + (a - 1.0) * ka_ref[...])
    aa = -kk
    bb = kk * a
    c = _dot_exact_lhs(tril_ref[...], lw)
    e_neg = jnp.exp(-c)
    a_t = aa * jnp.exp(c - lw)
    b_t = bb * e_neg
    k_t = k * e_neg
    r_t = r * jnp.exp(c)

    lane = lax.broadcasted_iota(jnp.int32, (1, LANES), 1)
    head0 = lane < HEAD_DIM
    ri = lax.broadcasted_iota(jnp.int32, (2 * C, 2 * C), 0)
    ci = lax.broadcasted_iota(jnp.int32, (2 * C, 2 * C), 1)
    strict = (ri % C) > (ci % C)
    incl = (ri % C) >= (ci % C)
    eye = (ri == ci).astype(F32)
    zero_bf = jnp.zeros((2 * C, LANES), BF16)

    def stack(t):
        return jnp.concatenate([jnp.where(head0, t, 0.0), jnp.where(head0, 0.0, t)],
                               axis=0).astype(BF16)

    def fold(t):
        return t[:C] + t[C:]

    n_pairs = d_r // LANES
    folds = [(q, p) for q in range(n_chunks) for p in range(n_pairs)]
    each = lambda fn, *cols: [fn(*args) for args in zip(*cols)]
    cat0 = lambda *ts: jnp.concatenate(ts, axis=0)
    cat1 = lambda *ts: jnp.concatenate(ts, axis=1)
    left = lambda t: t[:, :2 * C]
    right = lambda t: t[:, 2 * C:]

    p_ends, b_es, k_es = [], [], []
    for q in range(n_chunks):
        rs = slice(q * C, (q + 1) * C)
        c_end = c[rs][C - 1:C, :]
        e_end = jnp.exp(c_end - c[rs])
        b_es.append(bb[rs] * e_end)
        k_es.append(k[rs] * e_end)
        p_ends.append(jnp.exp(c_end))

    def stacks(t_of_q):
        return [stack(t_of_q(q)[:, p * LANES:(p + 1) * LANES]) for q, p in folds]

    chunk = lambda t: (lambda q: t[q * C:(q + 1) * C])
    am, bm, km, rm, vm = (stacks(chunk(t)) for t in (a_t, b_t, k_t, r_t, v))
    bem = stacks(lambda q: b_es[q])
    kem = stacks(lambda q: k_es[q])

    x = each(lambda a_, r_, b_, k_: _dot_nt(cat0(a_, r_), cat0(b_, k_)), am, rm, bm, km)
    n = each(lambda t: jnp.where(strict, t[:2 * C, :2 * C], 0.0), x)
    a_ak = each(lambda t: jnp.where(strict, t[:2 * C, 2 * C:], 0.0).astype(BF16), x)
    a_rb = each(lambda t: jnp.where(incl, t[2 * C:, :2 * C], 0.0).astype(BF16), x)
    a_rk = each(lambda t: jnp.where(incl, t[2 * C:, 2 * C:], 0.0).astype(BF16), x)
    t_inv = each(lambda t: eye + t, n)
    m = each(lambda t: _dot(t, t), n)
    av = each(lambda a_, v_: _dot(a_, v_).astype(BF16), a_ak, vm)
    for _ in range(int(math.log2(C)) - 2):
        z = each(lambda m_, t_: _dot(m_, cat1(m_.astype(BF16), t_.astype(BF16))), m, t_inv)
        t_inv = each(lambda t_, z_: t_ + right(z_), t_inv, z)
        m = each(left, z)
    t_inv = each(lambda t_, m_: t_ + _dot(m_, t_), t_inv, m)
    z = each(lambda t_, a_, av_: _dot(t_, cat1(a_, av_)), t_inv, am, av)
    ap = each(lambda z_: left(z_).astype(BF16), z)
    u0 = each(lambda z_: right(z_).astype(BF16), z)
    z = each(lambda rb_, rk_, ap_, u_, v_: _dot(cat1(rb_, rk_),
                                               cat0(cat1(ap_, u_), cat1(zero_bf, v_))),
             a_rb, a_rk, ap, u0, vm)
    g_m = each(lambda ap_, be_: _dot_tn(ap_, be_).astype(BF16), ap, bem)
    d0 = each(lambda u_, v_, be_, ke_: _dot_tn(cat0(u_, v_), cat0(be_, ke_)), u0, vm, bem, kem)
    rp_t = each(lambda r_, z_: jnp.transpose(r_.astype(F32) + left(z_)).astype(BF16), rm, z)
    y0 = each(lambda z_: fold(right(z_)), z)

    s = [s_ref[p] for p in range(n_pairs)]
    y_cols = []
    for q in range(n_chunks):
        f0 = q * n_pairs
        zs = [_dot(s[p], cat1(g_m[f0 + p], rp_t[f0 + p])) for p in range(n_pairs)]
        y_cols.append([fold(jnp.transpose(right(zs[p]))) + y0[f0 + p] for p in range(n_pairs)])
        s = [s[p] * p_ends[q][:, p * LANES:(p + 1) * LANES] + left(zs[p]) + d0[f0 + p]
             for p in range(n_pairs)]
    for p in range(n_pairs):
        s_ref[p] = s[p]
    y = cat0(*[cat1(*cols) for cols in y_cols])

    inv = 1.0 / HEAD_DIM
    mu = head_sum(y) * inv
    yc = y - mu
    var = head_sum(yc * yc) * inv
    yn = yc * lax.rsqrt(var + LNX_EPS) * lng_ref[...] + lnb_ref[...]
    bonus = head_sum(r * k * rk_ref[...]) * v
    y_ref[0] = ((yn + bonus) * g_ref[0]).astype(y_ref.dtype)


def _stage_wkv(rkv, lw, ai, g, mu_rkv, k_k, k_a, r_k, lnx_g, lnx_b):
    B, S, d_r = lw.shape
    n_chunks = WKV_CHUNKS_PER_STEP
    rows = WKV_CHUNK * n_chunks
    head = np.arange(MXU_DEPTH) // HEAD_DIM
    seg = jnp.asarray(head[:, None] == head[None, :], BF16)
    t = np.arange(rows)
    tril = jnp.asarray((t[:, None] >= t[None, :])
                       & (t[:, None] // WKV_CHUNK == t[None, :] // WKV_CHUNK), BF16)
    const = lambda shape: pl.BlockSpec(shape, lambda b, j: (0,) * len(shape))
    tile = lambda c: pl.BlockSpec((1, rows, c), lambda b, j: (b, j, 0))
    row = lambda t: t.reshape(1, -1)
    return pl.pallas_call(
        functools.partial(_wkv_kernel, d_r=d_r, n_chunks=n_chunks),
        grid=(B, S // rows),
        in_specs=[tile(3 * d_r), tile(d_r), tile(d_r), tile(d_r), const((1, 3 * d_r)),
                  const((1, d_r)), const((1, d_r)), const((1, d_r)), const((1, d_r)),
                  const((1, d_r)), const((MXU_DEPTH, MXU_DEPTH)), const((rows, rows))],
        out_specs=tile(d_r),
        out_shape=jax.ShapeDtypeStruct((B, S, d_r), BF16),
        scratch_shapes=[pltpu.VMEM((d_r // LANES, LANES, LANES), F32),
                        pltpu.VMEM((8, 3 * d_r), F32)],
        compiler_params=_params(("parallel", "arbitrary")),
        name="wkv",
    )(rkv, lw, ai, g, row(mu_rkv), row(k_k), row(k_a), row(r_k), row(lnx_g), row(lnx_b), seg, tril)


def _t5_bucket(n):
    exact = N_BUCKETS // 2
    nf = np.maximum(n, 1).astype(np.float32)
    large = exact + (np.log(nf / exact) / math.log(MAX_DISTANCE / exact)
                     * (N_BUCKETS - exact)).astype(np.int32)
    large = np.minimum(large, N_BUCKETS - 1)
    return np.where(n < exact, n, large).astype(np.int32)


def _band_bias(rel_bias, dil):
    blk = ATT_BLK
    L = 3 * blk
    rel = (2 * blk - 1) - np.arange(L)
    valid = (rel >= 0) & (rel <= blk)
    base = rel_bias[_t5_bucket(np.clip(rel, 0, None) * dil)].astype(F32).T
    base = jnp.where(valid[None], base, NEG_INF)
    skew = jnp.tile(base, (1, blk))[:, :blk * (L - 1)].reshape(-1, blk, L - 1)
    return skew[:, :, blk - 1:3 * blk - 1]


def _attend(units, bias_ref, scale):
    lane = lax.broadcasted_iota(jnp.int32, (1, LANES), 1)
    head0 = lane < HEAD_DIM
    jobs = [(u, h) for u in range(len(units)) for h in range(2)]
    s = []
    for u, h in jobs:
        q, k, _, col0 = units[u]
        qh = jnp.where(head0 if h == 0 else jnp.logical_not(head0), q * scale, 0.0)
        s.append(_dot_nt(qh, k) + bias_ref[h, :, col0:col0 + k.shape[0]])
    m = [jnp.max(t, axis=-1, keepdims=True) for t in s]
    e = [jnp.exp2(t - m_) for t, m_ in zip(s, m)]
    ol = [_dot(t, jnp.where(head0 if h == 0 else jnp.logical_not(head0), units[u][2], 1.0))
          for t, (u, h) in zip(e, jobs)]
    both = lambda t, u: jnp.where(head0, t[2 * u], t[2 * u + 1])
    swapped = lambda u: jnp.where(head0, ol[2 * u + 1], ol[2 * u])
    return [(both(ol, u), both(m, u), pltpu.roll(swapped(u), HEAD_DIM, 1))
            for u in range(len(units))]


def _attn_kernel(q_ref, k_ref, v_ref, b1_ref, b4_ref, b16_ref, o_ref, acc_o, acc_m, acc_l, *, seq,
                 scale):
    blk = ATT_BLK
    group = ATT_UNITS

    def rows(ref, start, n, dil):
        if dil == 1:
            return ref[0, pl.ds(start, n), :]
        return ref[0, pl.ds(start, n, stride=dil), :]

    def unit(start, dil, has_prev):
        q = rows(q_ref, start, blk, dil)
        if has_prev:
            k0 = start - blk * dil
            return (q, rows(k_ref, k0, 2 * blk, dil), rows(v_ref, k0, 2 * blk, dil), 0)
        return (q, rows(k_ref, start, blk, dil), rows(v_ref, start, blk, dil), blk)

    def acc_rows(ref, start, dil):
        if dil == 1:
            return ref.at[pl.ds(start, blk), :]
        return ref.at[pl.ds(start, blk, stride=dil), :]

    def run(starts, dil, prevs, bias_ref, first, last):
        units = [unit(s, dil, hp) for s, hp in zip(starts, prevs)]
        for s, (o, m, l) in zip(starts, _attend(units, bias_ref, scale)):
            ao, am, al = (acc_rows(ref, s, dil) for ref in (acc_o, acc_m, acc_l))
            if first:
                ao[...] = o
                am[...] = m
                al[...] = l
            else:
                m_old = am[...]
                m_new = jnp.maximum(m_old, m)
                w_old, w_new = jnp.exp2(m_old - m_new), jnp.exp2(m - m_new)
                ao[...] = w_old * ao[...] + w_new * o
                al[...] = w_old * al[...] + w_new * l
                if not last:
                    am[...] = m_new

    run([n * blk for n in range(group)], 1, [n > 0 for n in range(group)], b1_ref, True, False)

    def body1(g, c):
        base = pl.multiple_of(g * (group * blk), group * blk)
        run([base + n * blk for n in range(group)], 1, [True] * group, b1_ref, True, False)
        return c
    lax.fori_loop(1, seq // (group * blk), body1, 0)

    nb4 = seq // 4 // blk

    per4 = group // nb4

    def body4(g, c):
        rs = [g * per4 + j for j in range(per4)]
        run([r + n * blk * 4 for r in rs for n in range(nb4)], 4,
            [n > 0 for r in rs for n in range(nb4)], b4_ref, False, False)
        return c
    lax.fori_loop(0, 4 // per4, body4, 0)

    def body16(g, c):
        run([g * group + j for j in range(group)], 16, [False] * group, b16_ref, False, True)
        return c
    lax.fori_loop(0, 16 // group, body16, 0)

    o_ref[0] = (acc_o[...] / acc_l[...]).astype(o_ref.dtype)


def _stage_attn(qkv, rel_bias):
    B, S, d3 = qkv.shape
    d_att = d3 // 3
    n_pairs = d_att // LANES
    assert DIL_PATTERNS == ((ATT_BLK, 1), (4 * ATT_BLK, 4), (16 * ATT_BLK, 16))
    assert S == 16 * ATT_BLK
    log2e = math.log2(math.e)
    biases = [_band_bias(rel_bias, dil) * log2e for _, dil in DIL_PATTERNS]
    part = lambda i: pl.BlockSpec((1, S, LANES), lambda b, p: (b, 0, i * n_pairs + p))
    bias_spec = pl.BlockSpec((2, ATT_BLK, 2 * ATT_BLK), lambda b, p: (p, 0, 0))
    return pl.pallas_call(
        functools.partial(_attn_kernel, seq=S, scale=HEAD_DIM ** -0.5 * log2e),
        grid=(B, n_pairs),
        in_specs=[part(0), part(1), part(2), bias_spec, bias_spec, bias_spec],
        out_specs=pl.BlockSpec((1, S, LANES), lambda b, p: (b, 0, p)),
        out_shape=jax.ShapeDtypeStruct((B, S, d_att), BF16),
        scratch_shapes=[pltpu.VMEM((S, LANES), F32)] * 3,
        compiler_params=_params(("parallel", "parallel")),
        name="attn",
    )(qkv, qkv, qkv, *biases)


def _mix_kernel(x_ref, yr_ref, att_ref, p_ref, wor_ref, woa_ref, g_ref, b_ref, rhi_ref, rlo_ref,
                rb_ref, pg_ref, pp_ref, x1_ref, x1t_ref, lg_ref, ple_ref, *, alpha):
    mix = _dot(yr_ref[...], wor_ref[...]) + _dot(att_ref[...], woa_ref[...])
    emb = _dot(p_ref[...], pp_ref[...])
    x1 = _layer_norm(alpha * x_ref[...] + mix, g_ref[...], b_ref[...])
    x1_ref[...] = x1
    _store_token_tiles(x1t_ref, x1)
    hi, lo = _split(x1)
    logits = (jnp.dot(hi, rhi_ref[...], preferred_element_type=F32)
              + jnp.dot(lo, rhi_ref[...], preferred_element_type=F32)
              + jnp.dot(hi, rlo_ref[...], preferred_element_type=F32) + rb_ref[...])
    ple_ref[...] = (_sigmoid(_dot(hi, pg_ref[...])) * emb).astype(ple_ref.dtype)

    lane = lax.broadcasted_iota(jnp.int32, logits.shape, 1).astype(F32)
    big = float(ROUTER_LANES)
    rmax = lambda t: jnp.max(t, axis=-1, keepdims=True)
    first = lambda hit: jnp.min(jnp.where(hit, lane, big), axis=-1, keepdims=True)
    off = float("-inf")
    is_grp = lane < N_GROUPS
    lg = jnp.where(is_grp, logits, off)
    mg = rmax(lg)
    wg = 1.0 / jnp.sum(jnp.where(is_grp, jnp.exp(lg - mg), 0.0), axis=-1, keepdims=True)
    lo_lane = N_GROUPS + EXPERTS_PER_GROUP * first(lg == mg)
    le = jnp.where((lane >= lo_lane) & (lane < lo_lane + EXPERTS_PER_GROUP), logits, off)
    v1 = rmax(le)
    i1 = first(le == v1)
    le = jnp.where(lane == i1, off, le)
    v2 = rmax(le)
    i2 = first(le == v2)
    e2 = jnp.exp(v2 - v1)
    w1 = wg / (1.0 + e2)
    route = jnp.where(lane == 0, w1, jnp.where(lane == 1, w1 * e2, jnp.where(
        lane == 2, i1 - N_GROUPS, jnp.where(lane == 3, i2 - N_GROUPS, 0.0))))
    lg_ref[...] = route


def _stage_mix(x2, yr, att, p2, w_o, ln_g, ln_b, router_g, router_g_b, router_e, router_e_b,
               ple_gate, ple_proj, alpha):
    N, D = x2.shape
    d_r = yr.shape[1]
    rows = min(MIX_ROWS, N)
    d_att = att.shape[1]
    n_log = N_GROUPS + N_EXPERTS
    rw = jnp.zeros((D, ROUTER_LANES), F32).at[:, :N_GROUPS].set(router_g)
    rw = rw.at[:, N_GROUPS:n_log].set(router_e)
    rb = jnp.zeros((1, ROUTER_LANES), F32).at[0, :N_GROUPS].set(router_g_b)
    rb = rb.at[0, N_GROUPS:n_log].set(router_e_b)
    rhi, rlo = _split(rw)
    const = lambda shape: pl.BlockSpec(shape, lambda i: (0,) * len(shape))
    tile = lambda c: pl.BlockSpec((rows, c), lambda i: (i, 0))
    return pl.pallas_call(
        functools.partial(_mix_kernel, alpha=alpha),
        grid=(N // rows,),
        in_specs=[tile(D), tile(d_r), tile(d_att), tile(p2.shape[1]),
                  const((d_r, D)), const((d_att, D)), const((1, D)), const((1, D)),
                  const((D, ROUTER_LANES)), const((D, ROUTER_LANES)), const((1, ROUTER_LANES)),
                  const((D, D)), const(ple_proj.shape)],
        out_specs=[tile(D), pl.BlockSpec((rows * SUBLANES, LANES), lambda i: (i, 0)),
                   tile(ROUTER_LANES), tile(D)],
        out_shape=[jax.ShapeDtypeStruct((N, D), F32),
                   jax.ShapeDtypeStruct((N * SUBLANES, LANES), F32),
                   jax.ShapeDtypeStruct((N, ROUTER_LANES), F32),
                   jax.ShapeDtypeStruct((N, D), BF16)],
        compiler_params=_params(("parallel",)),
        name="mix",
    )(x2, yr, att, p2, w_o[:d_r].astype(BF16), w_o[d_r:].astype(BF16),
      ln_g.reshape(1, D), ln_b.reshape(1, D), rhi, rlo, rb, ple_gate.astype(BF16),
      ple_proj.astype(BF16))


def _route(route, rows):
    N = route.shape[0]
    we = route[:, :TOP_K]
    eid = route[:, TOP_K:2 * TOP_K].astype(jnp.int32)
    A = N * TOP_K
    e_flat = eid.T.reshape(A).astype(jnp.int32)
    order = jnp.argsort(e_flat).astype(jnp.int32)
    counts = jnp.sum((e_flat[:, None] == jnp.arange(N_EXPERTS)[None, :]).astype(jnp.int32), axis=0)
    start = jnp.cumsum(counts) - counts
    padded = (counts + rows - 1) // rows * rows
    pend = jnp.cumsum(padded)
    pstart = pend - padded
    nblk = -(-A // rows) + N_EXPERTS
    nused = (pend[-1] // rows).astype(jnp.int32)
    blk = jnp.arange(nblk, dtype=jnp.int32)
    blk_e = jnp.sum((pend[None, :] <= (blk * rows)[:, None]).astype(jnp.int32), axis=1)
    blk_e = jnp.minimum(blk_e, N_EXPERTS - 1)
    blk_e = jnp.where(blk < nused, blk_e, blk_e[nused - 1]).astype(jnp.int32)
    r = jnp.arange(rows, dtype=jnp.int32)[None, :]
    off = (blk * rows - pstart[blk_e])[:, None] + r
    real = (off < counts[blk_e][:, None]) & (blk < nused)[:, None]
    asg = order[jnp.clip(start[blk_e][:, None] + off, 0, A - 1)]
    row_tok = jnp.where(real, asg % N, r).astype(jnp.int32)
    row_dst = jnp.where(real, asg, A + (blk % 2)[:, None] * rows + r).astype(jnp.int32)
    return (row_tok.reshape(nblk, 1, rows), row_dst.reshape(nblk, 1, rows), blk_e,
            nused.reshape(1), we)


def _moe_kernel(blk_e_ref, nused_ref, tok0_ref, tokn_ref, dstp_ref, x_hbm, wg_ref, wu_ref, wd_ref,
                y_hbm, xbuf, ybuf, gsem, ssem, *, rows, n_real):
    i = pl.program_id(0)
    nused = nused_ref[0]
    slot = i % 2
    tile_rows = lambda first: pl.ds(pl.multiple_of(first, SUBLANES), SUBLANES)

    def gather_row(tab_ref, sl, r):
        return pltpu.make_async_copy(x_hbm.at[tile_rows(tab_ref[0, 0, r])],
                                     xbuf.at[sl, pl.ds(r * SUBLANES, SUBLANES)], gsem.at[sl])

    def scatter_row(sl, r):
        return pltpu.make_async_copy(ybuf.at[sl, pl.ds(r * SUBLANES, SUBLANES)],
                                     y_hbm.at[tile_rows(dstp_ref[0, 0, r])], ssem.at[sl])

    def gather_all(sl):
        return pltpu.make_async_copy(x_hbm.at[pl.ds(0, rows * SUBLANES)], xbuf.at[sl],
                                     gsem.at[sl])

    def scatter_all(sl, first_row=0):
        return pltpu.make_async_copy(
            ybuf.at[sl], y_hbm.at[pl.ds(first_row * SUBLANES, rows * SUBLANES)], ssem.at[sl])

    def rolled(fn):
        def body(r, c):
            fn(r)
            return c
        lax.fori_loop(0, rows, body, 0, unroll=8)

    def inline(fn):
        for r in range(rows):
            fn(r)

    def ffn(sl):
        xb = _load_token_tiles(xbuf.at[sl]).astype(BF16)
        gate = _dot(xb, wg_ref[0])
        up = _dot(xb, wu_ref[0])
        _store_token_tiles(ybuf.at[sl], _dot(gate * _sigmoid(gate) * up, wd_ref[0]))

    @pl.when(i == 0)
    def _():
        ybuf[...] = jnp.zeros_like(ybuf)
        for sl in range(2):
            scatter_all(sl, n_real + sl * rows).start()
        for sl in range(2):
            scatter_all(sl, n_real + sl * rows).wait()
        rolled(lambda r: gather_row(tok0_ref, 0, r).start())

    @pl.when(i < nused)
    def _():
        gather_all(slot).wait()

    @pl.when((i >= 2) & (i <= nused))
    def _():
        scatter_all(slot).wait()

    @pl.when(i == 0)
    def _():
        inline(lambda r: gather_row(tokn_ref, 1, r).start())
        ffn(0)

    for s in range(2):
        @pl.when((i >= 1) & (i < nused) & (slot == s))
        def _():
            inline(lambda r: gather_row(tokn_ref, 1 - s, r).start(priority=r % 2))
            inline(lambda r: scatter_row(1 - s, r).start(priority=r % 2))
            ffn(s)

    @pl.when(i == nused)
    def _():
        rolled(lambda r: scatter_row(1 - slot, r).start())
        scatter_all(1 - slot).wait()
        gather_all(slot).wait()


def _stage_moe(x1t, row_tok, row_dst, blk_e, nused, w_gate, w_up, w_down):
    N = x1t.shape[0] // SUBLANES
    D = SUBLANES * LANES
    nblk, _, rows = row_tok.shape
    d_e = w_gate.shape[2]
    n_real = N * TOP_K
    last = nblk - 1
    wspec = lambda shape: pl.BlockSpec((1,) + shape,
                                       lambda i, be, nu: (be[jnp.minimum(i, last)], 0, 0))
    table = lambda fn: pl.BlockSpec((1, 1, rows), lambda i, be, nu: (fn(i), 0, 0),
                                    memory_space=pltpu.SMEM)
    grid_spec = pltpu.PrefetchScalarGridSpec(
        num_scalar_prefetch=2,
        grid=(nblk + 1,),
        in_specs=[table(lambda i: 0), table(lambda i: jnp.minimum(i + 1, last)),
                  table(lambda i: jnp.clip(i - 1, 0, last)),
                  pl.BlockSpec(memory_space=pl.ANY), wspec((D, d_e)), wspec((D, d_e)),
                  wspec((d_e, D))],
        out_specs=pl.BlockSpec(memory_space=pl.ANY),
        scratch_shapes=[pltpu.VMEM((2, rows * SUBLANES, LANES), F32),
                        pltpu.VMEM((2, rows * SUBLANES, LANES), F32),
                        pltpu.SemaphoreType.DMA((2,)), pltpu.SemaphoreType.DMA((2,))],
    )
    return pl.pallas_call(
        functools.partial(_moe_kernel, rows=rows, n_real=n_real),
        grid_spec=grid_spec,
        out_shape=jax.ShapeDtypeStruct(((n_real + 2 * rows) * SUBLANES, LANES), F32),
        compiler_params=_params(("arbitrary",)),
        name="moe",
    )(blk_e, nused, row_tok * SUBLANES, row_tok * SUBLANES, row_dst * SUBLANES, x1t, w_gate, w_up,
      w_down)


def _out_kernel(x1_ref, y0_ref, y1_ref, we_ref, ple_ref, g_ref, b_ref, o_ref, *, alpha):
    we = we_ref[...]
    moe = we[:, 0:1] * _load_token_tiles(y0_ref) + we[:, 1:2] * _load_token_tiles(y1_ref)
    o_ref[...] = _layer_norm(alpha * x1_ref[...] + moe + ple_ref[...], g_ref[...], b_ref[...])


def _stage_out(x1, yb, we, ple, ln_g, ln_b, alpha):
    N, D = x1.shape
    rows = min(OUT_ROWS, N)
    const = lambda shape: pl.BlockSpec(shape, lambda i: (0,) * len(shape))
    tile = lambda c: pl.BlockSpec((rows, c), lambda i: (i, 0))
    slot = lambda j: pl.BlockSpec((rows * SUBLANES, LANES), lambda i: (i + j * (N // rows), 0))
    return pl.pallas_call(
        functools.partial(_out_kernel, alpha=alpha),
        grid=(N // rows,),
        in_specs=[tile(D), slot(0), slot(1), tile(TOP_K), tile(D), const((1, D)), const((1, D))],
        out_specs=tile(D),
        out_shape=jax.ShapeDtypeStruct((N, D), F32),
        compiler_params=_params(("parallel",)),
        name="out",
    )(x1, yb, yb, we, ple, ln_g.reshape(1, D), ln_b.reshape(1, D))


def kernel(x, p, w_in, mu_rkv, mu_lora, w0, w_lora1, w_lora2, a0, a_lora1, a_lora2, g_lora1, g_lora2, k_k, k_a, r_k, lnx_g, lnx_b, rel_bias, w_o, ln1_g, ln1_b, router_g, router_g_b, router_e, router_e_b, w_gate, w_up, w_down, ple_gate, ple_proj, ln2_g, ln2_b):
    B, S, D = x.shape
    depth = w_in.shape[0]
    alpha = (2 * depth) ** 0.25
    for i in range(depth):
        rkv, qkv, lw, ai, g = _stage_proj(x, w_in[i], w_lora1[i], a_lora1[i], g_lora1[i],
                                          mu_lora[i], w_lora2[i], a_lora2[i], g_lora2[i],
                                          w0[i], a0[i])
        yr = _stage_wkv(rkv, lw, ai, g, mu_rkv[i], k_k[i], k_a[i], r_k[i], lnx_g[i], lnx_b[i])
        att = _stage_attn(qkv, rel_bias).reshape(B * S, -1)
        x1, x1t, route, ple = _stage_mix(x.reshape(B * S, D), yr.reshape(B * S, -1), att,
                                          p[i].reshape(B * S, -1), w_o[i], ln1_g[i], ln1_b[i],
                                          router_g[i], router_g_b[i], router_e[i],
                                          router_e_b[i], ple_gate[i], ple_proj[i], alpha)
        row_tok, row_dst, blk_e, nused, we = _route(route, MOE_ROWS)
        yb = _stage_moe(x1t, row_tok, row_dst, blk_e, nused, w_gate[i], w_up[i], w_down[i])
        x = _stage_out(x1, yb, we, ple, ln2_g[i], ln2_b[i], alpha).reshape(B, S, D)
    return x
```

```python
import functools
import math

import jax
import jax.numpy as jnp
import numpy as np
from jax import lax
from jax.experimental import pallas as pl
from jax.experimental.pallas import tpu as pltpu

F32 = jnp.float32
BF16 = jnp.bfloat16

HEAD_DIM = 64
LANES = 128
SUBLANES = 8
MXU_DEPTH = 256
DECAY_LORA = 64
ICLR_LORA = 64
GATE_LORA = 128
LNX_EPS = 64e-5
DIL_PATTERNS = ((128, 1), (512, 4), (2048, 16))
NEG_INF = -1e30
N_BUCKETS = 32
MAX_DISTANCE = 2048
N_GROUPS = 4
EXPERTS_PER_GROUP = 8
N_EXPERTS = N_GROUPS * EXPERTS_PER_GROUP
TOP_K = 2
LN_EPS = 1e-5

WKV_CHUNK = 64
WKV_CHUNKS_PER_STEP = 4
ATT_BLK = 128
ATT_UNITS = 16
PROJ_ROWS = 512
MIX_ROWS = 512
MOE_ROWS = 256
OUT_ROWS = 1024
ROUTER_LANES = 128
VMEM_LIMIT = 56 * 1024 * 1024


def _params(sem):
    return pltpu.CompilerParams(dimension_semantics=sem, vmem_limit_bytes=VMEM_LIMIT)


def _dot(a, b):
    return jnp.dot(a.astype(BF16), b.astype(BF16), preferred_element_type=F32)


def _dot_nt(a, b):
    return lax.dot_general(a.astype(BF16), b.astype(BF16), (((1,), (1,)), ((), ())),
                           preferred_element_type=F32)


def _dot_tn(a, b):
    return lax.dot_general(a.astype(BF16), b.astype(BF16), (((0,), (0,)), ((), ())),
                           preferred_element_type=F32)


def _split(x):
    hi = x.astype(BF16)
    lo = (x - hi.astype(F32)).astype(BF16)
    return hi, lo


def _dot_exact_lhs(w01, x):
    hi, lo = _split(x)
    return (jnp.dot(w01, hi, preferred_element_type=F32)
            + jnp.dot(w01, lo, preferred_element_type=F32))


def _sigmoid(x):
    return 1.0 / (1.0 + jnp.exp(-x))


def _shift_rows(t, prev_row):
    rolled = pltpu.roll(t, 1, 0)
    row = lax.broadcasted_iota(jnp.int32, t.shape, 0)
    return jnp.where(row == 0, prev_row, rolled)


def _store_token_tiles(ref, t):
    rows = t.shape[0]
    for s in range(SUBLANES):
        ref[pl.ds(s, rows, stride=SUBLANES), :] = t[:, s * LANES:(s + 1) * LANES]


def _load_token_tiles(ref):
    rows = ref.shape[0] // SUBLANES
    return jnp.concatenate([ref[pl.ds(s, rows, stride=SUBLANES), :] for s in range(SUBLANES)],
                           axis=1)


def _layer_norm(z, g, b):
    mu = jnp.mean(z, axis=-1, keepdims=True)
    zc = z - mu
    var = jnp.mean(zc * zc, axis=-1, keepdims=True)
    return zc * lax.rsqrt(var + LN_EPS) * g + b


def _proj_kernel(x_ref, win_ref, w1_ref, a1_ref, g1_ref, mul_ref, w2_ref, a2_ref, g2_ref,
                 w0_ref, a0_ref, rkv_ref, qkv_ref, lw_ref, ai_ref, g_ref, prev_ref, *, d_rkv):
    @pl.when(pl.program_id(1) == 0)
    def _():
        prev_ref[...] = jnp.zeros_like(prev_ref)

    h = x_ref[0]
    rows = h.shape[0]
    hprev = _shift_rows(h, prev_ref[0:1, :])
    prev_ref[0:1, :] = h[rows - 1:rows, :]

    dh = hprev - h
    hb = h.astype(BF16)
    low_w = _dot(h + dh * mul_ref[0:1, :], w1_ref[...])
    low_a = _dot(h + dh * mul_ref[1:2, :], a1_ref[...])
    low_g = _dot(h + dh * mul_ref[2:3, :], g1_ref[...])
    rkv_ref[0] = _dot(hb, win_ref[:, :d_rkv])
    wl = w0_ref[...] + _dot(jnp.tanh(low_w), w2_ref[...])
    al = a0_ref[...] + _dot(low_a, a2_ref[...])
    g_ref[0] = _dot(_sigmoid(low_g), g2_ref[...]).astype(g_ref.dtype)
    qkv_ref[0] = _dot(hb, win_ref[:, d_rkv:])
    z = -wl
    softplus = jnp.maximum(z, 0.0) + jnp.log(1.0 + jnp.exp(-jnp.abs(z)))
    lw_ref[0] = -jnp.exp(-softplus - 0.5)
    ai_ref[0] = _sigmoid(al)


def _stage_proj(x, w_in, w1, a1, g1, mu_lora, w2, a2, g2, w0, a0):
    B, S, D = x.shape
    d_in = w_in.shape[1]
    d_r = w2.shape[1]
    d_rkv = 3 * d_r
    d_att = d_in - d_rkv
    rows = min(PROJ_ROWS, S)
    const = lambda shape: pl.BlockSpec(shape, lambda b, j: (0,) * len(shape))
    tile = lambda c: pl.BlockSpec((1, rows, c), lambda b, j: (b, j, 0))
    return pl.pallas_call(
        functools.partial(_proj_kernel, d_rkv=d_rkv),
        grid=(B, S // rows),
        in_specs=[tile(D), const((D, d_in)), const(w1.shape), const(a1.shape), const(g1.shape),
                  const(mu_lora.shape), const(w2.shape), const(a2.shape), const(g2.shape),
                  const((1, d_r)), const((1, d_r))],
        out_specs=[tile(d_rkv), tile(d_att), tile(d_r), tile(d_r), tile(d_r)],
        out_shape=[jax.ShapeDtypeStruct((B, S, d_rkv), F32),
                   jax.ShapeDtypeStruct((B, S, d_att), F32),
                   jax.ShapeDtypeStruct((B, S, d_r), F32),
                   jax.ShapeDtypeStruct((B, S, d_r), F32),
                   jax.ShapeDtypeStruct((B, S, d_r), BF16)],
        scratch_shapes=[pltpu.VMEM((8, D), F32)],
        compiler_params=_params(("parallel", "arbitrary")),
        name="proj",
    )(x, w_in.astype(BF16), w1.astype(BF16), a1.astype(BF16), g1.astype(BF16), mu_lora,
      w2.astype(BF16), a2.astype(BF16), g2.astype(BF16), w0.reshape(1, d_r), a0.reshape(1, d_r))


def _wkv_kernel(rkv_ref, lw_ref, ai_ref, g_ref, mu_ref, kk_ref, ka_ref, rk_ref, lng_ref, lnb_ref,
                seg_ref, tril_ref, y_ref, s_ref, prev_ref, *, d_r, n_chunks):
    C = WKV_CHUNK
    rows = C * n_chunks

    @pl.when(pl.program_id(1) == 0)
    def _():
        s_ref[...] = jnp.zeros_like(s_ref)
        prev_ref[...] = jnp.zeros_like(prev_ref)

    rkv = rkv_ref[0]
    prev = _shift_rows(rkv, prev_ref[0:1, :])
    prev_ref[0:1, :] = rkv[rows - 1:rows, :]
    mixed = rkv + (prev - rkv) * mu_ref[...]
    r = mixed[:, :d_r]
    k = mixed[:, d_r:2 * d_r]
    v = mixed[:, 2 * d_r:]
    lw = lw_ref[0]
    a = ai_ref[0]
    seg = seg_ref[...]
    seg_w = seg.shape[0]

    def head_sum(t):
        return jnp.concatenate([_dot(t[:, j:j + seg_w], seg) for j in range(0, d_r, seg_w)],
                               axis=1)

    kk = k * kk_ref[...]
    kk = kk * lax.rsqrt(jnp.maximum(head_sum(kk * kk), 1e-24))
    k = k * (1.0 + (a - 1.0) * ka_ref[...])
    aa = -kk
    bb = kk * a
    c = _dot_exact_lhs(tril_ref[...], lw)
    e_neg = jnp.exp(-c)
    a_t = aa * jnp.exp(c - lw)
    b_t = bb * e_neg
    k_t = k * e_neg
    r_t = r * jnp.exp(c)

    lane = lax.broadcasted_iota(jnp.int32, (1, LANES), 1)
    head0 = lane < HEAD_DIM
    ri = lax.broadcasted_iota(jnp.int32, (2 * C, 2 * C), 0)
    ci = lax.broadcasted_iota(jnp.int32, (2 * C, 2 * C), 1)
    strict = (ri % C) > (ci % C)
    incl = (ri % C) >= (ci % C)
    eye = (ri == ci).astype(F32)
    zero_bf = jnp.zeros((2 * C, LANES), BF16)

    def stack(t):
        return jnp.concatenate([jnp.where(head0, t, 0.0), jnp.where(head0, 0.0, t)],
                               axis=0).astype(BF16)

    def fold(t):
        return t[:C] + t[C:]

    n_pairs = d_r // LANES
    folds = [(q, p) for q in range(n_chunks) for p in range(n_pairs)]
    each = lambda fn, *cols: [fn(*args) for args in zip(*cols)]
    cat0 = lambda *ts: jnp.concatenate(ts, axis=0)
    cat1 = lambda *ts: jnp.concatenate(ts, axis=1)
    left = lambda t: t[:, :2 * C]
    right = lambda t: t[:, 2 * C:]

    p_ends, b_es, k_es = [], [], []
    for q in range(n_chunks):
        rs = slice(q * C, (q + 1) * C)
        c_end = c[rs][C - 1:C, :]
        e_end = jnp.exp(c_end - c[rs])
        b_es.append(bb[rs] * e_end)
        k_es.append(k[rs] * e_end)
        p_ends.append(jnp.exp(c_end))

    def stacks(t_of_q):
        return [stack(t_of_q(q)[:, p * LANES:(p + 1) * LANES]) for q, p in folds]

    chunk = lambda t: (lambda q: t[q * C:(q + 1) * C])
    am, bm, km, rm, vm = (stacks(chunk(t)) for t in (a_t, b_t, k_t, r_t, v))
    bem = stacks(lambda q: b_es[q])
    kem = stacks(lambda q: k_es[q])

    x = each(lambda a_, r_, b_, k_: _dot_nt(cat0(a_, r_), cat0(b_, k_)), am, rm, bm, km)
    n = each(lambda t: jnp.where(strict, t[:2 * C, :2 * C], 0.0), x)
    a_ak = each(lambda t: jnp.where(strict, t[:2 * C, 2 * C:], 0.0).astype(BF16), x)
    a_rb = each(lambda t: jnp.where(incl, t[2 * C:, :2 * C], 0.0).astype(BF16), x)
    a_rk = each(lambda t: jnp.where(incl, t[2 * C:, 2 * C:], 0.0).astype(BF16), x)
    t_inv = each(lambda t: eye + t, n)
    m = each(lambda t: _dot(t, t), n)
    av = each(lambda a_, v_: _dot(a_, v_).astype(BF16), a_ak, vm)
    for _ in range(int(math.log2(C)) - 2):
        z = each(lambda m_, t_: _dot(m_, cat1(m_.astype(BF16), t_.astype(BF16))), m, t_inv)
        t_inv = each(lambda t_, z_: t_ + right(z_), t_inv, z)
        m = each(left, z)
    t_inv = each(lambda t_, m_: t_ + _dot(m_, t_), t_inv, m)
    z = each(lambda t_, a_, av_: _dot(t_, cat1(a_, av_)), t_inv, am, av)
    ap = each(lambda z_: left(z_).astype(BF16), z)
    u0 = each(lambda z_: right(z_).astype(BF16), z)
    z = each(lambda rb_, rk_, ap_, u_, v_: _dot(cat1(rb_, rk_),
                                               cat0(cat1(ap_, u_), cat1(zero_bf, v_))),
             a_rb, a_rk, ap, u0, vm)
    g_m = each(lambda ap_, be_: _dot_tn(ap_, be_).astype(BF16), ap, bem)
    d0 = each(lambda u_, v_, be_, ke_: _dot_tn(cat0(u_, v_), cat0(be_, ke_)), u0, vm, bem, kem)
    rp_t = each(lambda r_, z_: jnp.transpose(r_.astype(F32) + left(z_)).astype(BF16), rm, z)
    y0 = each(lambda z_: fold(right(z_)), z)

    s = [s_ref[p] for p in range(n_pairs)]
    y_cols = []
    for q in range(n_chunks):
        f0 = q * n_pairs
        zs = [_dot(s[p], cat1(g_m[f0 + p], rp_t[f0 + p])) for p in range(n_pairs)]
        y_cols.append([fold(jnp.transpose(right(zs[p]))) + y0[f0 + p] for p in range(n_pairs)])
        s = [s[p] * p_ends[q][:, p * LANES:(p + 1) * LANES] + left(zs[p]) + d0[f0 + p]
             for p in range(n_pairs)]
    for p in range(n_pairs):
        s_ref[p] = s[p]
    y = cat0(*[cat1(*cols) for cols in y_cols])

    inv = 1.0 / HEAD_DIM
    mu = head_sum(y) * inv
    yc = y - mu
    var = head_sum(yc * yc) * inv
    yn = yc * lax.rsqrt(var + LNX_EPS) * lng_ref[...] + lnb_ref[...]
    bonus = head_sum(r * k * rk_ref[...]) * v
    y_ref[0] = ((yn + bonus) * g_ref[0]).astype(y_ref.dtype)


def _stage_wkv(rkv, lw, ai, g, mu_rkv, k_k, k_a, r_k, lnx_g, lnx_b):
    B, S, d_r = lw.shape
    n_chunks = WKV_CHUNKS_PER_STEP
    rows = WKV_CHUNK * n_chunks
    head = np.arange(MXU_DEPTH) // HEAD_DIM
    seg = jnp.asarray(head[:, None] == head[None, :], BF16)
    t = np.arange(rows)
    tril = jnp.asarray((t[:, None] >= t[None, :])
                       & (t[:, None] // WKV_CHUNK == t[None, :] // WKV_CHUNK), BF16)
    const = lambda shape: pl.BlockSpec(shape, lambda b, j: (0,) * len(shape))
    tile = lambda c: pl.BlockSpec((1, rows, c), lambda b, j: (b, j, 0))
    row = lambda t: t.reshape(1, -1)
    return pl.pallas_call(
        functools.partial(_wkv_kernel, d_r=d_r, n_chunks=n_chunks),
        grid=(B, S // rows),
        in_specs=[tile(3 * d_r), tile(d_r), tile(d_r), tile(d_r), const((1, 3 * d_r)),
                  const((1, d_r)), const((1, d_r)), const((1, d_r)), const((1, d_r)),
                  const((1, d_r)), const((MXU_DEPTH, MXU_DEPTH)), const((rows, rows))],
        out_specs=tile(d_r),
        out_shape=jax.ShapeDtypeStruct((B, S, d_r), BF16),
        scratch_shapes=[pltpu.VMEM((d_r // LANES, LANES, LANES), F32),
                        pltpu.VMEM((8, 3 * d_r), F32)],
        compiler_params=_params(("parallel", "arbitrary")),
        name="wkv",
    )(rkv, lw, ai, g, row(mu_rkv), row(k_k), row(k_a), row(r_k), row(lnx_g), row(lnx_b), seg, tril)


def _t5_bucket(n):
    exact = N_BUCKETS // 2
    nf = np.maximum(n, 1).astype(np.float32)
    large = exact + (np.log(nf / exact) / math.log(MAX_DISTANCE / exact)
                     * (N_BUCKETS - exact)).astype(np.int32)
    large = np.minimum(large, N_BUCKETS - 1)
    return np.where(n < exact, n, large).astype(np.int32)


def _band_bias(rel_bias, dil):
    blk = ATT_BLK
    L = 3 * blk
    rel = (2 * blk - 1) - np.arange(L)
    valid = (rel >= 0) & (rel <= blk)
    base = rel_bias[_t5_bucket(np.clip(rel, 0, None) * dil)].astype(F32).T
    base = jnp.where(valid[None], base, NEG_INF)
    skew = jnp.tile(base, (1, blk))[:, :blk * (L - 1)].reshape(-1, blk, L - 1)
    return skew[:, :, blk - 1:3 * blk - 1]


def _attend(units, bias_ref, scale):
    lane = lax.broadcasted_iota(jnp.int32, (1, LANES), 1)
    head0 = lane < HEAD_DIM
    jobs = [(u, h) for u in range(len(units)) for h in range(2)]
    s = []
    for u, h in jobs:
        q, k, _, col0 = units[u]
        qh = jnp.where(head0 if h == 0 else jnp.logical_not(head0), q * scale, 0.0)
        s.append(_dot_nt(qh, k) + bias_ref[h, :, col0:col0 + k.shape[0]])
    m = [jnp.max(t, axis=-1, keepdims=True) for t in s]
    e = [jnp.exp2(t - m_) for t, m_ in zip(s, m)]
    ol = [_dot(t, jnp.where(head0 if h == 0 else jnp.logical_not(head0), units[u][2], 1.0))
          for t, (u, h) in zip(e, jobs)]
    both = lambda t, u: jnp.where(head0, t[2 * u], t[2 * u + 1])
    swapped = lambda u: jnp.where(head0, ol[2 * u + 1], ol[2 * u])
    return [(both(ol, u), both(m, u), pltpu.roll(swapped(u), HEAD_DIM, 1))
            for u in range(len(units))]


def _attn_kernel(q_ref, k_ref, v_ref, b1_ref, b4_ref, b16_ref, o_ref, acc_o, acc_m, acc_l, *, seq,
                 scale):
    blk = ATT_BLK
    group = ATT_UNITS

    def rows(ref, start, n, dil):
        if dil == 1:
            return ref[0, pl.ds(start, n), :]
        return ref[0, pl.ds(start, n, stride=dil), :]

    def unit(start, dil, has_prev):
        q = rows(q_ref, start, blk, dil)
        if has_prev:
            k0 = start - blk * dil
            return (q, rows(k_ref, k0, 2 * blk, dil), rows(v_ref, k0, 2 * blk, dil), 0)
        return (q, rows(k_ref, start, blk, dil), rows(v_ref, start, blk, dil), blk)

    def acc_rows(ref, start, dil):
        if dil == 1:
            return ref.at[pl.ds(start, blk), :]
        return ref.at[pl.ds(start, blk, stride=dil), :]

    def run(starts, dil, prevs, bias_ref, first, last):
        units = [unit(s, dil, hp) for s, hp in zip(starts, prevs)]
        for s, (o, m, l) in zip(starts, _attend(units, bias_ref, scale)):
            ao, am, al = (acc_rows(ref, s, dil) for ref in (acc_o, acc_m, acc_l))
            if first:
                ao[...] = o
                am[...] = m
                al[...] = l
            else:
                m_old = am[...]
                m_new = jnp.maximum(m_old, m)
                w_old, w_new = jnp.exp2(m_old - m_new), jnp.exp2(m - m_new)
                ao[...] = w_old * ao[...] + w_new * o
                al[...] = w_old * al[...] + w_new * l
                if not last:
                    am[...] = m_new

    run([n * blk for n in range(group)], 1, [n > 0 for n in range(group)], b1_ref, True, False)

    def body1(g, c):
        base = pl.multiple_of(g * (group * blk), group * blk)
        run([base + n * blk for n in range(group)], 1, [True] * group, b1_ref, True, False)
        return c
    if seq // (group * blk) > 1:
        lax.fori_loop(1, seq // (group * blk), body1, 0)

    nb4 = seq // 4 // blk

    per4 = group // nb4

    def body4(g, c):
        rs = [g * per4 + j for j in range(per4)]
        run([r + n * blk * 4 for r in rs for n in range(nb4)], 4,
            [n > 0 for r in rs for n in range(nb4)], b4_ref, False, False)
        return c
    lax.fori_loop(0, 4 // per4, body4, 0)

    def body16(g, c):
        run([g * group + j for j in range(group)], 16, [False] * group, b16_ref, False, True)
        return c
    lax.fori_loop(0, 16 // group, body16, 0)

    o_ref[0] = (acc_o[...] / acc_l[...]).astype(o_ref.dtype)


def _stage_attn(qkv, rel_bias):
    B, S, d3 = qkv.shape
    d_att = d3 // 3
    n_pairs = d_att // LANES
    assert DIL_PATTERNS == ((ATT_BLK, 1), (4 * ATT_BLK, 4), (16 * ATT_BLK, 16))
    assert S == 16 * ATT_BLK
    log2e = math.log2(math.e)
    biases = [_band_bias(rel_bias, dil) * log2e for _, dil in DIL_PATTERNS]
    part = lambda i: pl.BlockSpec((1, S, LANES), lambda b, p: (b, 0, i * n_pairs + p))
    bias_spec = pl.BlockSpec((2, ATT_BLK, 2 * ATT_BLK), lambda b, p: (p, 0, 0))
    return pl.pallas_call(
        functools.partial(_attn_kernel, seq=S, scale=HEAD_DIM ** -0.5 * log2e),
        grid=(B, n_pairs),
        in_specs=[part(0), part(1), part(2), bias_spec, bias_spec, bias_spec],
        out_specs=pl.BlockSpec((1, S, LANES), lambda b, p: (b, 0, p)),
        out_shape=jax.ShapeDtypeStruct((B, S, d_att), BF16),
        scratch_shapes=[pltpu.VMEM((S, LANES), F32)] * 3,
        compiler_params=_params(("parallel", "parallel")),
        name="attn",
    )(qkv, qkv, qkv, *biases)


def _mix_kernel(x_ref, yr_ref, att_ref, p_ref, wor_ref, woa_ref, g_ref, b_ref, rhi_ref, rlo_ref,
                rb_ref, pg_ref, pp_ref, x1_ref, x1t_ref, lg_ref, ple_ref, *, alpha):
    mix = _dot(yr_ref[...], wor_ref[...]) + _dot(att_ref[...], woa_ref[...])
    emb = _dot(p_ref[...], pp_ref[...])
    x1 = _layer_norm(alpha * x_ref[...] + mix, g_ref[...], b_ref[...])
    x1_ref[...] = x1
    _store_token_tiles(x1t_ref, x1)
    hi, lo = _split(x1)
    logits = (jnp.dot(hi, rhi_ref[...], preferred_element_type=F32)
              + jnp.dot(lo, rhi_ref[...], preferred_element_type=F32)
              + jnp.dot(hi, rlo_ref[...], preferred_element_type=F32) + rb_ref[...])
    ple_ref[...] = (_sigmoid(_dot(hi, pg_ref[...])) * emb).astype(ple_ref.dtype)

    lane = lax.broadcasted_iota(jnp.int32, logits.shape, 1).astype(F32)
    big = float(ROUTER_LANES)
    rmax = lambda t: jnp.max(t, axis=-1, keepdims=True)
    first = lambda hit: jnp.min(jnp.where(hit, lane, big), axis=-1, keepdims=True)
    off = float("-inf")
    is_grp = lane < N_GROUPS
    lg = jnp.where(is_grp, logits, off)
    mg = rmax(lg)
    wg = 1.0 / jnp.sum(jnp.where(is_grp, jnp.exp(lg - mg), 0.0), axis=-1, keepdims=True)
    lo_lane = N_GROUPS + EXPERTS_PER_GROUP * first(lg == mg)
    le = jnp.where((lane >= lo_lane) & (lane < lo_lane + EXPERTS_PER_GROUP), logits, off)
    v1 = rmax(le)
    i1 = first(le == v1)
    le = jnp.where(lane == i1, off, le)
    v2 = rmax(le)
    i2 = first(le == v2)
    e2 = jnp.exp(v2 - v1)
    w1 = wg / (1.0 + e2)
    route = jnp.where(lane == 0, w1, jnp.where(lane == 1, w1 * e2, jnp.where(
        lane == 2, i1 - N_GROUPS, jnp.where(lane == 3, i2 - N_GROUPS, 0.0))))
    lg_ref[...] = route


def _stage_mix(x2, yr, att, p2, w_o, ln_g, ln_b, router_g, router_g_b, router_e, router_e_b,
               ple_gate, ple_proj, alpha):
    N, D = x2.shape
    d_r = yr.shape[1]
    rows = min(MIX_ROWS, N)
    d_att = att.shape[1]
    n_log = N_GROUPS + N_EXPERTS
    rw = jnp.zeros((D, ROUTER_LANES), F32).at[:, :N_GROUPS].set(router_g)
    rw = rw.at[:, N_GROUPS:n_log].set(router_e)
    rb = jnp.zeros((1, ROUTER_LANES), F32).at[0, :N_GROUPS].set(router_g_b)
    rb = rb.at[0, N_GROUPS:n_log].set(router_e_b)
    rhi, rlo = _split(rw)
    const = lambda shape: pl.BlockSpec(shape, lambda i: (0,) * len(shape))
    tile = lambda c: pl.BlockSpec((rows, c), lambda i: (i, 0))
    return pl.pallas_call(
        functools.partial(_mix_kernel, alpha=alpha),
        grid=(N // rows,),
        in_specs=[tile(D), tile(d_r), tile(d_att), tile(p2.shape[1]),
                  const((d_r, D)), const((d_att, D)), const((1, D)), const((1, D)),
                  const((D, ROUTER_LANES)), const((D, ROUTER_LANES)), const((1, ROUTER_LANES)),
                  const((D, D)), const(ple_proj.shape)],
        out_specs=[tile(D), pl.BlockSpec((rows * SUBLANES, LANES), lambda i: (i, 0)),
                   tile(ROUTER_LANES), tile(D)],
        out_shape=[jax.ShapeDtypeStruct((N, D), F32),
                   jax.ShapeDtypeStruct((N * SUBLANES, LANES), F32),
                   jax.ShapeDtypeStruct((N, ROUTER_LANES), F32),
                   jax.ShapeDtypeStruct((N, D), BF16)],
        compiler_params=_params(("parallel",)),
        name="mix",
    )(x2, yr, att, p2, w_o[:d_r].astype(BF16), w_o[d_r:].astype(BF16),
      ln_g.reshape(1, D), ln_b.reshape(1, D), rhi, rlo, rb, ple_gate.astype(BF16),
      ple_proj.astype(BF16))


def _route(route, rows):
    N = route.shape[0]
    we = route[:, :TOP_K]
    eid = route[:, TOP_K:2 * TOP_K].astype(jnp.int32)
    A = N * TOP_K
    e_flat = eid.T.reshape(A).astype(jnp.int32)
    order = jnp.argsort(e_flat).astype(jnp.int32)
    counts = jnp.sum((e_flat[:, None] == jnp.arange(N_EXPERTS)[None, :]).astype(jnp.int32), axis=0)
    start = jnp.cumsum(counts) - counts
    padded = (counts + rows - 1) // rows * rows
    pend = jnp.cumsum(padded)
    pstart = pend - padded
    nblk = -(-A // rows) + N_EXPERTS
    nused = (pend[-1] // rows).astype(jnp.int32)
    blk = jnp.arange(nblk, dtype=jnp.int32)
    blk_e = jnp.sum((pend[None, :] <= (blk * rows)[:, None]).astype(jnp.int32), axis=1)
    blk_e = jnp.minimum(blk_e, N_EXPERTS - 1)
    blk_e = jnp.where(blk < nused, blk_e, blk_e[nused - 1]).astype(jnp.int32)
    r = jnp.arange(rows, dtype=jnp.int32)[None, :]
    off = (blk * rows - pstart[blk_e])[:, None] + r
    real = (off < counts[blk_e][:, None]) & (blk < nused)[:, None]
    asg = order[jnp.clip(start[blk_e][:, None] + off, 0, A - 1)]
    row_tok = jnp.where(real, asg % N, r).astype(jnp.int32)
    row_dst = jnp.where(real, asg, A + (blk % 2)[:, None] * rows + r).astype(jnp.int32)
    return (row_tok.reshape(nblk, 1, rows), row_dst.reshape(nblk, 1, rows), blk_e,
            nused.reshape(1), we)


def _moe_kernel(blk_e_ref, nused_ref, tok0_ref, tokn_ref, dstp_ref, x_hbm, wg_ref, wu_ref, wd_ref,
                y_hbm, xbuf, ybuf, gsem, ssem, *, rows, n_real):
    i = pl.program_id(0)
    nused = nused_ref[0]
    slot = i % 2
    tile_rows = lambda first: pl.ds(pl.multiple_of(first, SUBLANES), SUBLANES)

    def gather_row(tab_ref, sl, r):
        return pltpu.make_async_copy(x_hbm.at[tile_rows(tab_ref[0, 0, r])],
                                     xbuf.at[sl, pl.ds(r * SUBLANES, SUBLANES)], gsem.at[sl])

    def scatter_row(sl, r):
        return pltpu.make_async_copy(ybuf.at[sl, pl.ds(r * SUBLANES, SUBLANES)],
                                     y_hbm.at[tile_rows(dstp_ref[0, 0, r])], ssem.at[sl])

    def gather_all(sl):
        return pltpu.make_async_copy(x_hbm.at[pl.ds(0, rows * SUBLANES)], xbuf.at[sl],
                                     gsem.at[sl])

    def scatter_all(sl, first_row=0):
        return pltpu.make_async_copy(
            ybuf.at[sl], y_hbm.at[pl.ds(first_row * SUBLANES, rows * SUBLANES)], ssem.at[sl])

    def rolled(fn):
        def body(r, c):
            fn(r)
            return c
        lax.fori_loop(0, rows, body, 0, unroll=8)

    def inline(fn):
        for r in range(rows):
            fn(r)

    def ffn(sl):
        xb = _load_token_tiles(xbuf.at[sl]).astype(BF16)
        gate = _dot(xb, wg_ref[0])
        up = _dot(xb, wu_ref[0])
        _store_token_tiles(ybuf.at[sl], _dot(gate * _sigmoid(gate) * up, wd_ref[0]))

    @pl.when(i == 0)
    def _():
        ybuf[...] = jnp.zeros_like(ybuf)
        for sl in range(2):
            scatter_all(sl, n_real + sl * rows).start()
        for sl in range(2):
            scatter_all(sl, n_real + sl * rows).wait()
        rolled(lambda r: gather_row(tok0_ref, 0, r).start())

    @pl.when(i < nused)
    def _():
        gather_all(slot).wait()

    @pl.when((i >= 2) & (i <= nused))
    def _():
        scatter_all(slot).wait()

    @pl.when(i == 0)
    def _():
        inline(lambda r: gather_row(tokn_ref, 1, r).start())
        ffn(0)

    for s in range(2):
        @pl.when((i >= 1) & (i < nused) & (slot == s))
        def _():
            inline(lambda r: gather_row(tokn_ref, 1 - s, r).start(priority=r % 2))
            inline(lambda r: scatter_row(1 - s, r).start(priority=r % 2))
            ffn(s)

    @pl.when(i == nused)
    def _():
        rolled(lambda r: scatter_row(1 - slot, r).start())
        scatter_all(1 - slot).wait()
        gather_all(slot).wait()


def _stage_moe(x1t, row_tok, row_dst, blk_e, nused, w_gate, w_up, w_down):
    N = x1t.shape[0] // SUBLANES
    D = SUBLANES * LANES
    nblk, _, rows = row_tok.shape
    d_e = w_gate.shape[2]
    n_real = N * TOP_K
    last = nblk - 1
    wspec = lambda shape: pl.BlockSpec((1,) + shape,
                                       lambda i, be, nu: (be[jnp.minimum(i, last)], 0, 0))
    table = lambda fn: pl.BlockSpec((1, 1, rows), lambda i, be, nu: (fn(i), 0, 0),
                                    memory_space=pltpu.SMEM)
    grid_spec = pltpu.PrefetchScalarGridSpec(
        num_scalar_prefetch=2,
        grid=(nblk + 1,),
        in_specs=[table(lambda i: 0), table(lambda i: jnp.minimum(i + 1, last)),
                  table(lambda i: jnp.clip(i - 1, 0, last)),
                  pl.BlockSpec(memory_space=pl.ANY), wspec((D, d_e)), wspec((D, d_e)),
                  wspec((d_e, D))],
        out_specs=pl.BlockSpec(memory_space=pl.ANY),
        scratch_shapes=[pltpu.VMEM((2, rows * SUBLANES, LANES), F32),
                        pltpu.VMEM((2, rows * SUBLANES, LANES), F32),
                        pltpu.SemaphoreType.DMA((2,)), pltpu.SemaphoreType.DMA((2,))],
    )
    return pl.pallas_call(
        functools.partial(_moe_kernel, rows=rows, n_real=n_real),
        grid_spec=grid_spec,
        out_shape=jax.ShapeDtypeStruct(((n_real + 2 * rows) * SUBLANES, LANES), F32),
        compiler_params=_params(("arbitrary",)),
        name="moe",
    )(blk_e, nused, row_tok * SUBLANES, row_tok * SUBLANES, row_dst * SUBLANES, x1t, w_gate, w_up,
      w_down)


def _out_kernel(x1_ref, y0_ref, y1_ref, we_ref, ple_ref, g_ref, b_ref, o_ref, *, alpha):
    we = we_ref[...]
    moe = we[:, 0:1] * _load_token_tiles(y0_ref) + we[:, 1:2] * _load_token_tiles(y1_ref)
    o_ref[...] = _layer_norm(alpha * x1_ref[...] + moe + ple_ref[...], g_ref[...], b_ref[...])


def _stage_out(x1, yb, we, ple, ln_g, ln_b, alpha):
    N, D = x1.shape
    rows = min(OUT_ROWS, N)
    const = lambda shape: pl.BlockSpec(shape, lambda i: (0,) * len(shape))
    tile = lambda c: pl.BlockSpec((rows, c), lambda i: (i, 0))
    slot = lambda j: pl.BlockSpec((rows * SUBLANES, LANES), lambda i: (i + j * (N // rows), 0))
    return pl.pallas_call(
        functools.partial(_out_kernel, alpha=alpha),
        grid=(N // rows,),
        in_specs=[tile(D), slot(0), slot(1), tile(TOP_K), tile(D), const((1, D)), const((1, D))],
        out_specs=tile(D),
        out_shape=jax.ShapeDtypeStruct((N, D), F32),
        compiler_params=_params(("parallel",)),
        name="out",
    )(x1, yb, yb, we, ple, ln_g.reshape(1, D), ln_b.reshape(1, D))


def kernel(x, p, w_in, mu_rkv, mu_lora, w0, w_lora1, w_lora2, a0, a_lora1, a_lora2, g_lora1, g_lora2, k_k, k_a, r_k, lnx_g, lnx_b, rel_bias, w_o, ln1_g, ln1_b, router_g, router_g_b, router_e, router_e_b, w_gate, w_up, w_down, ple_gate, ple_proj, ln2_g, ln2_b):
    B, S, D = x.shape
    depth = w_in.shape[0]
    alpha = (2 * depth) ** 0.25
    for i in range(depth):
        rkv, qkv, lw, ai, g = _stage_proj(x, w_in[i], w_lora1[i], a_lora1[i], g_lora1[i],
                                          mu_lora[i], w_lora2[i], a_lora2[i], g_lora2[i],
                                          w0[i], a0[i])
        yr = _stage_wkv(rkv, lw, ai, g, mu_rkv[i], k_k[i], k_a[i], r_k[i], lnx_g[i], lnx_b[i])
        att = _stage_attn(qkv, rel_bias).reshape(B * S, -1)
        x1, x1t, route, ple = _stage_mix(x.reshape(B * S, D), yr.reshape(B * S, -1), att,
                                          p[i].reshape(B * S, -1), w_o[i], ln1_g[i], ln1_b[i],
                                          router_g[i], router_g_b[i], router_e[i],
                                          router_e_b[i], ple_gate[i], ple_proj[i], alpha)
        row_tok, row_dst, blk_e, nused, we = _route(route, MOE_ROWS)
        yb = _stage_moe(x1t, row_tok, row_dst, blk_e, nused, w_gate[i], w_up[i], w_down[i])
        x = _stage_out(x1, yb, we, ple, ln2_g[i], ln2_b[i], alpha).reshape(B, S, D)
    return x
```

```python
import functools
import math

import jax
import jax.numpy as jnp
import numpy as np
from jax import lax
from jax.experimental import pallas as pl
from jax.experimental.pallas import tpu as pltpu

F32 = jnp.float32
BF16 = jnp.bfloat16

HEAD_DIM = 64
LANES = 128
SUBLANES = 8
MXU_DEPTH = 256
DECAY_LORA = 64
ICLR_LORA = 64
GATE_LORA = 128
LNX_EPS = 64e-5
DIL_PATTERNS = ((128, 1), (512, 4), (2048, 16))
NEG_INF = -1e30
N_BUCKETS = 32
MAX_DISTANCE = 2048
N_GROUPS = 4
EXPERTS_PER_GROUP = 8
N_EXPERTS = N_GROUPS * EXPERTS_PER_GROUP
TOP_K = 2
LN_EPS = 1e-5

WKV_CHUNK = 64
WKV_CHUNKS_PER_STEP = 8
ATT_BLK = 128
ATT_UNITS = 16
PROJ_ROWS = 512
MIX_ROWS = 512
MOE_ROWS = 256
OUT_ROWS = 1024
ROUTER_LANES = 128
VMEM_LIMIT = 56 * 1024 * 1024


def _params(sem):
    return pltpu.CompilerParams(dimension_semantics=sem, vmem_limit_bytes=VMEM_LIMIT)


def _dot(a, b):
    return jnp.dot(a.astype(BF16), b.astype(BF16), preferred_element_type=F32)


def _dot_nt(a, b):
    return lax.dot_general(a.astype(BF16), b.astype(BF16), (((1,), (1,)), ((), ())),
                           preferred_element_type=F32)


def _dot_tn(a, b):
    return lax.dot_general(a.astype(BF16), b.astype(BF16), (((0,), (0,)), ((), ())),
                           preferred_element_type=F32)


def _split(x):
    hi = x.astype(BF16)
    lo = (x - hi.astype(F32)).astype(BF16)
    return hi, lo


def _dot_exact_lhs(w01, x):
    hi, lo = _split(x)
    return (jnp.dot(w01, hi, preferred_element_type=F32)
            + jnp.dot(w01, lo, preferred_element_type=F32))


def _sigmoid(x):
    return 1.0 / (1.0 + jnp.exp(-x))


def _shift_rows(t, prev_row):
    rolled = pltpu.roll(t, 1, 0)
    row = lax.broadcasted_iota(jnp.int32, t.shape, 0)
    return jnp.where(row == 0, prev_row, rolled)


def _store_token_tiles(ref, t):
    rows = t.shape[0]
    for s in range(SUBLANES):
        ref[pl.ds(s, rows, stride=SUBLANES), :] = t[:, s * LANES:(s + 1) * LANES]


def _load_token_tiles(ref):
    rows = ref.shape[0] // SUBLANES
    return jnp.concatenate([ref[pl.ds(s, rows, stride=SUBLANES), :] for s in range(SUBLANES)],
                           axis=1)


def _layer_norm(z, g, b):
    mu = jnp.mean(z, axis=-1, keepdims=True)
    zc = z - mu
    var = jnp.mean(zc * zc, axis=-1, keepdims=True)
    return zc * lax.rsqrt(var + LN_EPS) * g + b


def _proj_kernel(x_ref, win_ref, w1_ref, a1_ref, g1_ref, mul_ref, w2_ref, a2_ref, g2_ref,
                 w0_ref, a0_ref, rkv_ref, qkv_ref, lw_ref, ai_ref, g_ref, prev_ref, *, d_rkv):
    @pl.when(pl.program_id(1) == 0)
    def _():
        prev_ref[...] = jnp.zeros_like(prev_ref)

    h = x_ref[0]
    rows = h.shape[0]
    hprev = _shift_rows(h, prev_ref[0:1, :])
    prev_ref[0:1, :] = h[rows - 1:rows, :]

    dh = hprev - h
    hb = h.astype(BF16)
    low_w = _dot(h + dh * mul_ref[0:1, :], w1_ref[...])
    low_a = _dot(h + dh * mul_ref[1:2, :], a1_ref[...])
    low_g = _dot(h + dh * mul_ref[2:3, :], g1_ref[...])
    rkv_ref[0] = _dot(hb, win_ref[:, :d_rkv])
    wl = w0_ref[...] + _dot(jnp.tanh(low_w), w2_ref[...])
    al = a0_ref[...] + _dot(low_a, a2_ref[...])
    g_ref[0] = _dot(_sigmoid(low_g), g2_ref[...]).astype(g_ref.dtype)
    qkv_ref[0] = _dot(hb, win_ref[:, d_rkv:])
    z = -wl
    softplus = jnp.maximum(z, 0.0) + jnp.log(1.0 + jnp.exp(-jnp.abs(z)))
    lw_ref[0] = -jnp.exp(-softplus - 0.5)
    ai_ref[0] = _sigmoid(al)


def _stage_proj(x, w_in, w1, a1, g1, mu_lora, w2, a2, g2, w0, a0):
    B, S, D = x.shape
    d_in = w_in.shape[1]
    d_r = w2.shape[1]
    d_rkv = 3 * d_r
    d_att = d_in - d_rkv
    rows = min(PROJ_ROWS, S)
    const = lambda shape: pl.BlockSpec(shape, lambda b, j: (0,) * len(shape))
    tile = lambda c: pl.BlockSpec((1, rows, c), lambda b, j: (b, j, 0))
    return pl.pallas_call(
        functools.partial(_proj_kernel, d_rkv=d_rkv),
        grid=(B, S // rows),
        in_specs=[tile(D), const((D, d_in)), const(w1.shape), const(a1.shape), const(g1.shape),
                  const(mu_lora.shape), const(w2.shape), const(a2.shape), const(g2.shape),
                  const((1, d_r)), const((1, d_r))],
        out_specs=[tile(d_rkv), tile(d_att), tile(d_r), tile(d_r), tile(d_r)],
        out_shape=[jax.ShapeDtypeStruct((B, S, d_rkv), F32),
                   jax.ShapeDtypeStruct((B, S, d_att), F32),
                   jax.ShapeDtypeStruct((B, S, d_r), F32),
                   jax.ShapeDtypeStruct((B, S, d_r), F32),
                   jax.ShapeDtypeStruct((B, S, d_r), BF16)],
        scratch_shapes=[pltpu.VMEM((8, D), F32)],
        compiler_params=_params(("parallel", "arbitrary")),
        name="proj",
    )(x, w_in.astype(BF16), w1.astype(BF16), a1.astype(BF16), g1.astype(BF16), mu_lora,
      w2.astype(BF16), a2.astype(BF16), g2.astype(BF16), w0.reshape(1, d_r), a0.reshape(1, d_r))


def _wkv_kernel(rkv_ref, lw_ref, ai_ref, g_ref, mu_ref, kk_ref, ka_ref, rk_ref, lng_ref, lnb_ref,
                seg_ref, tril_ref, y_ref, s_ref, prev_ref, *, d_r, n_chunks):
    C = WKV_CHUNK
    rows = C * n_chunks

    @pl.when(pl.program_id(1) == 0)
    def _():
        s_ref[...] = jnp.zeros_like(s_ref)
        prev_ref[...] = jnp.zeros_like(prev_ref)

    rkv = rkv_ref[0]
    prev = _shift_rows(rkv, prev_ref[0:1, :])
    prev_ref[0:1, :] = rkv[rows - 1:rows, :]
    mixed = rkv + (prev - rkv) * mu_ref[...]
    r = mixed[:, :d_r]
    k = mixed[:, d_r:2 * d_r]
    v = mixed[:, 2 * d_r:]
    lw = lw_ref[0]
    a = ai_ref[0]
    seg = seg_ref[...]
    seg_w = seg.shape[0]

    def head_sum(t):
        return jnp.concatenate([_dot(t[:, j:j + seg_w], seg) for j in range(0, d_r, seg_w)],
                               axis=1)

    kk = k * kk_ref[...]
    kk = kk * lax.rsqrt(jnp.maximum(head_sum(kk * kk), 1e-24))
    k = k * (1.0 + (a - 1.0) * ka_ref[...])
    aa = -kk
    bb = kk * a
    c = _dot_exact_lhs(tril_ref[...], lw)
    e_neg = jnp.exp(-c)
    a_t = aa * jnp.exp(c - lw)
    b_t = bb * e_neg
    k_t = k * e_neg
    r_t = r * jnp.exp(c)

    lane = lax.broadcasted_iota(jnp.int32, (1, LANES), 1)
    head0 = lane < HEAD_DIM
    ri = lax.broadcasted_iota(jnp.int32, (2 * C, 2 * C), 0)
    ci = lax.broadcasted_iota(jnp.int32, (2 * C, 2 * C), 1)
    strict = (ri % C) > (ci % C)
    incl = (ri % C) >= (ci % C)
    eye = (ri == ci).astype(F32)
    zero_bf = jnp.zeros((2 * C, LANES), BF16)

    def stack(t):
        return jnp.concatenate([jnp.where(head0, t, 0.0), jnp.where(head0, 0.0, t)],
                               axis=0).astype(BF16)

    def fold(t):
        return t[:C] + t[C:]

    n_pairs = d_r // LANES
    folds = [(q, p) for q in range(n_chunks) for p in range(n_pairs)]
    each = lambda fn, *cols: [fn(*args) for args in zip(*cols)]
    cat0 = lambda *ts: jnp.concatenate(ts, axis=0)
    cat1 = lambda *ts: jnp.concatenate(ts, axis=1)
    left = lambda t: t[:, :2 * C]
    right = lambda t: t[:, 2 * C:]

    p_ends, b_es, k_es = [], [], []
    for q in range(n_chunks):
        rs = slice(q * C, (q + 1) * C)
        c_end = c[rs][C - 1:C, :]
        e_end = jnp.exp(c_end - c[rs])
        b_es.append(bb[rs] * e_end)
        k_es.append(k[rs] * e_end)
        p_ends.append(jnp.exp(c_end))

    def stacks(t_of_q):
        return [stack(t_of_q(q)[:, p * LANES:(p + 1) * LANES]) for q, p in folds]

    chunk = lambda t: (lambda q: t[q * C:(q + 1) * C])
    am, bm, km, rm, vm = (stacks(chunk(t)) for t in (a_t, b_t, k_t, r_t, v))
    bem = stacks(lambda q: b_es[q])
    kem = stacks(lambda q: k_es[q])

    x = each(lambda a_, r_, b_, k_: _dot_nt(cat0(a_, r_), cat0(b_, k_)), am, rm, bm, km)
    n = each(lambda t: jnp.where(strict, t[:2 * C, :2 * C], 0.0), x)
    a_ak = each(lambda t: jnp.where(strict, t[:2 * C, 2 * C:], 0.0).astype(BF16), x)
    a_rb = each(lambda t: jnp.where(incl, t[2 * C:, :2 * C], 0.0).astype(BF16), x)
    a_rk = each(lambda t: jnp.where(incl, t[2 * C:, 2 * C:], 0.0).astype(BF16), x)
    t_inv = each(lambda t: eye + t, n)
    m = each(lambda t: _dot(t, t), n)
    av = each(lambda a_, v_: _dot(a_, v_).astype(BF16), a_ak, vm)
    for _ in range(int(math.log2(C)) - 2):
        z = each(lambda m_, t_: _dot(m_, cat1(m_.astype(BF16), t_.astype(BF16))), m, t_inv)
        t_inv = each(lambda t_, z_: t_ + right(z_), t_inv, z)
        m = each(left, z)
    t_inv = each(lambda t_, m_: t_ + _dot(m_, t_), t_inv, m)
    z = each(lambda t_, a_, av_: _dot(t_, cat1(a_, av_)), t_inv, am, av)
    ap = each(lambda z_: left(z_).astype(BF16), z)
    u0 = each(lambda z_: right(z_).astype(BF16), z)
    z = each(lambda rb_, rk_, ap_, u_, v_: _dot(cat1(rb_, rk_),
                                               cat0(cat1(ap_, u_), cat1(zero_bf, v_))),
             a_rb, a_rk, ap, u0, vm)
    g_m = each(lambda ap_, be_: _dot_tn(ap_, be_).astype(BF16), ap, bem)
    d0 = each(lambda u_, v_, be_, ke_: _dot_tn(cat0(u_, v_), cat0(be_, ke_)), u0, vm, bem, kem)
    rp_t = each(lambda r_, z_: jnp.transpose(r_.astype(F32) + left(z_)).astype(BF16), rm, z)
    y0 = each(lambda z_: fold(right(z_)), z)

    s = [s_ref[p] for p in range(n_pairs)]
    y_cols = []
    for q in range(n_chunks):
        f0 = q * n_pairs
        zs = [_dot(s[p], cat1(g_m[f0 + p], rp_t[f0 + p])) for p in range(n_pairs)]
        y_cols.append([fold(jnp.transpose(right(zs[p]))) + y0[f0 + p] for p in range(n_pairs)])
        s = [s[p] * p_ends[q][:, p * LANES:(p + 1) * LANES] + left(zs[p]) + d0[f0 + p]
             for p in range(n_pairs)]
    for p in range(n_pairs):
        s_ref[p] = s[p]
    y = cat0(*[cat1(*cols) for cols in y_cols])

    inv = 1.0 / HEAD_DIM
    mu = head_sum(y) * inv
    yc = y - mu
    var = head_sum(yc * yc) * inv
    yn = yc * lax.rsqrt(var + LNX_EPS) * lng_ref[...] + lnb_ref[...]
    bonus = head_sum(r * k * rk_ref[...]) * v
    y_ref[0] = ((yn + bonus) * g_ref[0]).astype(y_ref.dtype)


def _stage_wkv(rkv, lw, ai, g, mu_rkv, k_k, k_a, r_k, lnx_g, lnx_b):
    B, S, d_r = lw.shape
    n_chunks = WKV_CHUNKS_PER_STEP
    rows = WKV_CHUNK * n_chunks
    assert S % rows == 0 and d_r % MXU_DEPTH == 0
    head = np.arange(MXU_DEPTH) // HEAD_DIM
    seg = jnp.asarray(head[:, None] == head[None, :], BF16)
    t = np.arange(rows)
    tril = jnp.asarray((t[:, None] >= t[None, :])
                       & (t[:, None] // WKV_CHUNK == t[None, :] // WKV_CHUNK), BF16)
    const = lambda shape: pl.BlockSpec(shape, lambda b, j: (0,) * len(shape))
    tile = lambda c: pl.BlockSpec((1, rows, c), lambda b, j: (b, j, 0))
    row = lambda t: t.reshape(1, -1)
    return pl.pallas_call(
        functools.partial(_wkv_kernel, d_r=d_r, n_chunks=n_chunks),
        grid=(B, S // rows),
        in_specs=[tile(3 * d_r), tile(d_r), tile(d_r), tile(d_r), const((1, 3 * d_r)),
                  const((1, d_r)), const((1, d_r)), const((1, d_r)), const((1, d_r)),
                  const((1, d_r)), const((MXU_DEPTH, MXU_DEPTH)), const((rows, rows))],
        out_specs=tile(d_r),
        out_shape=jax.ShapeDtypeStruct((B, S, d_r), BF16),
        scratch_shapes=[pltpu.VMEM((d_r // LANES, LANES, LANES), F32),
                        pltpu.VMEM((8, 3 * d_r), F32)],
        compiler_params=_params(("parallel", "arbitrary")),
        name="wkv",
    )(rkv, lw, ai, g, row(mu_rkv), row(k_k), row(k_a), row(r_k), row(lnx_g), row(lnx_b), seg, tril)


def _t5_bucket(n):
    exact = N_BUCKETS // 2
    nf = np.maximum(n, 1).astype(np.float32)
    large = exact + (np.log(nf / exact) / math.log(MAX_DISTANCE / exact)
                     * (N_BUCKETS - exact)).astype(np.int32)
    large = np.minimum(large, N_BUCKETS - 1)
    return np.where(n < exact, n, large).astype(np.int32)


def _band_bias(rel_bias, dil):
    blk = ATT_BLK
    L = 3 * blk
    rel = (2 * blk - 1) - np.arange(L)
    valid = (rel >= 0) & (rel <= blk)
    base = rel_bias[_t5_bucket(np.clip(rel, 0, None) * dil)].astype(F32).T
    base = jnp.where(valid[None], base, NEG_INF)
    skew = jnp.tile(base, (1, blk))[:, :blk * (L - 1)].reshape(-1, blk, L - 1)
    return skew[:, :, blk - 1:3 * blk - 1]


def _attend(units, bias_ref, scale):
    lane = lax.broadcasted_iota(jnp.int32, (1, LANES), 1)
    head0 = lane < HEAD_DIM
    jobs = [(u, h) for u in range(len(units)) for h in range(2)]
    s = []
    for u, h in jobs:
        q, k, _, col0 = units[u]
        qh = jnp.where(head0 if h == 0 else jnp.logical_not(head0), q * scale, 0.0)
        s.append(_dot_nt(qh, k) + bias_ref[h, :, col0:col0 + k.shape[0]])
    m = [jnp.max(t, axis=-1, keepdims=True) for t in s]
    e = [jnp.exp2(t - m_) for t, m_ in zip(s, m)]
    ol = [_dot(t, jnp.where(head0 if h == 0 else jnp.logical_not(head0), units[u][2], 1.0))
          for t, (u, h) in zip(e, jobs)]
    both = lambda t, u: jnp.where(head0, t[2 * u], t[2 * u + 1])
    swapped = lambda u: jnp.where(head0, ol[2 * u + 1], ol[2 * u])
    return [(both(ol, u), both(m, u), pltpu.roll(swapped(u), HEAD_DIM, 1))
            for u in range(len(units))]


def _attn_kernel(q_ref, k_ref, v_ref, b1_ref, b4_ref, b16_ref, o_ref, acc_o, acc_m, acc_l, *, seq,
                 scale):
    blk = ATT_BLK
    group = ATT_UNITS

    def rows(ref, start, n, dil):
        if dil == 1:
            return ref[0, pl.ds(start, n), :]
        return ref[0, pl.ds(start, n, stride=dil), :]

    def unit(start, dil, has_prev):
        q = rows(q_ref, start, blk, dil)
        if has_prev:
            k0 = start - blk * dil
            return (q, rows(k_ref, k0, 2 * blk, dil), rows(v_ref, k0, 2 * blk, dil), 0)
        return (q, rows(k_ref, start, blk, dil), rows(v_ref, start, blk, dil), blk)

    def acc_rows(ref, start, dil):
        if dil == 1:
            return ref.at[pl.ds(start, blk), :]
        return ref.at[pl.ds(start, blk, stride=dil), :]

    def run(starts, dil, prevs, bias_ref, first, last):
        units = [unit(s, dil, hp) for s, hp in zip(starts, prevs)]
        for s, (o, m, l) in zip(starts, _attend(units, bias_ref, scale)):
            ao, am, al = (acc_rows(ref, s, dil) for ref in (acc_o, acc_m, acc_l))
            if first:
                ao[...] = o
                am[...] = m
                al[...] = l
            else:
                m_old = am[...]
                m_new = jnp.maximum(m_old, m)
                w_old, w_new = jnp.exp2(m_old - m_new), jnp.exp2(m - m_new)
                ao[...] = w_old * ao[...] + w_new * o
                al[...] = w_old * al[...] + w_new * l
                if not last:
                    am[...] = m_new

    run([n * blk for n in range(group)], 1, [n > 0 for n in range(group)], b1_ref, True, False)

    def body1(g, c):
        base = pl.multiple_of(g * (group * blk), group * blk)
        run([base + n * blk for n in range(group)], 1, [True] * group, b1_ref, True, False)
        return c
    if seq // (group * blk) > 1:
        lax.fori_loop(1, seq // (group * blk), body1, 0)

    nb4 = seq // 4 // blk

    per4 = group // nb4

    def body4(g, c):
        rs = [g * per4 + j for j in range(per4)]
        run([r + n * blk * 4 for r in rs for n in range(nb4)], 4,
            [n > 0 for r in rs for n in range(nb4)], b4_ref, False, False)
        return c
    lax.fori_loop(0, 4 // per4, body4, 0)

    def body16(g, c):
        run([g * group + j for j in range(group)], 16, [False] * group, b16_ref, False, True)
        return c
    lax.fori_loop(0, 16 // group, body16, 0)

    o_ref[0] = (acc_o[...] / acc_l[...]).astype(o_ref.dtype)


def _stage_attn(qkv, rel_bias):
    B, S, d3 = qkv.shape
    d_att = d3 // 3
    n_pairs = d_att // LANES
    assert DIL_PATTERNS == ((ATT_BLK, 1), (4 * ATT_BLK, 4), (16 * ATT_BLK, 16))
    assert S == 16 * ATT_BLK
    log2e = math.log2(math.e)
    biases = [_band_bias(rel_bias, dil) * log2e for _, dil in DIL_PATTERNS]
    part = lambda i: pl.BlockSpec((1, S, LANES), lambda b, p: (b, 0, i * n_pairs + p))
    bias_spec = pl.BlockSpec((2, ATT_BLK, 2 * ATT_BLK), lambda b, p: (p, 0, 0))
    return pl.pallas_call(
        functools.partial(_attn_kernel, seq=S, scale=HEAD_DIM ** -0.5 * log2e),
        grid=(B, n_pairs),
        in_specs=[part(0), part(1), part(2), bias_spec, bias_spec, bias_spec],
        out_specs=pl.BlockSpec((1, S, LANES), lambda b, p: (b, 0, p)),
        out_shape=jax.ShapeDtypeStruct((B, S, d_att), BF16),
        scratch_shapes=[pltpu.VMEM((S, LANES), F32)] * 3,
        compiler_params=_params(("parallel", "parallel")),
        name="attn",
    )(qkv, qkv, qkv, *biases)


def _mix_kernel(x_ref, yr_ref, att_ref, p_ref, wor_ref, woa_ref, g_ref, b_ref, rhi_ref, rlo_ref,
                rb_ref, pg_ref, pp_ref, x1_ref, x1t_ref, lg_ref, ple_ref, *, alpha):
    mix = _dot(yr_ref[...], wor_ref[...]) + _dot(att_ref[...], woa_ref[...])
    emb = _dot(p_ref[...], pp_ref[...])
    x1 = _layer_norm(alpha * x_ref[...] + mix, g_ref[...], b_ref[...])
    x1_ref[...] = x1
    _store_token_tiles(x1t_ref, x1)
    hi, lo = _split(x1)
    logits = (jnp.dot(hi, rhi_ref[...], preferred_element_type=F32)
              + jnp.dot(lo, rhi_ref[...], preferred_element_type=F32)
              + jnp.dot(hi, rlo_ref[...], preferred_element_type=F32) + rb_ref[...])
    ple_ref[...] = (_sigmoid(_dot(hi, pg_ref[...])) * emb).astype(ple_ref.dtype)

    lane = lax.broadcasted_iota(jnp.int32, logits.shape, 1).astype(F32)
    big = float(ROUTER_LANES)
    rmax = lambda t: jnp.max(t, axis=-1, keepdims=True)
    first = lambda hit: jnp.min(jnp.where(hit, lane, big), axis=-1, keepdims=True)
    off = float("-inf")
    is_grp = lane < N_GROUPS
    lg = jnp.where(is_grp, logits, off)
    mg = rmax(lg)
    wg = 1.0 / jnp.sum(jnp.where(is_grp, jnp.exp(lg - mg), 0.0), axis=-1, keepdims=True)
    lo_lane = N_GROUPS + EXPERTS_PER_GROUP * first(lg == mg)
    le = jnp.where((lane >= lo_lane) & (lane < lo_lane + EXPERTS_PER_GROUP), logits, off)
    v1 = rmax(le)
    i1 = first(le == v1)
    le = jnp.where(lane == i1, off, le)
    v2 = rmax(le)
    i2 = first(le == v2)
    e2 = jnp.exp(v2 - v1)
    w1 = wg / (1.0 + e2)
    route = jnp.where(lane == 0, w1, jnp.where(lane == 1, w1 * e2, jnp.where(
        lane == 2, i1 - N_GROUPS, jnp.where(lane == 3, i2 - N_GROUPS, 0.0))))
    lg_ref[...] = route


def _stage_mix(x2, yr, att, p2, w_o, ln_g, ln_b, router_g, router_g_b, router_e, router_e_b,
               ple_gate, ple_proj, alpha):
    N, D = x2.shape
    d_r = yr.shape[1]
    rows = min(MIX_ROWS, N)
    d_att = att.shape[1]
    n_log = N_GROUPS + N_EXPERTS
    rw = jnp.zeros((D, ROUTER_LANES), F32).at[:, :N_GROUPS].set(router_g)
    rw = rw.at[:, N_GROUPS:n_log].set(router_e)
    rb = jnp.zeros((1, ROUTER_LANES), F32).at[0, :N_GROUPS].set(router_g_b)
    rb = rb.at[0, N_GROUPS:n_log].set(router_e_b)
    rhi, rlo = _split(rw)
    const = lambda shape: pl.BlockSpec(shape, lambda i: (0,) * len(shape))
    tile = lambda c: pl.BlockSpec((rows, c), lambda i: (i, 0))
    return pl.pallas_call(
        functools.partial(_mix_kernel, alpha=alpha),
        grid=(N // rows,),
        in_specs=[tile(D), tile(d_r), tile(d_att), tile(p2.shape[1]),
                  const((d_r, D)), const((d_att, D)), const((1, D)), const((1, D)),
                  const((D, ROUTER_LANES)), const((D, ROUTER_LANES)), const((1, ROUTER_LANES)),
                  const((D, D)), const(ple_proj.shape)],
        out_specs=[tile(D), pl.BlockSpec((rows * SUBLANES, LANES), lambda i: (i, 0)),
                   tile(ROUTER_LANES), tile(D)],
        out_shape=[jax.ShapeDtypeStruct((N, D), F32),
                   jax.ShapeDtypeStruct((N * SUBLANES, LANES), F32),
                   jax.ShapeDtypeStruct((N, ROUTER_LANES), F32),
                   jax.ShapeDtypeStruct((N, D), BF16)],
        compiler_params=_params(("parallel",)),
        name="mix",
    )(x2, yr, att, p2, w_o[:d_r].astype(BF16), w_o[d_r:].astype(BF16),
      ln_g.reshape(1, D), ln_b.reshape(1, D), rhi, rlo, rb, ple_gate.astype(BF16),
      ple_proj.astype(BF16))


def _route(route, rows):
    N = route.shape[0]
    we = route[:, :TOP_K]
    eid = route[:, TOP_K:2 * TOP_K].astype(jnp.int32)
    A = N * TOP_K
    e_flat = eid.T.reshape(A).astype(jnp.int32)
    order = jnp.argsort(e_flat).astype(jnp.int32)
    counts = jnp.sum((e_flat[None, :] == jnp.arange(N_EXPERTS)[:, None]).astype(jnp.int32), axis=1)
    start = jnp.cumsum(counts) - counts
    padded = (counts + rows - 1) // rows * rows
    pend = jnp.cumsum(padded)
    pstart = pend - padded
    nblk = -(-A // rows) + N_EXPERTS
    nused = (pend[-1] // rows).astype(jnp.int32)
    blk = jnp.arange(nblk, dtype=jnp.int32)
    blk_e = jnp.sum((pend[None, :] <= (blk * rows)[:, None]).astype(jnp.int32), axis=1)
    blk_e = jnp.minimum(blk_e, N_EXPERTS - 1)
    blk_e = jnp.where(blk < nused, blk_e, blk_e[nused - 1]).astype(jnp.int32)
    r = jnp.arange(rows, dtype=jnp.int32)[None, :]
    off = (blk * rows - pstart[blk_e])[:, None] + r
    real = (off < counts[blk_e][:, None]) & (blk < nused)[:, None]
    asg = order[jnp.clip(start[blk_e][:, None] + off, 0, A - 1)]
    row_tok = jnp.where(real, asg % N, r).astype(jnp.int32)
    row_dst = jnp.where(real, asg, A + (blk % 2)[:, None] * rows + r).astype(jnp.int32)
    return (row_tok.reshape(nblk, 1, rows), row_dst.reshape(nblk, 1, rows), blk_e,
            nused.reshape(1), we)


def _moe_kernel(blk_e_ref, nused_ref, tok0_ref, tokn_ref, dstp_ref, x_hbm, wg_ref, wu_ref, wd_ref,
                y_hbm, xbuf, ybuf, gsem, ssem, *, rows, n_real):
    i = pl.program_id(0)
    nused = nused_ref[0]
    slot = i % 2
    tile_rows = lambda first: pl.ds(pl.multiple_of(first, SUBLANES), SUBLANES)

    def gather_row(tab_ref, sl, r):
        return pltpu.make_async_copy(x_hbm.at[tile_rows(tab_ref[0, 0, r])],
                                     xbuf.at[sl, pl.ds(r * SUBLANES, SUBLANES)], gsem.at[sl])

    def scatter_row(sl, r):
        return pltpu.make_async_copy(ybuf.at[sl, pl.ds(r * SUBLANES, SUBLANES)],
                                     y_hbm.at[tile_rows(dstp_ref[0, 0, r])], ssem.at[sl])

    def gather_all(sl):
        return pltpu.make_async_copy(x_hbm.at[pl.ds(0, rows * SUBLANES)], xbuf.at[sl],
                                     gsem.at[sl])

    def scatter_all(sl, first_row=0):
        return pltpu.make_async_copy(
            ybuf.at[sl], y_hbm.at[pl.ds(first_row * SUBLANES, rows * SUBLANES)], ssem.at[sl])

    def rolled(fn):
        def body(r, c):
            fn(r)
            return c
        lax.fori_loop(0, rows, body, 0, unroll=8)

    def inline(fn):
        for r in range(rows):
            fn(r)

    def ffn(sl):
        xb = _load_token_tiles(xbuf.at[sl]).astype(BF16)
        gate = _dot(xb, wg_ref[0])
        up = _dot(xb, wu_ref[0])
        _store_token_tiles(ybuf.at[sl], _dot(gate * _sigmoid(gate) * up, wd_ref[0]))

    @pl.when(i == 0)
    def _():
        ybuf[...] = jnp.zeros_like(ybuf)
        for sl in range(2):
            scatter_all(sl, n_real + sl * rows).start()
        for sl in range(2):
            scatter_all(sl, n_real + sl * rows).wait()
        rolled(lambda r: gather_row(tok0_ref, 0, r).start())

    @pl.when(i < nused)
    def _():
        gather_all(slot).wait()

    @pl.when((i >= 2) & (i <= nused))
    def _():
        scatter_all(slot).wait()

    @pl.when(i == 0)
    def _():
        inline(lambda r: gather_row(tokn_ref, 1, r).start())
        ffn(0)

    for s in range(2):
        @pl.when((i >= 1) & (i < nused) & (slot == s))
        def _():
            inline(lambda r: gather_row(tokn_ref, 1 - s, r).start(priority=r % 2))
            inline(lambda r: scatter_row(1 - s, r).start(priority=r % 2))
            ffn(s)

    @pl.when(i == nused)
    def _():
        rolled(lambda r: scatter_row(1 - slot, r).start())
        scatter_all(1 - slot).wait()
        gather_all(slot).wait()


def _stage_moe(x1t, row_tok, row_dst, blk_e, nused, w_gate, w_up, w_down):
    N = x1t.shape[0] // SUBLANES
    D = SUBLANES * LANES
    nblk, _, rows = row_tok.shape
    d_e = w_gate.shape[2]
    n_real = N * TOP_K
    last = nblk - 1
    wspec = lambda shape: pl.BlockSpec((1,) + shape,
                                       lambda i, be, nu: (be[jnp.minimum(i, last)], 0, 0))
    table = lambda fn: pl.BlockSpec((1, 1, rows), lambda i, be, nu: (fn(i), 0, 0),
                                    memory_space=pltpu.SMEM)
    grid_spec = pltpu.PrefetchScalarGridSpec(
        num_scalar_prefetch=2,
        grid=(nblk + 1,),
        in_specs=[table(lambda i: 0), table(lambda i: jnp.minimum(i + 1, last)),
                  table(lambda i: jnp.clip(i - 1, 0, last)),
                  pl.BlockSpec(memory_space=pl.ANY), wspec((D, d_e)), wspec((D, d_e)),
                  wspec((d_e, D))],
        out_specs=pl.BlockSpec(memory_space=pl.ANY),
        scratch_shapes=[pltpu.VMEM((2, rows * SUBLANES, LANES), F32),
                        pltpu.VMEM((2, rows * SUBLANES, LANES), F32),
                        pltpu.SemaphoreType.DMA((2,)), pltpu.SemaphoreType.DMA((2,))],
    )
    return pl.pallas_call(
        functools.partial(_moe_kernel, rows=rows, n_real=n_real),
        grid_spec=grid_spec,
        out_shape=jax.ShapeDtypeStruct(((n_real + 2 * rows) * SUBLANES, LANES), F32),
        compiler_params=_params(("arbitrary",)),
        name="moe",
    )(blk_e, nused, row_tok * SUBLANES, row_tok * SUBLANES, row_dst * SUBLANES, x1t, w_gate, w_up,
      w_down)


def _out_kernel(x1_ref, y0_ref, y1_ref, we_ref, ple_ref, g_ref, b_ref, o_ref, *, alpha):
    we = we_ref[...]
    moe = we[:, 0:1] * _load_token_tiles(y0_ref) + we[:, 1:2] * _load_token_tiles(y1_ref)
    o_ref[...] = _layer_norm(alpha * x1_ref[...] + moe + ple_ref[...], g_ref[...], b_ref[...])


def _stage_out(x1, yb, we, ple, ln_g, ln_b, alpha):
    N, D = x1.shape
    rows = min(OUT_ROWS, N)
    const = lambda shape: pl.BlockSpec(shape, lambda i: (0,) * len(shape))
    tile = lambda c: pl.BlockSpec((rows, c), lambda i: (i, 0))
    slot = lambda j: pl.BlockSpec((rows * SUBLANES, LANES), lambda i: (i + j * (N // rows), 0))
    return pl.pallas_call(
        functools.partial(_out_kernel, alpha=alpha),
        grid=(N // rows,),
        in_specs=[tile(D), slot(0), slot(1), tile(TOP_K), tile(D), const((1, D)), const((1, D))],
        out_specs=tile(D),
        out_shape=jax.ShapeDtypeStruct((N, D), F32),
        compiler_params=_params(("parallel",)),
        name="out",
    )(x1, yb, yb, we, ple, ln_g.reshape(1, D), ln_b.reshape(1, D))


def kernel(x, p, w_in, mu_rkv, mu_lora, w0, w_lora1, w_lora2, a0, a_lora1, a_lora2, g_lora1, g_lora2, k_k, k_a, r_k, lnx_g, lnx_b, rel_bias, w_o, ln1_g, ln1_b, router_g, router_g_b, router_e, router_e_b, w_gate, w_up, w_down, ple_gate, ple_proj, ln2_g, ln2_b):
    B, S, D = x.shape
    depth = w_in.shape[0]
    alpha = (2 * depth) ** 0.25
    for i in range(depth):
        rkv, qkv, lw, ai, g = _stage_proj(x, w_in[i], w_lora1[i], a_lora1[i], g_lora1[i],
                                          mu_lora[i], w_lora2[i], a_lora2[i], g_lora2[i],
                                          w0[i], a0[i])
        yr = _stage_wkv(rkv, lw, ai, g, mu_rkv[i], k_k[i], k_a[i], r_k[i], lnx_g[i], lnx_b[i])
        att = _stage_attn(qkv, rel_bias).reshape(B * S, -1)
        x1, x1t, route, ple = _stage_mix(x.reshape(B * S, D), yr.reshape(B * S, -1), att,
                                          p[i].reshape(B * S, -1), w_o[i], ln1_g[i], ln1_b[i],
                                          router_g[i], router_g_b[i], router_e[i],
                                          router_e_b[i], ple_gate[i], ple_proj[i], alpha)
        row_tok, row_dst, blk_e, nused, we = _route(route, MOE_ROWS)
        yb = _stage_moe(x1t, row_tok, row_dst, blk_e, nused, w_gate[i], w_up[i], w_down[i])
        x = _stage_out(x1, yb, we, ple, ln2_g[i], ln2_b[i], alpha).reshape(B, S, D)
    return x
```

```python
import functools
import math

import jax
import jax.numpy as jnp
import numpy as np
from jax import lax
from jax.experimental import pallas as pl
from jax.experimental.pallas import tpu as pltpu

F32 = jnp.float32
BF16 = jnp.bfloat16

HEAD_DIM = 64
LANES = 128
SUBLANES = 8
MXU_DEPTH = 256
LNX_EPS = 64e-5
DIL_PATTERNS = ((128, 1), (512, 4), (2048, 16))
NEG_INF = -1e30
N_BUCKETS = 32
MAX_DISTANCE = 2048
N_GROUPS = 4
EXPERTS_PER_GROUP = 8
N_EXPERTS = N_GROUPS * EXPERTS_PER_GROUP
TOP_K = 2
LN_EPS = 1e-5

WKV_CHUNK = 64
WKV_CHUNKS_PER_STEP = 4
ATT_BLK = 128
ATT_UNITS = 16
PROJ_ROWS = 512
MIX_ROWS = 512
MOE_ROWS = 256
OUT_ROWS = 1024
ROUTER_LANES = 128
VMEM_LIMIT = 56 * 1024 * 1024


def _params(sem):
    return pltpu.CompilerParams(dimension_semantics=sem, vmem_limit_bytes=VMEM_LIMIT)


def _dot(a, b):
    return jnp.dot(a.astype(BF16), b.astype(BF16), preferred_element_type=F32)


def _dot_nt(a, b):
    return lax.dot_general(a.astype(BF16), b.astype(BF16), (((1,), (1,)), ((), ())),
                           preferred_element_type=F32)


def _dot_tn(a, b):
    return lax.dot_general(a.astype(BF16), b.astype(BF16), (((0,), (0,)), ((), ())),
                           preferred_element_type=F32)


def _split(x):
    hi = x.astype(BF16)
    lo = (x - hi.astype(F32)).astype(BF16)
    return hi, lo


def _dot_exact_lhs(w01, x):
    hi, lo = _split(x)
    return (jnp.dot(w01, hi, preferred_element_type=F32)
            + jnp.dot(w01, lo, preferred_element_type=F32))


def _sigmoid(x):
    return 1.0 / (1.0 + jnp.exp(-x))


def _shift_rows(t, prev_row):
    rolled = pltpu.roll(t, 1, 0)
    row = lax.broadcasted_iota(jnp.int32, t.shape, 0)
    return jnp.where(row == 0, prev_row, rolled)


def _store_token_tiles(ref, t):
    rows = t.shape[0]
    for s in range(SUBLANES):
        ref[pl.ds(s, rows, stride=SUBLANES), :] = t[:, s * LANES:(s + 1) * LANES]


def _load_token_tiles(ref):
    rows = ref.shape[0] // SUBLANES
    return jnp.concatenate([ref[pl.ds(s, rows, stride=SUBLANES), :] for s in range(SUBLANES)],
                           axis=1)


def _layer_norm(z, g, b):
    mu = jnp.mean(z, axis=-1, keepdims=True)
    zc = z - mu
    var = jnp.mean(zc * zc, axis=-1, keepdims=True)
    return zc * lax.rsqrt(var + LN_EPS) * g + b


def _proj_kernel(x_ref, win_ref, w1_ref, a1_ref, g1_ref, mul_ref, w2_ref, a2_ref, g2_ref,
                 w0_ref, a0_ref, rkv_ref, qkv_ref, lw_ref, ai_ref, g_ref, prev_ref, *, d_rkv):
    @pl.when(pl.program_id(1) == 0)
    def _():
        prev_ref[...] = jnp.zeros_like(prev_ref)

    h = x_ref[0]
    rows = h.shape[0]
    hprev = _shift_rows(h, prev_ref[0:1, :])
    prev_ref[0:1, :] = h[rows - 1:rows, :]

    dh = hprev - h
    hb = h.astype(BF16)
    low_w = _dot(h + dh * mul_ref[0:1, :], w1_ref[...])
    low_a = _dot(h + dh * mul_ref[1:2, :], a1_ref[...])
    low_g = _dot(h + dh * mul_ref[2:3, :], g1_ref[...])
    rkv_ref[0] = _dot(hb, win_ref[:, :d_rkv])
    wl = w0_ref[...] + _dot(jnp.tanh(low_w), w2_ref[...])
    al = a0_ref[...] + _dot(low_a, a2_ref[...])
    g_ref[0] = _dot(_sigmoid(low_g), g2_ref[...]).astype(g_ref.dtype)
    qkv_ref[0] = _dot(hb, win_ref[:, d_rkv:])
    z = -wl
    softplus = jnp.maximum(z, 0.0) + jnp.log(1.0 + jnp.exp(-jnp.abs(z)))
    lw_ref[0] = -jnp.exp(-softplus - 0.5)
    ai_ref[0] = _sigmoid(al)


def _stage_proj(x, w_in, w1, a1, g1, mu_lora, w2, a2, g2, w0, a0):
    B, S, D = x.shape
    d_in = w_in.shape[1]
    d_r = w2.shape[1]
    d_rkv = 3 * d_r
    d_att = d_in - d_rkv
    rows = min(PROJ_ROWS, S)
    const = lambda shape: pl.BlockSpec(shape, lambda b, j: (0,) * len(shape))
    tile = lambda c: pl.BlockSpec((1, rows, c), lambda b, j: (b, j, 0))
    return pl.pallas_call(
        functools.partial(_proj_kernel, d_rkv=d_rkv),
        grid=(B, S // rows),
        in_specs=[tile(D), const((D, d_in)), const(w1.shape), const(a1.shape), const(g1.shape),
                  const(mu_lora.shape), const(w2.shape), const(a2.shape), const(g2.shape),
                  const((1, d_r)), const((1, d_r))],
        out_specs=[tile(d_rkv), tile(d_att), tile(d_r), tile(d_r), tile(d_r)],
        out_shape=[jax.ShapeDtypeStruct((B, S, d_rkv), F32),
                   jax.ShapeDtypeStruct((B, S, d_att), F32),
                   jax.ShapeDtypeStruct((B, S, d_r), F32),
                   jax.ShapeDtypeStruct((B, S, d_r), F32),
                   jax.ShapeDtypeStruct((B, S, d_r), BF16)],
        scratch_shapes=[pltpu.VMEM((SUBLANES, D), F32)],
        compiler_params=_params(("parallel", "arbitrary")),
        name="proj",
    )(x, w_in.astype(BF16), w1.astype(BF16), a1.astype(BF16), g1.astype(BF16), mu_lora,
      w2.astype(BF16), a2.astype(BF16), g2.astype(BF16), w0.reshape(1, d_r), a0.reshape(1, d_r))


def _wkv_kernel(rkv_ref, lw_ref, ai_ref, g_ref, mu_ref, kk_ref, ka_ref, rk_ref, lng_ref, lnb_ref,
                seg_ref, tril_ref, y_ref, s_ref, prev_ref, *, d_r, n_chunks):
    C = WKV_CHUNK
    rows = C * n_chunks

    @pl.when(pl.program_id(1) == 0)
    def _():
        s_ref[...] = jnp.zeros_like(s_ref)
        prev_ref[...] = jnp.zeros_like(prev_ref)

    rkv = rkv_ref[0]
    prev = _shift_rows(rkv, prev_ref[0:1, :])
    prev_ref[0:1, :] = rkv[rows - 1:rows, :]
    mixed = rkv + (prev - rkv) * mu_ref[...]
    r = mixed[:, :d_r]
    k = mixed[:, d_r:2 * d_r]
    v = mixed[:, 2 * d_r:]
    lw = lw_ref[0]
    a = ai_ref[0]
    seg = seg_ref[...]
    seg_w = seg.shape[0]

    def head_sum(t):
        return jnp.concatenate([_dot(t[:, j:j + seg_w], seg) for j in range(0, d_r, seg_w)],
                               axis=1)

    kk = k * kk_ref[...]
    kk = kk * lax.rsqrt(jnp.maximum(head_sum(kk * kk), 1e-24))
    k = k * (1.0 + (a - 1.0) * ka_ref[...])
    aa = -kk
    bb = kk * a
    c = _dot_exact_lhs(tril_ref[...], lw)
    e_neg = jnp.exp(-c)
    a_t = aa * jnp.exp(c - lw)
    b_t = bb * e_neg
    k_t = k * e_neg
    r_t = r * jnp.exp(c)

    lane = lax.broadcasted_iota(jnp.int32, (1, LANES), 1)
    head0 = lane < HEAD_DIM
    ri = lax.broadcasted_iota(jnp.int32, (2 * C, 2 * C), 0)
    ci = lax.broadcasted_iota(jnp.int32, (2 * C, 2 * C), 1)
    strict = (ri % C) > (ci % C)
    incl = (ri % C) >= (ci % C)
    eye = (ri == ci).astype(F32)
    zero_bf = jnp.zeros((2 * C, LANES), BF16)

    def stack(t):
        return jnp.concatenate([jnp.where(head0, t, 0.0), jnp.where(head0, 0.0, t)],
                               axis=0).astype(BF16)

    def fold(t):
        return t[:C] + t[C:]

    n_pairs = d_r // LANES
    folds = [(q, p) for q in range(n_chunks) for p in range(n_pairs)]
    each = lambda fn, *cols: [fn(*args) for args in zip(*cols)]
    cat0 = lambda *ts: jnp.concatenate(ts, axis=0)
    cat1 = lambda *ts: jnp.concatenate(ts, axis=1)
    left = lambda t: t[:, :2 * C]
    right = lambda t: t[:, 2 * C:]

    p_ends, b_es, k_es = [], [], []
    for q in range(n_chunks):
        rs = slice(q * C, (q + 1) * C)
        c_end = c[rs][C - 1:C, :]
        e_end = jnp.exp(c_end - c[rs])
        b_es.append(bb[rs] * e_end)
        k_es.append(k[rs] * e_end)
        p_ends.append(jnp.exp(c_end))

    def stacks(t_of_q):
        return [stack(t_of_q(q)[:, p * LANES:(p + 1) * LANES]) for q, p in folds]

    chunk = lambda t: (lambda q: t[q * C:(q + 1) * C])
    am, bm, km, rm, vm = (stacks(chunk(t)) for t in (a_t, b_t, k_t, r_t, v))
    bem = stacks(lambda q: b_es[q])
    kem = stacks(lambda q: k_es[q])

    x = each(lambda a_, r_, b_, k_: _dot_nt(cat0(a_, r_), cat0(b_, k_)), am, rm, bm, km)
    n = each(lambda t: jnp.where(strict, t[:2 * C, :2 * C], 0.0), x)
    a_ak = each(lambda t: jnp.where(strict, t[:2 * C, 2 * C:], 0.0).astype(BF16), x)
    a_rb = each(lambda t: jnp.where(incl, t[2 * C:, :2 * C], 0.0).astype(BF16), x)
    a_rk = each(lambda t: jnp.where(incl, t[2 * C:, 2 * C:], 0.0).astype(BF16), x)
    t_inv = each(lambda t: eye + t, n)
    m = each(lambda t: _dot(t, t), n)
    av = each(lambda a_, v_: _dot(a_, v_).astype(BF16), a_ak, vm)
    for _ in range(int(math.log2(C)) - 2):
        z = each(lambda m_, t_: _dot(m_, cat1(m_.astype(BF16), t_.astype(BF16))), m, t_inv)
        t_inv = each(lambda t_, z_: t_ + right(z_), t_inv, z)
        m = each(left, z)
    t_inv = each(lambda t_, m_: t_ + _dot(m_, t_), t_inv, m)
    z = each(lambda t_, a_, av_: _dot(t_, cat1(a_, av_)), t_inv, am, av)
    ap = each(lambda z_: left(z_).astype(BF16), z)
    u0 = each(lambda z_: right(z_).astype(BF16), z)
    z = each(lambda rb_, rk_, ap_, u_, v_: _dot(cat1(rb_, rk_),
                                               cat0(cat1(ap_, u_), cat1(zero_bf, v_))),
             a_rb, a_rk, ap, u0, vm)
    g_m = each(lambda ap_, be_: _dot_tn(ap_, be_).astype(BF16), ap, bem)
    d0 = each(lambda u_, v_, be_, ke_: _dot_tn(cat0(u_, v_), cat0(be_, ke_)), u0, vm, bem, kem)
    rp_t = each(lambda r_, z_: jnp.transpose(r_.astype(F32) + left(z_)).astype(BF16), rm, z)
    y0 = each(lambda z_: fold(right(z_)), z)

    s = [s_ref[p] for p in range(n_pairs)]
    y_cols = []
    for q in range(n_chunks):
        f0 = q * n_pairs
        zs = [_dot(s[p], cat1(g_m[f0 + p], rp_t[f0 + p])) for p in range(n_pairs)]
        y_cols.append([fold(jnp.transpose(right(zs[p]))) + y0[f0 + p] for p in range(n_pairs)])
        s = [s[p] * p_ends[q][:, p * LANES:(p + 1) * LANES] + left(zs[p]) + d0[f0 + p]
             for p in range(n_pairs)]
    for p in range(n_pairs):
        s_ref[p] = s[p]
    y = cat0(*[cat1(*cols) for cols in y_cols])

    inv = 1.0 / HEAD_DIM
    mu = head_sum(y) * inv
    yc = y - mu
    var = head_sum(yc * yc) * inv
    yn = yc * lax.rsqrt(var + LNX_EPS) * lng_ref[...] + lnb_ref[...]
    bonus = head_sum(r * k * rk_ref[...]) * v
    y_ref[0] = ((yn + bonus) * g_ref[0]).astype(y_ref.dtype)


def _stage_wkv(rkv, lw, ai, g, mu_rkv, k_k, k_a, r_k, lnx_g, lnx_b):
    B, S, d_r = lw.shape
    n_chunks = WKV_CHUNKS_PER_STEP
    rows = WKV_CHUNK * n_chunks
    assert S % rows == 0 and d_r % MXU_DEPTH == 0
    head = np.arange(MXU_DEPTH) // HEAD_DIM
    seg = jnp.asarray(head[:, None] == head[None, :], BF16)
    t = np.arange(rows)
    tril = jnp.asarray((t[:, None] >= t[None, :])
                       & (t[:, None] // WKV_CHUNK == t[None, :] // WKV_CHUNK), BF16)
    const = lambda shape: pl.BlockSpec(shape, lambda b, j: (0,) * len(shape))
    tile = lambda c: pl.BlockSpec((1, rows, c), lambda b, j: (b, j, 0))
    row = lambda t: t.reshape(1, -1)
    return pl.pallas_call(
        functools.partial(_wkv_kernel, d_r=d_r, n_chunks=n_chunks),
        grid=(B, S // rows),
        in_specs=[tile(3 * d_r), tile(d_r), tile(d_r), tile(d_r), const((1, 3 * d_r)),
                  const((1, d_r)), const((1, d_r)), const((1, d_r)), const((1, d_r)),
                  const((1, d_r)), const((MXU_DEPTH, MXU_DEPTH)), const((rows, rows))],
        out_specs=tile(d_r),
        out_shape=jax.ShapeDtypeStruct((B, S, d_r), BF16),
        scratch_shapes=[pltpu.VMEM((d_r // LANES, LANES, LANES), F32),
                        pltpu.VMEM((SUBLANES, 3 * d_r), F32)],
        compiler_params=_params(("parallel", "arbitrary")),
        name="wkv",
    )(rkv, lw, ai, g, row(mu_rkv), row(k_k), row(k_a), row(r_k), row(lnx_g), row(lnx_b), seg, tril)


def _t5_bucket(n):
    exact = N_BUCKETS // 2
    nf = np.maximum(n, 1).astype(np.float32)
    large = exact + (np.log(nf / exact) / math.log(MAX_DISTANCE / exact)
                     * (N_BUCKETS - exact)).astype(np.int32)
    large = np.minimum(large, N_BUCKETS - 1)
    return np.where(n < exact, n, large).astype(np.int32)


def _band_bias(rel_bias, dil):
    blk = ATT_BLK
    L = 3 * blk
    rel = (2 * blk - 1) - np.arange(L)
    valid = (rel >= 0) & (rel <= blk)
    base = rel_bias[_t5_bucket(np.clip(rel, 0, None) * dil)].astype(F32).T
    base = jnp.where(valid[None], base, NEG_INF)
    skew = jnp.tile(base, (1, blk))[:, :blk * (L - 1)].reshape(-1, blk, L - 1)
    return skew[:, :, blk - 1:3 * blk - 1]


def _attend(units, bias_ref, scale):
    lane = lax.broadcasted_iota(jnp.int32, (1, LANES), 1)
    head0 = lane < HEAD_DIM
    jobs = [(u, h) for u in range(len(units)) for h in range(2)]
    s = []
    for u, h in jobs:
        q, k, _, col0 = units[u]
        qh = jnp.where(head0 if h == 0 else jnp.logical_not(head0), q * scale, 0.0)
        s.append(_dot_nt(qh, k) + bias_ref[h, :, col0:col0 + k.shape[0]])
    m = [jnp.max(t, axis=-1, keepdims=True) for t in s]
    e = [jnp.exp2(t - m_) for t, m_ in zip(s, m)]
    ol = [_dot(t, jnp.where(head0 if h == 0 else jnp.logical_not(head0), units[u][2], 1.0))
          for t, (u, h) in zip(e, jobs)]
    both = lambda t, u: jnp.where(head0, t[2 * u], t[2 * u + 1])
    swapped = lambda u: jnp.where(head0, ol[2 * u + 1], ol[2 * u])
    return [(both(ol, u), both(m, u), pltpu.roll(swapped(u), HEAD_DIM, 1))
            for u in range(len(units))]


def _attn_kernel(q_ref, k_ref, v_ref, b1_ref, b4_ref, b16_ref, o_ref, acc_o, acc_m, acc_l, *, seq,
                 scale):
    blk = ATT_BLK
    group = ATT_UNITS

    def rows(ref, start, n, dil):
        if dil == 1:
            return ref[0, pl.ds(start, n), :]
        return ref[0, pl.ds(start, n, stride=dil), :]

    def unit(start, dil, has_prev):
        q = rows(q_ref, start, blk, dil)
        if has_prev:
            k0 = start - blk * dil
            return (q, rows(k_ref, k0, 2 * blk, dil), rows(v_ref, k0, 2 * blk, dil), 0)
        return (q, rows(k_ref, start, blk, dil), rows(v_ref, start, blk, dil), blk)

    def acc_rows(ref, start, dil):
        if dil == 1:
            return ref.at[pl.ds(start, blk), :]
        return ref.at[pl.ds(start, blk, stride=dil), :]

    def run(starts, dil, prevs, bias_ref, first, last):
        units = [unit(s, dil, hp) for s, hp in zip(starts, prevs)]
        for s, (o, m, l) in zip(starts, _attend(units, bias_ref, scale)):
            ao, am, al = (acc_rows(ref, s, dil) for ref in (acc_o, acc_m, acc_l))
            if first:
                ao[...] = o
                am[...] = m
                al[...] = l
            else:
                m_old = am[...]
                m_new = jnp.maximum(m_old, m)
                w_old, w_new = jnp.exp2(m_old - m_new), jnp.exp2(m - m_new)
                ao[...] = w_old * ao[...] + w_new * o
                al[...] = w_old * al[...] + w_new * l
                if not last:
                    am[...] = m_new

    run([n * blk for n in range(group)], 1, [n > 0 for n in range(group)], b1_ref, True, False)

    def body1(g, c):
        base = pl.multiple_of(g * (group * blk), group * blk)
        run([base + n * blk for n in range(group)], 1, [True] * group, b1_ref, True, False)
        return c
    if seq // (group * blk) > 1:
        lax.fori_loop(1, seq // (group * blk), body1, 0)

    nb4 = seq // 4 // blk

    per4 = group // nb4

    def body4(g, c):
        rs = [g * per4 + j for j in range(per4)]
        run([r + n * blk * 4 for r in rs for n in range(nb4)], 4,
            [n > 0 for r in rs for n in range(nb4)], b4_ref, False, False)
        return c
    lax.fori_loop(0, 4 // per4, body4, 0)

    def body16(g, c):
        run([g * group + j for j in range(group)], 16, [False] * group, b16_ref, False, True)
        return c
    lax.fori_loop(0, 16 // group, body16, 0)

    o_ref[0] = (acc_o[...] / acc_l[...]).astype(o_ref.dtype)


def _stage_attn(qkv, rel_bias):
    B, S, d3 = qkv.shape
    d_att = d3 // 3
    n_pairs = d_att // LANES
    assert DIL_PATTERNS == ((ATT_BLK, 1), (4 * ATT_BLK, 4), (16 * ATT_BLK, 16))
    assert S == 16 * ATT_BLK
    log2e = math.log2(math.e)
    biases = [_band_bias(rel_bias, dil) * log2e for _, dil in DIL_PATTERNS]
    part = lambda i: pl.BlockSpec((1, S, LANES), lambda b, p: (b, 0, i * n_pairs + p))
    bias_spec = pl.BlockSpec((2, ATT_BLK, 2 * ATT_BLK), lambda b, p: (p, 0, 0))
    return pl.pallas_call(
        functools.partial(_attn_kernel, seq=S, scale=HEAD_DIM ** -0.5 * log2e),
        grid=(B, n_pairs),
        in_specs=[part(0), part(1), part(2), bias_spec, bias_spec, bias_spec],
        out_specs=pl.BlockSpec((1, S, LANES), lambda b, p: (b, 0, p)),
        out_shape=jax.ShapeDtypeStruct((B, S, d_att), BF16),
        scratch_shapes=[pltpu.VMEM((S, LANES), F32)] * 3,
        compiler_params=_params(("parallel", "parallel")),
        name="attn",
    )(qkv, qkv, qkv, *biases)


def _mix_kernel(x_ref, yr_ref, att_ref, p_ref, wor_ref, woa_ref, g_ref, b_ref, rhi_ref, rlo_ref,
                rb_ref, pg_ref, pp_ref, x1t_ref, lg_ref, ple_ref, *, alpha):
    mix = _dot(yr_ref[...], wor_ref[...]) + _dot(att_ref[...], woa_ref[...])
    emb = _dot(p_ref[...], pp_ref[...])
    x1 = _layer_norm(alpha * x_ref[...] + mix, g_ref[...], b_ref[...])
    _store_token_tiles(x1t_ref, x1)
    hi, lo = _split(x1)
    logits = (jnp.dot(hi, rhi_ref[...], preferred_element_type=F32)
              + jnp.dot(lo, rhi_ref[...], preferred_element_type=F32)
              + jnp.dot(hi, rlo_ref[...], preferred_element_type=F32) + rb_ref[...])
    ple_ref[...] = (_sigmoid(_dot(hi, pg_ref[...])) * emb).astype(ple_ref.dtype)

    lane = lax.broadcasted_iota(jnp.int32, logits.shape, 1).astype(F32)
    big = float(ROUTER_LANES)
    rmax = lambda t: jnp.max(t, axis=-1, keepdims=True)
    first = lambda hit: jnp.min(jnp.where(hit, lane, big), axis=-1, keepdims=True)
    off = float("-inf")
    is_grp = lane < N_GROUPS
    lg = jnp.where(is_grp, logits, off)
    mg = rmax(lg)
    wg = 1.0 / jnp.sum(jnp.where(is_grp, jnp.exp(lg - mg), 0.0), axis=-1, keepdims=True)
    lo_lane = N_GROUPS + EXPERTS_PER_GROUP * first(lg == mg)
    le = jnp.where((lane >= lo_lane) & (lane < lo_lane + EXPERTS_PER_GROUP), logits, off)
    v1 = rmax(le)
    i1 = first(le == v1)
    le = jnp.where(lane == i1, off, le)
    v2 = rmax(le)
    i2 = first(le == v2)
    e2 = jnp.exp(v2 - v1)
    w1 = wg / (1.0 + e2)
    route = jnp.where(lane == 0, w1, jnp.where(lane == 1, w1 * e2, jnp.where(
        lane == 2, i1 - N_GROUPS, jnp.where(lane == 3, i2 - N_GROUPS, 0.0))))
    lg_ref[...] = route


def _stage_mix(x2, yr, att, p2, w_o, ln_g, ln_b, router_g, router_g_b, router_e, router_e_b,
               ple_gate, ple_proj, alpha):
    N, D = x2.shape
    d_r = yr.shape[1]
    rows = min(MIX_ROWS, N)
    d_att = att.shape[1]
    n_log = N_GROUPS + N_EXPERTS
    rw = jnp.zeros((D, ROUTER_LANES), F32).at[:, :N_GROUPS].set(router_g)
    rw = rw.at[:, N_GROUPS:n_log].set(router_e)
    rb = jnp.zeros((1, ROUTER_LANES), F32).at[0, :N_GROUPS].set(router_g_b)
    rb = rb.at[0, N_GROUPS:n_log].set(router_e_b)
    rhi, rlo = _split(rw)
    const = lambda shape: pl.BlockSpec(shape, lambda i: (0,) * len(shape))
    tile = lambda c: pl.BlockSpec((rows, c), lambda i: (i, 0))
    return pl.pallas_call(
        functools.partial(_mix_kernel, alpha=alpha),
        grid=(N // rows,),
        in_specs=[tile(D), tile(d_r), tile(d_att), tile(p2.shape[1]),
                  const((d_r, D)), const((d_att, D)), const((1, D)), const((1, D)),
                  const((D, ROUTER_LANES)), const((D, ROUTER_LANES)), const((1, ROUTER_LANES)),
                  const((D, D)), const(ple_proj.shape)],
        out_specs=[pl.BlockSpec((rows * SUBLANES, LANES), lambda i: (i, 0)),
                   tile(ROUTER_LANES), tile(D)],
        out_shape=[jax.ShapeDtypeStruct((N * SUBLANES, LANES), F32),
                   jax.ShapeDtypeStruct((N, ROUTER_LANES), F32),
                   jax.ShapeDtypeStruct((N, D), BF16)],
        compiler_params=_params(("parallel",)),
        name="mix",
    )(x2, yr, att, p2, w_o[:d_r].astype(BF16), w_o[d_r:].astype(BF16),
      ln_g.reshape(1, D), ln_b.reshape(1, D), rhi, rlo, rb, ple_gate.astype(BF16),
      ple_proj.astype(BF16))


def _route(route, rows):
    N = route.shape[0]
    we = route[:, :TOP_K]
    eid = route[:, TOP_K:2 * TOP_K].astype(jnp.int32)
    A = N * TOP_K
    e_flat = eid.T.reshape(A).astype(jnp.int32)
    order = jnp.argsort(e_flat).astype(jnp.int32)
    counts = jnp.sum((e_flat[None, :] == jnp.arange(N_EXPERTS)[:, None]).astype(jnp.int32), axis=1)
    start = jnp.cumsum(counts) - counts
    padded = (counts + rows - 1) // rows * rows
    pend = jnp.cumsum(padded)
    pstart = pend - padded
    nblk = -(-A // rows) + N_EXPERTS
    nused = (pend[-1] // rows).astype(jnp.int32)
    blk = jnp.arange(nblk, dtype=jnp.int32)
    blk_e = jnp.sum((pend[None, :] <= (blk * rows)[:, None]).astype(jnp.int32), axis=1)
    blk_e = jnp.minimum(blk_e, N_EXPERTS - 1)
    blk_e = jnp.where(blk < nused, blk_e, blk_e[nused - 1]).astype(jnp.int32)
    r = jnp.arange(rows, dtype=jnp.int32)[None, :]
    off = (blk * rows - pstart[blk_e])[:, None] + r
    real = (off < counts[blk_e][:, None]) & (blk < nused)[:, None]
    asg = order[jnp.clip(start[blk_e][:, None] + off, 0, A - 1)]
    row_tok = jnp.where(real, asg % N, r).astype(jnp.int32)
    row_dst = jnp.where(real, asg, A + (blk % 2)[:, None] * rows + r).astype(jnp.int32)
    return (row_tok.reshape(nblk, 1, rows), row_dst.reshape(nblk, 1, rows), blk_e,
            nused.reshape(1), we)


def _moe_kernel(blk_e_ref, nused_ref, tok0_ref, tokn_ref, dstp_ref, x_hbm, wg_ref, wu_ref, wd_ref,
                y_hbm, xbuf, ybuf, gsem, ssem, *, rows, n_real):
    i = pl.program_id(0)
    nused = nused_ref[0]
    slot = i % 2
    tile_rows = lambda first: pl.ds(pl.multiple_of(first, SUBLANES), SUBLANES)

    def gather_row(tab_ref, sl, r):
        return pltpu.make_async_copy(x_hbm.at[tile_rows(tab_ref[0, 0, r])],
                                     xbuf.at[sl, pl.ds(r * SUBLANES, SUBLANES)], gsem.at[sl])

    def scatter_row(sl, r):
        return pltpu.make_async_copy(ybuf.at[sl, pl.ds(r * SUBLANES, SUBLANES)],
                                     y_hbm.at[tile_rows(dstp_ref[0, 0, r])], ssem.at[sl])

    def gather_all(sl):
        return pltpu.make_async_copy(x_hbm.at[pl.ds(0, rows * SUBLANES)], xbuf.at[sl],
                                     gsem.at[sl])

    def scatter_all(sl, first_row=0):
        return pltpu.make_async_copy(
            ybuf.at[sl], y_hbm.at[pl.ds(first_row * SUBLANES, rows * SUBLANES)], ssem.at[sl])

    def rolled(fn):
        def body(r, c):
            fn(r)
            return c
        lax.fori_loop(0, rows, body, 0, unroll=8)

    def inline(fn):
        for r in range(rows):
            fn(r)

    def ffn(sl):
        xb = _load_token_tiles(xbuf.at[sl]).astype(BF16)
        gate = _dot(xb, wg_ref[0])
        up = _dot(xb, wu_ref[0])
        _store_token_tiles(ybuf.at[sl], _dot(gate * _sigmoid(gate) * up, wd_ref[0]))

    @pl.when(i == 0)
    def _():
        ybuf[...] = jnp.zeros_like(ybuf)
        for sl in range(2):
            scatter_all(sl, n_real + sl * rows).start()
        for sl in range(2):
            scatter_all(sl, n_real + sl * rows).wait()
        rolled(lambda r: gather_row(tok0_ref, 0, r).start())

    @pl.when(i < nused)
    def _():
        gather_all(slot).wait()

    @pl.when((i >= 2) & (i <= nused))
    def _():
        scatter_all(slot).wait()

    @pl.when(i == 0)
    def _():
        inline(lambda r: gather_row(tokn_ref, 1, r).start())
        ffn(0)

    for s in range(2):
        @pl.when((i >= 1) & (i < nused) & (slot == s))
        def _():
            inline(lambda r: gather_row(tokn_ref, 1 - s, r).start(priority=r % 2))
            inline(lambda r: scatter_row(1 - s, r).start(priority=r % 2))
            ffn(s)

    @pl.when(i == nused)
    def _():
        rolled(lambda r: scatter_row(1 - slot, r).start())
        scatter_all(1 - slot).wait()
        gather_all(slot).wait()


def _stage_moe(x1t, row_tok, row_dst, blk_e, nused, w_gate, w_up, w_down):
    N = x1t.shape[0] // SUBLANES
    D = SUBLANES * LANES
    nblk, _, rows = row_tok.shape
    d_e = w_gate.shape[2]
    n_real = N * TOP_K
    last = nblk - 1
    wspec = lambda shape: pl.BlockSpec((1,) + shape,
                                       lambda i, be, nu: (be[jnp.minimum(i, last)], 0, 0))
    table = lambda fn: pl.BlockSpec((1, 1, rows), lambda i, be, nu: (fn(i), 0, 0),
                                    memory_space=pltpu.SMEM)
    grid_spec = pltpu.PrefetchScalarGridSpec(
        num_scalar_prefetch=2,
        grid=(nblk + 1,),
        in_specs=[table(lambda i: 0), table(lambda i: jnp.minimum(i + 1, last)),
                  table(lambda i: jnp.clip(i - 1, 0, last)),
                  pl.BlockSpec(memory_space=pl.ANY), wspec((D, d_e)), wspec((D, d_e)),
                  wspec((d_e, D))],
        out_specs=pl.BlockSpec(memory_space=pl.ANY),
        scratch_shapes=[pltpu.VMEM((2, rows * SUBLANES, LANES), F32),
                        pltpu.VMEM((2, rows * SUBLANES, LANES), F32),
                        pltpu.SemaphoreType.DMA((2,)), pltpu.SemaphoreType.DMA((2,))],
    )
    return pl.pallas_call(
        functools.partial(_moe_kernel, rows=rows, n_real=n_real),
        grid_spec=grid_spec,
        out_shape=jax.ShapeDtypeStruct(((n_real + 2 * rows) * SUBLANES, LANES), F32),
        compiler_params=_params(("arbitrary",)),
        name="moe",
    )(blk_e, nused, row_tok * SUBLANES, row_tok * SUBLANES, row_dst * SUBLANES, x1t, w_gate, w_up,
      w_down)


def _out_kernel(x1t_ref, y0_ref, y1_ref, we_ref, ple_ref, g_ref, b_ref, o_ref, *, alpha):
    we = we_ref[...]
    moe = we[:, 0:1] * _load_token_tiles(y0_ref) + we[:, 1:2] * _load_token_tiles(y1_ref)
    o_ref[...] = _layer_norm(alpha * _load_token_tiles(x1t_ref) + moe + ple_ref[...], g_ref[...],
                             b_ref[...])


def _stage_out(x1t, yb, we, ple, ln_g, ln_b, alpha):
    N = x1t.shape[0] // SUBLANES
    D = SUBLANES * LANES
    rows = min(OUT_ROWS, N)
    const = lambda shape: pl.BlockSpec(shape, lambda i: (0,) * len(shape))
    tile = lambda c: pl.BlockSpec((rows, c), lambda i: (i, 0))
    slot = lambda j: pl.BlockSpec((rows * SUBLANES, LANES), lambda i: (i + j * (N // rows), 0))
    return pl.pallas_call(
        functools.partial(_out_kernel, alpha=alpha),
        grid=(N // rows,),
        in_specs=[slot(0), slot(0), slot(1), tile(TOP_K), tile(D), const((1, D)), const((1, D))],
        out_specs=tile(D),
        out_shape=jax.ShapeDtypeStruct((N, D), F32),
        compiler_params=_params(("parallel",)),
        name="out",
    )(x1t, yb, yb, we, ple, ln_g.reshape(1, D), ln_b.reshape(1, D))


def kernel(x, p, w_in, mu_rkv, mu_lora, w0, w_lora1, w_lora2, a0, a_lora1, a_lora2, g_lora1, g_lora2, k_k, k_a, r_k, lnx_g, lnx_b, rel_bias, w_o, ln1_g, ln1_b, router_g, router_g_b, router_e, router_e_b, w_gate, w_up, w_down, ple_gate, ple_proj, ln2_g, ln2_b):
    B, S, D = x.shape
    depth = w_in.shape[0]
    alpha = (2 * depth) ** 0.25
    for i in range(depth):
        rkv, qkv, lw, ai, g = _stage_proj(x, w_in[i], w_lora1[i], a_lora1[i], g_lora1[i],
                                          mu_lora[i], w_lora2[i], a_lora2[i], g_lora2[i],
                                          w0[i], a0[i])
        yr = _stage_wkv(rkv, lw, ai, g, mu_rkv[i], k_k[i], k_a[i], r_k[i], lnx_g[i], lnx_b[i])
        att = _stage_attn(qkv, rel_bias).reshape(B * S, -1)
        x1t, route, ple = _stage_mix(x.reshape(B * S, D), yr.reshape(B * S, -1), att,
                                      p[i].reshape(B * S, -1), w_o[i], ln1_g[i], ln1_b[i],
                                      router_g[i], router_g_b[i], router_e[i], router_e_b[i],
                                      ple_gate[i], ple_proj[i], alpha)
        row_tok, row_dst, blk_e, nused, we = _route(route, MOE_ROWS)
        yb = _stage_moe(x1t, row_tok, row_dst, blk_e, nused, w_gate[i], w_up[i], w_down[i])
        x = _stage_out(x1t, yb, we, ple, ln2_g[i], ln2_b[i], alpha).reshape(B, S, D)
    return x
```

```python
import functools
import math

import jax
import jax.numpy as jnp
import numpy as np
from jax import lax
from jax.experimental import pallas as pl
from jax.experimental.pallas import tpu as pltpu

F32 = jnp.float32
BF16 = jnp.bfloat16

HEAD_DIM = 64
LANES = 128
SUBLANES = 8
MXU_DEPTH = 256
LNX_EPS = 64e-5
DIL_PATTERNS = ((128, 1), (512, 4), (2048, 16))
NEG_INF = -1e30
N_BUCKETS = 32
MAX_DISTANCE = 2048
N_GROUPS = 4
EXPERTS_PER_GROUP = 8
N_EXPERTS = N_GROUPS * EXPERTS_PER_GROUP
TOP_K = 2
LN_EPS = 1e-5

WKV_CHUNK = 64
WKV_CHUNKS_PER_STEP = 4
ATT_BLK = 128
ATT_UNITS = 16
PROJ_ROWS = 512
MIX_ROWS = 512
MOE_ROWS = 256
OUT_ROWS = 1024
ROUTER_LANES = 128
VMEM_LIMIT = 56 * 1024 * 1024


def _params(sem):
    return pltpu.CompilerParams(dimension_semantics=sem, vmem_limit_bytes=VMEM_LIMIT)


def _dot(a, b):
    return jnp.dot(a.astype(BF16), b.astype(BF16), preferred_element_type=F32)


def _dot_nt(a, b):
    return lax.dot_general(a.astype(BF16), b.astype(BF16), (((1,), (1,)), ((), ())),
                           preferred_element_type=F32)


def _dot_tn(a, b):
    return lax.dot_general(a.astype(BF16), b.astype(BF16), (((0,), (0,)), ((), ())),
                           preferred_element_type=F32)


def _split(x):
    hi = x.astype(BF16)
    lo = (x - hi.astype(F32)).astype(BF16)
    return hi, lo


def _dot_exact_lhs(w01, x):
    hi, lo = _split(x)
    return (jnp.dot(w01, hi, preferred_element_type=F32)
            + jnp.dot(w01, lo, preferred_element_type=F32))


def _sigmoid(x):
    return 1.0 / (1.0 + jnp.exp(-x))


def _shift_rows(t, prev_row):
    rolled = pltpu.roll(t, 1, 0)
    row = lax.broadcasted_iota(jnp.int32, t.shape, 0)
    return jnp.where(row == 0, prev_row, rolled)


def _store_token_tiles(ref, t):
    rows = t.shape[0]
    for s in range(SUBLANES):
        ref[pl.ds(s, rows, stride=SUBLANES), :] = t[:, s * LANES:(s + 1) * LANES]


def _load_token_tiles(ref):
    rows = ref.shape[0] // SUBLANES
    return jnp.concatenate([ref[pl.ds(s, rows, stride=SUBLANES), :] for s in range(SUBLANES)],
                           axis=1)


def _layer_norm(z, g, b):
    mu = jnp.mean(z, axis=-1, keepdims=True)
    zc = z - mu
    var = jnp.mean(zc * zc, axis=-1, keepdims=True)
    return zc * lax.rsqrt(var + LN_EPS) * g + b


def _proj_kernel(x_ref, win_ref, w1_ref, a1_ref, g1_ref, mul_ref, w2_ref, a2_ref, g2_ref,
                 w0_ref, a0_ref, rkv_ref, qkv_ref, lw_ref, ai_ref, g_ref, prev_ref, *, d_rkv):
    @pl.when(pl.program_id(1) == 0)
    def _():
        prev_ref[...] = jnp.zeros_like(prev_ref)

    h = x_ref[0]
    rows = h.shape[0]
    hprev = _shift_rows(h, prev_ref[0:1, :])
    prev_ref[0:1, :] = h[rows - 1:rows, :]

    dh = hprev - h
    hb = h.astype(BF16)
    low_w = _dot(h + dh * mul_ref[0:1, :], w1_ref[...])
    low_a = _dot(h + dh * mul_ref[1:2, :], a1_ref[...])
    low_g = _dot(h + dh * mul_ref[2:3, :], g1_ref[...])
    rkv_ref[0] = _dot(hb, win_ref[:, :d_rkv])
    wl = w0_ref[...] + _dot(jnp.tanh(low_w), w2_ref[...])
    al = a0_ref[...] + _dot(low_a, a2_ref[...])
    g_ref[0] = _dot(_sigmoid(low_g), g2_ref[...]).astype(g_ref.dtype)
    qkv_ref[0] = _dot(hb, win_ref[:, d_rkv:])
    z = -wl
    softplus = jnp.maximum(z, 0.0) + jnp.log(1.0 + jnp.exp(-jnp.abs(z)))
    lw_ref[0] = -jnp.exp(-softplus - 0.5)
    ai_ref[0] = _sigmoid(al)


def _stage_proj(x, w_in, w1, a1, g1, mu_lora, w2, a2, g2, w0, a0):
    B, S, D = x.shape
    d_in = w_in.shape[1]
    d_r = w2.shape[1]
    d_rkv = 3 * d_r
    d_att = d_in - d_rkv
    rows = min(PROJ_ROWS, S)
    const = lambda shape: pl.BlockSpec(shape, lambda b, j: (0,) * len(shape))
    tile = lambda c: pl.BlockSpec((1, rows, c), lambda b, j: (b, j, 0))
    return pl.pallas_call(
        functools.partial(_proj_kernel, d_rkv=d_rkv),
        grid=(B, S // rows),
        in_specs=[tile(D), const((D, d_in)), const(w1.shape), const(a1.shape), const(g1.shape),
                  const(mu_lora.shape), const(w2.shape), const(a2.shape), const(g2.shape),
                  const((1, d_r)), const((1, d_r))],
        out_specs=[tile(d_rkv), tile(d_att), tile(d_r), tile(d_r), tile(d_r)],
        out_shape=[jax.ShapeDtypeStruct((B, S, d_rkv), F32),
                   jax.ShapeDtypeStruct((B, S, d_att), F32),
                   jax.ShapeDtypeStruct((B, S, d_r), F32),
                   jax.ShapeDtypeStruct((B, S, d_r), F32),
                   jax.ShapeDtypeStruct((B, S, d_r), BF16)],
        scratch_shapes=[pltpu.VMEM((SUBLANES, D), F32)],
        compiler_params=_params(("parallel", "arbitrary")),
        name="proj",
    )(x, w_in.astype(BF16), w1.astype(BF16), a1.astype(BF16), g1.astype(BF16), mu_lora,
      w2.astype(BF16), a2.astype(BF16), g2.astype(BF16), w0.reshape(1, d_r), a0.reshape(1, d_r))


def _wkv_kernel(rkv_ref, lw_ref, ai_ref, g_ref, mu_ref, kk_ref, ka_ref, rk_ref, lng_ref, lnb_ref,
                seg_ref, tril_ref, y_ref, s_ref, prev_ref, *, d_r, n_chunks):
    C = WKV_CHUNK
    rows = C * n_chunks

    @pl.when(pl.program_id(1) == 0)
    def _():
        s_ref[...] = jnp.zeros_like(s_ref)
        prev_ref[...] = jnp.zeros_like(prev_ref)

    rkv = rkv_ref[0]
    prev = _shift_rows(rkv, prev_ref[0:1, :])
    prev_ref[0:1, :] = rkv[rows - 1:rows, :]
    mixed = rkv + (prev - rkv) * mu_ref[...]
    r = mixed[:, :d_r]
    k = mixed[:, d_r:2 * d_r]
    v = mixed[:, 2 * d_r:]
    lw = lw_ref[0]
    a = ai_ref[0]
    seg = seg_ref[...]
    seg_w = seg.shape[0]

    def head_sum(t):
        return jnp.concatenate([_dot(t[:, j:j + seg_w], seg) for j in range(0, d_r, seg_w)],
                               axis=1)

    kk = k * kk_ref[...]
    kk = kk * lax.rsqrt(jnp.maximum(head_sum(kk * kk), 1e-24))
    k = k * (1.0 + (a - 1.0) * ka_ref[...])
    aa = -kk
    bb = kk * a
    c = _dot_exact_lhs(tril_ref[...], lw)
    e_neg = jnp.exp(-c)
    a_t = aa * jnp.exp(c - lw)
    b_t = bb * e_neg
    k_t = k * e_neg
    r_t = r * jnp.exp(c)

    lane = lax.broadcasted_iota(jnp.int32, (1, LANES), 1)
    head0 = lane < HEAD_DIM
    ri = lax.broadcasted_iota(jnp.int32, (2 * C, 2 * C), 0)
    ci = lax.broadcasted_iota(jnp.int32, (2 * C, 2 * C), 1)
    strict = (ri % C) > (ci % C)
    incl = (ri % C) >= (ci % C)
    eye = (ri == ci).astype(F32)
    zero_bf = jnp.zeros((2 * C, LANES), BF16)

    def stack(t):
        return jnp.concatenate([jnp.where(head0, t, 0.0), jnp.where(head0, 0.0, t)],
                               axis=0).astype(BF16)

    def fold(t):
        return t[:C] + t[C:]

    n_pairs = d_r // LANES
    folds = [(q, p) for q in range(n_chunks) for p in range(n_pairs)]
    each = lambda fn, *cols: [fn(*args) for args in zip(*cols)]
    cat0 = lambda *ts: jnp.concatenate(ts, axis=0)
    cat1 = lambda *ts: jnp.concatenate(ts, axis=1)
    left = lambda t: t[:, :2 * C]
    right = lambda t: t[:, 2 * C:]

    p_ends, b_es, k_es = [], [], []
    for q in range(n_chunks):
        rs = slice(q * C, (q + 1) * C)
        c_end = c[rs][C - 1:C, :]
        e_end = jnp.exp(c_end - c[rs])
        b_es.append(bb[rs] * e_end)
        k_es.append(k[rs] * e_end)
        p_ends.append(jnp.exp(c_end))

    def stacks(t_of_q):
        return [stack(t_of_q(q)[:, p * LANES:(p + 1) * LANES]) for q, p in folds]

    chunk = lambda t: (lambda q: t[q * C:(q + 1) * C])
    am, bm, km, rm, vm = (stacks(chunk(t)) for t in (a_t, b_t, k_t, r_t, v))
    bem = stacks(lambda q: b_es[q])
    kem = stacks(lambda q: k_es[q])

    x = each(lambda a_, r_, b_, k_: _dot_nt(cat0(a_, r_), cat0(b_, k_)), am, rm, bm, km)
    n = each(lambda t: jnp.where(strict, t[:2 * C, :2 * C], 0.0), x)
    a_ak = each(lambda t: jnp.where(strict, t[:2 * C, 2 * C:], 0.0).astype(BF16), x)
    a_rb = each(lambda t: jnp.where(incl, t[2 * C:, :2 * C], 0.0).astype(BF16), x)
    a_rk = each(lambda t: jnp.where(incl, t[2 * C:, 2 * C:], 0.0).astype(BF16), x)
    t_inv = each(lambda t: eye + t, n)
    m = each(lambda t: _dot(t, t), n)
    av = each(lambda a_, v_: _dot(a_, v_).astype(BF16), a_ak, vm)
    for _ in range(int(math.log2(C)) - 2):
        z = each(lambda m_, t_: _dot(m_, cat1(m_.astype(BF16), t_.astype(BF16))), m, t_inv)
        t_inv = each(lambda t_, z_: t_ + right(z_), t_inv, z)
        m = each(left, z)
    t_inv = each(lambda t_, m_: t_ + _dot(m_, t_), t_inv, m)
    z = each(lambda t_, a_, av_: _dot(t_, cat1(a_, av_)), t_inv, am, av)
    ap = each(lambda z_: left(z_).astype(BF16), z)
    u0 = each(lambda z_: right(z_).astype(BF16), z)
    z = each(lambda rb_, rk_, ap_, u_, v_: _dot(cat1(rb_, rk_),
                                               cat0(cat1(ap_, u_), cat1(zero_bf, v_))),
             a_rb, a_rk, ap, u0, vm)
    g_m = each(lambda ap_, be_: _dot_tn(ap_, be_).astype(BF16), ap, bem)
    d0 = each(lambda u_, v_, be_, ke_: _dot_tn(cat0(u_, v_), cat0(be_, ke_)), u0, vm, bem, kem)
    rp_t = each(lambda r_, z_: jnp.transpose(r_.astype(F32) + left(z_)).astype(BF16), rm, z)
    y0 = each(lambda z_: fold(right(z_)), z)

    s = [s_ref[p] for p in range(n_pairs)]
    y_cols = []
    for q in range(n_chunks):
        f0 = q * n_pairs
        zs = [_dot(s[p], cat1(g_m[f0 + p], rp_t[f0 + p])) for p in range(n_pairs)]
        y_cols.append([fold(jnp.transpose(right(zs[p]))) + y0[f0 + p] for p in range(n_pairs)])
        s = [s[p] * p_ends[q][:, p * LANES:(p + 1) * LANES] + left(zs[p]) + d0[f0 + p]
             for p in range(n_pairs)]
    for p in range(n_pairs):
        s_ref[p] = s[p]
    y = cat0(*[cat1(*cols) for cols in y_cols])

    inv = 1.0 / HEAD_DIM
    mu = head_sum(y) * inv
    yc = y - mu
    var = head_sum(yc * yc) * inv
    yn = yc * lax.rsqrt(var + LNX_EPS) * lng_ref[...] + lnb_ref[...]
    bonus = head_sum(r * k * rk_ref[...]) * v
    y_ref[0] = ((yn + bonus) * g_ref[0]).astype(y_ref.dtype)


def _stage_wkv(rkv, lw, ai, g, mu_rkv, k_k, k_a, r_k, lnx_g, lnx_b):
    B, S, d_r = lw.shape
    n_chunks = WKV_CHUNKS_PER_STEP
    rows = WKV_CHUNK * n_chunks
    assert S % rows == 0 and d_r % MXU_DEPTH == 0
    head = np.arange(MXU_DEPTH) // HEAD_DIM
    seg = jnp.asarray(head[:, None] == head[None, :], BF16)
    t = np.arange(rows)
    tril = jnp.asarray((t[:, None] >= t[None, :])
                       & (t[:, None] // WKV_CHUNK == t[None, :] // WKV_CHUNK), BF16)
    const = lambda shape: pl.BlockSpec(shape, lambda b, j: (0,) * len(shape))
    tile = lambda c: pl.BlockSpec((1, rows, c), lambda b, j: (b, j, 0))
    row = lambda t: t.reshape(1, -1)
    return pl.pallas_call(
        functools.partial(_wkv_kernel, d_r=d_r, n_chunks=n_chunks),
        grid=(B, S // rows),
        in_specs=[tile(3 * d_r), tile(d_r), tile(d_r), tile(d_r), const((1, 3 * d_r)),
                  const((1, d_r)), const((1, d_r)), const((1, d_r)), const((1, d_r)),
                  const((1, d_r)), const((MXU_DEPTH, MXU_DEPTH)), const((rows, rows))],
        out_specs=tile(d_r),
        out_shape=jax.ShapeDtypeStruct((B, S, d_r), BF16),
        scratch_shapes=[pltpu.VMEM((d_r // LANES, LANES, LANES), F32),
                        pltpu.VMEM((SUBLANES, 3 * d_r), F32)],
        compiler_params=_params(("parallel", "arbitrary")),
        name="wkv",
    )(rkv, lw, ai, g, row(mu_rkv), row(k_k), row(k_a), row(r_k), row(lnx_g), row(lnx_b), seg, tril)


def _t5_bucket(n):
    exact = N_BUCKETS // 2
    nf = np.maximum(n, 1).astype(np.float32)
    large = exact + (np.log(nf / exact) / math.log(MAX_DISTANCE / exact)
                     * (N_BUCKETS - exact)).astype(np.int32)
    large = np.minimum(large, N_BUCKETS - 1)
    return np.where(n < exact, n, large).astype(np.int32)


def _band_bias(rel_bias, dil):
    blk = ATT_BLK
    L = 3 * blk
    rel = (2 * blk - 1) - np.arange(L)
    valid = (rel >= 0) & (rel <= blk)
    base = rel_bias[_t5_bucket(np.clip(rel, 0, None) * dil)].astype(F32).T
    base = jnp.where(valid[None], base, NEG_INF)
    skew = jnp.tile(base, (1, blk))[:, :blk * (L - 1)].reshape(-1, blk, L - 1)
    return skew[:, :, blk - 1:3 * blk - 1]


def _attend(units, bias_ref, scale):
    lane = lax.broadcasted_iota(jnp.int32, (1, LANES), 1)
    head0 = lane < HEAD_DIM
    jobs = [(u, h) for u in range(len(units)) for h in range(2)]
    s = []
    for u, h in jobs:
        q, k, _, col0 = units[u]
        qh = jnp.where(head0 if h == 0 else jnp.logical_not(head0), q * scale, 0.0)
        s.append(_dot_nt(qh, k) + bias_ref[h, :, col0:col0 + k.shape[0]])
    m = [jnp.max(t, axis=-1, keepdims=True) for t in s]
    e = [jnp.exp2(t - m_) for t, m_ in zip(s, m)]
    ol = [_dot(t, jnp.where(head0 if h == 0 else jnp.logical_not(head0), units[u][2], 1.0))
          for t, (u, h) in zip(e, jobs)]
    both = lambda t, u: jnp.where(head0, t[2 * u], t[2 * u + 1])
    swapped = lambda u: jnp.where(head0, ol[2 * u + 1], ol[2 * u])
    return [(both(ol, u), both(m, u), pltpu.roll(swapped(u), HEAD_DIM, 1))
            for u in range(len(units))]


def _attn_kernel(q_ref, k_ref, v_ref, b1_ref, b4_ref, b16_ref, o_ref, acc_o, acc_m, acc_l, *, seq,
                 scale):
    blk = ATT_BLK
    group = ATT_UNITS

    def rows(ref, start, n, dil):
        if dil == 1:
            return ref[0, pl.ds(start, n), :]
        return ref[0, pl.ds(start, n, stride=dil), :]

    def unit(start, dil, has_prev):
        q = rows(q_ref, start, blk, dil)
        if has_prev:
            k0 = start - blk * dil
            return (q, rows(k_ref, k0, 2 * blk, dil), rows(v_ref, k0, 2 * blk, dil), 0)
        return (q, rows(k_ref, start, blk, dil), rows(v_ref, start, blk, dil), blk)

    def acc_rows(ref, start, dil):
        if dil == 1:
            return ref.at[pl.ds(start, blk), :]
        return ref.at[pl.ds(start, blk, stride=dil), :]

    def run(starts, dil, prevs, bias_ref, first, last):
        units = [unit(s, dil, hp) for s, hp in zip(starts, prevs)]
        for s, (o, m, l) in zip(starts, _attend(units, bias_ref, scale)):
            ao, am, al = (acc_rows(ref, s, dil) for ref in (acc_o, acc_m, acc_l))
            if first:
                ao[...] = o
                am[...] = m
                al[...] = l
            else:
                m_old = am[...]
                m_new = jnp.maximum(m_old, m)
                w_old, w_new = jnp.exp2(m_old - m_new), jnp.exp2(m - m_new)
                ao[...] = w_old * ao[...] + w_new * o
                al[...] = w_old * al[...] + w_new * l
                if not last:
                    am[...] = m_new

    run([n * blk for n in range(group)], 1, [n > 0 for n in range(group)], b1_ref, True, False)

    def body1(g, c):
        base = pl.multiple_of(g * (group * blk), group * blk)
        run([base + n * blk for n in range(group)], 1, [True] * group, b1_ref, True, False)
        return c
    if seq // (group * blk) > 1:
        lax.fori_loop(1, seq // (group * blk), body1, 0)

    nb4 = seq // 4 // blk

    per4 = group // nb4

    def body4(g, c):
        rs = [g * per4 + j for j in range(per4)]
        run([r + n * blk * 4 for r in rs for n in range(nb4)], 4,
            [n > 0 for r in rs for n in range(nb4)], b4_ref, False, False)
        return c
    lax.fori_loop(0, 4 // per4, body4, 0)

    def body16(g, c):
        run([g * group + j for j in range(group)], 16, [False] * group, b16_ref, False, True)
        return c
    lax.fori_loop(0, 16 // group, body16, 0)

    o_ref[0] = (acc_o[...] / acc_l[...]).astype(o_ref.dtype)


def _stage_attn(qkv, rel_bias):
    B, S, d3 = qkv.shape
    d_att = d3 // 3
    n_pairs = d_att // LANES
    assert DIL_PATTERNS == ((ATT_BLK, 1), (4 * ATT_BLK, 4), (16 * ATT_BLK, 16))
    assert S == 16 * ATT_BLK
    log2e = math.log2(math.e)
    biases = [_band_bias(rel_bias, dil) * log2e for _, dil in DIL_PATTERNS]
    part = lambda i: pl.BlockSpec((1, S, LANES), lambda b, p: (b, 0, i * n_pairs + p))
    bias_spec = pl.BlockSpec((2, ATT_BLK, 2 * ATT_BLK), lambda b, p: (p, 0, 0))
    return pl.pallas_call(
        functools.partial(_attn_kernel, seq=S, scale=HEAD_DIM ** -0.5 * log2e),
        grid=(B, n_pairs),
        in_specs=[part(0), part(1), part(2), bias_spec, bias_spec, bias_spec],
        out_specs=pl.BlockSpec((1, S, LANES), lambda b, p: (b, 0, p)),
        out_shape=jax.ShapeDtypeStruct((B, S, d_att), BF16),
        scratch_shapes=[pltpu.VMEM((S, LANES), F32)] * 3,
        compiler_params=_params(("parallel", "parallel")),
        name="attn",
    )(qkv, qkv, qkv, *biases)


def _mix_kernel(x_ref, yr_ref, att_ref, p_ref, wor_ref, woa_ref, g_ref, b_ref, rhi_ref, rlo_ref,
                rb_ref, pg_ref, pp_ref, x1t_ref, lg_ref, ple_ref, *, alpha):
    mix = _dot(yr_ref[...], wor_ref[...]) + _dot(att_ref[...], woa_ref[...])
    emb = _dot(p_ref[...], pp_ref[...])
    x1 = _layer_norm(alpha * x_ref[...] + mix, g_ref[...], b_ref[...])
    _store_token_tiles(x1t_ref, x1)
    hi, lo = _split(x1)
    logits = (jnp.dot(hi, rhi_ref[...], preferred_element_type=F32)
              + jnp.dot(lo, rhi_ref[...], preferred_element_type=F32)
              + jnp.dot(hi, rlo_ref[...], preferred_element_type=F32) + rb_ref[...])
    ple_ref[...] = (_sigmoid(_dot(hi, pg_ref[...])) * emb).astype(ple_ref.dtype)

    lane = lax.broadcasted_iota(jnp.int32, logits.shape, 1).astype(F32)
    big = float(ROUTER_LANES)
    rmax = lambda t: jnp.max(t, axis=-1, keepdims=True)
    first = lambda hit: jnp.min(jnp.where(hit, lane, big), axis=-1, keepdims=True)
    off = float("-inf")
    is_grp = lane < N_GROUPS
    lg = jnp.where(is_grp, logits, off)
    mg = rmax(lg)
    wg = 1.0 / jnp.sum(jnp.where(is_grp, jnp.exp(lg - mg), 0.0), axis=-1, keepdims=True)
    lo_lane = N_GROUPS + EXPERTS_PER_GROUP * first(lg == mg)
    le = jnp.where((lane >= lo_lane) & (lane < lo_lane + EXPERTS_PER_GROUP), logits, off)
    v1 = rmax(le)
    i1 = first(le == v1)
    le = jnp.where(lane == i1, off, le)
    v2 = rmax(le)
    i2 = first(le == v2)
    e2 = jnp.exp(v2 - v1)
    w1 = wg / (1.0 + e2)
    route = jnp.where(lane == 0, w1, jnp.where(lane == 1, w1 * e2, jnp.where(
        lane == 2, i1 - N_GROUPS, jnp.where(lane == 3, i2 - N_GROUPS, 0.0))))
    lg_ref[...] = route


def _stage_mix(x2, yr, att, p2, w_o, ln_g, ln_b, router_g, router_g_b, router_e, router_e_b,
               ple_gate, ple_proj, alpha):
    N, D = x2.shape
    d_r = yr.shape[1]
    rows = min(MIX_ROWS, N)
    d_att = att.shape[1]
    n_log = N_GROUPS + N_EXPERTS
    rw = jnp.zeros((D, ROUTER_LANES), F32).at[:, :N_GROUPS].set(router_g)
    rw = rw.at[:, N_GROUPS:n_log].set(router_e)
    rb = jnp.zeros((1, ROUTER_LANES), F32).at[0, :N_GROUPS].set(router_g_b)
    rb = rb.at[0, N_GROUPS:n_log].set(router_e_b)
    rhi, rlo = _split(rw)
    const = lambda shape: pl.BlockSpec(shape, lambda i: (0,) * len(shape))
    tile = lambda c: pl.BlockSpec((rows, c), lambda i: (i, 0))
    return pl.pallas_call(
        functools.partial(_mix_kernel, alpha=alpha),
        grid=(N // rows,),
        in_specs=[tile(D), tile(d_r), tile(d_att), tile(p2.shape[1]),
                  const((d_r, D)), const((d_att, D)), const((1, D)), const((1, D)),
                  const((D, ROUTER_LANES)), const((D, ROUTER_LANES)), const((1, ROUTER_LANES)),
                  const((D, D)), const(ple_proj.shape)],
        out_specs=[pl.BlockSpec((rows * SUBLANES, LANES), lambda i: (i, 0)),
                   tile(ROUTER_LANES), tile(D)],
        out_shape=[jax.ShapeDtypeStruct((N * SUBLANES, LANES), F32),
                   jax.ShapeDtypeStruct((N, ROUTER_LANES), F32),
                   jax.ShapeDtypeStruct((N, D), BF16)],
        compiler_params=_params(("parallel",)),
        name="mix",
    )(x2, yr, att, p2, w_o[:d_r].astype(BF16), w_o[d_r:].astype(BF16),
      ln_g.reshape(1, D), ln_b.reshape(1, D), rhi, rlo, rb, ple_gate.astype(BF16),
      ple_proj.astype(BF16))


def _route(route, rows):
    N = route.shape[0]
    we = route[:, :TOP_K]
    eid = route[:, TOP_K:2 * TOP_K].astype(jnp.int32)
    A = N * TOP_K
    e_flat = eid.T.reshape(A).astype(jnp.int32)
    order = jnp.argsort(e_flat).astype(jnp.int32)
    counts = jnp.sum((e_flat[None, :] == jnp.arange(N_EXPERTS)[:, None]).astype(jnp.int32), axis=1)
    start = jnp.cumsum(counts) - counts
    padded = (counts + rows - 1) // rows * rows
    pend = jnp.cumsum(padded)
    pstart = pend - padded
    nblk = -(-A // rows) + N_EXPERTS
    nused = (pend[-1] // rows).astype(jnp.int32)
    blk = jnp.arange(nblk, dtype=jnp.int32)
    blk_e = jnp.sum((pend[None, :] <= (blk * rows)[:, None]).astype(jnp.int32), axis=1)
    blk_e = jnp.minimum(blk_e, N_EXPERTS - 1)
    blk_e = jnp.where(blk < nused, blk_e, blk_e[nused - 1]).astype(jnp.int32)
    r = jnp.arange(rows, dtype=jnp.int32)[None, :]
    off = (blk * rows - pstart[blk_e])[:, None] + r
    real = (off < counts[blk_e][:, None]) & (blk < nused)[:, None]
    asg = order[jnp.clip(start[blk_e][:, None] + off, 0, A - 1)]
    row_tok = jnp.where(real, asg % N, r).astype(jnp.int32)
    row_dst = jnp.where(real, asg, A + (blk % 2)[:, None] * rows + r).astype(jnp.int32)
    return (row_tok.reshape(nblk, 1, rows), row_dst.reshape(nblk, 1, rows), blk_e,
            nused.reshape(1), we)


def _moe_kernel(blk_e_ref, nused_ref, tok0_ref, tokn_ref, dstp_ref, x_hbm, wg_ref, wu_ref, wd_ref,
                y_hbm, xbuf, ybuf, gsem, ssem, *, rows, n_real):
    i = pl.program_id(0)
    nused = nused_ref[0]
    slot = i % 2
    tile_rows = lambda first: pl.ds(pl.multiple_of(first, SUBLANES), SUBLANES)

    def gather_row(tab_ref, sl, r):
        return pltpu.make_async_copy(x_hbm.at[tile_rows(tab_ref[0, 0, r])],
                                     xbuf.at[sl, pl.ds(r * SUBLANES, SUBLANES)], gsem.at[sl])

    def scatter_row(sl, r):
        return pltpu.make_async_copy(ybuf.at[sl, pl.ds(r * SUBLANES, SUBLANES)],
                                     y_hbm.at[tile_rows(dstp_ref[0, 0, r])], ssem.at[sl])

    def gather_all(sl):
        return pltpu.make_async_copy(x_hbm.at[pl.ds(0, rows * SUBLANES)], xbuf.at[sl],
                                     gsem.at[sl])

    def scatter_all(sl, first_row=0):
        return pltpu.make_async_copy(
            ybuf.at[sl], y_hbm.at[pl.ds(first_row * SUBLANES, rows * SUBLANES)], ssem.at[sl])

    def rolled(fn):
        def body(r, c):
            fn(r)
            return c
        lax.fori_loop(0, rows, body, 0, unroll=8)

    def inline(fn):
        for r in range(rows):
            fn(r)

    def ffn(sl):
        xb = _load_token_tiles(xbuf.at[sl]).astype(BF16)
        gate = _dot(xb, wg_ref[0])
        up = _dot(xb, wu_ref[0])
        _store_token_tiles(ybuf.at[sl], _dot(gate * _sigmoid(gate) * up, wd_ref[0]))

    @pl.when(i == 0)
    def _():
        ybuf[...] = jnp.zeros_like(ybuf)
        for sl in range(2):
            scatter_all(sl, n_real + sl * rows).start()
        for sl in range(2):
            scatter_all(sl, n_real + sl * rows).wait()
        rolled(lambda r: gather_row(tok0_ref, 0, r).start())

    @pl.when(i < nused)
    def _():
        gather_all(slot).wait()

    @pl.when((i >= 2) & (i <= nused))
    def _():
        scatter_all(slot).wait()

    @pl.when(i == 0)
    def _():
        inline(lambda r: gather_row(tokn_ref, 1, r).start())
        ffn(0)

    for s in range(2):
        @pl.when((i >= 1) & (i < nused) & (slot == s))
        def _():
            inline(lambda r: gather_row(tokn_ref, 1 - s, r).start(priority=0))
            inline(lambda r: scatter_row(1 - s, r).start(priority=1))
            ffn(s)

    @pl.when(i == nused)
    def _():
        rolled(lambda r: scatter_row(1 - slot, r).start())
        scatter_all(1 - slot).wait()
        gather_all(slot).wait()


def _stage_moe(x1t, row_tok, row_dst, blk_e, nused, w_gate, w_up, w_down):
    N = x1t.shape[0] // SUBLANES
    D = SUBLANES * LANES
    nblk, _, rows = row_tok.shape
    d_e = w_gate.shape[2]
    n_real = N * TOP_K
    last = nblk - 1
    wspec = lambda shape: pl.BlockSpec((1,) + shape,
                                       lambda i, be, nu: (be[jnp.minimum(i, last)], 0, 0))
    table = lambda fn: pl.BlockSpec((1, 1, rows), lambda i, be, nu: (fn(i), 0, 0),
                                    memory_space=pltpu.SMEM)
    grid_spec = pltpu.PrefetchScalarGridSpec(
        num_scalar_prefetch=2,
        grid=(nblk + 1,),
        in_specs=[table(lambda i: 0), table(lambda i: jnp.minimum(i + 1, last)),
                  table(lambda i: jnp.clip(i - 1, 0, last)),
                  pl.BlockSpec(memory_space=pl.ANY), wspec((D, d_e)), wspec((D, d_e)),
                  wspec((d_e, D))],
        out_specs=pl.BlockSpec(memory_space=pl.ANY),
        scratch_shapes=[pltpu.VMEM((2, rows * SUBLANES, LANES), F32),
                        pltpu.VMEM((2, rows * SUBLANES, LANES), F32),
                        pltpu.SemaphoreType.DMA((2,)), pltpu.SemaphoreType.DMA((2,))],
    )
    return pl.pallas_call(
        functools.partial(_moe_kernel, rows=rows, n_real=n_real),
        grid_spec=grid_spec,
        out_shape=jax.ShapeDtypeStruct(((n_real + 2 * rows) * SUBLANES, LANES), F32),
        compiler_params=_params(("arbitrary",)),
        name="moe",
    )(blk_e, nused, row_tok * SUBLANES, row_tok * SUBLANES, row_dst * SUBLANES, x1t, w_gate, w_up,
      w_down)


def _out_kernel(x1t_ref, y0_ref, y1_ref, we_ref, ple_ref, g_ref, b_ref, o_ref, *, alpha):
    we = we_ref[...]
    moe = we[:, 0:1] * _load_token_tiles(y0_ref) + we[:, 1:2] * _load_token_tiles(y1_ref)
    o_ref[...] = _layer_norm(alpha * _load_token_tiles(x1t_ref) + moe + ple_ref[...], g_ref[...],
                             b_ref[...])


def _stage_out(x1t, yb, we, ple, ln_g, ln_b, alpha):
    N = x1t.shape[0] // SUBLANES
    D = SUBLANES * LANES
    rows = min(OUT_ROWS, N)
    const = lambda shape: pl.BlockSpec(shape, lambda i: (0,) * len(shape))
    tile = lambda c: pl.BlockSpec((rows, c), lambda i: (i, 0))
    slot = lambda j: pl.BlockSpec((rows * SUBLANES, LANES), lambda i: (i + j * (N // rows), 0))
    return pl.pallas_call(
        functools.partial(_out_kernel, alpha=alpha),
        grid=(N // rows,),
        in_specs=[slot(0), slot(0), slot(1), tile(TOP_K), tile(D), const((1, D)), const((1, D))],
        out_specs=tile(D),
        out_shape=jax.ShapeDtypeStruct((N, D), F32),
        compiler_params=_params(("parallel",)),
        name="out",
    )(x1t, yb, yb, we, ple, ln_g.reshape(1, D), ln_b.reshape(1, D))


def kernel(x, p, w_in, mu_rkv, mu_lora, w0, w_lora1, w_lora2, a0, a_lora1, a_lora2, g_lora1, g_lora2, k_k, k_a, r_k, lnx_g, lnx_b, rel_bias, w_o, ln1_g, ln1_b, router_g, router_g_b, router_e, router_e_b, w_gate, w_up, w_down, ple_gate, ple_proj, ln2_g, ln2_b):
    B, S, D = x.shape
    depth = w_in.shape[0]
    alpha = (2 * depth) ** 0.25
    for i in range(depth):
        rkv, qkv, lw, ai, g = _stage_proj(x, w_in[i], w_lora1[i], a_lora1[i], g_lora1[i],
                                          mu_lora[i], w_lora2[i], a_lora2[i], g_lora2[i],
                                          w0[i], a0[i])
        yr = _stage_wkv(rkv, lw, ai, g, mu_rkv[i], k_k[i], k_a[i], r_k[i], lnx_g[i], lnx_b[i])
        att = _stage_attn(qkv, rel_bias).reshape(B * S, -1)
        x1t, route, ple = _stage_mix(x.reshape(B * S, D), yr.reshape(B * S, -1), att,
                                      p[i].reshape(B * S, -1), w_o[i], ln1_g[i], ln1_b[i],
                                      router_g[i], router_g_b[i], router_e[i], router_e_b[i],
                                      ple_gate[i], ple_proj[i], alpha)
        row_tok, row_dst, blk_e, nused, we = _route(route, MOE_ROWS)
        yb = _stage_moe(x1t, row_tok, row_dst, blk_e, nused, w_gate[i], w_up[i], w_down[i])
        x = _stage_out(x1t, yb, we, ple, ln2_g[i], ln2_b[i], alpha).reshape(B, S, D)
    return x
```

```python
import functools
import math

import jax
import jax.numpy as jnp
import numpy as np
from jax import lax
from jax.experimental import pallas as pl
from jax.experimental.pallas import tpu as pltpu

F32 = jnp.float32
BF16 = jnp.bfloat16

HEAD_DIM = 64
LANES = 128
SUBLANES = 8
MXU_DEPTH = 256
LNX_EPS = 64e-5
DIL_PATTERNS = ((128, 1), (512, 4), (2048, 16))
NEG_INF = -1e30
N_BUCKETS = 32
MAX_DISTANCE = 2048
N_GROUPS = 4
EXPERTS_PER_GROUP = 8
N_EXPERTS = N_GROUPS * EXPERTS_PER_GROUP
TOP_K = 2
LN_EPS = 1e-5

WKV_CHUNK = 64
WKV_CHUNKS_PER_STEP = 4
ATT_BLK = 128
ATT_UNITS = 16
PROJ_ROWS = 512
MIX_ROWS = 512
MOE_ROWS = 256
OUT_ROWS = 1024
ROUTER_LANES = 128
VMEM_LIMIT = 56 * 1024 * 1024


def _params(sem):
    return pltpu.CompilerParams(dimension_semantics=sem, vmem_limit_bytes=VMEM_LIMIT)


def _dot(a, b):
    return jnp.dot(a.astype(BF16), b.astype(BF16), preferred_element_type=F32)


def _dot_nt(a, b):
    return lax.dot_general(a.astype(BF16), b.astype(BF16), (((1,), (1,)), ((), ())),
                           preferred_element_type=F32)


def _dot_tn(a, b):
    return lax.dot_general(a.astype(BF16), b.astype(BF16), (((0,), (0,)), ((), ())),
                           preferred_element_type=F32)


def _split(x):
    hi = x.astype(BF16)
    lo = (x - hi.astype(F32)).astype(BF16)
    return hi, lo


def _dot_exact_lhs(w01, x):
    hi, lo = _split(x)
    return (jnp.dot(w01, hi, preferred_element_type=F32)
            + jnp.dot(w01, lo, preferred_element_type=F32))


def _sigmoid(x):
    return 1.0 / (1.0 + jnp.exp(-x))


def _shift_rows(t, prev_row):
    rolled = pltpu.roll(t, 1, 0)
    row = lax.broadcasted_iota(jnp.int32, t.shape, 0)
    return jnp.where(row == 0, prev_row, rolled)


def _store_token_tiles(ref, t):
    rows = t.shape[0]
    for s in range(SUBLANES):
        ref[pl.ds(s, rows, stride=SUBLANES), :] = t[:, s * LANES:(s + 1) * LANES]


def _load_token_tiles(ref):
    rows = ref.shape[0] // SUBLANES
    return jnp.concatenate([ref[pl.ds(s, rows, stride=SUBLANES), :] for s in range(SUBLANES)],
                           axis=1)


def _layer_norm(z, g, b):
    mu = jnp.mean(z, axis=-1, keepdims=True)
    zc = z - mu
    var = jnp.mean(zc * zc, axis=-1, keepdims=True)
    return zc * lax.rsqrt(var + LN_EPS) * g + b


def _proj_kernel(x_ref, win_ref, w1_ref, a1_ref, g1_ref, mul_ref, w2_ref, a2_ref, g2_ref,
                 w0_ref, a0_ref, rkv_ref, qkv_ref, lw_ref, ai_ref, g_ref, prev_ref, *, d_rkv):
    @pl.when(pl.program_id(1) == 0)
    def _():
        prev_ref[...] = jnp.zeros_like(prev_ref)

    h = x_ref[0]
    rows = h.shape[0]
    hprev = _shift_rows(h, prev_ref[0:1, :])
    prev_ref[0:1, :] = h[rows - 1:rows, :]

    dh = hprev - h
    hb = h.astype(BF16)
    low_w = _dot(h + dh * mul_ref[0:1, :], w1_ref[...])
    low_a = _dot(h + dh * mul_ref[1:2, :], a1_ref[...])
    low_g = _dot(h + dh * mul_ref[2:3, :], g1_ref[...])
    rkv_ref[0] = _dot(hb, win_ref[:, :d_rkv])
    wl = w0_ref[...] + _dot(jnp.tanh(low_w), w2_ref[...])
    al = a0_ref[...] + _dot(low_a, a2_ref[...])
    g_ref[0] = _dot(_sigmoid(low_g), g2_ref[...]).astype(g_ref.dtype)
    qkv_ref[0] = _dot(hb, win_ref[:, d_rkv:])
    z = -wl
    softplus = jnp.maximum(z, 0.0) + jnp.log(1.0 + jnp.exp(-jnp.abs(z)))
    lw_ref[0] = -jnp.exp(-softplus - 0.5)
    ai_ref[0] = _sigmoid(al)


def _stage_proj(x, w_in, w1, a1, g1, mu_lora, w2, a2, g2, w0, a0):
    B, S, D = x.shape
    d_in = w_in.shape[1]
    d_r = w2.shape[1]
    d_rkv = 3 * d_r
    d_att = d_in - d_rkv
    rows = min(PROJ_ROWS, S)
    const = lambda shape: pl.BlockSpec(shape, lambda b, j: (0,) * len(shape))
    tile = lambda c: pl.BlockSpec((1, rows, c), lambda b, j: (b, j, 0))
    return pl.pallas_call(
        functools.partial(_proj_kernel, d_rkv=d_rkv),
        grid=(B, S // rows),
        in_specs=[tile(D), const((D, d_in)), const(w1.shape), const(a1.shape), const(g1.shape),
                  const(mu_lora.shape), const(w2.shape), const(a2.shape), const(g2.shape),
                  const((1, d_r)), const((1, d_r))],
        out_specs=[tile(d_rkv), tile(d_att), tile(d_r), tile(d_r), tile(d_r)],
        out_shape=[jax.ShapeDtypeStruct((B, S, d_rkv), F32),
                   jax.ShapeDtypeStruct((B, S, d_att), F32),
                   jax.ShapeDtypeStruct((B, S, d_r), F32),
                   jax.ShapeDtypeStruct((B, S, d_r), F32),
                   jax.ShapeDtypeStruct((B, S, d_r), BF16)],
        scratch_shapes=[pltpu.VMEM((SUBLANES, D), F32)],
        compiler_params=_params(("parallel", "arbitrary")),
        name="proj",
    )(x, w_in.astype(BF16), w1.astype(BF16), a1.astype(BF16), g1.astype(BF16), mu_lora,
      w2.astype(BF16), a2.astype(BF16), g2.astype(BF16), w0.reshape(1, d_r), a0.reshape(1, d_r))


def _wkv_kernel(rkv_ref, lw_ref, ai_ref, g_ref, mu_ref, kk_ref, ka_ref, rk_ref, lng_ref, lnb_ref,
                seg_ref, tril_ref, y_ref, s_ref, prev_ref, *, d_r, n_chunks):
    C = WKV_CHUNK
    rows = C * n_chunks

    @pl.when(pl.program_id(1) == 0)
    def _():
        s_ref[...] = jnp.zeros_like(s_ref)
        prev_ref[...] = jnp.zeros_like(prev_ref)

    rkv = rkv_ref[0]
    prev = _shift_rows(rkv, prev_ref[0:1, :])
    prev_ref[0:1, :] = rkv[rows - 1:rows, :]
    mixed = rkv + (prev - rkv) * mu_ref[...]
    r = mixed[:, :d_r]
    k = mixed[:, d_r:2 * d_r]
    v = mixed[:, 2 * d_r:]
    lw = lw_ref[0]
    a = ai_ref[0]
    seg = seg_ref[...]
    seg_w = seg.shape[0]

    def head_sum(t):
        return jnp.concatenate([_dot(t[:, j:j + seg_w], seg) for j in range(0, d_r, seg_w)],
                               axis=1)

    kk = k * kk_ref[...]
    kk = kk * lax.rsqrt(jnp.maximum(head_sum(kk * kk), 1e-24))
    k = k * (1.0 + (a - 1.0) * ka_ref[...])
    aa = -kk
    bb = kk * a
    c = _dot_exact_lhs(tril_ref[...], lw)
    e_neg = jnp.exp(-c)
    a_t = aa * jnp.exp(c - lw)
    b_t = bb * e_neg
    k_t = k * e_neg
    r_t = r * jnp.exp(c)

    lane = lax.broadcasted_iota(jnp.int32, (1, LANES), 1)
    head0 = lane < HEAD_DIM
    ri = lax.broadcasted_iota(jnp.int32, (2 * C, 2 * C), 0)
    ci = lax.broadcasted_iota(jnp.int32, (2 * C, 2 * C), 1)
    strict = (ri % C) > (ci % C)
    incl = (ri % C) >= (ci % C)
    eye = (ri == ci).astype(F32)
    zero_bf = jnp.zeros((2 * C, LANES), BF16)

    def stack(t):
        return jnp.concatenate([jnp.where(head0, t, 0.0), jnp.where(head0, 0.0, t)],
                               axis=0).astype(BF16)

    def fold(t):
        return t[:C] + t[C:]

    n_pairs = d_r // LANES
    folds = [(q, p) for q in range(n_chunks) for p in range(n_pairs)]
    each = lambda fn, *cols: [fn(*args) for args in zip(*cols)]
    cat0 = lambda *ts: jnp.concatenate(ts, axis=0)
    cat1 = lambda *ts: jnp.concatenate(ts, axis=1)
    left = lambda t: t[:, :2 * C]
    right = lambda t: t[:, 2 * C:]

    p_ends, b_es, k_es = [], [], []
    for q in range(n_chunks):
        rs = slice(q * C, (q + 1) * C)
        c_end = c[rs][C - 1:C, :]
        e_end = jnp.exp(c_end - c[rs])
        b_es.append(bb[rs] * e_end)
        k_es.append(k[rs] * e_end)
        p_ends.append(jnp.exp(c_end))

    def stacks(t_of_q):
        return [stack(t_of_q(q)[:, p * LANES:(p + 1) * LANES]) for q, p in folds]

    chunk = lambda t: (lambda q: t[q * C:(q + 1) * C])
    am, bm, km, rm, vm = (stacks(chunk(t)) for t in (a_t, b_t, k_t, r_t, v))
    bem = stacks(lambda q: b_es[q])
    kem = stacks(lambda q: k_es[q])

    x = each(lambda a_, r_, b_, k_: _dot_nt(cat0(a_, r_), cat0(b_, k_)), am, rm, bm, km)
    n = each(lambda t: jnp.where(strict, t[:2 * C, :2 * C], 0.0), x)
    a_ak = each(lambda t: jnp.where(strict, t[:2 * C, 2 * C:], 0.0).astype(BF16), x)
    a_rb = each(lambda t: jnp.where(incl, t[2 * C:, :2 * C], 0.0).astype(BF16), x)
    a_rk = each(lambda t: jnp.where(incl, t[2 * C:, 2 * C:], 0.0).astype(BF16), x)
    t_inv = each(lambda t: eye + t, n)
    m = each(lambda t: _dot(t, t), n)
    av = each(lambda a_, v_: _dot(a_, v_).astype(BF16), a_ak, vm)
    for _ in range(int(math.log2(C)) - 2):
        z = each(lambda m_, t_: _dot(m_, cat1(m_.astype(BF16), t_.astype(BF16))), m, t_inv)
        t_inv = each(lambda t_, z_: t_ + right(z_), t_inv, z)
        m = each(left, z)
    t_inv = each(lambda t_, m_: t_ + _dot(m_, t_), t_inv, m)
    z = each(lambda t_, a_, av_: _dot(t_, cat1(a_, av_)), t_inv, am, av)
    ap = each(lambda z_: left(z_).astype(BF16), z)
    u0 = each(lambda z_: right(z_).astype(BF16), z)
    z = each(lambda rb_, rk_, ap_, u_, v_: _dot(cat1(rb_, rk_),
                                               cat0(cat1(ap_, u_), cat1(zero_bf, v_))),
             a_rb, a_rk, ap, u0, vm)
    g_m = each(lambda ap_, be_: _dot_tn(ap_, be_).astype(BF16), ap, bem)
    d0 = each(lambda u_, v_, be_, ke_: _dot_tn(cat0(u_, v_), cat0(be_, ke_)), u0, vm, bem, kem)
    rp_t = each(lambda r_, z_: jnp.transpose(r_.astype(F32) + left(z_)).astype(BF16), rm, z)
    y0 = each(lambda z_: fold(right(z_)), z)

    s = [s_ref[p] for p in range(n_pairs)]
    y_cols = []
    for q in range(n_chunks):
        f0 = q * n_pairs
        zs = [_dot(s[p], cat1(g_m[f0 + p], rp_t[f0 + p])) for p in range(n_pairs)]
        y_cols.append([fold(jnp.transpose(right(zs[p]))) + y0[f0 + p] for p in range(n_pairs)])
        s = [s[p] * p_ends[q][:, p * LANES:(p + 1) * LANES] + left(zs[p]) + d0[f0 + p]
             for p in range(n_pairs)]
    for p in range(n_pairs):
        s_ref[p] = s[p]
    y = cat0(*[cat1(*cols) for cols in y_cols])

    inv = 1.0 / HEAD_DIM
    mu = head_sum(y) * inv
    yc = y - mu
    var = head_sum(yc * yc) * inv
    yn = yc * lax.rsqrt(var + LNX_EPS) * lng_ref[...] + lnb_ref[...]
    bonus = head_sum(r * k * rk_ref[...]) * v
    y_ref[0] = ((yn + bonus) * g_ref[0]).astype(y_ref.dtype)


def _stage_wkv(rkv, lw, ai, g, mu_rkv, k_k, k_a, r_k, lnx_g, lnx_b):
    B, S, d_r = lw.shape
    n_chunks = WKV_CHUNKS_PER_STEP
    rows = WKV_CHUNK * n_chunks
    assert S % rows == 0 and d_r % MXU_DEPTH == 0
    head = np.arange(MXU_DEPTH) // HEAD_DIM
    seg = jnp.asarray(head[:, None] == head[None, :], BF16)
    t = np.arange(rows)
    tril = jnp.asarray((t[:, None] >= t[None, :])
                       & (t[:, None] // WKV_CHUNK == t[None, :] // WKV_CHUNK), BF16)
    const = lambda shape: pl.BlockSpec(shape, lambda b, j: (0,) * len(shape))
    tile = lambda c: pl.BlockSpec((1, rows, c), lambda b, j: (b, j, 0))
    row = lambda t: t.reshape(1, -1)
    return pl.pallas_call(
        functools.partial(_wkv_kernel, d_r=d_r, n_chunks=n_chunks),
        grid=(B, S // rows),
        in_specs=[tile(3 * d_r), tile(d_r), tile(d_r), tile(d_r), const((1, 3 * d_r)),
                  const((1, d_r)), const((1, d_r)), const((1, d_r)), const((1, d_r)),
                  const((1, d_r)), const((MXU_DEPTH, MXU_DEPTH)), const((rows, rows))],
        out_specs=tile(d_r),
        out_shape=jax.ShapeDtypeStruct((B, S, d_r), BF16),
        scratch_shapes=[pltpu.VMEM((d_r // LANES, LANES, LANES), F32),
                        pltpu.VMEM((SUBLANES, 3 * d_r), F32)],
        compiler_params=_params(("parallel", "arbitrary")),
        name="wkv",
    )(rkv, lw, ai, g, row(mu_rkv), row(k_k), row(k_a), row(r_k), row(lnx_g), row(lnx_b), seg, tril)


def _t5_bucket(n):
    exact = N_BUCKETS // 2
    nf = np.maximum(n, 1).astype(np.float32)
    large = exact + (np.log(nf / exact) / math.log(MAX_DISTANCE / exact)
                     * (N_BUCKETS - exact)).astype(np.int32)
    large = np.minimum(large, N_BUCKETS - 1)
    return np.where(n < exact, n, large).astype(np.int32)


def _band_bias(rel_bias, dil):
    blk = ATT_BLK
    L = 3 * blk
    rel = (2 * blk - 1) - np.arange(L)
    valid = (rel >= 0) & (rel <= blk)
    base = rel_bias[_t5_bucket(np.clip(rel, 0, None) * dil)].astype(F32).T
    base = jnp.where(valid[None], base, NEG_INF)
    skew = jnp.tile(base, (1, blk))[:, :blk * (L - 1)].reshape(-1, blk, L - 1)
    return skew[:, :, blk - 1:3 * blk - 1]


def _attend(units, bias_ref, scale):
    lane = lax.broadcasted_iota(jnp.int32, (1, LANES), 1)
    head0 = lane < HEAD_DIM
    jobs = [(u, h) for u in range(len(units)) for h in range(2)]
    s = []
    for u, h in jobs:
        q, k, _, col0 = units[u]
        qh = jnp.where(head0 if h == 0 else jnp.logical_not(head0), q * scale, 0.0)
        s.append(_dot_nt(qh, k) + bias_ref[h, :, col0:col0 + k.shape[0]])
    m = [jnp.max(t, axis=-1, keepdims=True) for t in s]
    e = [jnp.exp2(t - m_) for t, m_ in zip(s, m)]
    ol = [_dot(t, jnp.where(head0 if h == 0 else jnp.logical_not(head0), units[u][2], 1.0))
          for t, (u, h) in zip(e, jobs)]
    both = lambda t, u: jnp.where(head0, t[2 * u], t[2 * u + 1])
    swapped = lambda u: jnp.where(head0, ol[2 * u + 1], ol[2 * u])
    return [(both(ol, u), both(m, u), pltpu.roll(swapped(u), HEAD_DIM, 1))
            for u in range(len(units))]


def _attn_kernel(q_ref, k_ref, v_ref, b1_ref, b4_ref, b16_ref, o_ref, acc_o, acc_m, acc_l, *, seq,
                 scale):
    blk = ATT_BLK
    group = ATT_UNITS

    def rows(ref, start, n, dil):
        if dil == 1:
            return ref[0, pl.ds(start, n), :]
        return ref[0, pl.ds(start, n, stride=dil), :]

    def unit(start, dil, has_prev):
        q = rows(q_ref, start, blk, dil)
        if has_prev:
            k0 = start - blk * dil
            return (q, rows(k_ref, k0, 2 * blk, dil), rows(v_ref, k0, 2 * blk, dil), 0)
        return (q, rows(k_ref, start, blk, dil), rows(v_ref, start, blk, dil), blk)

    def acc_rows(ref, start, dil):
        if dil == 1:
            return ref.at[pl.ds(start, blk), :]
        return ref.at[pl.ds(start, blk, stride=dil), :]

    def run(starts, dil, prevs, bias_ref, first, last):
        units = [unit(s, dil, hp) for s, hp in zip(starts, prevs)]
        for s, (o, m, l) in zip(starts, _attend(units, bias_ref, scale)):
            ao, am, al = (acc_rows(ref, s, dil) for ref in (acc_o, acc_m, acc_l))
            if first:
                ao[...] = o
                am[...] = m
                al[...] = l
            else:
                m_old = am[...]
                m_new = jnp.maximum(m_old, m)
                w_old, w_new = jnp.exp2(m_old - m_new), jnp.exp2(m - m_new)
                ao[...] = w_old * ao[...] + w_new * o
                al[...] = w_old * al[...] + w_new * l
                if not last:
                    am[...] = m_new

    run([n * blk for n in range(group)], 1, [n > 0 for n in range(group)], b1_ref, True, False)

    def body1(g, c):
        base = pl.multiple_of(g * (group * blk), group * blk)
        run([base + n * blk for n in range(group)], 1, [True] * group, b1_ref, True, False)
        return c
    if seq // (group * blk) > 1:
        lax.fori_loop(1, seq // (group * blk), body1, 0)

    nb4 = seq // 4 // blk

    per4 = group // nb4

    def body4(g, c):
        rs = [g * per4 + j for j in range(per4)]
        run([r + n * blk * 4 for r in rs for n in range(nb4)], 4,
            [n > 0 for r in rs for n in range(nb4)], b4_ref, False, False)
        return c
    lax.fori_loop(0, 4 // per4, body4, 0)

    def body16(g, c):
        run([g * group + j for j in range(group)], 16, [False] * group, b16_ref, False, True)
        return c
    lax.fori_loop(0, 16 // group, body16, 0)

    o_ref[0] = (acc_o[...] / acc_l[...]).astype(o_ref.dtype)


def _stage_attn(qkv, rel_bias):
    B, S, d3 = qkv.shape
    d_att = d3 // 3
    n_pairs = d_att // LANES
    assert DIL_PATTERNS == ((ATT_BLK, 1), (4 * ATT_BLK, 4), (16 * ATT_BLK, 16))
    assert S == 16 * ATT_BLK
    log2e = math.log2(math.e)
    biases = [_band_bias(rel_bias, dil) * log2e for _, dil in DIL_PATTERNS]
    part = lambda i: pl.BlockSpec((1, S, LANES), lambda b, p: (b, 0, i * n_pairs + p))
    bias_spec = pl.BlockSpec((2, ATT_BLK, 2 * ATT_BLK), lambda b, p: (p, 0, 0))
    return pl.pallas_call(
        functools.partial(_attn_kernel, seq=S, scale=HEAD_DIM ** -0.5 * log2e),
        grid=(B, n_pairs),
        in_specs=[part(0), part(1), part(2), bias_spec, bias_spec, bias_spec],
        out_specs=pl.BlockSpec((1, S, LANES), lambda b, p: (b, 0, p)),
        out_shape=jax.ShapeDtypeStruct((B, S, d_att), BF16),
        scratch_shapes=[pltpu.VMEM((S, LANES), F32)] * 3,
        compiler_params=_params(("parallel", "parallel")),
        name="attn",
    )(qkv, qkv, qkv, *biases)


def _mix_kernel(x_ref, yr_ref, att_ref, p_ref, wor_ref, woa_ref, g_ref, b_ref, rhi_ref, rlo_ref,
                rb_ref, pg_ref, pp_ref, x1t_ref, lg_ref, ple_ref, *, alpha):
    mix = _dot(yr_ref[...], wor_ref[...]) + _dot(att_ref[...], woa_ref[...])
    emb = _dot(p_ref[...], pp_ref[...])
    x1 = _layer_norm(alpha * x_ref[...] + mix, g_ref[...], b_ref[...])
    _store_token_tiles(x1t_ref, x1)
    hi, lo = _split(x1)
    logits = (jnp.dot(hi, rhi_ref[...], preferred_element_type=F32)
              + jnp.dot(lo, rhi_ref[...], preferred_element_type=F32)
              + jnp.dot(hi, rlo_ref[...], preferred_element_type=F32) + rb_ref[...])
    ple_ref[...] = (_sigmoid(_dot(hi, pg_ref[...])) * emb).astype(ple_ref.dtype)

    lane = lax.broadcasted_iota(jnp.int32, logits.shape, 1).astype(F32)
    big = float(ROUTER_LANES)
    rmax = lambda t: jnp.max(t, axis=-1, keepdims=True)
    first = lambda hit: jnp.min(jnp.where(hit, lane, big), axis=-1, keepdims=True)
    off = float("-inf")
    is_grp = lane < N_GROUPS
    lg = jnp.where(is_grp, logits, off)
    mg = rmax(lg)
    wg = 1.0 / jnp.sum(jnp.where(is_grp, jnp.exp(lg - mg), 0.0), axis=-1, keepdims=True)
    lo_lane = N_GROUPS + EXPERTS_PER_GROUP * first(lg == mg)
    le = jnp.where((lane >= lo_lane) & (lane < lo_lane + EXPERTS_PER_GROUP), logits, off)
    v1 = rmax(le)
    i1 = first(le == v1)
    le = jnp.where(lane == i1, off, le)
    v2 = rmax(le)
    i2 = first(le == v2)
    e2 = jnp.exp(v2 - v1)
    w1 = wg / (1.0 + e2)
    route = jnp.where(lane == 0, w1, jnp.where(lane == 1, w1 * e2, jnp.where(
        lane == 2, i1 - N_GROUPS, jnp.where(lane == 3, i2 - N_GROUPS, 0.0))))
    lg_ref[...] = route


def _stage_mix(x2, yr, att, p2, w_o, ln_g, ln_b, router_g, router_g_b, router_e, router_e_b,
               ple_gate, ple_proj, alpha):
    N, D = x2.shape
    d_r = yr.shape[1]
    rows = min(MIX_ROWS, N)
    d_att = att.shape[1]
    n_log = N_GROUPS + N_EXPERTS
    rw = jnp.zeros((D, ROUTER_LANES), F32).at[:, :N_GROUPS].set(router_g)
    rw = rw.at[:, N_GROUPS:n_log].set(router_e)
    rb = jnp.zeros((1, ROUTER_LANES), F32).at[0, :N_GROUPS].set(router_g_b)
    rb = rb.at[0, N_GROUPS:n_log].set(router_e_b)
    rhi, rlo = _split(rw)
    const = lambda shape: pl.BlockSpec(shape, lambda i: (0,) * len(shape))
    tile = lambda c: pl.BlockSpec((rows, c), lambda i: (i, 0))
    return pl.pallas_call(
        functools.partial(_mix_kernel, alpha=alpha),
        grid=(N // rows,),
        in_specs=[tile(D), tile(d_r), tile(d_att), tile(p2.shape[1]),
                  const((d_r, D)), const((d_att, D)), const((1, D)), const((1, D)),
                  const((D, ROUTER_LANES)), const((D, ROUTER_LANES)), const((1, ROUTER_LANES)),
                  const((D, D)), const(ple_proj.shape)],
        out_specs=[pl.BlockSpec((rows * SUBLANES, LANES), lambda i: (i, 0)),
                   tile(ROUTER_LANES), tile(D)],
        out_shape=[jax.ShapeDtypeStruct((N * SUBLANES, LANES), F32),
                   jax.ShapeDtypeStruct((N, ROUTER_LANES), F32),
                   jax.ShapeDtypeStruct((N, D), BF16)],
        compiler_params=_params(("parallel",)),
        name="mix",
    )(x2, yr, att, p2, w_o[:d_r].astype(BF16), w_o[d_r:].astype(BF16),
      ln_g.reshape(1, D), ln_b.reshape(1, D), rhi, rlo, rb, ple_gate.astype(BF16),
      ple_proj.astype(BF16))


def _route(route, rows):
    N = route.shape[0]
    we = route[:, :TOP_K]
    eid = route[:, TOP_K:2 * TOP_K].astype(jnp.int32)
    A = N * TOP_K
    e_flat = eid.T.reshape(A).astype(jnp.int32)
    assert N_EXPERTS * A < 2 ** 31
    order = jnp.sort(e_flat * A + jnp.arange(A, dtype=jnp.int32)) % A
    counts = jnp.sum((e_flat[None, :] == jnp.arange(N_EXPERTS)[:, None]).astype(jnp.int32), axis=1)
    start = jnp.cumsum(counts) - counts
    padded = (counts + rows - 1) // rows * rows
    pend = jnp.cumsum(padded)
    pstart = pend - padded
    nblk = -(-A // rows) + N_EXPERTS
    nused = (pend[-1] // rows).astype(jnp.int32)
    blk = jnp.arange(nblk, dtype=jnp.int32)
    blk_e = jnp.sum((pend[None, :] <= (blk * rows)[:, None]).astype(jnp.int32), axis=1)
    blk_e = jnp.minimum(blk_e, N_EXPERTS - 1)
    blk_e = jnp.where(blk < nused, blk_e, blk_e[nused - 1]).astype(jnp.int32)
    r = jnp.arange(rows, dtype=jnp.int32)[None, :]
    mine = blk_e[:, None] == jnp.arange(N_EXPERTS)[None, :]
    of_blk = lambda t: jnp.sum(jnp.where(mine, t[None, :], 0), axis=1)
    off = (blk * rows - of_blk(pstart))[:, None] + r
    real = (off < of_blk(counts)[:, None]) & (blk < nused)[:, None]
    asg = order[jnp.clip(of_blk(start)[:, None] + off, 0, A - 1)]
    row_tok = jnp.where(real, asg % N, r).astype(jnp.int32)
    row_dst = jnp.where(real, asg, A + (blk % 2)[:, None] * rows + r).astype(jnp.int32)
    return (row_tok.reshape(nblk, 1, rows), row_dst.reshape(nblk, 1, rows), blk_e,
            nused.reshape(1), we)


def _moe_kernel(blk_e_ref, nused_ref, tok0_ref, tokn_ref, dstp_ref, x_hbm, wg_ref, wu_ref, wd_ref,
                y_hbm, xbuf, ybuf, gsem, ssem, *, rows, n_real):
    i = pl.program_id(0)
    nused = nused_ref[0]
    slot = i % 2
    tile_rows = lambda first: pl.ds(pl.multiple_of(first, SUBLANES), SUBLANES)

    def gather_row(tab_ref, sl, r):
        return pltpu.make_async_copy(x_hbm.at[tile_rows(tab_ref[0, 0, r])],
                                     xbuf.at[sl, pl.ds(r * SUBLANES, SUBLANES)], gsem.at[sl])

    def scatter_row(sl, r):
        return pltpu.make_async_copy(ybuf.at[sl, pl.ds(r * SUBLANES, SUBLANES)],
                                     y_hbm.at[tile_rows(dstp_ref[0, 0, r])], ssem.at[sl])

    def gather_all(sl):
        return pltpu.make_async_copy(x_hbm.at[pl.ds(0, rows * SUBLANES)], xbuf.at[sl],
                                     gsem.at[sl])

    def scatter_all(sl, first_row=0):
        return pltpu.make_async_copy(
            ybuf.at[sl], y_hbm.at[pl.ds(first_row * SUBLANES, rows * SUBLANES)], ssem.at[sl])

    def rolled(fn):
        def body(r, c):
            fn(r)
            return c
        lax.fori_loop(0, rows, body, 0, unroll=8)

    def inline(fn):
        for r in range(rows):
            fn(r)

    def ffn(sl):
        xb = _load_token_tiles(xbuf.at[sl]).astype(BF16)
        gate = _dot(xb, wg_ref[0])
        up = _dot(xb, wu_ref[0])
        _store_token_tiles(ybuf.at[sl], _dot(gate * _sigmoid(gate) * up, wd_ref[0]))

    @pl.when(i == 0)
    def _():
        ybuf[...] = jnp.zeros_like(ybuf)
        for sl in range(2):
            scatter_all(sl, n_real + sl * rows).start()
        for sl in range(2):
            scatter_all(sl, n_real + sl * rows).wait()
        rolled(lambda r: gather_row(tok0_ref, 0, r).start())

    @pl.when(i < nused)
    def _():
        gather_all(slot).wait()

    @pl.when((i >= 2) & (i <= nused))
    def _():
        scatter_all(slot).wait()

    @pl.when(i == 0)
    def _():
        inline(lambda r: gather_row(tokn_ref, 1, r).start())
        ffn(0)

    for s in range(2):
        @pl.when((i >= 1) & (i < nused) & (slot == s))
        def _():
            inline(lambda r: gather_row(tokn_ref, 1 - s, r).start(priority=r % 2))
            inline(lambda r: scatter_row(1 - s, r).start(priority=r % 2))
            ffn(s)

    @pl.when(i == nused)
    def _():
        rolled(lambda r: scatter_row(1 - slot, r).start())
        scatter_all(1 - slot).wait()
        gather_all(slot).wait()


def _stage_moe(x1t, row_tok, row_dst, blk_e, nused, w_gate, w_up, w_down):
    N = x1t.shape[0] // SUBLANES
    D = SUBLANES * LANES
    nblk, _, rows = row_tok.shape
    d_e = w_gate.shape[2]
    n_real = N * TOP_K
    last = nblk - 1
    wspec = lambda shape: pl.BlockSpec((1,) + shape,
                                       lambda i, be, nu: (be[jnp.minimum(i, last)], 0, 0))
    table = lambda fn: pl.BlockSpec((1, 1, rows), lambda i, be, nu: (fn(i), 0, 0),
                                    memory_space=pltpu.SMEM)
    grid_spec = pltpu.PrefetchScalarGridSpec(
        num_scalar_prefetch=2,
        grid=(nblk + 1,),
        in_specs=[table(lambda i: 0), table(lambda i: jnp.minimum(i + 1, last)),
                  table(lambda i: jnp.clip(i - 1, 0, last)),
                  pl.BlockSpec(memory_space=pl.ANY), wspec((D, d_e)), wspec((D, d_e)),
                  wspec((d_e, D))],
        out_specs=pl.BlockSpec(memory_space=pl.ANY),
        scratch_shapes=[pltpu.VMEM((2, rows * SUBLANES, LANES), F32),
                        pltpu.VMEM((2, rows * SUBLANES, LANES), F32),
                        pltpu.SemaphoreType.DMA((2,)), pltpu.SemaphoreType.DMA((2,))],
    )
    return pl.pallas_call(
        functools.partial(_moe_kernel, rows=rows, n_real=n_real),
        grid_spec=grid_spec,
        out_shape=jax.ShapeDtypeStruct(((n_real + 2 * rows) * SUBLANES, LANES), F32),
        compiler_params=_params(("arbitrary",)),
        name="moe",
    )(blk_e, nused, row_tok * SUBLANES, row_tok * SUBLANES, row_dst * SUBLANES, x1t, w_gate, w_up,
      w_down)


def _out_kernel(x1t_ref, y0_ref, y1_ref, we_ref, ple_ref, g_ref, b_ref, o_ref, *, alpha):
    we = we_ref[...]
    moe = we[:, 0:1] * _load_token_tiles(y0_ref) + we[:, 1:2] * _load_token_tiles(y1_ref)
    o_ref[...] = _layer_norm(alpha * _load_token_tiles(x1t_ref) + moe + ple_ref[...], g_ref[...],
                             b_ref[...])


def _stage_out(x1t, yb, we, ple, ln_g, ln_b, alpha):
    N = x1t.shape[0] // SUBLANES
    D = SUBLANES * LANES
    rows = min(OUT_ROWS, N)
    const = lambda shape: pl.BlockSpec(shape, lambda i: (0,) * len(shape))
    tile = lambda c: pl.BlockSpec((rows, c), lambda i: (i, 0))
    slot = lambda j: pl.BlockSpec((rows * SUBLANES, LANES), lambda i: (i + j * (N // rows), 0))
    return pl.pallas_call(
        functools.partial(_out_kernel, alpha=alpha),
        grid=(N // rows,),
        in_specs=[slot(0), slot(0), slot(1), tile(TOP_K), tile(D), const((1, D)), const((1, D))],
        out_specs=tile(D),
        out_shape=jax.ShapeDtypeStruct((N, D), F32),
        compiler_params=_params(("parallel",)),
        name="out",
    )(x1t, yb, yb, we, ple, ln_g.reshape(1, D), ln_b.reshape(1, D))


def kernel(x, p, w_in, mu_rkv, mu_lora, w0, w_lora1, w_lora2, a0, a_lora1, a_lora2, g_lora1, g_lora2, k_k, k_a, r_k, lnx_g, lnx_b, rel_bias, w_o, ln1_g, ln1_b, router_g, router_g_b, router_e, router_e_b, w_gate, w_up, w_down, ple_gate, ple_proj, ln2_g, ln2_b):
    B, S, D = x.shape
    depth = w_in.shape[0]
    alpha = (2 * depth) ** 0.25
    for i in range(depth):
        rkv, qkv, lw, ai, g = _stage_proj(x, w_in[i], w_lora1[i], a_lora1[i], g_lora1[i],
                                          mu_lora[i], w_lora2[i], a_lora2[i], g_lora2[i],
                                          w0[i], a0[i])
        yr = _stage_wkv(rkv, lw, ai, g, mu_rkv[i], k_k[i], k_a[i], r_k[i], lnx_g[i], lnx_b[i])
        att = _stage_attn(qkv, rel_bias).reshape(B * S, -1)
        x1t, route, ple = _stage_mix(x.reshape(B * S, D), yr.reshape(B * S, -1), att,
                                      p[i].reshape(B * S, -1), w_o[i], ln1_g[i], ln1_b[i],
                                      router_g[i], router_g_b[i], router_e[i], router_e_b[i],
                                      ple_gate[i], ple_proj[i], alpha)
        row_tok, row_dst, blk_e, nused, we = _route(route, MOE_ROWS)
        yb = _stage_moe(x1t, row_tok, row_dst, blk_e, nused, w_gate[i], w_up[i], w_down[i])
        x = _stage_out(x1t, yb, we, ple, ln2_g[i], ln2_b[i], alpha).reshape(B, S, D)
    return x
```

```python
import functools
import math

import jax
import jax.numpy as jnp
import numpy as np
from jax import lax
from jax.experimental import pallas as pl
from jax.experimental.pallas import tpu as pltpu

F32 = jnp.float32
BF16 = jnp.bfloat16

HEAD_DIM = 64
LANES = 128
SUBLANES = 8
MXU_DEPTH = 256
LNX_EPS = 64e-5
DIL_PATTERNS = ((128, 1), (512, 4), (2048, 16))
NEG_INF = -1e30
N_BUCKETS = 32
MAX_DISTANCE = 2048
N_GROUPS = 4
EXPERTS_PER_GROUP = 8
N_EXPERTS = N_GROUPS * EXPERTS_PER_GROUP
TOP_K = 2
LN_EPS = 1e-5

WKV_CHUNK = 64
WKV_CHUNKS_PER_STEP = 4
ATT_BLK = 128
ATT_UNITS = 16
PROJ_ROWS = 512
MIX_ROWS = 512
MOE_ROWS = 256
OUT_ROWS = 1024
ROUTER_LANES = 128
VMEM_LIMIT = 56 * 1024 * 1024


def _params(sem):
    return pltpu.CompilerParams(dimension_semantics=sem, vmem_limit_bytes=VMEM_LIMIT)


def _dot(a, b):
    return jnp.dot(a.astype(BF16), b.astype(BF16), preferred_element_type=F32)


def _dot_nt(a, b):
    return lax.dot_general(a.astype(BF16), b.astype(BF16), (((1,), (1,)), ((), ())),
                           preferred_element_type=F32)


def _dot_tn(a, b):
    return lax.dot_general(a.astype(BF16), b.astype(BF16), (((0,), (0,)), ((), ())),
                           preferred_element_type=F32)


def _split(x):
    hi = x.astype(BF16)
    lo = (x - hi.astype(F32)).astype(BF16)
    return hi, lo


def _dot_exact_lhs(w01, x):
    hi, lo = _split(x)
    return (jnp.dot(w01, hi, preferred_element_type=F32)
            + jnp.dot(w01, lo, preferred_element_type=F32))


def _sigmoid(x):
    return 1.0 / (1.0 + jnp.exp(-x))


def _shift_rows(t, prev_row):
    rolled = pltpu.roll(t, 1, 0)
    row = lax.broadcasted_iota(jnp.int32, t.shape, 0)
    return jnp.where(row == 0, prev_row, rolled)


def _store_token_tiles(ref, t):
    rows = t.shape[0]
    for s in range(SUBLANES):
        ref[pl.ds(s, rows, stride=SUBLANES), :] = t[:, s * LANES:(s + 1) * LANES]


def _load_token_tiles(ref):
    rows = ref.shape[0] // SUBLANES
    return jnp.concatenate([ref[pl.ds(s, rows, stride=SUBLANES), :] for s in range(SUBLANES)],
                           axis=1)


def _layer_norm(z, g, b):
    mu = jnp.mean(z, axis=-1, keepdims=True)
    zc = z - mu
    var = jnp.mean(zc * zc, axis=-1, keepdims=True)
    return zc * lax.rsqrt(var + LN_EPS) * g + b


def _proj_kernel(x_ref, win_ref, w1_ref, a1_ref, g1_ref, mul_ref, w2_ref, a2_ref, g2_ref,
                 w0_ref, a0_ref, rkv_ref, qkv_ref, lw_ref, ai_ref, g_ref, prev_ref, *, d_rkv):
    @pl.when(pl.program_id(1) == 0)
    def _():
        prev_ref[...] = jnp.zeros_like(prev_ref)

    h = x_ref[0]
    rows = h.shape[0]
    hprev = _shift_rows(h, prev_ref[0:1, :])
    prev_ref[0:1, :] = h[rows - 1:rows, :]

    dh = hprev - h
    hb = h.astype(BF16)
    low_w = _dot(h + dh * mul_ref[0:1, :], w1_ref[...])
    low_a = _dot(h + dh * mul_ref[1:2, :], a1_ref[...])
    low_g = _dot(h + dh * mul_ref[2:3, :], g1_ref[...])
    rkv_ref[0] = _dot(hb, win_ref[:, :d_rkv])
    wl = w0_ref[...] + _dot(jnp.tanh(low_w), w2_ref[...])
    al = a0_ref[...] + _dot(low_a, a2_ref[...])
    g_ref[0] = _dot(_sigmoid(low_g), g2_ref[...]).astype(g_ref.dtype)
    qkv_ref[0] = _dot(hb, win_ref[:, d_rkv:])
    z = -wl
    softplus = jnp.maximum(z, 0.0) + jnp.log(1.0 + jnp.exp(-jnp.abs(z)))
    lw_ref[0] = -jnp.exp(-softplus - 0.5)
    ai_ref[0] = _sigmoid(al)


def _stage_proj(x, w_in, w1, a1, g1, mu_lora, w2, a2, g2, w0, a0):
    B, S, D = x.shape
    d_in = w_in.shape[1]
    d_r = w2.shape[1]
    d_rkv = 3 * d_r
    d_att = d_in - d_rkv
    rows = min(PROJ_ROWS, S)
    const = lambda shape: pl.BlockSpec(shape, lambda b, j: (0,) * len(shape))
    tile = lambda c: pl.BlockSpec((1, rows, c), lambda b, j: (b, j, 0))
    return pl.pallas_call(
        functools.partial(_proj_kernel, d_rkv=d_rkv),
        grid=(B, S // rows),
        in_specs=[tile(D), const((D, d_in)), const(w1.shape), const(a1.shape), const(g1.shape),
                  const(mu_lora.shape), const(w2.shape), const(a2.shape), const(g2.shape),
                  const((1, d_r)), const((1, d_r))],
        out_specs=[tile(d_rkv), tile(d_att), tile(d_r), tile(d_r), tile(d_r)],
        out_shape=[jax.ShapeDtypeStruct((B, S, d_rkv), F32),
                   jax.ShapeDtypeStruct((B, S, d_att), F32),
                   jax.ShapeDtypeStruct((B, S, d_r), F32),
                   jax.ShapeDtypeStruct((B, S, d_r), F32),
                   jax.ShapeDtypeStruct((B, S, d_r), BF16)],
        scratch_shapes=[pltpu.VMEM((SUBLANES, D), F32)],
        compiler_params=_params(("parallel", "arbitrary")),
        name="proj",
    )(x, w_in.astype(BF16), w1.astype(BF16), a1.astype(BF16), g1.astype(BF16), mu_lora,
      w2.astype(BF16), a2.astype(BF16), g2.astype(BF16), w0.reshape(1, d_r), a0.reshape(1, d_r))


def _wkv_kernel(rkv_ref, lw_ref, ai_ref, g_ref, mu_ref, kk_ref, ka_ref, rk_ref, lng_ref, lnb_ref,
                seg_ref, tril_ref, y_ref, s_ref, prev_ref, *, d_r, n_chunks):
    C = WKV_CHUNK
    rows = C * n_chunks

    @pl.when(pl.program_id(1) == 0)
    def _():
        s_ref[...] = jnp.zeros_like(s_ref)
        prev_ref[...] = jnp.zeros_like(prev_ref)

    rkv = rkv_ref[0]
    prev = _shift_rows(rkv, prev_ref[0:1, :])
    prev_ref[0:1, :] = rkv[rows - 1:rows, :]
    mixed = rkv + (prev - rkv) * mu_ref[...]
    r = mixed[:, :d_r]
    k = mixed[:, d_r:2 * d_r]
    v = mixed[:, 2 * d_r:]
    lw = lw_ref[0]
    a = ai_ref[0]
    seg = seg_ref[...]
    seg_w = seg.shape[0]

    def head_sum(t):
        return jnp.concatenate([_dot(t[:, j:j + seg_w], seg) for j in range(0, d_r, seg_w)],
                               axis=1)

    kk = k * kk_ref[...]
    kk = kk * lax.rsqrt(jnp.maximum(head_sum(kk * kk), 1e-24))
    k = k * (1.0 + (a - 1.0) * ka_ref[...])
    aa = -kk
    bb = kk * a
    c = _dot_exact_lhs(tril_ref[...], lw)
    e_neg = jnp.exp(-c)
    a_t = aa * jnp.exp(c - lw)
    b_t = bb * e_neg
    k_t = k * e_neg
    r_t = r * jnp.exp(c)

    lane = lax.broadcasted_iota(jnp.int32, (1, LANES), 1)
    head0 = lane < HEAD_DIM
    ri = lax.broadcasted_iota(jnp.int32, (2 * C, 2 * C), 0)
    ci = lax.broadcasted_iota(jnp.int32, (2 * C, 2 * C), 1)
    strict = (ri % C) > (ci % C)
    incl = (ri % C) >= (ci % C)
    eye = (ri == ci).astype(F32)
    zero_bf = jnp.zeros((2 * C, LANES), BF16)

    def stack(t):
        return jnp.concatenate([jnp.where(head0, t, 0.0), jnp.where(head0, 0.0, t)],
                               axis=0).astype(BF16)

    def fold(t):
        return t[:C] + t[C:]

    n_pairs = d_r // LANES
    folds = [(q, p) for q in range(n_chunks) for p in range(n_pairs)]
    each = lambda fn, *cols: [fn(*args) for args in zip(*cols)]
    cat0 = lambda *ts: jnp.concatenate(ts, axis=0)
    cat1 = lambda *ts: jnp.concatenate(ts, axis=1)
    left = lambda t: t[:, :2 * C]
    right = lambda t: t[:, 2 * C:]

    p_ends, b_es, k_es = [], [], []
    for q in range(n_chunks):
        rs = slice(q * C, (q + 1) * C)
        c_end = c[rs][C - 1:C, :]
        e_end = jnp.exp(c_end - c[rs])
        b_es.append(bb[rs] * e_end)
        k_es.append(k[rs] * e_end)
        p_ends.append(jnp.exp(c_end))

    def stacks(t_of_q):
        return [stack(t_of_q(q)[:, p * LANES:(p + 1) * LANES]) for q, p in folds]

    chunk = lambda t: (lambda q: t[q * C:(q + 1) * C])
    am, bm, km, rm, vm = (stacks(chunk(t)) for t in (a_t, b_t, k_t, r_t, v))
    bem = stacks(lambda q: b_es[q])
    kem = stacks(lambda q: k_es[q])

    x = each(lambda a_, r_, b_, k_: _dot_nt(cat0(a_, r_), cat0(b_, k_)), am, rm, bm, km)
    n = each(lambda t: jnp.where(strict, t[:2 * C, :2 * C], 0.0), x)
    a_ak = each(lambda t: jnp.where(strict, t[:2 * C, 2 * C:], 0.0).astype(BF16), x)
    a_rb = each(lambda t: jnp.where(incl, t[2 * C:, :2 * C], 0.0).astype(BF16), x)
    a_rk = each(lambda t: jnp.where(incl, t[2 * C:, 2 * C:], 0.0).astype(BF16), x)
    t_inv = each(lambda t: eye + t, n)
    m = each(lambda t: _dot(t, t), n)
    av = each(lambda a_, v_: _dot(a_, v_).astype(BF16), a_ak, vm)
    for _ in range(int(math.log2(C)) - 2):
        z = each(lambda m_, t_: _dot(m_, cat1(m_.astype(BF16), t_.astype(BF16))), m, t_inv)
        t_inv = each(lambda t_, z_: t_ + right(z_), t_inv, z)
        m = each(left, z)
    t_inv = each(lambda t_, m_: t_ + _dot(m_, t_), t_inv, m)
    z = each(lambda t_, a_, av_: _dot(t_, cat1(a_, av_)), t_inv, am, av)
    ap = each(lambda z_: left(z_).astype(BF16), z)
    u0 = each(lambda z_: right(z_).astype(BF16), z)
    z = each(lambda rb_, rk_, ap_, u_, v_: _dot(cat1(rb_, rk_),
                                               cat0(cat1(ap_, u_), cat1(zero_bf, v_))),
             a_rb, a_rk, ap, u0, vm)
    g_m = each(lambda ap_, be_: _dot_tn(ap_, be_).astype(BF16), ap, bem)
    d0 = each(lambda u_, v_, be_, ke_: _dot_tn(cat0(u_, v_), cat0(be_, ke_)), u0, vm, bem, kem)
    rp_t = each(lambda r_, z_: jnp.transpose(r_.astype(F32) + left(z_)).astype(BF16), rm, z)
    y0 = each(lambda z_: fold(right(z_)), z)

    s = [s_ref[p] for p in range(n_pairs)]
    y_cols = []
    for q in range(n_chunks):
        f0 = q * n_pairs
        zs = [_dot(s[p], cat1(g_m[f0 + p], rp_t[f0 + p])) for p in range(n_pairs)]
        y_cols.append([fold(jnp.transpose(right(zs[p]))) + y0[f0 + p] for p in range(n_pairs)])
        s = [s[p] * p_ends[q][:, p * LANES:(p + 1) * LANES] + left(zs[p]) + d0[f0 + p]
             for p in range(n_pairs)]
    for p in range(n_pairs):
        s_ref[p] = s[p]
    y = cat0(*[cat1(*cols) for cols in y_cols])

    inv = 1.0 / HEAD_DIM
    mu = head_sum(y) * inv
    yc = y - mu
    var = head_sum(yc * yc) * inv
    yn = yc * lax.rsqrt(var + LNX_EPS) * lng_ref[...] + lnb_ref[...]
    bonus = head_sum(r * k * rk_ref[...]) * v
    y_ref[0] = ((yn + bonus) * g_ref[0]).astype(y_ref.dtype)


def _stage_wkv(rkv, lw, ai, g, mu_rkv, k_k, k_a, r_k, lnx_g, lnx_b):
    B, S, d_r = lw.shape
    n_chunks = WKV_CHUNKS_PER_STEP
    rows = WKV_CHUNK * n_chunks
    assert S % rows == 0 and d_r % MXU_DEPTH == 0
    head = np.arange(MXU_DEPTH) // HEAD_DIM
    seg = jnp.asarray(head[:, None] == head[None, :], BF16)
    t = np.arange(rows)
    tril = jnp.asarray((t[:, None] >= t[None, :])
                       & (t[:, None] // WKV_CHUNK == t[None, :] // WKV_CHUNK), BF16)
    const = lambda shape: pl.BlockSpec(shape, lambda b, j: (0,) * len(shape))
    tile = lambda c: pl.BlockSpec((1, rows, c), lambda b, j: (b, j, 0))
    row = lambda t: t.reshape(1, -1)
    return pl.pallas_call(
        functools.partial(_wkv_kernel, d_r=d_r, n_chunks=n_chunks),
        grid=(B, S // rows),
        in_specs=[tile(3 * d_r), tile(d_r), tile(d_r), tile(d_r), const((1, 3 * d_r)),
                  const((1, d_r)), const((1, d_r)), const((1, d_r)), const((1, d_r)),
                  const((1, d_r)), const((MXU_DEPTH, MXU_DEPTH)), const((rows, rows))],
        out_specs=tile(d_r),
        out_shape=jax.ShapeDtypeStruct((B, S, d_r), BF16),
        scratch_shapes=[pltpu.VMEM((d_r // LANES, LANES, LANES), F32),
                        pltpu.VMEM((SUBLANES, 3 * d_r), F32)],
        compiler_params=_params(("parallel", "arbitrary")),
        name="wkv",
    )(rkv, lw, ai, g, row(mu_rkv), row(k_k), row(k_a), row(r_k), row(lnx_g), row(lnx_b), seg, tril)


def _t5_bucket(n):
    exact = N_BUCKETS // 2
    nf = np.maximum(n, 1).astype(np.float32)
    large = exact + (np.log(nf / exact) / math.log(MAX_DISTANCE / exact)
                     * (N_BUCKETS - exact)).astype(np.int32)
    large = np.minimum(large, N_BUCKETS - 1)
    return np.where(n < exact, n, large).astype(np.int32)


def _band_bias(rel_bias, dil):
    blk = ATT_BLK
    L = 3 * blk
    rel = (2 * blk - 1) - np.arange(L)
    valid = (rel >= 0) & (rel <= blk)
    base = rel_bias[_t5_bucket(np.clip(rel, 0, None) * dil)].astype(F32).T
    base = jnp.where(valid[None], base, NEG_INF)
    skew = jnp.tile(base, (1, blk))[:, :blk * (L - 1)].reshape(-1, blk, L - 1)
    return skew[:, :, blk - 1:3 * blk - 1]


def _attend(units, bias_ref, scale):
    lane = lax.broadcasted_iota(jnp.int32, (1, LANES), 1)
    head0 = lane < HEAD_DIM
    jobs = [(u, h) for u in range(len(units)) for h in range(2)]
    s = []
    for u, h in jobs:
        q, k, _, col0 = units[u]
        qh = jnp.where(head0 if h == 0 else jnp.logical_not(head0), q * scale, 0.0)
        s.append(_dot_nt(qh, k) + bias_ref[h, :, col0:col0 + k.shape[0]])
    m = [jnp.max(t, axis=-1, keepdims=True) for t in s]
    e = [jnp.exp2(t - m_) for t, m_ in zip(s, m)]
    ol = [_dot(t, jnp.where(head0 if h == 0 else jnp.logical_not(head0), units[u][2], 1.0))
          for t, (u, h) in zip(e, jobs)]
    both = lambda t, u: jnp.where(head0, t[2 * u], t[2 * u + 1])
    swapped = lambda u: jnp.where(head0, ol[2 * u + 1], ol[2 * u])
    return [(both(ol, u), both(m, u), pltpu.roll(swapped(u), HEAD_DIM, 1))
            for u in range(len(units))]


def _attn_kernel(q_ref, k_ref, v_ref, b1_ref, b4_ref, b16_ref, o_ref, acc_o, acc_m, acc_l, *, seq,
                 scale):
    blk = ATT_BLK
    group = ATT_UNITS

    def rows(ref, start, n, dil):
        if dil == 1:
            return ref[0, pl.ds(start, n), :]
        return ref[0, pl.ds(start, n, stride=dil), :]

    def unit(start, dil, has_prev):
        q = rows(q_ref, start, blk, dil)
        if has_prev:
            k0 = start - blk * dil
            return (q, rows(k_ref, k0, 2 * blk, dil), rows(v_ref, k0, 2 * blk, dil), 0)
        return (q, rows(k_ref, start, blk, dil), rows(v_ref, start, blk, dil), blk)

    def acc_rows(ref, start, dil):
        if dil == 1:
            return ref.at[pl.ds(start, blk), :]
        return ref.at[pl.ds(start, blk, stride=dil), :]

    def run(starts, dil, prevs, bias_ref, first, last):
        units = [unit(s, dil, hp) for s, hp in zip(starts, prevs)]
        for s, (o, m, l) in zip(starts, _attend(units, bias_ref, scale)):
            ao, am, al = (acc_rows(ref, s, dil) for ref in (acc_o, acc_m, acc_l))
            if first:
                ao[...] = o
                am[...] = m
                al[...] = l
            else:
                m_old = am[...]
                m_new = jnp.maximum(m_old, m)
                w_old, w_new = jnp.exp2(m_old - m_new), jnp.exp2(m - m_new)
                ao[...] = w_old * ao[...] + w_new * o
                al[...] = w_old * al[...] + w_new * l
                if not last:
                    am[...] = m_new

    run([n * blk for n in range(group)], 1, [n > 0 for n in range(group)], b1_ref, True, False)

    def body1(g, c):
        base = pl.multiple_of(g * (group * blk), group * blk)
        run([base + n * blk for n in range(group)], 1, [True] * group, b1_ref, True, False)
        return c
    if seq // (group * blk) > 1:
        lax.fori_loop(1, seq // (group * blk), body1, 0)

    nb4 = seq // 4 // blk

    per4 = group // nb4

    def body4(g, c):
        rs = [g * per4 + j for j in range(per4)]
        run([r + n * blk * 4 for r in rs for n in range(nb4)], 4,
            [n > 0 for r in rs for n in range(nb4)], b4_ref, False, False)
        return c
    lax.fori_loop(0, 4 // per4, body4, 0)

    def body16(g, c):
        run([g * group + j for j in range(group)], 16, [False] * group, b16_ref, False, True)
        return c
    lax.fori_loop(0, 16 // group, body16, 0)

    o_ref[0] = (acc_o[...] / acc_l[...]).astype(o_ref.dtype)


def _stage_attn(qkv, rel_bias):
    B, S, d3 = qkv.shape
    d_att = d3 // 3
    n_pairs = d_att // LANES
    assert DIL_PATTERNS == ((ATT_BLK, 1), (4 * ATT_BLK, 4), (16 * ATT_BLK, 16))
    assert S == 16 * ATT_BLK
    log2e = math.log2(math.e)
    biases = [_band_bias(rel_bias, dil) * log2e for _, dil in DIL_PATTERNS]
    part = lambda i: pl.BlockSpec((1, S, LANES), lambda b, p: (b, 0, i * n_pairs + p))
    bias_spec = pl.BlockSpec((2, ATT_BLK, 2 * ATT_BLK), lambda b, p: (p, 0, 0))
    return pl.pallas_call(
        functools.partial(_attn_kernel, seq=S, scale=HEAD_DIM ** -0.5 * log2e),
        grid=(B, n_pairs),
        in_specs=[part(0), part(1), part(2), bias_spec, bias_spec, bias_spec],
        out_specs=pl.BlockSpec((1, S, LANES), lambda b, p: (b, 0, p)),
        out_shape=jax.ShapeDtypeStruct((B, S, d_att), BF16),
        scratch_shapes=[pltpu.VMEM((S, LANES), F32)] * 3,
        compiler_params=_params(("parallel", "parallel")),
        name="attn",
    )(qkv, qkv, qkv, *biases)


def _mix_kernel(x_ref, yr_ref, att_ref, p_ref, wor_ref, woa_ref, g_ref, b_ref, rhi_ref, rlo_ref,
                rb_ref, pg_ref, pp_ref, x1t_ref, lg_ref, ple_ref, cnt_ref, *, alpha):
    mix = _dot(yr_ref[...], wor_ref[...]) + _dot(att_ref[...], woa_ref[...])
    emb = _dot(p_ref[...], pp_ref[...])
    x1 = _layer_norm(alpha * x_ref[...] + mix, g_ref[...], b_ref[...])
    _store_token_tiles(x1t_ref, x1)
    hi, lo = _split(x1)
    logits = (jnp.dot(hi, rhi_ref[...], preferred_element_type=F32)
              + jnp.dot(lo, rhi_ref[...], preferred_element_type=F32)
              + jnp.dot(hi, rlo_ref[...], preferred_element_type=F32) + rb_ref[...])
    ple_ref[...] = (_sigmoid(_dot(hi, pg_ref[...])) * emb).astype(ple_ref.dtype)

    lane = lax.broadcasted_iota(jnp.int32, logits.shape, 1).astype(F32)
    big = float(ROUTER_LANES)
    rmax = lambda t: jnp.max(t, axis=-1, keepdims=True)
    first = lambda hit: jnp.min(jnp.where(hit, lane, big), axis=-1, keepdims=True)
    off = float("-inf")
    is_grp = lane < N_GROUPS
    lg = jnp.where(is_grp, logits, off)
    mg = rmax(lg)
    wg = 1.0 / jnp.sum(jnp.where(is_grp, jnp.exp(lg - mg), 0.0), axis=-1, keepdims=True)
    lo_lane = N_GROUPS + EXPERTS_PER_GROUP * first(lg == mg)
    le = jnp.where((lane >= lo_lane) & (lane < lo_lane + EXPERTS_PER_GROUP), logits, off)
    v1 = rmax(le)
    i1 = first(le == v1)
    le = jnp.where(lane == i1, off, le)
    v2 = rmax(le)
    i2 = first(le == v2)
    e2 = jnp.exp(v2 - v1)
    w1 = wg / (1.0 + e2)
    route = jnp.where(lane == 0, w1, jnp.where(lane == 1, w1 * e2, jnp.where(
        lane == 2, i1 - N_GROUPS, jnp.where(lane == 3, i2 - N_GROUPS, 0.0))))
    lg_ref[...] = route

    @pl.when(pl.program_id(0) == 0)
    def _():
        cnt_ref[...] = jnp.zeros_like(cnt_ref)
    chosen = jnp.where((lane == i1) | (lane == i2), 1.0, 0.0)
    cnt_ref[0:1, :] += jnp.sum(chosen, axis=0, keepdims=True)


def _stage_mix(x2, yr, att, p2, w_o, ln_g, ln_b, router_g, router_g_b, router_e, router_e_b,
               ple_gate, ple_proj, alpha):
    N, D = x2.shape
    d_r = yr.shape[1]
    rows = min(MIX_ROWS, N)
    d_att = att.shape[1]
    n_log = N_GROUPS + N_EXPERTS
    rw = jnp.zeros((D, ROUTER_LANES), F32).at[:, :N_GROUPS].set(router_g)
    rw = rw.at[:, N_GROUPS:n_log].set(router_e)
    rb = jnp.zeros((1, ROUTER_LANES), F32).at[0, :N_GROUPS].set(router_g_b)
    rb = rb.at[0, N_GROUPS:n_log].set(router_e_b)
    rhi, rlo = _split(rw)
    const = lambda shape: pl.BlockSpec(shape, lambda i: (0,) * len(shape))
    tile = lambda c: pl.BlockSpec((rows, c), lambda i: (i, 0))
    return pl.pallas_call(
        functools.partial(_mix_kernel, alpha=alpha),
        grid=(N // rows,),
        in_specs=[tile(D), tile(d_r), tile(d_att), tile(p2.shape[1]),
                  const((d_r, D)), const((d_att, D)), const((1, D)), const((1, D)),
                  const((D, ROUTER_LANES)), const((D, ROUTER_LANES)), const((1, ROUTER_LANES)),
                  const((D, D)), const(ple_proj.shape)],
        out_specs=[pl.BlockSpec((rows * SUBLANES, LANES), lambda i: (i, 0)),
                   tile(ROUTER_LANES), tile(D), const((SUBLANES, ROUTER_LANES))],
        out_shape=[jax.ShapeDtypeStruct((N * SUBLANES, LANES), F32),
                   jax.ShapeDtypeStruct((N, ROUTER_LANES), F32),
                   jax.ShapeDtypeStruct((N, D), BF16),
                   jax.ShapeDtypeStruct((SUBLANES, ROUTER_LANES), F32)],
        compiler_params=_params(("arbitrary",)),
        name="mix",
    )(x2, yr, att, p2, w_o[:d_r].astype(BF16), w_o[d_r:].astype(BF16),
      ln_g.reshape(1, D), ln_b.reshape(1, D), rhi, rlo, rb, ple_gate.astype(BF16),
      ple_proj.astype(BF16))


def _route(route, lane_counts, rows):
    N = route.shape[0]
    we = route[:, :TOP_K]
    eid = route[:, TOP_K:2 * TOP_K].astype(jnp.int32)
    A = N * TOP_K
    e_flat = eid.T.reshape(A).astype(jnp.int32)
    assert N_EXPERTS * A < 2 ** 31
    order = jnp.sort(e_flat * A + jnp.arange(A, dtype=jnp.int32)) % A
    counts = jnp.round(lane_counts[0, N_GROUPS:N_GROUPS + N_EXPERTS]).astype(jnp.int32)
    start = jnp.cumsum(counts) - counts
    padded = (counts + rows - 1) // rows * rows
    pend = jnp.cumsum(padded)
    pstart = pend - padded
    nblk = -(-A // rows) + N_EXPERTS
    nused = (pend[-1] // rows).astype(jnp.int32)
    blk = jnp.arange(nblk, dtype=jnp.int32)
    blk_e = jnp.sum((pend[None, :] <= (blk * rows)[:, None]).astype(jnp.int32), axis=1)
    blk_e = jnp.minimum(blk_e, N_EXPERTS - 1)
    blk_e = jnp.where(blk < nused, blk_e, blk_e[nused - 1]).astype(jnp.int32)
    r = jnp.arange(rows, dtype=jnp.int32)[None, :]
    mine = blk_e[:, None] == jnp.arange(N_EXPERTS)[None, :]
    of_blk = lambda t: jnp.sum(jnp.where(mine, t[None, :], 0), axis=1)
    off = (blk * rows - of_blk(pstart))[:, None] + r
    real = (off < of_blk(counts)[:, None]) & (blk < nused)[:, None]
    asg = order[jnp.clip(of_blk(start)[:, None] + off, 0, A - 1)]
    row_tok = jnp.where(real, asg % N, r).astype(jnp.int32)
    row_dst = jnp.where(real, asg, A + (blk % 2)[:, None] * rows + r).astype(jnp.int32)
    return (row_tok.reshape(nblk, 1, rows), row_dst.reshape(nblk, 1, rows), blk_e,
            nused.reshape(1), we)


def _moe_kernel(blk_e_ref, nused_ref, tok0_ref, tokn_ref, dstp_ref, x_hbm, wg_ref, wu_ref, wd_ref,
                y_hbm, xbuf, ybuf, gsem, ssem, *, rows, n_real):
    i = pl.program_id(0)
    nused = nused_ref[0]
    slot = i % 2
    tile_rows = lambda first: pl.ds(pl.multiple_of(first, SUBLANES), SUBLANES)

    def gather_row(tab_ref, sl, r):
        return pltpu.make_async_copy(x_hbm.at[tile_rows(tab_ref[0, 0, r])],
                                     xbuf.at[sl, pl.ds(r * SUBLANES, SUBLANES)], gsem.at[sl])

    def scatter_row(sl, r):
        return pltpu.make_async_copy(ybuf.at[sl, pl.ds(r * SUBLANES, SUBLANES)],
                                     y_hbm.at[tile_rows(dstp_ref[0, 0, r])], ssem.at[sl])

    def gather_all(sl):
        return pltpu.make_async_copy(x_hbm.at[pl.ds(0, rows * SUBLANES)], xbuf.at[sl],
                                     gsem.at[sl])

    def scatter_all(sl, first_row=0):
        return pltpu.make_async_copy(
            ybuf.at[sl], y_hbm.at[pl.ds(first_row * SUBLANES, rows * SUBLANES)], ssem.at[sl])

    def rolled(fn):
        def body(r, c):
            fn(r)
            return c
        lax.fori_loop(0, rows, body, 0, unroll=8)

    def inline(fn):
        for r in range(rows):
            fn(r)

    def ffn(sl):
        xb = _load_token_tiles(xbuf.at[sl]).astype(BF16)
        gate = _dot(xb, wg_ref[0])
        up = _dot(xb, wu_ref[0])
        _store_token_tiles(ybuf.at[sl], _dot(gate * _sigmoid(gate) * up, wd_ref[0]))

    @pl.when(i == 0)
    def _():
        ybuf[...] = jnp.zeros_like(ybuf)
        for sl in range(2):
            scatter_all(sl, n_real + sl * rows).start()
        for sl in range(2):
            scatter_all(sl, n_real + sl * rows).wait()
        rolled(lambda r: gather_row(tok0_ref, 0, r).start())

    @pl.when(i < nused)
    def _():
        gather_all(slot).wait()

    @pl.when((i >= 2) & (i <= nused))
    def _():
        scatter_all(slot).wait()

    @pl.when(i == 0)
    def _():
        inline(lambda r: gather_row(tokn_ref, 1, r).start())
        ffn(0)

    for s in range(2):
        @pl.when((i >= 1) & (i < nused) & (slot == s))
        def _():
            inline(lambda r: gather_row(tokn_ref, 1 - s, r).start(priority=r % 2))
            inline(lambda r: scatter_row(1 - s, r).start(priority=r % 2))
            ffn(s)

    @pl.when(i == nused)
    def _():
        rolled(lambda r: scatter_row(1 - slot, r).start())
        scatter_all(1 - slot).wait()
        gather_all(slot).wait()


def _stage_moe(x1t, row_tok, row_dst, blk_e, nused, w_gate, w_up, w_down):
    N = x1t.shape[0] // SUBLANES
    D = SUBLANES * LANES
    nblk, _, rows = row_tok.shape
    d_e = w_gate.shape[2]
    n_real = N * TOP_K
    last = nblk - 1
    wspec = lambda shape: pl.BlockSpec((1,) + shape,
                                       lambda i, be, nu: (be[jnp.minimum(i, last)], 0, 0))
    table = lambda fn: pl.BlockSpec((1, 1, rows), lambda i, be, nu: (fn(i), 0, 0),
                                    memory_space=pltpu.SMEM)
    grid_spec = pltpu.PrefetchScalarGridSpec(
        num_scalar_prefetch=2,
        grid=(nblk + 1,),
        in_specs=[table(lambda i: 0), table(lambda i: jnp.minimum(i + 1, last)),
                  table(lambda i: jnp.clip(i - 1, 0, last)),
                  pl.BlockSpec(memory_space=pl.ANY), wspec((D, d_e)), wspec((D, d_e)),
                  wspec((d_e, D))],
        out_specs=pl.BlockSpec(memory_space=pl.ANY),
        scratch_shapes=[pltpu.VMEM((2, rows * SUBLANES, LANES), F32),
                        pltpu.VMEM((2, rows * SUBLANES, LANES), F32),
                        pltpu.SemaphoreType.DMA((2,)), pltpu.SemaphoreType.DMA((2,))],
    )
    return pl.pallas_call(
        functools.partial(_moe_kernel, rows=rows, n_real=n_real),
        grid_spec=grid_spec,
        out_shape=jax.ShapeDtypeStruct(((n_real + 2 * rows) * SUBLANES, LANES), F32),
        compiler_params=_params(("arbitrary",)),
        name="moe",
    )(blk_e, nused, row_tok * SUBLANES, row_tok * SUBLANES, row_dst * SUBLANES, x1t, w_gate, w_up,
      w_down)


def _out_kernel(x1t_ref, y0_ref, y1_ref, we_ref, ple_ref, g_ref, b_ref, o_ref, *, alpha):
    we = we_ref[...]
    moe = we[:, 0:1] * _load_token_tiles(y0_ref) + we[:, 1:2] * _load_token_tiles(y1_ref)
    o_ref[...] = _layer_norm(alpha * _load_token_tiles(x1t_ref) + moe + ple_ref[...], g_ref[...],
                             b_ref[...])


def _stage_out(x1t, yb, we, ple, ln_g, ln_b, alpha):
    N = x1t.shape[0] // SUBLANES
    D = SUBLANES * LANES
    rows = min(OUT_ROWS, N)
    const = lambda shape: pl.BlockSpec(shape, lambda i: (0,) * len(shape))
    tile = lambda c: pl.BlockSpec((rows, c), lambda i: (i, 0))
    slot = lambda j: pl.BlockSpec((rows * SUBLANES, LANES), lambda i: (i + j * (N // rows), 0))
    return pl.pallas_call(
        functools.partial(_out_kernel, alpha=alpha),
        grid=(N // rows,),
        in_specs=[slot(0), slot(0), slot(1), tile(TOP_K), tile(D), const((1, D)), const((1, D))],
        out_specs=tile(D),
        out_shape=jax.ShapeDtypeStruct((N, D), F32),
        compiler_params=_params(("parallel",)),
        name="out",
    )(x1t, yb, yb, we, ple, ln_g.reshape(1, D), ln_b.reshape(1, D))


def kernel(x, p, w_in, mu_rkv, mu_lora, w0, w_lora1, w_lora2, a0, a_lora1, a_lora2, g_lora1, g_lora2, k_k, k_a, r_k, lnx_g, lnx_b, rel_bias, w_o, ln1_g, ln1_b, router_g, router_g_b, router_e, router_e_b, w_gate, w_up, w_down, ple_gate, ple_proj, ln2_g, ln2_b):
    B, S, D = x.shape
    depth = w_in.shape[0]
    alpha = (2 * depth) ** 0.25
    for i in range(depth):
        rkv, qkv, lw, ai, g = _stage_proj(x, w_in[i], w_lora1[i], a_lora1[i], g_lora1[i],
                                          mu_lora[i], w_lora2[i], a_lora2[i], g_lora2[i],
                                          w0[i], a0[i])
        yr = _stage_wkv(rkv, lw, ai, g, mu_rkv[i], k_k[i], k_a[i], r_k[i], lnx_g[i], lnx_b[i])
        att = _stage_attn(qkv, rel_bias).reshape(B * S, -1)
        x1t, route, ple, lane_counts = _stage_mix(
            x.reshape(B * S, D), yr.reshape(B * S, -1), att, p[i].reshape(B * S, -1), w_o[i],
            ln1_g[i], ln1_b[i], router_g[i], router_g_b[i], router_e[i], router_e_b[i],
            ple_gate[i], ple_proj[i], alpha)
        row_tok, row_dst, blk_e, nused, we = _route(route, lane_counts, MOE_ROWS)
        yb = _stage_moe(x1t, row_tok, row_dst, blk_e, nused, w_gate[i], w_up[i], w_down[i])
        x = _stage_out(x1t, yb, we, ple, ln2_g[i], ln2_b[i], alpha).reshape(B, S, D)
    return x
```
